```python
import math
import jax, jax.numpy as jnp
from jax import lax
import numpy as np

D_MODEL = 1024
BATCH = 2
SEQ = 8192
DEPTH = 1

D_MIX = D_MODEL
DIL_HEAD_DIM = 64
DIL_HEADS = (D_MIX // 2) // DIL_HEAD_DIM
DIL_WIDTH = DIL_HEADS * DIL_HEAD_DIM
DIL_BRANCHES = ((128, 1), (512, 4), (2048, 16))
MLA_NOPE = 128
MLA_ROPE = 64
MLA_QK_DIM = MLA_NOPE + MLA_ROPE
MLA_V_DIM = 128
MLA_HEADS = (D_MIX - DIL_WIDTH) // MLA_V_DIM
MLA_WIDTH = MLA_HEADS * MLA_V_DIM
MLA_Q_RANK = 256
MLA_KV_RANK = 128
ROPE_BASE = 10000.0
REL_BUCKETS = 32
REL_MAX_DIST = 2048
D_FF = 2816
FFN_RESID = 0.5
Q_BLOCK = 128
EPS = 1e-6
IN_COLS = 3 * DIL_WIDTH + MLA_Q_RANK + MLA_KV_RANK + MLA_ROPE

kernel_name = "hymba_dilated_mla_macaron"


def _rms(x, g):
    xf = x.astype(jnp.float32)
    y = xf * lax.rsqrt(jnp.mean(xf * xf, axis=-1, keepdims=True) + EPS)
    return (y * g.astype(jnp.float32)).astype(x.dtype)


def _swiglu(x, g, w_gate, w_up, w_down):
    h = _rms(x, g)
    return (jax.nn.silu(h @ w_gate) * (h @ w_up)) @ w_down


def _t5_bucket(dist):
    max_exact = REL_BUCKETS // 2
    d = np.maximum(dist, 1).astype(np.float32)
    large = max_exact + (np.log(d / max_exact) / np.log(REL_MAX_DIST / max_exact)
                         * (REL_BUCKETS - max_exact)).astype(np.int32)
    large = np.minimum(large, REL_BUCKETS - 1)
    return np.where(dist < max_exact, dist, large).astype(np.int32)


def _dilated_branch(q, k, v, rel_bias, window, dilation):
    B, S, H, hd = q.shape
    L = S // dilation
    W = window // dilation
    Bq = math.gcd(L, Q_BLOCK)
    nb = L // Bq

    def to_sub(t):
        return t.reshape(B, L, dilation, H, hd).transpose(0, 2, 3, 1, 4)

    qs = to_sub(q).reshape(B, dilation, H, nb, Bq, hd)
    pad = ((0, 0), (0, 0), (0, 0), (W, 0), (0, 0))
    ks = jnp.pad(to_sub(k), pad)
    vs = jnp.pad(to_sub(v), pad)
    idx = np.arange(nb)[:, None] * Bq + np.arange(Bq + W)[None, :]
    kb = ks[:, :, :, idx]
    vb = vs[:, :, :, idx]
    logits = jnp.einsum('brhnqc,brhnkc->brhnqk', qs, kb) * (hd ** -0.5)

    i = np.arange(Bq)[:, None]
    j = np.arange(Bq + W)[None, :]
    delta = i + W - j
    key_sub = idx[:, None, :] - W
    valid = (delta >= 0) & (delta <= W) & (key_sub >= 0)
    bucket = _t5_bucket(np.clip(delta, 0, None) * dilation)
    bias = jnp.take(rel_bias.astype(jnp.float32), jnp.asarray(bucket), axis=1)
    logits = logits + bias[None, None, :, None]
    logits = jnp.where(jnp.asarray(valid)[None, None, None], logits, -jnp.inf)
    lse = jax.nn.logsumexp(logits, axis=-1)
    p = jnp.exp(logits - lse[..., None])
    o = jnp.einsum('brhnqk,brhnkc->brhnqc', p, vb)
    o = o.reshape(B, dilation, H, L, hd).transpose(0, 3, 1, 2, 4).reshape(B, S, H, hd)
    lse = lse.reshape(B, dilation, H, L).transpose(0, 3, 1, 2).reshape(B, S, H)
    return o, lse


def _dilated_attention(q, k, v, q_g, k_g, rel_bias):
    B, S = q.shape[:2]
    sh = (B, S, DIL_HEADS, DIL_HEAD_DIM)
    q = _rms(q.reshape(sh).astype(jnp.float32), q_g)
    k = _rms(k.reshape(sh).astype(jnp.float32), k_g)
    v = v.reshape(sh).astype(jnp.float32)
    outs, lses = [], []
    for window, dilation in DIL_BRANCHES:
        o, lse = _dilated_branch(q, k, v, rel_bias, window, dilation)
        outs.append(o)
        lses.append(lse)
    alpha = jax.nn.softmax(jnp.stack(lses, 0), axis=0)
    o = jnp.sum(alpha[..., None] * jnp.stack(outs, 0), axis=0)
    return o.reshape(B, S, DIL_WIDTH)


def _rope(x, pos):
    dim = x.shape[-1]
    inv_freq = ROPE_BASE ** (-jnp.arange(0, dim, 2, dtype=jnp.float32) / dim)
    ang = pos[:, None] * inv_freq[None, :]
    cos = jnp.cos(ang)[None, :, None, :]
    sin = jnp.sin(ang)[None, :, None, :]
    x1, x2 = x[..., : dim // 2], x[..., dim // 2:]
    return jnp.concatenate([x1 * cos - x2 * sin, x2 * cos + x1 * sin], axis=-1)


def _mla(cq, ckv, k_pe, q_a_norm, w_q_b, kv_a_norm, w_kv_b, q_g, k_g):
    B, S = cq.shape[:2]
    H = MLA_HEADS
    pos = jnp.arange(S, dtype=jnp.float32)
    q = (_rms(cq, q_a_norm) @ w_q_b).reshape(B, S, H, MLA_QK_DIM).astype(jnp.float32)
    kv = (_rms(ckv, kv_a_norm) @ w_kv_b).reshape(B, S, H, MLA_NOPE + MLA_V_DIM).astype(jnp.float32)
    k_nope, v = kv[..., :MLA_NOPE], kv[..., MLA_NOPE:]
    k_pe = jnp.broadcast_to(k_pe.astype(jnp.float32)[:, :, None, :], (B, S, H, MLA_ROPE))
    k = jnp.concatenate([k_nope, k_pe], axis=-1)
    q = _rms(q, q_g)
    k = _rms(k, k_g)
    q = jnp.concatenate([q[..., :MLA_NOPE], _rope(q[..., MLA_NOPE:], pos)], axis=-1)
    k = jnp.concatenate([k[..., :MLA_NOPE], _rope(k[..., MLA_NOPE:], pos)], axis=-1)

    nb = S // Q_BLOCK
    qb = (q * MLA_QK_DIM ** -0.5).transpose(0, 2, 1, 3).reshape(B, H, nb, Q_BLOCK, MLA_QK_DIM)
    qb = qb.transpose(2, 0, 1, 3, 4)
    kt = k.transpose(0, 2, 1, 3)
    vt = v.transpose(0, 2, 1, 3)
    key_pos = jnp.arange(S)

    def block(args):
        q_blk, n = args
        logits = jnp.einsum('bhqc,bhkc->bhqk', q_blk, kt)
        q_pos = n * Q_BLOCK + jnp.arange(Q_BLOCK)
        mask = key_pos[None, :] <= q_pos[:, None]
        p = jax.nn.softmax(jnp.where(mask, logits, -jnp.inf), axis=-1)
        return jnp.einsum('bhqk,bhkc->bhqc', p, vt)

    o = lax.map(block, (qb, jnp.arange(nb)))
    return o.transpose(1, 0, 3, 2, 4).reshape(B, S, MLA_WIDTH)


def setup_inputs(seed: int = 0) -> dict:
    key = jax.random.key(seed)
    ks = jax.random.split(key, 24)
    f32 = jnp.float32

    def w(k, shape, fan_in):
        return jax.random.normal(k, (DEPTH,) + shape, f32) * fan_in ** -0.5

    def g(k, dim):
        return 1.0 + 0.02 * jax.random.normal(k, (DEPTH, dim), f32)

    return {
        "x": jax.random.normal(ks[0], (BATCH, SEQ, D_MODEL), f32),
        "ffn1_norm": g(ks[1], D_MODEL),
        "ffn1_w_gate": w(ks[2], (D_MODEL, D_FF), D_MODEL),
        "ffn1_w_up": w(ks[3], (D_MODEL, D_FF), D_MODEL),
        "ffn1_w_down": w(ks[4], (D_FF, D_MODEL), D_FF),
        "mix_norm": g(ks[5], D_MODEL),
        "w_in": w(ks[6], (D_MODEL, IN_COLS), D_MODEL),
        "dil_q_norm": g(ks[7], DIL_HEAD_DIM),
        "dil_k_norm": g(ks[8], DIL_HEAD_DIM),
        "rel_bias": 0.2 * jax.random.normal(ks[9], (DIL_HEADS, REL_BUCKETS), f32),
        "mla_q_a_norm": g(ks[10], MLA_Q_RANK),
        "mla_w_q_b": w(ks[11], (MLA_Q_RANK, MLA_HEADS * MLA_QK_DIM), MLA_Q_RANK),
        "mla_kv_a_norm": g(ks[12], MLA_KV_RANK),
        "mla_w_kv_b": w(ks[13], (MLA_KV_RANK, MLA_HEADS * (MLA_NOPE + MLA_V_DIM)), MLA_KV_RANK),
        "mla_q_norm": g(ks[14], MLA_QK_DIM),
        "mla_k_norm": g(ks[15], MLA_QK_DIM),
        "out_norm_dil": g(ks[16], DIL_WIDTH),
        "out_norm_mla": g(ks[17], MLA_WIDTH),
        "w_out": w(ks[18], (D_MIX, D_MODEL), D_MIX),
        "ffn2_norm": g(ks[19], D_MODEL),
        "ffn2_w_gate": w(ks[20], (D_MODEL, D_FF), D_MODEL),
        "ffn2_w_up": w(ks[21], (D_MODEL, D_FF), D_MODEL),
        "ffn2_w_down": w(ks[22], (D_FF, D_MODEL), D_FF),
    }


def reference(x, ffn1_norm, ffn1_w_gate, ffn1_w_up, ffn1_w_down, mix_norm, w_in,
              dil_q_norm, dil_k_norm, rel_bias, mla_q_a_norm, mla_w_q_b, mla_kv_a_norm,
              mla_w_kv_b, mla_q_norm, mla_k_norm, out_norm_dil, out_norm_mla, w_out,
              ffn2_norm, ffn2_w_gate, ffn2_w_up, ffn2_w_down):
    splits = np.cumsum([DIL_WIDTH, DIL_WIDTH, DIL_WIDTH, MLA_Q_RANK, MLA_KV_RANK])
    for l in range(DEPTH):
        x = x + FFN_RESID * _swiglu(x, ffn1_norm[l], ffn1_w_gate[l], ffn1_w_up[l], ffn1_w_down[l])
        h = _rms(x, mix_norm[l])
        proj = h @ w_in[l]
        q_a, k_a, v_a, cq, ckv, k_pe = jnp.split(proj, splits, axis=-1)
        o_dil = _dilated_attention(q_a, k_a, v_a, dil_q_norm[l], dil_k_norm[l], rel_bias)
        o_mla = _mla(cq, ckv, k_pe, mla_q_a_norm[l], mla_w_q_b[l], mla_kv_a_norm[l],
                     mla_w_kv_b[l], mla_q_norm[l], mla_k_norm[l])
        o = jnp.concatenate([_rms(o_dil, out_norm_dil[l]), _rms(o_mla, out_norm_mla[l])], axis=-1)
        x = x + o.astype(x.dtype) @ w_out[l]
        x = x + FFN_RESID * _swiglu(x, ffn2_norm[l], ffn2_w_gate[l], ffn2_w_up[l], ffn2_w_down[l])
    return x
```

```python
import functools
import math

import numpy as np
import jax
import jax.numpy as jnp
from jax import lax
from jax.experimental import pallas as pl
from jax.experimental.pallas import tpu as pltpu

F32 = jnp.float32
BF16 = jnp.bfloat16

DIL_HEAD_DIM = 64
DIL_BRANCHES = ((128, 1), (512, 4), (2048, 16))
DIL_BAND = 128
MLA_NOPE = 128
MLA_ROPE = 64
MLA_V_DIM = 128
ROPE_BASE = 10000.0
REL_BUCKETS = 32
REL_MAX_DIST = 2048
FFN_RESID = 0.5
EPS = 1e-6

LANES = 128
MXU_DIM = 256
VMEM_LIMIT_BYTES = 56 * 1024 * 1024

NEG = -1e30

FFN_TM = 512
FFN_TF = 256
PROJ_TM = 512
DIL_CHUNK = 2048
MLA_TQ = 512
MLA_TK = 512


def _dot(a, b):
    return jnp.dot(a, b, preferred_element_type=F32)


def _dot_nt(a, b):
    return lax.dot_general(a, b, (((1,), (1,)), ((), ())), preferred_element_type=F32)


def _rms(x, g):
    return x * lax.rsqrt(jnp.mean(x * x, axis=-1, keepdims=True) + EPS) * g


def _swiglu_residual(x, g_ref, wg_ref, wu_ref, wd_ref, act_ref):
    h = _rms(x, g_ref[...]).astype(BF16)
    d_ff = wg_ref.shape[1]
    for c in range(d_ff // FFN_TF):
        sl = slice(c * FFN_TF, (c + 1) * FFN_TF)
        gate = _dot(h, wg_ref[:, sl])
        up = _dot(h, wu_ref[:, sl])
        act_ref[:, sl] = (gate * jax.nn.sigmoid(gate) * up).astype(BF16)
    return x + FFN_RESID * _dot(act_ref[...], wd_ref[...])


def _ffn_kernel(x_ref, g_ref, wg_ref, wu_ref, wd_ref, o_ref, act_ref):
    o_ref[...] = _swiglu_residual(x_ref[...], g_ref, wg_ref, wu_ref, wd_ref, act_ref)


def _out_ffn_kernel(x_ref, od_ref, om_ref, gd_ref, gm_ref, wo_ref, g_ref, wg_ref, wu_ref, wd_ref,
                    o_ref, act_ref):
    wd_rows = od_ref.shape[1]
    od = _rms(od_ref[...], gd_ref[...]).astype(BF16)
    om = _rms(om_ref[...], gm_ref[...]).astype(BF16)
    x2 = x_ref[...] + _dot(od, wo_ref[:wd_rows, :]) + _dot(om, wo_ref[wd_rows:, :])
    o_ref[...] = _swiglu_residual(x2, g_ref, wg_ref, wu_ref, wd_ref, act_ref)


def _resident(shape):
    return pl.BlockSpec(shape, lambda *_: (0,) * len(shape), pipeline_mode=pl.Buffered(1))


def _ffn_call(x2d, g, wg, wu, wd):
    m, d = x2d.shape
    f = wg.shape[1]
    row = pl.BlockSpec((FFN_TM, d), lambda i: (i, 0))
    return pl.pallas_call(
        _ffn_kernel,
        grid=(m // FFN_TM,),
        in_specs=[row, _resident((1, d)), _resident((d, f)), _resident((d, f)), _resident((f, d))],
        out_specs=row,
        out_shape=jax.ShapeDtypeStruct((m, d), F32),
        scratch_shapes=[pltpu.VMEM((FFN_TM, f), BF16)],
        compiler_params=pltpu.CompilerParams(dimension_semantics=("parallel",),
                                             vmem_limit_bytes=VMEM_LIMIT_BYTES),
        name="ffn",
    )(x2d, g, wg, wu, wd)


def _out_ffn_call(x2d, o_dil, o_mla, gd, gm, wo, g, wg, wu, wd):
    m, d = x2d.shape
    f = wg.shape[1]
    wdil, wmla = o_dil.shape[1], o_mla.shape[1]
    row = pl.BlockSpec((FFN_TM, d), lambda i: (i, 0))
    return pl.pallas_call(
        _out_ffn_kernel,
        grid=(m // FFN_TM,),
        in_specs=[row,
                  pl.BlockSpec((FFN_TM, wdil), lambda i: (i, 0)),
                  pl.BlockSpec((FFN_TM, wmla), lambda i: (i, 0)),
                  _resident((1, wdil)), _resident((1, wmla)), _resident((wdil + wmla, d)),
                  _resident((1, d)), _resident((d, f)), _resident((d, f)), _resident((f, d))],
        out_specs=row,
        out_shape=jax.ShapeDtypeStruct((m, d), F32),
        scratch_shapes=[pltpu.VMEM((FFN_TM, f), BF16)],
        compiler_params=pltpu.CompilerParams(dimension_semantics=("parallel",),
                                             vmem_limit_bytes=VMEM_LIMIT_BYTES),
        name="out_ffn",
    )(x2d, o_dil, o_mla, gd, gm, wo, g, wg, wu, wd)


def _rope(x, cos, sin_signed):
    lane = lax.broadcasted_iota(jnp.int32, x.shape, 1)
    first_half = (lane % MLA_ROPE) < (MLA_ROPE // 2)
    partner = jnp.where(first_half,
                        pltpu.roll(x, LANES - MLA_ROPE // 2, 1),
                        pltpu.roll(x, MLA_ROPE // 2, 1))
    return x * cos + partner * sin_signed


def _proj_kernel(x_ref, gmix_ref, win_ref, eh_ref, gqa_ref, gka_ref,
                 gcq_ref, wqb_ref, gckv_ref, wkvb_ref,
                 gqn_ref, gqr_ref, gkn_ref, gkr_ref, cos_ref, sin_ref,
                 qa_ref, ka_ref, va_ref, qm_ref, km_ref, vm_ref, *, n_mla_heads, dil_width):
    h = _rms(x_ref[...], gmix_ref[...]).astype(BF16)
    proj = _dot(h, win_ref[...])
    w = dil_width
    for src, g_ref, dst in ((proj[:, 0:w], gqa_ref, qa_ref), (proj[:, w:2 * w], gka_ref, ka_ref)):
        ms = _dot((src * src).astype(BF16), eh_ref[...]) * (1.0 / DIL_HEAD_DIM)
        dst[...] = src * lax.rsqrt(ms + EPS) * g_ref[...]
    va_ref[...] = proj[:, 2 * w:3 * w]

    o = 3 * w
    q_rank = gcq_ref.shape[1]
    kv_rank = gckv_ref.shape[1]
    cq = proj[:, o:o + q_rank]
    ckv = proj[:, o + q_rank:o + q_rank + kv_rank]
    kpe = proj[:, o + q_rank + kv_rank:o + q_rank + kv_rank + LANES]
    cos = cos_ref[...]
    sin = sin_ref[...]
    qk_dim = MLA_NOPE + MLA_ROPE
    nh = n_mla_heads

    qb = _dot(_rms(cq, gcq_ref[...]).astype(BF16), wqb_ref[...])
    for hd in range(nh):
        qn = qb[:, hd * MLA_NOPE:(hd + 1) * MLA_NOPE]
        qr = qb[:, nh * MLA_NOPE + hd * LANES:nh * MLA_NOPE + (hd + 1) * LANES]
        ss = jnp.sum(qn * qn, axis=-1, keepdims=True) + jnp.sum(qr * qr, axis=-1, keepdims=True)
        r = lax.rsqrt(ss * (1.0 / qk_dim) + EPS)
        qm_ref[:, hd * MXU_DIM:hd * MXU_DIM + LANES] = (qn * r * gqn_ref[...]).astype(BF16)
        qm_ref[:, hd * MXU_DIM + LANES:(hd + 1) * MXU_DIM] = _rope(qr * r * gqr_ref[...], cos, sin).astype(BF16)

    kvb = _dot(_rms(ckv, gckv_ref[...]).astype(BF16), wkvb_ref[...])
    ss_pe = jnp.sum(kpe * kpe, axis=-1, keepdims=True)
    kpe_rot = _rope(kpe * gkr_ref[...], cos, sin)
    for hd in range(nh):
        kn = kvb[:, hd * MLA_NOPE:(hd + 1) * MLA_NOPE]
        ss = jnp.sum(kn * kn, axis=-1, keepdims=True) + ss_pe
        r = lax.rsqrt(ss * (1.0 / qk_dim) + EPS)
        km_ref[:, hd * MXU_DIM:hd * MXU_DIM + LANES] = (kn * r * gkn_ref[...]).astype(BF16)
        km_ref[:, hd * MXU_DIM + LANES:(hd + 1) * MXU_DIM] = (kpe_rot * r).astype(BF16)
    vm_ref[...] = kvb[:, nh * MLA_NOPE:].astype(BF16)


def _proj_call(x2d, seq, gmix, win, eh, gqa, gka, gcq, wqb, gckv, wkvb, gqn, gqr, gkn, gkr, cos, sin,
               *, n_mla_heads, dil_width):
    m, d = x2d.shape
    tm = PROJ_TM
    n_seq_blocks = seq // tm
    row = lambda width: pl.BlockSpec((tm, width), lambda i: (i, 0))
    pos = pl.BlockSpec((tm, LANES), lambda i: (i % n_seq_blocks, 0))
    consts = [gmix, win, eh, gqa, gka, gcq, wqb, gckv, wkvb, gqn, gqr, gkn, gkr]
    mla_w = n_mla_heads * MXU_DIM
    return pl.pallas_call(
        functools.partial(_proj_kernel, n_mla_heads=n_mla_heads, dil_width=dil_width),
        grid=(m // tm,),
        in_specs=[row(d)] + [_resident(c.shape) for c in consts] + [pos, pos],
        out_specs=[row(dil_width), row(dil_width), row(dil_width), row(mla_w), row(mla_w),
                   row(n_mla_heads * MLA_V_DIM)],
        out_shape=[jax.ShapeDtypeStruct((m, dil_width), F32)] * 3
        + [jax.ShapeDtypeStruct((m, mla_w), BF16)] * 2
        + [jax.ShapeDtypeStruct((m, n_mla_heads * MLA_V_DIM), BF16)],
        compiler_params=pltpu.CompilerParams(dimension_semantics=("parallel",),
                                             vmem_limit_bytes=VMEM_LIMIT_BYTES),
        name="proj",
    )(x2d, *consts, cos, sin)


def _t5_bucket(dist):
    max_exact = REL_BUCKETS // 2
    d = np.maximum(dist, 1).astype(np.float32)
    large = max_exact + (np.log(d / max_exact) / np.log(REL_MAX_DIST / max_exact)
                         * (REL_BUCKETS - max_exact)).astype(np.int32)
    large = np.minimum(large, REL_BUCKETS - 1)
    return np.where(dist < max_exact, dist, large).astype(np.int32)


def _bucket_tiles():
    i = np.arange(DIL_BAND)[:, None]
    j = np.arange(2 * DIL_BAND)[None, :]
    delta = i + DIL_BAND - j
    valid = (delta >= 0) & (delta <= DIL_BAND)
    tiles = [np.where(valid, _t5_bucket(np.clip(delta, 0, None) * dil), -1) for _, dil in DIL_BRANCHES]
    return np.stack(tiles).astype(np.int32)


def _dil_bias_kernel(rel_ref, bucket_ref, o_ref):
    hd = pl.program_id(1)
    bucket = bucket_ref[...]
    bias = jnp.full(bucket.shape, NEG, F32)
    for b in range(REL_BUCKETS):
        bias = jnp.where(bucket == b, rel_ref[hd, b], bias)
    col = lax.broadcasted_iota(jnp.int32, bucket.shape, 1)
    o_ref[0] = bias
    o_ref[1] = jnp.where(col < DIL_BAND, NEG, bias)


def _dil_bias_call(rel_bias, buckets):
    nbr = buckets.shape[0]
    nh = rel_bias.shape[0]
    tile = buckets.shape[1:]
    return pl.pallas_call(
        _dil_bias_kernel,
        grid=(nbr, nh),
        in_specs=[pl.BlockSpec(memory_space=pltpu.SMEM),
                  pl.BlockSpec((None,) + tile, lambda b, h: (b, 0, 0))],
        out_specs=pl.BlockSpec((None, 2, None) + tile, lambda b, h: (b, 0, h, 0, 0)),
        out_shape=jax.ShapeDtypeStruct((nbr, 2, nh) + tile, F32),
        name="dil_bias",
    )(rel_bias, buckets)


def _dil_kernel(q_ref, kp_ref, kc_ref, vp_ref, vc_ref, bias_ref, o_ref, kwin, vwin, acc_scr, m_scr):
    chunk = q_ref.shape[0]
    first_chunk = pl.program_id(2) == 0
    kwin[chunk:, :] = kc_ref[...]
    vwin[chunk:, :] = vc_ref[...]

    @pl.when(first_chunk)
    def _():
        kwin[:chunk, :] = jnp.zeros((chunk, LANES), F32)
        vwin[:chunk, :] = jnp.zeros((chunk, LANES), F32)

    @pl.when(jnp.logical_not(first_chunk))
    def _():
        kwin[:chunk, :] = kp_ref[...]
        vwin[:chunk, :] = vp_ref[...]

    head_a = lax.broadcasted_iota(jnp.int32, (1, LANES), 1) < DIL_HEAD_DIM
    tiles_per_chunk = chunk // DIL_BAND

    for bi, (_, dil) in enumerate(DIL_BRANCHES):
        shift = int(math.log2(dil))

        def rows(start, size, dil=dil):
            return pl.ds(start, size) if dil == 1 else pl.ds(start, size, stride=dil)

        def tile(t, carry, bi=bi, dil=dil, shift=shift, rows=rows):
            r = t & (dil - 1)
            n = t >> shift
            q0 = r + dil * DIL_BAND * n
            k0 = chunk + q0 - dil * DIL_BAND
            q = q_ref[rows(q0, DIL_BAND), :]
            k = kwin[rows(k0, 2 * DIL_BAND), :].astype(BF16)
            v = vwin[rows(k0, 2 * DIL_BAND), :]
            variant = jnp.logical_and(first_chunk, n == 0).astype(jnp.int32)
            for hh in range(2):
                mine = head_a if hh == 0 else jnp.logical_not(head_a)
                s = _dot_nt(jnp.where(mine, q, 0.0).astype(BF16), k) + bias_ref[bi, variant, hh]
                m = jnp.max(s, axis=-1, keepdims=True)
                p = jnp.exp(s - m).astype(BF16)
                acc_scr[bi, hh, rows(q0, DIL_BAND), :] = _dot(p, jnp.where(mine, v, 1.0).astype(BF16))
                m_scr[bi, hh, rows(q0, DIL_BAND), :] = jnp.broadcast_to(m, (DIL_BAND, LANES))
            return carry

        lax.fori_loop(0, tiles_per_chunk, tile, 0)

    nbr = len(DIL_BRANCHES)
    merge_rows = 256

    def merge(i, carry):
        sl = pl.ds(pl.multiple_of(i * merge_rows, merge_rows), merge_rows)
        outs = []
        for hh in range(2):
            ms = [m_scr[bi, hh, sl, :] for bi in range(nbr)]
            m_all = functools.reduce(jnp.maximum, ms)
            tot = sum(jnp.exp(ms[bi] - m_all) * acc_scr[bi, hh, sl, :] for bi in range(nbr))
            outs.append(tot / pltpu.roll(tot, DIL_HEAD_DIM, 1))
        o_ref[sl, :] = jnp.where(head_a, outs[0], outs[1])
        return carry

    lax.fori_loop(0, chunk // merge_rows, merge, 0)


def _dil_call(qa, ka, va, bias):
    b, s, w = qa.shape
    chunk = DIL_CHUNK
    pairs = w // LANES
    nbr = bias.shape[0]
    cur = pl.BlockSpec((None, chunk, LANES), lambda bb, p, c: (bb, c, p))
    prev = pl.BlockSpec((None, chunk, LANES), lambda bb, p, c: (bb, jnp.maximum(c - 1, 0), p))
    return pl.pallas_call(
        _dil_kernel,
        grid=(b, pairs, s // chunk),
        in_specs=[cur, prev, cur, prev, cur,
                  pl.BlockSpec((nbr, 2, 2, DIL_BAND, 2 * DIL_BAND), lambda bb, p, c: (0, 0, p, 0, 0))],
        out_specs=cur,
        out_shape=jax.ShapeDtypeStruct((b, s, w), F32),
        scratch_shapes=[pltpu.VMEM((2 * chunk, LANES), F32), pltpu.VMEM((2 * chunk, LANES), F32),
                        pltpu.VMEM((nbr, 2, chunk, LANES), F32), pltpu.VMEM((nbr, 2, chunk, LANES), F32)],
        compiler_params=pltpu.CompilerParams(dimension_semantics=("parallel", "parallel", "arbitrary"),
                                             vmem_limit_bytes=VMEM_LIMIT_BYTES),
        name="dil_attn",
    )(qa, ka, ka, va, va, bias)


def _mla_kernel(q_ref, k_ref, v_ref, o_ref, m_ref, acc_ref):
    tq = q_ref.shape[0]
    tk = MLA_TK
    qi = pl.program_id(2)
    q = q_ref[...]
    m_ref[...] = jnp.full(m_ref.shape, NEG, F32)
    acc_ref[...] = jnp.zeros(acc_ref.shape, F32)
    ones = jnp.ones((tk, LANES), BF16)

    def step(j, masked):
        k0 = pl.multiple_of(j * tk, tk)
        s = _dot_nt(q, k_ref[pl.ds(k0, tk), :])
        if masked:
            row = lax.broadcasted_iota(jnp.int32, s.shape, 0)
            col = lax.broadcasted_iota(jnp.int32, s.shape, 1)
            s = jnp.where(col <= row, s, NEG)
        m_old = m_ref[...]
        m_new = jnp.maximum(m_old, jnp.max(s, axis=-1, keepdims=True))
        m_ref[...] = m_new
        p = jnp.concatenate([jnp.exp(s[:, c * LANES:(c + 1) * LANES] - m_new) for c in range(tk // LANES)],
                            axis=1).astype(BF16)
        alpha = jnp.exp(m_old - m_new)
        vext = jnp.concatenate([v_ref[pl.ds(k0, tk), :], ones], axis=1)
        acc_ref[...] = acc_ref[...] * jnp.concatenate([alpha, alpha], axis=1) + _dot(p, vext)

    def body(j, carry):
        step(j, masked=False)
        return carry

    lax.fori_loop(0, qi * (tq // tk), body, 0)
    step(qi * (tq // tk), masked=True)
    acc = acc_ref[...]
    o_ref[...] = acc[:, :MLA_V_DIM] / acc[:, MLA_V_DIM:]


def _mla_call(qm, km, vm, n_heads):
    b, s, _ = qm.shape
    tq = MLA_TQ
    assert MLA_TQ == MLA_TK
    return pl.pallas_call(
        _mla_kernel,
        grid=(b, n_heads, s // tq),
        in_specs=[pl.BlockSpec((None, tq, MXU_DIM), lambda bb, h, i: (bb, i, h)),
                  pl.BlockSpec((None, s, MXU_DIM), lambda bb, h, i: (bb, 0, h)),
                  pl.BlockSpec((None, s, MLA_V_DIM), lambda bb, h, i: (bb, 0, h))],
        out_specs=pl.BlockSpec((None, tq, MLA_V_DIM), lambda bb, h, i: (bb, i, h)),
        out_shape=jax.ShapeDtypeStruct((b, s, n_heads * MLA_V_DIM), F32),
        scratch_shapes=[pltpu.VMEM((tq, LANES), F32), pltpu.VMEM((tq, 2 * MLA_V_DIM), F32)],
        compiler_params=pltpu.CompilerParams(dimension_semantics=("parallel", "parallel", "arbitrary"),
                                             vmem_limit_bytes=VMEM_LIMIT_BYTES),
        name="mla_attn",
    )(qm, km, vm)


def _rope_tables(seq):
    half = MLA_ROPE // 2
    inv_freq = ROPE_BASE ** (-jnp.arange(0, MLA_ROPE, 2, dtype=F32) / MLA_ROPE)
    ang = jnp.arange(seq, dtype=F32)[:, None] * inv_freq[None, :]
    cos, sin = jnp.cos(ang), jnp.sin(ang)
    cos_t = jnp.concatenate([cos, cos] * (LANES // MLA_ROPE), axis=1)
    sin_t = jnp.concatenate([-sin, sin] * (LANES // MLA_ROPE), axis=1)
    assert half * 2 * (LANES // MLA_ROPE) == LANES
    return cos_t, sin_t


def _pad_lanes(v, width):
    return jnp.pad(v, ((0, 0), (0, width - v.shape[1])))


def kernel(x, ffn1_norm, ffn1_w_gate, ffn1_w_up, ffn1_w_down, mix_norm, w_in, dil_q_norm, dil_k_norm,
           rel_bias, mla_q_a_norm, mla_w_q_b, mla_kv_a_norm, mla_w_kv_b, mla_q_norm, mla_k_norm,
           out_norm_dil, out_norm_mla, w_out, ffn2_norm, ffn2_w_gate, ffn2_w_up, ffn2_w_down):
    batch, seq, d_model = x.shape
    depth = w_in.shape[0]
    dil_width = out_norm_dil.shape[1]
    dil_heads = dil_width // DIL_HEAD_DIM
    q_rank = mla_q_a_norm.shape[1]
    kv_rank = mla_kv_a_norm.shape[1]
    qk_dim = MLA_NOPE + MLA_ROPE
    n_mla = mla_w_q_b.shape[2] // qk_dim
    assert rel_bias.shape == (dil_heads, REL_BUCKETS)
    assert w_in.shape[2] == 3 * dil_width + q_rank + kv_rank + MLA_ROPE
    assert seq % DIL_CHUNK == 0 and seq % MLA_TQ == 0 and (batch * seq) % FFN_TM == 0

    cos_t, sin_t = _rope_tables(seq)
    dil_bias = _dil_bias_call(rel_bias, jnp.asarray(_bucket_tiles()))
    eh = jnp.asarray(np.kron(np.eye(dil_heads), np.ones((DIL_HEAD_DIM, DIL_HEAD_DIM))), BF16)

    x2d = x.reshape(batch * seq, d_model)
    row = lambda v: v.reshape(1, -1)
    for l in range(depth):
        x2d = _ffn_call(x2d, row(ffn1_norm[l]), ffn1_w_gate[l].astype(BF16), ffn1_w_up[l].astype(BF16),
                        ffn1_w_down[l].astype(BF16))

        win = _pad_lanes(w_in[l], w_in.shape[2] + LANES - MLA_ROPE).astype(BF16)
        wqb = mla_w_q_b[l].reshape(q_rank, n_mla, qk_dim)
        wqb = jnp.concatenate(
            [wqb[:, :, :MLA_NOPE].reshape(q_rank, n_mla * MLA_NOPE),
             jnp.pad(wqb[:, :, MLA_NOPE:], ((0, 0), (0, 0), (0, LANES - MLA_ROPE))).reshape(q_rank, n_mla * LANES)],
            axis=1).astype(BF16)
        wkvb = mla_w_kv_b[l].reshape(kv_rank, n_mla, MLA_NOPE + MLA_V_DIM)
        wkvb = jnp.concatenate([wkvb[:, :, :MLA_NOPE].reshape(kv_rank, n_mla * MLA_NOPE),
                                wkvb[:, :, MLA_NOPE:].reshape(kv_rank, n_mla * MLA_V_DIM)], axis=1).astype(BF16)
        gqa = row(jnp.tile(dil_q_norm[l], dil_heads)) * DIL_HEAD_DIM ** -0.5
        gka = row(jnp.tile(dil_k_norm[l], dil_heads))
        gq = row(mla_q_norm[l]) * qk_dim ** -0.5
        gk = row(mla_k_norm[l])
        gqn, gqr = gq[:, :MLA_NOPE], _pad_lanes(gq[:, MLA_NOPE:], LANES)
        gkn, gkr = gk[:, :MLA_NOPE], _pad_lanes(gk[:, MLA_NOPE:], LANES)

        qa, ka, va, qm, km, vm = _proj_call(
            x2d, seq, row(mix_norm[l]), win, eh, gqa, gka, row(mla_q_a_norm[l]), wqb,
            row(mla_kv_a_norm[l]), wkvb, gqn, gqr, gkn, gkr, cos_t, sin_t,
            n_mla_heads=n_mla, dil_width=dil_width)

        shp = lambda a: a.reshape(batch, seq, a.shape[1])
        o_dil = _dil_call(shp(qa), shp(ka), shp(va), dil_bias)
        o_mla = _mla_call(shp(qm), shp(km), shp(vm), n_mla)

        x2d = _out_ffn_call(x2d, o_dil.reshape(batch * seq, dil_width), o_mla.reshape(batch * seq, -1),
                            row(out_norm_dil[l]), row(out_norm_mla[l]), w_out[l].astype(BF16),
                            row(ffn2_norm[l]), ffn2_w_gate[l].astype(BF16), ffn2_w_up[l].astype(BF16),
                            ffn2_w_down[l].astype(BF16))
    return x2d.reshape(batch, seq, d_model)
```

```python
import functools

import numpy as np
import jax
import jax.numpy as jnp
from jax import lax
from jax.experimental import pallas as pl
from jax.experimental.pallas import tpu as pltpu

F32 = jnp.float32
BF16 = jnp.bfloat16

DIL_HEAD_DIM = 64
DIL_BRANCHES = ((128, 1), (512, 4), (2048, 16))
DIL_BAND = 128
DIL_RES = 16
MLA_NOPE = 128
MLA_ROPE = 64
MLA_V_DIM = 128
ROPE_BASE = 10000.0
REL_BUCKETS = 32
REL_MAX_DIST = 2048
FFN_RESID = 0.5
EPS = 1e-6

LANES = 128
SUBLANES = 8
MXU_DIM = 256
VMEM_LIMIT_BYTES = 56 * 1024 * 1024

NEG = -1e30
LOG2E = 1.4426950408889634

ROW_TILE = 512
FFN_TF = 256
DIL_CHUNK = DIL_BAND * DIL_RES
MLA_TQ = 1024
MLA_TK = 512


def _dot(a, b):
    return jnp.dot(a, b, preferred_element_type=F32)


def _dot_nt(a, b):
    return lax.dot_general(a, b, (((1,), (1,)), ((), ())), preferred_element_type=F32)


def _rms(x, g):
    return x * lax.rsqrt(jnp.mean(x * x, axis=-1, keepdims=True) + EPS) * g


def _residue_major_perm(rows):
    per = rows // DIL_RES
    dst = np.arange(rows)
    src = DIL_RES * (dst % per) + dst // per
    p = np.zeros((rows, rows), np.float32)
    p[dst, src] = 1.0
    return p


def _swiglu_residual(x, g_ref, wg_ref, wu_ref, wd_ref, act_ref):
    h = _rms(x, g_ref[...]).astype(BF16)
    d_ff = wg_ref.shape[1]
    for c in range(d_ff // FFN_TF):
        sl = slice(c * FFN_TF, (c + 1) * FFN_TF)
        gate = _dot(h, wg_ref[:, sl])
        up = _dot(h, wu_ref[:, sl])
        act_ref[:, sl] = (gate * jax.nn.sigmoid(gate) * up).astype(BF16)
    return x + FFN_RESID * _dot(act_ref[...], wd_ref[...])


def _ffn_kernel(x_ref, g_ref, wg_ref, wu_ref, wd_ref, o_ref, act_ref):
    o_ref[...] = _swiglu_residual(x_ref[...], g_ref, wg_ref, wu_ref, wd_ref, act_ref)


def _out_ffn_kernel(x_ref, od_ref, om_ref, unperm_ref, gd_ref, gm_ref, wo_ref, g_ref, wg_ref, wu_ref, wd_ref,
                    o_ref, act_ref):
    tm = x_ref.shape[0]
    wd_rows = od_ref.shape[-1]
    od = _rms(od_ref[...].reshape(tm, wd_rows), gd_ref[...]).astype(BF16)
    od = _dot(unperm_ref[...], od).astype(BF16)
    om = _rms(om_ref[...], gm_ref[...]).astype(BF16)
    x2 = x_ref[...] + _dot(od, wo_ref[:wd_rows, :]) + _dot(om, wo_ref[wd_rows:, :])
    o_ref[...] = _swiglu_residual(x2, g_ref, wg_ref, wu_ref, wd_ref, act_ref)


def _resident(shape):
    return pl.BlockSpec(shape, lambda *_: (0,) * len(shape), pipeline_mode=pl.Buffered(1))


def _residue_major_block(seq, width):
    per = ROW_TILE // DIL_RES
    blocks_per_seq = seq // ROW_TILE
    return pl.BlockSpec((None, DIL_RES, per, width), lambda i: (i // blocks_per_seq, 0, i % blocks_per_seq, 0))


def _ffn_call(x2d, g, wg, wu, wd):
    m, d = x2d.shape
    f = wg.shape[1]
    row = pl.BlockSpec((ROW_TILE, d), lambda i: (i, 0))
    return pl.pallas_call(
        _ffn_kernel,
        grid=(m // ROW_TILE,),
        in_specs=[row, _resident((1, d)), _resident((d, f)), _resident((d, f)), _resident((f, d))],
        out_specs=row,
        out_shape=jax.ShapeDtypeStruct((m, d), F32),
        scratch_shapes=[pltpu.VMEM((ROW_TILE, f), BF16)],
        compiler_params=pltpu.CompilerParams(dimension_semantics=("parallel",),
                                             vmem_limit_bytes=VMEM_LIMIT_BYTES),
        name="ffn",
    )(x2d, g, wg, wu, wd)


def _out_ffn_call(x2d, seq, o_dil, o_mla, unperm, gd, gm, wo, g, wg, wu, wd):
    m, d = x2d.shape
    f = wg.shape[1]
    wdil, wmla = o_dil.shape[-1], o_mla.shape[-1]
    row = pl.BlockSpec((ROW_TILE, d), lambda i: (i, 0))
    return pl.pallas_call(
        _out_ffn_kernel,
        grid=(m // ROW_TILE,),
        in_specs=[row,
                  _residue_major_block(seq, wdil),
                  pl.BlockSpec((ROW_TILE, wmla), lambda i: (i, 0)),
                  _resident(unperm.shape), _resident((1, wdil)), _resident((1, wmla)),
                  _resident((wdil + wmla, d)),
                  _resident((1, d)), _resident((d, f)), _resident((d, f)), _resident((f, d))],
        out_specs=row,
        out_shape=jax.ShapeDtypeStruct((m, d), F32),
        scratch_shapes=[pltpu.VMEM((ROW_TILE, f), BF16)],
        compiler_params=pltpu.CompilerParams(dimension_semantics=("parallel",),
                                             vmem_limit_bytes=VMEM_LIMIT_BYTES),
        name="out_ffn",
    )(x2d, o_dil, o_mla, unperm, gd, gm, wo, g, wg, wu, wd)


def _rope(x, cos, sin_signed):
    lane = lax.broadcasted_iota(jnp.int32, x.shape, 1)
    first_half = (lane % MLA_ROPE) < (MLA_ROPE // 2)
    partner = jnp.where(first_half,
                        pltpu.roll(x, LANES - MLA_ROPE // 2, 1),
                        pltpu.roll(x, MLA_ROPE // 2, 1))
    return x * cos + partner * sin_signed


def _proj_kernel(x_ref, gmix_ref, win_ref, eh_ref, perm_ref, gqa_ref, gka_ref,
                 gcq_ref, wqb_ref, gckv_ref, wkvb_ref,
                 gqn_ref, gqr_ref, gkn_ref, gkr_ref, cos_ref, sin_ref,
                 qa_ref, ka_ref, va_ref, qm_ref, km_ref, vm_ref, *, n_mla_heads, dil_width):
    h = _rms(x_ref[...], gmix_ref[...]).astype(BF16)
    proj = _dot(h, win_ref[...])
    w = dil_width

    def to_residue_major(val, dst):
        dst[...] = _dot(perm_ref[...], val.astype(BF16)).reshape(dst.shape)

    for src, g_ref, dst in ((proj[:, 0:w], gqa_ref, qa_ref), (proj[:, w:2 * w], gka_ref, ka_ref)):
        ms = _dot((src * src).astype(BF16), eh_ref[...]) * (1.0 / DIL_HEAD_DIM)
        to_residue_major(src * lax.rsqrt(ms + EPS) * g_ref[...], dst)
    to_residue_major(proj[:, 2 * w:3 * w], va_ref)

    o = 3 * w
    q_rank = gcq_ref.shape[1]
    kv_rank = gckv_ref.shape[1]
    cq = proj[:, o:o + q_rank]
    ckv = proj[:, o + q_rank:o + q_rank + kv_rank]
    kpe = proj[:, o + q_rank + kv_rank:o + q_rank + kv_rank + LANES]
    cos = cos_ref[...]
    sin = sin_ref[...]
    qk_dim = MLA_NOPE + MLA_ROPE
    nh = n_mla_heads

    qb = _dot(_rms(cq, gcq_ref[...]).astype(BF16), wqb_ref[...])
    for hd in range(nh):
        qn = qb[:, hd * MLA_NOPE:(hd + 1) * MLA_NOPE]
        qr = qb[:, nh * MLA_NOPE + hd * LANES:nh * MLA_NOPE + (hd + 1) * LANES]
        ss = jnp.sum(qn * qn, axis=-1, keepdims=True) + jnp.sum(qr * qr, axis=-1, keepdims=True)
        r = lax.rsqrt(ss * (1.0 / qk_dim) + EPS)
        qm_ref[:, hd * MXU_DIM:hd * MXU_DIM + LANES] = (qn * r * gqn_ref[...]).astype(BF16)
        qm_ref[:, hd * MXU_DIM + LANES:(hd + 1) * MXU_DIM] = _rope(qr * r * gqr_ref[...], cos, sin).astype(BF16)

    kvb = _dot(_rms(ckv, gckv_ref[...]).astype(BF16), wkvb_ref[...])
    ss_pe = jnp.sum(kpe * kpe, axis=-1, keepdims=True)
    kpe_rot = _rope(kpe * gkr_ref[...], cos, sin)
    for hd in range(nh):
        kn = kvb[:, hd * MLA_NOPE:(hd + 1) * MLA_NOPE]
        ss = jnp.sum(kn * kn, axis=-1, keepdims=True) + ss_pe
        r = lax.rsqrt(ss * (1.0 / qk_dim) + EPS)
        km_ref[:, hd * MXU_DIM:hd * MXU_DIM + LANES] = (kn * r * gkn_ref[...]).astype(BF16)
        km_ref[:, hd * MXU_DIM + LANES:(hd + 1) * MXU_DIM] = (kpe_rot * r).astype(BF16)
    vm_ref[...] = kvb[:, nh * MLA_NOPE:].astype(BF16)


def _proj_call(x2d, batch, seq, consts, cos, sin, *, n_mla_heads, dil_width):
    m, d = x2d.shape
    tm = ROW_TILE
    n_seq_blocks = seq // tm
    row = lambda width: pl.BlockSpec((tm, width), lambda i: (i, 0))
    pos = pl.BlockSpec((tm, LANES), lambda i: (i % n_seq_blocks, 0))
    mla_w = n_mla_heads * MXU_DIM
    dil_shape = jax.ShapeDtypeStruct((batch, DIL_RES, seq // DIL_RES, dil_width), F32)
    return pl.pallas_call(
        functools.partial(_proj_kernel, n_mla_heads=n_mla_heads, dil_width=dil_width),
        grid=(m // tm,),
        in_specs=[row(d)] + [_resident(c.shape) for c in consts] + [pos, pos],
        out_specs=[_residue_major_block(seq, dil_width)] * 3
        + [row(mla_w), row(mla_w), row(n_mla_heads * MLA_V_DIM)],
        out_shape=[dil_shape] * 3
        + [jax.ShapeDtypeStruct((m, mla_w), BF16)] * 2
        + [jax.ShapeDtypeStruct((m, n_mla_heads * MLA_V_DIM), BF16)],
        compiler_params=pltpu.CompilerParams(dimension_semantics=("parallel",),
                                             vmem_limit_bytes=VMEM_LIMIT_BYTES),
        name="proj",
    )(x2d, *consts, cos, sin)


def _t5_bucket(dist):
    max_exact = REL_BUCKETS // 2
    d = np.maximum(dist, 1).astype(np.float32)
    large = max_exact + (np.log(d / max_exact) / np.log(REL_MAX_DIST / max_exact)
                         * (REL_BUCKETS - max_exact)).astype(np.int32)
    large = np.minimum(large, REL_BUCKETS - 1)
    return np.where(dist < max_exact, dist, large).astype(np.int32)


def _band_tables():
    rho = np.arange(DIL_BAND)
    kap = np.arange(2 * DIL_BAND)
    buckets, prev = [], []
    for _, dil in DIL_BRANCHES:
        g = DIL_RES // dil
        run_q = DIL_BAND // g
        pos_q = g * (rho % run_q) + rho // run_q
        pos_k = g * (kap % (2 * run_q)) + kap // (2 * run_q) - DIL_BAND
        delta = pos_q[:, None] - pos_k[None, :]
        valid = (delta >= 0) & (delta <= DIL_BAND)
        buckets.append(np.where(valid, _t5_bucket(np.clip(delta, 0, None) * dil), -1))
        prev.append((pos_k < 0)[None, :])
    return np.stack(buckets).astype(np.int32), np.stack(prev).astype(np.int32)


def _dil_bias_kernel(rel_ref, bucket_ref, prev_ref, o_ref):
    hd = pl.program_id(1)
    bucket = bucket_ref[...]
    bias = jnp.full(bucket.shape, NEG, F32)
    for b in range(REL_BUCKETS):
        bias = jnp.where(bucket == b, rel_ref[hd, b] * LOG2E, bias)
    o_ref[0] = bias
    o_ref[1] = jnp.where(prev_ref[...] > 0, NEG, bias)


def _dil_bias_call(rel_bias, buckets, prev):
    nbr = buckets.shape[0]
    nh = rel_bias.shape[0]
    tile = buckets.shape[1:]
    return pl.pallas_call(
        _dil_bias_kernel,
        grid=(nbr, nh),
        in_specs=[pl.BlockSpec(memory_space=pltpu.SMEM),
                  pl.BlockSpec((None,) + tile, lambda b, h: (b, 0, 0)),
                  pl.BlockSpec((None, 1, tile[1]), lambda b, h: (b, 0, 0))],
        out_specs=pl.BlockSpec((None, 2, None) + tile, lambda b, h: (b, 0, h, 0, 0)),
        out_shape=jax.ShapeDtypeStruct((nbr, 2, nh) + tile, F32),
        name="dil_bias",
    )(rel_bias, buckets, prev)


def _band_tile(q, k, v, bias, head_a):
    zero = jnp.zeros_like(q)
    q2 = jnp.concatenate([jnp.where(head_a, q, zero), jnp.where(head_a, zero, q)], axis=0).astype(BF16)
    s = _dot_nt(q2, k.astype(BF16)) + bias
    m = jnp.max(s, axis=-1, keepdims=True)
    p = jnp.exp2(s - m).astype(BF16)
    ones = jnp.ones(v.shape, BF16)
    out = _dot(p, jnp.concatenate([v.astype(BF16), ones], axis=1))
    h = DIL_BAND
    num = jnp.where(head_a, out[:h, :LANES], out[h:, :LANES])
    den = jnp.where(head_a, out[:h, LANES:], out[h:, LANES:])
    mx = jnp.where(head_a, m[:h], m[h:])
    return num, den, mx


def _dil_kernel(q_ref, kp_ref, kc_ref, vp_ref, vc_ref, bias_ref, o_ref, num_scr, den_scr, max_scr):
    first_chunk = (pl.program_id(2) == 0).astype(jnp.int32)
    head_a = lax.broadcasted_iota(jnp.int32, (1, LANES), 1) < DIL_HEAD_DIM

    def tile(bi, dil, rbase, n):
        g = DIL_RES // dil
        run = DIL_BAND // g
        static_n = isinstance(n, int)
        q_rows = pl.ds(n * run, run) if static_n else pl.ds(pl.multiple_of(n * run, run), run)
        qs, ks, vs = [], [], []
        for u in range(g):
            res = rbase + dil * u
            qs.append(q_ref[res, q_rows, :])
            if static_n and n == 0:
                ks += [kp_ref[res, pl.ds(DIL_BAND - run, run), :], kc_ref[res, pl.ds(0, run), :]]
                vs += [vp_ref[res, pl.ds(DIL_BAND - run, run), :], vc_ref[res, pl.ds(0, run), :]]
            else:
                k_rows = (pl.ds((n - 1) * run, 2 * run) if static_n
                          else pl.ds(pl.multiple_of((n - 1) * run, run), 2 * run))
                ks.append(kc_ref[res, k_rows, :])
                vs.append(vc_ref[res, k_rows, :])
        cat = lambda parts: parts[0] if len(parts) == 1 else jnp.concatenate(parts, axis=0)
        variant = first_chunk if (static_n and n == 0) else 0
        num, den, mx = _band_tile(cat(qs), cat(ks), cat(vs), bias_ref[bi, variant], head_a)
        for u in range(g):
            res = rbase + dil * u
            part = slice(u * run, (u + 1) * run)
            num_scr[bi, res, q_rows, :] = num[part]
            den_scr[bi, res, q_rows, :] = den[part]
            max_scr[bi, res, q_rows, :] = mx[part]

    for n in range(DIL_RES):
        tile(0, 1, 0, n)
    for r4 in range(4):
        for n in range(4):
            tile(1, 4, r4, n)
    for r in range(DIL_RES):
        tile(2, 16, r, 0)

    nbr = len(DIL_BRANCHES)

    def merge(res, carry):
        ms = [max_scr[bi, res] for bi in range(nbr)]
        m_all = functools.reduce(jnp.maximum, ms)
        es = [jnp.exp2(mb - m_all) for mb in ms]
        num = sum(es[bi] * num_scr[bi, res] for bi in range(nbr))
        den = sum(es[bi] * den_scr[bi, res] for bi in range(nbr))
        o_ref[res] = num / den
        return carry

    lax.fori_loop(0, DIL_RES, merge, 0)


def _dil_call(qa, ka, va, bias):
    b, _, per, w = qa.shape
    pairs = w // LANES
    nbr = bias.shape[0]
    blk = (None, DIL_RES, DIL_BAND, LANES)
    cur = pl.BlockSpec(blk, lambda bb, p, c: (bb, 0, c, p))
    prev = pl.BlockSpec(blk, lambda bb, p, c: (bb, 0, jnp.maximum(c - 1, 0), p))
    scr = pltpu.VMEM((nbr, DIL_RES, DIL_BAND, LANES), F32)
    return pl.pallas_call(
        _dil_kernel,
        grid=(b, pairs, per // DIL_BAND),
        in_specs=[cur, prev, cur, prev, cur,
                  pl.BlockSpec((nbr, 2, None, 2 * DIL_BAND, 2 * DIL_BAND), lambda bb, p, c: (0, 0, p, 0, 0))],
        out_specs=cur,
        out_shape=jax.ShapeDtypeStruct(qa.shape, F32),
        scratch_shapes=[scr, scr, scr],
        compiler_params=pltpu.CompilerParams(dimension_semantics=("parallel", "parallel", "arbitrary"),
                                             vmem_limit_bytes=VMEM_LIMIT_BYTES),
        name="dil_attn",
    )(qa, ka, ka, va, va, bias)


def _mla_kernel(q_ref, k_ref, v_ref, o_ref, m_ref, acc_ref):
    tq = q_ref.shape[0]
    tk = MLA_TK
    nsub = tq // tk
    qi = pl.program_id(2)
    m_ref[...] = jnp.full(m_ref.shape, NEG, F32)
    acc_ref[...] = jnp.zeros(acc_ref.shape, F32)
    ones = jnp.ones((tk, LANES), BF16)

    def step(kblk, row0, diagonal):
        k0 = pl.multiple_of(kblk * tk, tk)
        rows = slice(row0, tq)
        s = _dot_nt(q_ref[rows, :], k_ref[pl.ds(k0, tk), :])
        if diagonal:
            row = lax.broadcasted_iota(jnp.int32, s.shape, 0)
            col = lax.broadcasted_iota(jnp.int32, s.shape, 1)
            s = jnp.where(col <= row, s, NEG)
        m_old = m_ref[rows, :]
        m_new = jnp.maximum(m_old, jnp.max(s, axis=-1, keepdims=True))
        m_ref[rows, :] = m_new
        p = jnp.concatenate([jnp.exp2(s[:, c * LANES:(c + 1) * LANES] - m_new) for c in range(tk // LANES)],
                            axis=1).astype(BF16)
        alpha = jnp.exp2(m_old - m_new)
        vext = jnp.concatenate([v_ref[pl.ds(k0, tk), :], ones], axis=1)
        acc_ref[rows, :] = acc_ref[rows, :] * jnp.concatenate([alpha, alpha], axis=1) + _dot(p, vext)

    def body(i, carry):
        for jj in range(nsub):
            step(i * nsub + jj, 0, diagonal=False)
        return carry

    lax.fori_loop(0, qi, body, 0)
    for jj in range(nsub):
        step(qi * nsub + jj, jj * tk, diagonal=True)
    acc = acc_ref[...]
    o_ref[...] = acc[:, :MLA_V_DIM] / acc[:, MLA_V_DIM:]


def _mla_call(qm, km, vm, n_heads):
    b, s, _ = qm.shape
    tq = MLA_TQ
    return pl.pallas_call(
        _mla_kernel,
        grid=(b, n_heads, s // tq),
        in_specs=[pl.BlockSpec((None, tq, MXU_DIM), lambda bb, h, i: (bb, i, h)),
                  pl.BlockSpec((None, s, MXU_DIM), lambda bb, h, i: (bb, 0, h)),
                  pl.BlockSpec((None, s, MLA_V_DIM), lambda bb, h, i: (bb, 0, h))],
        out_specs=pl.BlockSpec((None, tq, MLA_V_DIM), lambda bb, h, i: (bb, i, h)),
        out_shape=jax.ShapeDtypeStruct((b, s, n_heads * MLA_V_DIM), F32),
        scratch_shapes=[pltpu.VMEM((tq, LANES), F32), pltpu.VMEM((tq, 2 * MLA_V_DIM), F32)],
        compiler_params=pltpu.CompilerParams(dimension_semantics=("parallel", "parallel", "arbitrary"),
                                             vmem_limit_bytes=VMEM_LIMIT_BYTES),
        name="mla_attn",
    )(qm, km, vm)


def _rope_tables(seq):
    inv_freq = ROPE_BASE ** (-jnp.arange(0, MLA_ROPE, 2, dtype=F32) / MLA_ROPE)
    ang = jnp.arange(seq, dtype=F32)[:, None] * inv_freq[None, :]
    cos, sin = jnp.cos(ang), jnp.sin(ang)
    cos_t = jnp.concatenate([cos, cos] * (LANES // MLA_ROPE), axis=1)
    sin_t = jnp.concatenate([-sin, sin] * (LANES // MLA_ROPE), axis=1)
    return cos_t, sin_t


def _pad_lanes(v, width):
    return jnp.pad(v, ((0, 0), (0, width - v.shape[1])))


def kernel(x, ffn1_norm, ffn1_w_gate, ffn1_w_up, ffn1_w_down, mix_norm, w_in, dil_q_norm, dil_k_norm,
           rel_bias, mla_q_a_norm, mla_w_q_b, mla_kv_a_norm, mla_w_kv_b, mla_q_norm, mla_k_norm,
           out_norm_dil, out_norm_mla, w_out, ffn2_norm, ffn2_w_gate, ffn2_w_up, ffn2_w_down):
    batch, seq, d_model = x.shape
    depth = w_in.shape[0]
    dil_width = out_norm_dil.shape[1]
    dil_heads = dil_width // DIL_HEAD_DIM
    q_rank = mla_q_a_norm.shape[1]
    kv_rank = mla_kv_a_norm.shape[1]
    qk_dim = MLA_NOPE + MLA_ROPE
    n_mla = mla_w_q_b.shape[2] // qk_dim
    assert rel_bias.shape == (dil_heads, REL_BUCKETS)
    assert w_in.shape[2] == 3 * dil_width + q_rank + kv_rank + MLA_ROPE
    assert seq % DIL_CHUNK == 0 and seq % MLA_TQ == 0 and seq % ROW_TILE == 0
    assert all(win // dil == DIL_BAND and DIL_RES % dil == 0 for win, dil in DIL_BRANCHES)

    cos_t, sin_t = _rope_tables(seq)
    buckets, prev = _band_tables()
    dil_bias = _dil_bias_call(rel_bias, jnp.asarray(buckets), jnp.asarray(prev))
    dil_bias = dil_bias.reshape(dil_bias.shape[0], 2, dil_heads // 2, 2 * DIL_BAND, 2 * DIL_BAND)
    eh = jnp.asarray(np.kron(np.eye(dil_heads), np.ones((DIL_HEAD_DIM, DIL_HEAD_DIM))), BF16)
    perm_np = _residue_major_perm(ROW_TILE)
    perm, unperm = jnp.asarray(perm_np, BF16), jnp.asarray(perm_np.T, BF16)

    x2d = x.reshape(batch * seq, d_model)
    row = lambda v: v.reshape(1, -1)
    for l in range(depth):
        x2d = _ffn_call(x2d, row(ffn1_norm[l]), ffn1_w_gate[l].astype(BF16), ffn1_w_up[l].astype(BF16),
                        ffn1_w_down[l].astype(BF16))

        win = _pad_lanes(w_in[l], w_in.shape[2] + LANES - MLA_ROPE).astype(BF16)
        wqb = mla_w_q_b[l].reshape(q_rank, n_mla, qk_dim)
        wqb = jnp.concatenate(
            [wqb[:, :, :MLA_NOPE].reshape(q_rank, n_mla * MLA_NOPE),
             jnp.pad(wqb[:, :, MLA_NOPE:], ((0, 0), (0, 0), (0, LANES - MLA_ROPE))).reshape(q_rank, n_mla * LANES)],
            axis=1).astype(BF16)
        wkvb = mla_w_kv_b[l].reshape(kv_rank, n_mla, MLA_NOPE + MLA_V_DIM)
        wkvb = jnp.concatenate([wkvb[:, :, :MLA_NOPE].reshape(kv_rank, n_mla * MLA_NOPE),
                                wkvb[:, :, MLA_NOPE:].reshape(kv_rank, n_mla * MLA_V_DIM)], axis=1).astype(BF16)
        gqa = row(jnp.tile(dil_q_norm[l], dil_heads)) * (LOG2E * DIL_HEAD_DIM ** -0.5)
        gka = row(jnp.tile(dil_k_norm[l], dil_heads))
        gq = row(mla_q_norm[l]) * (LOG2E * qk_dim ** -0.5)
        gk = row(mla_k_norm[l])
        gqn, gqr = gq[:, :MLA_NOPE], _pad_lanes(gq[:, MLA_NOPE:], LANES)
        gkn, gkr = gk[:, :MLA_NOPE], _pad_lanes(gk[:, MLA_NOPE:], LANES)

        consts = [row(mix_norm[l]), win, eh, perm, gqa, gka, row(mla_q_a_norm[l]), wqb,
                  row(mla_kv_a_norm[l]), wkvb, gqn, gqr, gkn, gkr]
        qa, ka, va, qm, km, vm = _proj_call(x2d, batch, seq, consts, cos_t, sin_t,
                                            n_mla_heads=n_mla, dil_width=dil_width)

        shp = lambda a: a.reshape(batch, seq, a.shape[1])
        o_dil = _dil_call(qa, ka, va, dil_bias)
        o_mla = _mla_call(shp(qm), shp(km), shp(vm), n_mla)

        x2d = _out_ffn_call(x2d, seq, o_dil, o_mla.reshape(batch * seq, -1), unperm,
                            row(out_norm_dil[l]), row(out_norm_mla[l]), w_out[l].astype(BF16),
                            row(ffn2_norm[l]), ffn2_w_gate[l].astype(BF16), ffn2_w_up[l].astype(BF16),
                            ffn2_w_down[l].astype(BF16))
    return x2d.reshape(batch, seq, d_model)
```

```python
import functools

import numpy as np
import jax
import jax.numpy as jnp
from jax import lax
from jax.experimental import pallas as pl
from jax.experimental.pallas import tpu as pltpu

F32 = jnp.float32
BF16 = jnp.bfloat16

DIL_HEAD_DIM = 64
DIL_BRANCHES = ((128, 1), (512, 4), (2048, 16))
DIL_BAND = 128
DIL_RES = 16
MLA_NOPE = 128
MLA_ROPE = 64
MLA_V_DIM = 128
ROPE_BASE = 10000.0
REL_BUCKETS = 32
REL_MAX_DIST = 2048
FFN_RESID = 0.5
EPS = 1e-6

LANES = 128
SUBLANES = 8
MXU_DIM = 256
VMEM_LIMIT_BYTES = 56 * 1024 * 1024

NEG = -1e30
LOG2E = 1.4426950408889634

ROW_TILE = 512
FFN_TF = 256
DIL_CHUNK = DIL_BAND * DIL_RES
MLA_TQ = 1024
MLA_TK = 512


def _dot(a, b):
    return jnp.dot(a, b, preferred_element_type=F32)


def _dot_nt(a, b):
    return lax.dot_general(a, b, (((1,), (1,)), ((), ())), preferred_element_type=F32)


def _rms(x, g):
    return x * lax.rsqrt(jnp.mean(x * x, axis=-1, keepdims=True) + EPS) * g


def _residue_major_perm(rows):
    per = rows // DIL_RES
    dst = np.arange(rows)
    src = DIL_RES * (dst % per) + dst // per
    p = np.zeros((rows, rows), np.float32)
    p[dst, src] = 1.0
    return p


def _swiglu_residual(x, g_ref, wg_ref, wu_ref, wd_ref, act_ref):
    h = _rms(x, g_ref[...]).astype(BF16)
    d_ff = wg_ref.shape[1]
    for c in range(d_ff // FFN_TF):
        sl = slice(c * FFN_TF, (c + 1) * FFN_TF)
        gate = _dot(h, wg_ref[:, sl])
        up = _dot(h, wu_ref[:, sl])
        act_ref[:, sl] = (gate * jax.nn.sigmoid(gate) * up).astype(BF16)
    return x + FFN_RESID * _dot(act_ref[...], wd_ref[...])


def _ffn_kernel(x_ref, g_ref, wg_ref, wu_ref, wd_ref, o_ref, act_ref):
    o_ref[...] = _swiglu_residual(x_ref[...], g_ref, wg_ref, wu_ref, wd_ref, act_ref)


def _out_ffn_kernel(x_ref, od_ref, om_ref, unperm_ref, gd_ref, gm_ref, wo_ref, g_ref, wg_ref, wu_ref, wd_ref,
                    o_ref, act_ref):
    tm = x_ref.shape[0]
    wd_rows = od_ref.shape[-1]
    od = _rms(od_ref[...].reshape(tm, wd_rows), gd_ref[...]).astype(BF16)
    od = _dot(unperm_ref[...], od).astype(BF16)
    om = _rms(om_ref[...], gm_ref[...]).astype(BF16)
    x2 = x_ref[...] + _dot(od, wo_ref[:wd_rows, :]) + _dot(om, wo_ref[wd_rows:, :])
    o_ref[...] = _swiglu_residual(x2, g_ref, wg_ref, wu_ref, wd_ref, act_ref)


def _resident(shape):
    return pl.BlockSpec(shape, lambda *_: (0,) * len(shape), pipeline_mode=pl.Buffered(1))


def _residue_major_block(seq, width):
    per = ROW_TILE // DIL_RES
    blocks_per_seq = seq // ROW_TILE
    return pl.BlockSpec((None, DIL_RES, per, width), lambda i: (i // blocks_per_seq, 0, i % blocks_per_seq, 0))


def _ffn_call(x2d, g, wg, wu, wd):
    m, d = x2d.shape
    f = wg.shape[1]
    row = pl.BlockSpec((ROW_TILE, d), lambda i: (i, 0))
    return pl.pallas_call(
        _ffn_kernel,
        grid=(m // ROW_TILE,),
        in_specs=[row, _resident((1, d)), _resident((d, f)), _resident((d, f)), _resident((f, d))],
        out_specs=row,
        out_shape=jax.ShapeDtypeStruct((m, d), F32),
        scratch_shapes=[pltpu.VMEM((ROW_TILE, f), BF16)],
        compiler_params=pltpu.CompilerParams(dimension_semantics=("parallel",),
                                             vmem_limit_bytes=VMEM_LIMIT_BYTES),
        name="ffn",
    )(x2d, g, wg, wu, wd)


def _out_ffn_call(x2d, seq, o_dil, o_mla, unperm, gd, gm, wo, g, wg, wu, wd):
    m, d = x2d.shape
    f = wg.shape[1]
    wdil, wmla = o_dil.shape[-1], o_mla.shape[-1]
    row = pl.BlockSpec((ROW_TILE, d), lambda i: (i, 0))
    return pl.pallas_call(
        _out_ffn_kernel,
        grid=(m // ROW_TILE,),
        in_specs=[row,
                  _residue_major_block(seq, wdil),
                  pl.BlockSpec((ROW_TILE, wmla), lambda i: (i, 0)),
                  _resident(unperm.shape), _resident((1, wdil)), _resident((1, wmla)),
                  _resident((wdil + wmla, d)),
                  _resident((1, d)), _resident((d, f)), _resident((d, f)), _resident((f, d))],
        out_specs=row,
        out_shape=jax.ShapeDtypeStruct((m, d), F32),
        scratch_shapes=[pltpu.VMEM((ROW_TILE, f), BF16)],
        compiler_params=pltpu.CompilerParams(dimension_semantics=("parallel",),
                                             vmem_limit_bytes=VMEM_LIMIT_BYTES),
        name="out_ffn",
    )(x2d, o_dil, o_mla, unperm, gd, gm, wo, g, wg, wu, wd)


def _rope(x, cos, sin_signed):
    lane = lax.broadcasted_iota(jnp.int32, x.shape, 1)
    first_half = (lane % MLA_ROPE) < (MLA_ROPE // 2)
    partner = jnp.where(first_half,
                        pltpu.roll(x, LANES - MLA_ROPE // 2, 1),
                        pltpu.roll(x, MLA_ROPE // 2, 1))
    return x * cos + partner * sin_signed


def _proj_kernel(x_ref, gmix_ref, win_ref, eh_ref, perm_ref, gqa_ref, gka_ref,
                 gcq_ref, wqb_ref, gckv_ref, wkvb_ref,
                 gqn_ref, gqr_ref, gkn_ref, gkr_ref, cos_ref, sin_ref,
                 qa_ref, ka_ref, va_ref, qm_ref, km_ref, vm_ref, *, n_mla_heads, dil_width):
    h = _rms(x_ref[...], gmix_ref[...]).astype(BF16)
    w = dil_width
    o = 3 * w
    q_rank = gcq_ref.shape[1]
    kv_rank = gckv_ref.shape[1]
    qk_dim = MLA_NOPE + MLA_ROPE
    nh = n_mla_heads
    cos = cos_ref[...]
    sin = sin_ref[...]

    pm = _dot(h, win_ref[:, o:])
    cq = pm[:, :q_rank]
    ckv = pm[:, q_rank:q_rank + kv_rank]
    kpe = pm[:, q_rank + kv_rank:]

    qb = _dot(_rms(cq, gcq_ref[...]).astype(BF16), wqb_ref[...])
    for hd in range(nh):
        qn = qb[:, hd * MLA_NOPE:(hd + 1) * MLA_NOPE]
        qr = qb[:, nh * MLA_NOPE + hd * LANES:nh * MLA_NOPE + (hd + 1) * LANES]
        ss = jnp.sum(qn * qn, axis=-1, keepdims=True) + jnp.sum(qr * qr, axis=-1, keepdims=True)
        r = lax.rsqrt(ss * (1.0 / qk_dim) + EPS)
        qm_ref[:, hd * MXU_DIM:hd * MXU_DIM + LANES] = (qn * r * gqn_ref[...]).astype(BF16)
        qm_ref[:, hd * MXU_DIM + LANES:(hd + 1) * MXU_DIM] = _rope(qr * r * gqr_ref[...], cos, sin).astype(BF16)

    kvb = _dot(_rms(ckv, gckv_ref[...]).astype(BF16), wkvb_ref[...])
    ss_pe = jnp.sum(kpe * kpe, axis=-1, keepdims=True)
    kpe_rot = _rope(kpe * gkr_ref[...], cos, sin)
    for hd in range(nh):
        kn = kvb[:, hd * MLA_NOPE:(hd + 1) * MLA_NOPE]
        ss = jnp.sum(kn * kn, axis=-1, keepdims=True) + ss_pe
        r = lax.rsqrt(ss * (1.0 / qk_dim) + EPS)
        km_ref[:, hd * MXU_DIM:hd * MXU_DIM + LANES] = (kn * r * gkn_ref[...]).astype(BF16)
        km_ref[:, hd * MXU_DIM + LANES:(hd + 1) * MXU_DIM] = (kpe_rot * r).astype(BF16)
    vm_ref[...] = kvb[:, nh * MLA_NOPE:].astype(BF16)

    def to_residue_major(val, dst):
        dst[...] = _dot(perm_ref[...], val.astype(BF16)).reshape(dst.shape)

    for c, (g_ref, dst) in enumerate(((gqa_ref, qa_ref), (gka_ref, ka_ref))):
        src = _dot(h, win_ref[:, c * w:(c + 1) * w])
        ms = _dot((src * src).astype(BF16), eh_ref[...]) * (1.0 / DIL_HEAD_DIM)
        to_residue_major(src * lax.rsqrt(ms + EPS) * g_ref[...], dst)
    to_residue_major(_dot(h, win_ref[:, 2 * w:o]), va_ref)


def _proj_call(x2d, batch, seq, consts, cos, sin, *, n_mla_heads, dil_width):
    m, d = x2d.shape
    tm = ROW_TILE
    n_seq_blocks = seq // tm
    row = lambda width: pl.BlockSpec((tm, width), lambda i: (i, 0))
    pos = pl.BlockSpec((tm, LANES), lambda i: (i % n_seq_blocks, 0))
    mla_w = n_mla_heads * MXU_DIM
    dil_shape = jax.ShapeDtypeStruct((batch, DIL_RES, seq // DIL_RES, dil_width), F32)
    return pl.pallas_call(
        functools.partial(_proj_kernel, n_mla_heads=n_mla_heads, dil_width=dil_width),
        grid=(m // tm,),
        in_specs=[row(d)] + [_resident(c.shape) for c in consts] + [pos, pos],
        out_specs=[_residue_major_block(seq, dil_width)] * 3
        + [row(mla_w), row(mla_w), row(n_mla_heads * MLA_V_DIM)],
        out_shape=[dil_shape] * 3
        + [jax.ShapeDtypeStruct((m, mla_w), BF16)] * 2
        + [jax.ShapeDtypeStruct((m, n_mla_heads * MLA_V_DIM), BF16)],
        compiler_params=pltpu.CompilerParams(dimension_semantics=("parallel",),
                                             vmem_limit_bytes=VMEM_LIMIT_BYTES),
        name="proj",
    )(x2d, *consts, cos, sin)


def _t5_bucket(dist):
    max_exact = REL_BUCKETS // 2
    d = np.maximum(dist, 1).astype(np.float32)
    large = max_exact + (np.log(d / max_exact) / np.log(REL_MAX_DIST / max_exact)
                         * (REL_BUCKETS - max_exact)).astype(np.int32)
    large = np.minimum(large, REL_BUCKETS - 1)
    return np.where(dist < max_exact, dist, large).astype(np.int32)


def _band_tables():
    rho = np.arange(DIL_BAND)
    kap = np.arange(2 * DIL_BAND)
    buckets, prev = [], []
    for _, dil in DIL_BRANCHES:
        g = DIL_RES // dil
        run_q = DIL_BAND // g
        pos_q = g * (rho % run_q) + rho // run_q
        pos_k = g * (kap % (2 * run_q)) + kap // (2 * run_q) - DIL_BAND
        delta = pos_q[:, None] - pos_k[None, :]
        valid = (delta >= 0) & (delta <= DIL_BAND)
        buckets.append(np.where(valid, _t5_bucket(np.clip(delta, 0, None) * dil), -1))
        prev.append((pos_k < 0)[None, :])
    return np.stack(buckets).astype(np.int32), np.stack(prev).astype(np.int32)


def _dil_bias_kernel(rel_ref, bucket_ref, prev_ref, o_ref):
    bucket = bucket_ref[...]
    prev = prev_ref[...] > 0
    for hd in range(o_ref.shape[1]):
        bias = jnp.full(bucket.shape, NEG, F32)
        for b in range(REL_BUCKETS):
            bias = jnp.where(bucket == b, rel_ref[hd, b] * LOG2E, bias)
        o_ref[0, hd] = bias
        o_ref[1, hd] = jnp.where(prev, NEG, bias)


def _dil_bias_call(rel_bias, buckets, prev):
    nbr = buckets.shape[0]
    nh = rel_bias.shape[0]
    tile = buckets.shape[1:]
    return pl.pallas_call(
        _dil_bias_kernel,
        grid=(nbr,),
        in_specs=[pl.BlockSpec(memory_space=pltpu.SMEM),
                  pl.BlockSpec((None,) + tile, lambda b: (b, 0, 0)),
                  pl.BlockSpec((None, 1, tile[1]), lambda b: (b, 0, 0))],
        out_specs=pl.BlockSpec((None, 2, nh) + tile, lambda b: (b, 0, 0, 0, 0)),
        out_shape=jax.ShapeDtypeStruct((nbr, 2, nh) + tile, F32),
        name="dil_bias",
    )(rel_bias, buckets, prev)


def _band_tile(q, k, v, bias, head_a):
    zero = jnp.zeros_like(q)
    q2 = jnp.concatenate([jnp.where(head_a, q, zero), jnp.where(head_a, zero, q)], axis=0).astype(BF16)
    s = _dot_nt(q2, k.astype(BF16)) + bias
    m = jnp.max(s, axis=-1, keepdims=True)
    p = jnp.exp2(s - m).astype(BF16)
    ones = jnp.ones(v.shape, BF16)
    out = _dot(p, jnp.concatenate([v.astype(BF16), ones], axis=1))
    h = DIL_BAND
    num = jnp.where(head_a, out[:h, :LANES], out[h:, :LANES])
    den = jnp.where(head_a, out[:h, LANES:], out[h:, LANES:])
    mx = jnp.where(head_a, m[:h], m[h:])
    return num, den, mx


def _dil_kernel(q_ref, kp_ref, kc_ref, vp_ref, vc_ref, bias_ref, o_ref, num_scr, den_scr, max_scr):
    first_chunk = (pl.program_id(2) == 0).astype(jnp.int32)
    head_a = lax.broadcasted_iota(jnp.int32, (1, LANES), 1) < DIL_HEAD_DIM

    def tile(bi, dil, rbase, n):
        g = DIL_RES // dil
        run = DIL_BAND // g
        static_n = isinstance(n, int)
        q_rows = pl.ds(n * run, run) if static_n else pl.ds(pl.multiple_of(n * run, run), run)
        qs, ks, vs = [], [], []
        for u in range(g):
            res = rbase + dil * u
            qs.append(q_ref[res, q_rows, :])
            if static_n and n == 0:
                ks += [kp_ref[res, pl.ds(DIL_BAND - run, run), :], kc_ref[res, pl.ds(0, run), :]]
                vs += [vp_ref[res, pl.ds(DIL_BAND - run, run), :], vc_ref[res, pl.ds(0, run), :]]
            else:
                k_rows = (pl.ds((n - 1) * run, 2 * run) if static_n
                          else pl.ds(pl.multiple_of((n - 1) * run, run), 2 * run))
                ks.append(kc_ref[res, k_rows, :])
                vs.append(vc_ref[res, k_rows, :])
        cat = lambda parts: parts[0] if len(parts) == 1 else jnp.concatenate(parts, axis=0)
        variant = first_chunk if (static_n and n == 0) else 0
        num, den, mx = _band_tile(cat(qs), cat(ks), cat(vs), bias_ref[bi, variant], head_a)
        for u in range(g):
            res = rbase + dil * u
            part = slice(u * run, (u + 1) * run)
            num_scr[bi, res, q_rows, :] = num[part]
            den_scr[bi, res, q_rows, :] = den[part]
            max_scr[bi, res, q_rows, :] = mx[part]

    for n in range(DIL_RES):
        tile(0, 1, 0, n)
    for r4 in range(4):
        for n in range(4):
            tile(1, 4, r4, n)
    for r in range(DIL_RES):
        tile(2, 16, r, 0)

    nbr = len(DIL_BRANCHES)

    def merge(res, carry):
        ms = [max_scr[bi, res] for bi in range(nbr)]
        m_all = functools.reduce(jnp.maximum, ms)
        es = [jnp.exp2(mb - m_all) for mb in ms]
        num = sum(es[bi] * num_scr[bi, res] for bi in range(nbr))
        den = sum(es[bi] * den_scr[bi, res] for bi in range(nbr))
        o_ref[res] = num / den
        return carry

    lax.fori_loop(0, DIL_RES, merge, 0)


def _dil_call(qa, ka, va, bias):
    b, _, per, w = qa.shape
    pairs = w // LANES
    nbr = bias.shape[0]
    blk = (None, DIL_RES, DIL_BAND, LANES)
    cur = pl.BlockSpec(blk, lambda bb, p, c: (bb, 0, c, p))
    prev = pl.BlockSpec(blk, lambda bb, p, c: (bb, 0, jnp.maximum(c - 1, 0), p))
    scr = pltpu.VMEM((nbr, DIL_RES, DIL_BAND, LANES), F32)
    return pl.pallas_call(
        _dil_kernel,
        grid=(b, pairs, per // DIL_BAND),
        in_specs=[cur, prev, cur, prev, cur,
                  pl.BlockSpec((nbr, 2, None, 2 * DIL_BAND, 2 * DIL_BAND), lambda bb, p, c: (0, 0, p, 0, 0))],
        out_specs=cur,
        out_shape=jax.ShapeDtypeStruct(qa.shape, F32),
        scratch_shapes=[scr, scr, scr],
        compiler_params=pltpu.CompilerParams(dimension_semantics=("parallel", "parallel", "arbitrary"),
                                             vmem_limit_bytes=VMEM_LIMIT_BYTES),
        name="dil_attn",
    )(qa, ka, ka, va, va, bias)


def _mla_kernel(q_ref, k_ref, v_ref, o_ref, m_ref, acc_ref, s0_ref, s1_ref):
    tq = q_ref.shape[0]
    tk = MLA_TK
    qi = pl.program_id(2)
    m_ref[...] = jnp.full(m_ref.shape, NEG, F32)
    acc_ref[...] = jnp.zeros(acc_ref.shape, F32)
    ones = jnp.ones((tk, LANES), BF16)

    def scores(kblk, row0=0):
        k0 = pl.multiple_of(kblk * tk, tk)
        return _dot_nt(q_ref[row0:, :], k_ref[pl.ds(k0, tk), :])

    def update(s, kblk, row0, diagonal):
        k0 = pl.multiple_of(kblk * tk, tk)
        rows = slice(row0, tq)
        if diagonal:
            row = lax.broadcasted_iota(jnp.int32, s.shape, 0)
            col = lax.broadcasted_iota(jnp.int32, s.shape, 1)
            s = jnp.where(col <= row, s, NEG)
        m_old = m_ref[rows, :]
        m_new = jnp.maximum(m_old, jnp.max(s, axis=-1, keepdims=True))
        m_ref[rows, :] = m_new
        p = jnp.concatenate([jnp.exp2(s[:, c * LANES:(c + 1) * LANES] - m_new) for c in range(tk // LANES)],
                            axis=1).astype(BF16)
        alpha = jnp.exp2(m_old - m_new)
        vext = jnp.concatenate([v_ref[pl.ds(k0, tk), :], ones], axis=1)
        acc_ref[rows, :] = acc_ref[rows, :] * jnp.concatenate([alpha, alpha], axis=1) + _dot(p, vext)

    s0_ref[...] = scores(0)

    def body(t, carry):
        s1_ref[...] = scores(2 * t + 1)
        update(s0_ref[...], 2 * t, 0, diagonal=False)
        s0_ref[...] = scores(2 * t + 2)
        update(s1_ref[...], 2 * t + 1, 0, diagonal=False)
        return carry

    lax.fori_loop(0, qi, body, 0)
    s_last = scores(2 * qi + 1, row0=tk)
    update(s0_ref[...], 2 * qi, 0, diagonal=True)
    update(s_last, 2 * qi + 1, tk, diagonal=True)
    acc = acc_ref[...]
    o_ref[...] = acc[:, :MLA_V_DIM] / acc[:, MLA_V_DIM:]


def _mla_call(qm, km, vm, n_heads):
    b, s, _ = qm.shape
    tq = MLA_TQ
    assert tq == 2 * MLA_TK
    return pl.pallas_call(
        _mla_kernel,
        grid=(b, n_heads, s // tq),
        in_specs=[pl.BlockSpec((None, tq, MXU_DIM), lambda bb, h, i: (bb, i, h)),
                  pl.BlockSpec((None, s, MXU_DIM), lambda bb, h, i: (bb, 0, h)),
                  pl.BlockSpec((None, s, MLA_V_DIM), lambda bb, h, i: (bb, 0, h))],
        out_specs=pl.BlockSpec((None, tq, MLA_V_DIM), lambda bb, h, i: (bb, i, h)),
        out_shape=jax.ShapeDtypeStruct((b, s, n_heads * MLA_V_DIM), F32),
        scratch_shapes=[pltpu.VMEM((tq, LANES), F32), pltpu.VMEM((tq, 2 * MLA_V_DIM), F32),
                        pltpu.VMEM((tq, MLA_TK), F32), pltpu.VMEM((tq, MLA_TK), F32)],
        compiler_params=pltpu.CompilerParams(dimension_semantics=("parallel", "parallel", "arbitrary"),
                                             vmem_limit_bytes=VMEM_LIMIT_BYTES),
        name="mla_attn",
    )(qm, km, vm)


def _rope_tables(seq):
    inv_freq = ROPE_BASE ** (-np.arange(0, MLA_ROPE, 2, dtype=np.float64) / MLA_ROPE)
    ang = np.arange(seq, dtype=np.float64)[:, None] * inv_freq[None, :]
    cos, sin = np.cos(ang), np.sin(ang)
    cos_t = np.concatenate([cos, cos] * (LANES // MLA_ROPE), axis=1)
    sin_t = np.concatenate([-sin, sin] * (LANES // MLA_ROPE), axis=1)
    return jnp.asarray(cos_t, F32), jnp.asarray(sin_t, F32)


def _pad_lanes(v, width):
    return jnp.pad(v, ((0, 0), (0, width - v.shape[1])))


def kernel(x, ffn1_norm, ffn1_w_gate, ffn1_w_up, ffn1_w_down, mix_norm, w_in, dil_q_norm, dil_k_norm,
           rel_bias, mla_q_a_norm, mla_w_q_b, mla_kv_a_norm, mla_w_kv_b, mla_q_norm, mla_k_norm,
           out_norm_dil, out_norm_mla, w_out, ffn2_norm, ffn2_w_gate, ffn2_w_up, ffn2_w_down):
    batch, seq, d_model = x.shape
    depth = w_in.shape[0]
    dil_width = out_norm_dil.shape[1]
    dil_heads = dil_width // DIL_HEAD_DIM
    q_rank = mla_q_a_norm.shape[1]
    kv_rank = mla_kv_a_norm.shape[1]
    qk_dim = MLA_NOPE + MLA_ROPE
    n_mla = mla_w_q_b.shape[2] // qk_dim
    assert rel_bias.shape == (dil_heads, REL_BUCKETS)
    assert w_in.shape[2] == 3 * dil_width + q_rank + kv_rank + MLA_ROPE
    assert seq % DIL_CHUNK == 0 and seq % MLA_TQ == 0 and seq % ROW_TILE == 0
    assert all(win // dil == DIL_BAND and DIL_RES % dil == 0 for win, dil in DIL_BRANCHES)

    cos_t, sin_t = _rope_tables(seq)
    buckets, prev = _band_tables()
    dil_bias = _dil_bias_call(rel_bias, jnp.asarray(buckets), jnp.asarray(prev))
    dil_bias = dil_bias.reshape(dil_bias.shape[0], 2, dil_heads // 2, 2 * DIL_BAND, 2 * DIL_BAND)
    eh = jnp.asarray(np.kron(np.eye(dil_heads), np.ones((DIL_HEAD_DIM, DIL_HEAD_DIM))), BF16)
    perm_np = _residue_major_perm(ROW_TILE)
    perm, unperm = jnp.asarray(perm_np, BF16), jnp.asarray(perm_np.T, BF16)

    x2d = x.reshape(batch * seq, d_model)
    row = lambda v: v.reshape(1, -1)
    for l in range(depth):
        x2d = _ffn_call(x2d, row(ffn1_norm[l]), ffn1_w_gate[l].astype(BF16), ffn1_w_up[l].astype(BF16),
                        ffn1_w_down[l].astype(BF16))

        win = _pad_lanes(w_in[l], w_in.shape[2] + LANES - MLA_ROPE).astype(BF16)
        wqb = mla_w_q_b[l].reshape(q_rank, n_mla, qk_dim)
        wqb = jnp.concatenate(
            [wqb[:, :, :MLA_NOPE].reshape(q_rank, n_mla * MLA_NOPE),
             jnp.pad(wqb[:, :, MLA_NOPE:], ((0, 0), (0, 0), (0, LANES - MLA_ROPE))).reshape(q_rank, n_mla * LANES)],
            axis=1).astype(BF16)
        wkvb = mla_w_kv_b[l].reshape(kv_rank, n_mla, MLA_NOPE + MLA_V_DIM)
        wkvb = jnp.concatenate([wkvb[:, :, :MLA_NOPE].reshape(kv_rank, n_mla * MLA_NOPE),
                                wkvb[:, :, MLA_NOPE:].reshape(kv_rank, n_mla * MLA_V_DIM)], axis=1).astype(BF16)
        gqa = row(jnp.tile(dil_q_norm[l], dil_heads)) * (LOG2E * DIL_HEAD_DIM ** -0.5)
        gka = row(jnp.tile(dil_k_norm[l], dil_heads))
        gq = row(mla_q_norm[l]) * (LOG2E * qk_dim ** -0.5)
        gk = row(mla_k_norm[l])
        gqn, gqr = gq[:, :MLA_NOPE], _pad_lanes(gq[:, MLA_NOPE:], LANES)
        gkn, gkr = gk[:, :MLA_NOPE], _pad_lanes(gk[:, MLA_NOPE:], LANES)

        consts = [row(mix_norm[l]), win, eh, perm, gqa, gka, row(mla_q_a_norm[l]), wqb,
                  row(mla_kv_a_norm[l]), wkvb, gqn, gqr, gkn, gkr]
        qa, ka, va, qm, km, vm = _proj_call(x2d, batch, seq, consts, cos_t, sin_t,
                                            n_mla_heads=n_mla, dil_width=dil_width)

        shp = lambda a: a.reshape(batch, seq, a.shape[1])
        o_dil = _dil_call(qa, ka, va, dil_bias)
        o_mla = _mla_call(shp(qm), shp(km), shp(vm), n_mla)

        x2d = _out_ffn_call(x2d, seq, o_dil, o_mla.reshape(batch * seq, -1), unperm,
                            row(out_norm_dil[l]), row(out_norm_mla[l]), w_out[l].astype(BF16),
                            row(ffn2_norm[l]), ffn2_w_gate[l].astype(BF16), ffn2_w_up[l].astype(BF16),
                            ffn2_w_down[l].astype(BF16))
    return x2d.reshape(batch, seq, d_model)
```

```python
import functools

import numpy as np
import jax
import jax.numpy as jnp
from jax import lax
from jax.experimental import pallas as pl
from jax.experimental.pallas import tpu as pltpu

F32 = jnp.float32
BF16 = jnp.bfloat16

DIL_HEAD_DIM = 64
DIL_BRANCHES = ((128, 1), (512, 4), (2048, 16))
DIL_BAND = 128
DIL_RES = 16
MLA_NOPE = 128
MLA_ROPE = 64
MLA_V_DIM = 128
ROPE_BASE = 10000.0
REL_BUCKETS = 32
REL_MAX_DIST = 2048
FFN_RESID = 0.5
EPS = 1e-6

LANES = 128
SUBLANES = 8
MXU_DIM = 256
VMEM_LIMIT_BYTES = 56 * 1024 * 1024

NEG = -1e30
LOG2E = 1.4426950408889634

ROW_TILE = 512
FFN_TF = 256
DIL_CHUNK = DIL_BAND * DIL_RES
MLA_TQ = 1024
MLA_TK = 512


def _dot(a, b):
    return jnp.dot(a, b, preferred_element_type=F32)


def _dot_nt(a, b):
    return lax.dot_general(a, b, (((1,), (1,)), ((), ())), preferred_element_type=F32)


def _rms(x, g):
    return x * lax.rsqrt(jnp.mean(x * x, axis=-1, keepdims=True) + EPS) * g


def _residue_major_perm(rows):
    per = rows // DIL_RES
    dst = np.arange(rows)
    src = DIL_RES * (dst % per) + dst // per
    p = np.zeros((rows, rows), np.float32)
    p[dst, src] = 1.0
    return p


def _swiglu_residual(x, g_ref, wg_ref, wu_ref, wd_ref, act_ref):
    h = _rms(x, g_ref[...]).astype(BF16)
    d_ff = wg_ref.shape[1]
    for c in range(d_ff // FFN_TF):
        sl = slice(c * FFN_TF, (c + 1) * FFN_TF)
        gate = _dot(h, wg_ref[:, sl].astype(BF16))
        up = _dot(h, wu_ref[:, sl].astype(BF16))
        act_ref[:, sl] = (gate * jax.nn.sigmoid(gate) * up).astype(BF16)
    return x + FFN_RESID * _dot(act_ref[...], wd_ref[...].astype(BF16))


def _ffn_kernel(x_ref, g_ref, wg_ref, wu_ref, wd_ref, o_ref, act_ref):
    o_ref[...] = _swiglu_residual(x_ref[...], g_ref, wg_ref, wu_ref, wd_ref, act_ref)


def _out_ffn_kernel(x_ref, od_ref, om_ref, unperm_ref, gd_ref, gm_ref, wo_ref, g_ref, wg_ref, wu_ref, wd_ref,
                    o_ref, act_ref):
    tm = x_ref.shape[0]
    wd_rows = od_ref.shape[-1]
    od = _rms(od_ref[...].reshape(tm, wd_rows), gd_ref[...]).astype(BF16)
    od = _dot(unperm_ref[...], od).astype(BF16)
    om = _rms(om_ref[...], gm_ref[...]).astype(BF16)
    x2 = x_ref[...] + _dot(od, wo_ref[:wd_rows, :]) + _dot(om, wo_ref[wd_rows:, :])
    o_ref[...] = _swiglu_residual(x2, g_ref, wg_ref, wu_ref, wd_ref, act_ref)


def _resident(shape):
    return pl.BlockSpec(shape, lambda *_: (0,) * len(shape), pipeline_mode=pl.Buffered(1))


def _residue_major_block(seq, width):
    per = ROW_TILE // DIL_RES
    blocks_per_seq = seq // ROW_TILE
    return pl.BlockSpec((None, DIL_RES, per, width), lambda i: (i // blocks_per_seq, 0, i % blocks_per_seq, 0))


def _ffn_call(x2d, g, wg, wu, wd):
    m, d = x2d.shape
    f = wg.shape[1]
    row = pl.BlockSpec((ROW_TILE, d), lambda i: (i, 0))
    return pl.pallas_call(
        _ffn_kernel,
        grid=(m // ROW_TILE,),
        in_specs=[row, _resident((1, d)), _resident((d, f)), _resident((d, f)), _resident((f, d))],
        out_specs=row,
        out_shape=jax.ShapeDtypeStruct((m, d), F32),
        scratch_shapes=[pltpu.VMEM((ROW_TILE, f), BF16)],
        compiler_params=pltpu.CompilerParams(dimension_semantics=("parallel",),
                                             vmem_limit_bytes=VMEM_LIMIT_BYTES),
        name="ffn",
    )(x2d, g, wg, wu, wd)


def _out_ffn_call(x2d, seq, o_dil, o_mla, unperm, gd, gm, wo, g, wg, wu, wd):
    m, d = x2d.shape
    f = wg.shape[1]
    wdil, wmla = o_dil.shape[-1], o_mla.shape[-1]
    row = pl.BlockSpec((ROW_TILE, d), lambda i: (i, 0))
    return pl.pallas_call(
        _out_ffn_kernel,
        grid=(m // ROW_TILE,),
        in_specs=[row,
                  _residue_major_block(seq, wdil),
                  pl.BlockSpec((ROW_TILE, wmla), lambda i: (i, 0)),
                  _resident(unperm.shape), _resident((1, wdil)), _resident((1, wmla)),
                  _resident((wdil + wmla, d)),
                  _resident((1, d)), _resident((d, f)), _resident((d, f)), _resident((f, d))],
        out_specs=row,
        out_shape=jax.ShapeDtypeStruct((m, d), F32),
        scratch_shapes=[pltpu.VMEM((ROW_TILE, f), BF16)],
        compiler_params=pltpu.CompilerParams(dimension_semantics=("parallel",),
                                             vmem_limit_bytes=VMEM_LIMIT_BYTES),
        name="out_ffn",
    )(x2d, o_dil, o_mla, unperm, gd, gm, wo, g, wg, wu, wd)


def _rope(x, cos, sin_signed):
    lane = lax.broadcasted_iota(jnp.int32, x.shape, 1)
    first_half = (lane % MLA_ROPE) < (MLA_ROPE // 2)
    partner = jnp.where(first_half,
                        pltpu.roll(x, LANES - MLA_ROPE // 2, 1),
                        pltpu.roll(x, MLA_ROPE // 2, 1))
    return x * cos + partner * sin_signed


def _proj_kernel(x_ref, gmix_ref, win_ref, eh_ref, perm_ref, gqa_ref, gka_ref,
                 gcq_ref, wqb_ref, gckv_ref, wkvb_ref,
                 gqn_ref, gqr_ref, gkn_ref, gkr_ref, cos_ref, sin_ref,
                 qa_ref, ka_ref, va_ref, qm_ref, km_ref, vm_ref, *, n_mla_heads, dil_width):
    h = _rms(x_ref[...], gmix_ref[...]).astype(BF16)
    w = dil_width
    o = 3 * w
    q_rank = gcq_ref.shape[1]
    kv_rank = gckv_ref.shape[1]
    qk_dim = MLA_NOPE + MLA_ROPE
    nh = n_mla_heads
    cos = cos_ref[...]
    sin = sin_ref[...]

    pm = _dot(h, win_ref[:, o:])
    cq = pm[:, :q_rank]
    ckv = pm[:, q_rank:q_rank + kv_rank]
    kpe = pm[:, q_rank + kv_rank:]

    qb = _dot(_rms(cq, gcq_ref[...]).astype(BF16), wqb_ref[...])
    for hd in range(nh):
        qn = qb[:, hd * MLA_NOPE:(hd + 1) * MLA_NOPE]
        qr = qb[:, nh * MLA_NOPE + hd * LANES:nh * MLA_NOPE + (hd + 1) * LANES]
        ss = jnp.sum(qn * qn, axis=-1, keepdims=True) + jnp.sum(qr * qr, axis=-1, keepdims=True)
        r = lax.rsqrt(ss * (1.0 / qk_dim) + EPS)
        qm_ref[:, hd * MXU_DIM:hd * MXU_DIM + LANES] = (qn * r * gqn_ref[...]).astype(BF16)
        qm_ref[:, hd * MXU_DIM + LANES:(hd + 1) * MXU_DIM] = _rope(qr * r * gqr_ref[...], cos, sin).astype(BF16)

    kvb = _dot(_rms(ckv, gckv_ref[...]).astype(BF16), wkvb_ref[...])
    ss_pe = jnp.sum(kpe * kpe, axis=-1, keepdims=True)
    kpe_rot = _rope(kpe * gkr_ref[...], cos, sin)
    for hd in range(nh):
        kn = kvb[:, hd * MLA_NOPE:(hd + 1) * MLA_NOPE]
        ss = jnp.sum(kn * kn, axis=-1, keepdims=True) + ss_pe
        r = lax.rsqrt(ss * (1.0 / qk_dim) + EPS)
        km_ref[:, hd * MXU_DIM:hd * MXU_DIM + LANES] = (kn * r * gkn_ref[...]).astype(BF16)
        km_ref[:, hd * MXU_DIM + LANES:(hd + 1) * MXU_DIM] = (kpe_rot * r).astype(BF16)
    vm_ref[...] = kvb[:, nh * MLA_NOPE:].astype(BF16)

    def to_residue_major(val, dst):
        dst[...] = _dot(perm_ref[...], val.astype(BF16)).reshape(dst.shape)

    for c, (g_ref, dst) in enumerate(((gqa_ref, qa_ref), (gka_ref, ka_ref))):
        src = _dot(h, win_ref[:, c * w:(c + 1) * w])
        ms = _dot((src * src).astype(BF16), eh_ref[...]) * (1.0 / DIL_HEAD_DIM)
        to_residue_major(src * lax.rsqrt(ms + EPS) * g_ref[...], dst)
    to_residue_major(_dot(h, win_ref[:, 2 * w:o]), va_ref)


def _proj_call(x2d, batch, seq, consts, cos, sin, *, n_mla_heads, dil_width):
    m, d = x2d.shape
    tm = ROW_TILE
    n_seq_blocks = seq // tm
    row = lambda width: pl.BlockSpec((tm, width), lambda i: (i, 0))
    pos = pl.BlockSpec((tm, LANES), lambda i: (i % n_seq_blocks, 0))
    mla_w = n_mla_heads * MXU_DIM
    dil_shape = jax.ShapeDtypeStruct((batch, DIL_RES, seq // DIL_RES, dil_width), F32)
    return pl.pallas_call(
        functools.partial(_proj_kernel, n_mla_heads=n_mla_heads, dil_width=dil_width),
        grid=(m // tm,),
        in_specs=[row(d)] + [_resident(c.shape) for c in consts] + [pos, pos],
        out_specs=[_residue_major_block(seq, dil_width)] * 3
        + [row(mla_w), row(mla_w), row(n_mla_heads * MLA_V_DIM)],
        out_shape=[dil_shape] * 3
        + [jax.ShapeDtypeStruct((m, mla_w), BF16)] * 2
        + [jax.ShapeDtypeStruct((m, n_mla_heads * MLA_V_DIM), BF16)],
        compiler_params=pltpu.CompilerParams(dimension_semantics=("parallel",),
                                             vmem_limit_bytes=VMEM_LIMIT_BYTES),
        name="proj",
    )(x2d, *consts, cos, sin)


def _t5_bucket(dist):
    max_exact = REL_BUCKETS // 2
    d = np.maximum(dist, 1).astype(np.float32)
    large = max_exact + (np.log(d / max_exact) / np.log(REL_MAX_DIST / max_exact)
                         * (REL_BUCKETS - max_exact)).astype(np.int32)
    large = np.minimum(large, REL_BUCKETS - 1)
    return np.where(dist < max_exact, dist, large).astype(np.int32)


def _band_tables():
    rho = np.arange(DIL_BAND)
    kap = np.arange(2 * DIL_BAND)
    buckets, prev = [], []
    for _, dil in DIL_BRANCHES:
        g = DIL_RES // dil
        run_q = DIL_BAND // g
        pos_q = g * (rho % run_q) + rho // run_q
        pos_k = g * (kap % (2 * run_q)) + kap // (2 * run_q) - DIL_BAND
        delta = pos_q[:, None] - pos_k[None, :]
        valid = (delta >= 0) & (delta <= DIL_BAND)
        buckets.append(np.where(valid, _t5_bucket(np.clip(delta, 0, None) * dil), -1))
        prev.append((pos_k < 0)[None, :])
    return np.stack(buckets).astype(np.int32), np.stack(prev).astype(np.int32)


def _dil_bias_kernel(rel_ref, bucket_ref, prev_ref, o_ref):
    bucket = bucket_ref[...]
    prev = prev_ref[...] > 0
    for hd in range(o_ref.shape[1]):
        bias = jnp.full(bucket.shape, NEG, F32)
        for b in range(REL_BUCKETS):
            bias = jnp.where(bucket == b, rel_ref[hd, b] * LOG2E, bias)
        o_ref[0, hd] = bias
        o_ref[1, hd] = jnp.where(prev, NEG, bias)


def _dil_bias_call(rel_bias, buckets, prev):
    nbr = buckets.shape[0]
    nh = rel_bias.shape[0]
    tile = buckets.shape[1:]
    return pl.pallas_call(
        _dil_bias_kernel,
        grid=(nbr,),
        in_specs=[pl.BlockSpec(memory_space=pltpu.SMEM),
                  pl.BlockSpec((None,) + tile, lambda b: (b, 0, 0)),
                  pl.BlockSpec((None, 1, tile[1]), lambda b: (b, 0, 0))],
        out_specs=pl.BlockSpec((None, 2, nh) + tile, lambda b: (b, 0, 0, 0, 0)),
        out_shape=jax.ShapeDtypeStruct((nbr, 2, nh) + tile, F32),
        name="dil_bias",
    )(rel_bias, buckets, prev)


def _band_tile(q, k, v, bias, head_a):
    zero = jnp.zeros_like(q)
    q2 = jnp.concatenate([jnp.where(head_a, q, zero), jnp.where(head_a, zero, q)], axis=0).astype(BF16)
    s = _dot_nt(q2, k.astype(BF16)) + bias
    m = jnp.max(s, axis=-1, keepdims=True)
    p = jnp.exp2(s - m).astype(BF16)
    ones = jnp.ones(v.shape, BF16)
    out = _dot(p, jnp.concatenate([v.astype(BF16), ones], axis=1))
    h = DIL_BAND
    num = jnp.where(head_a, out[:h, :LANES], out[h:, :LANES])
    den = jnp.where(head_a, out[:h, LANES:], out[h:, LANES:])
    mx = jnp.where(head_a, m[:h], m[h:])
    return num, den, mx


def _dil_kernel(q_ref, kp_ref, kc_ref, vp_ref, vc_ref, bias_ref, o_ref, num_scr, den_scr, max_scr):
    first_chunk = (pl.program_id(2) == 0).astype(jnp.int32)
    head_a = lax.broadcasted_iota(jnp.int32, (1, LANES), 1) < DIL_HEAD_DIM

    def tile(bi, dil, rbase, n):
        g = DIL_RES // dil
        run = DIL_BAND // g
        static_n = isinstance(n, int)
        q_rows = pl.ds(n * run, run) if static_n else pl.ds(pl.multiple_of(n * run, run), run)
        qs, ks, vs = [], [], []
        for u in range(g):
            res = rbase + dil * u
            qs.append(q_ref[res, q_rows, :])
            if static_n and n == 0:
                ks += [kp_ref[res, pl.ds(DIL_BAND - run, run), :], kc_ref[res, pl.ds(0, run), :]]
                vs += [vp_ref[res, pl.ds(DIL_BAND - run, run), :], vc_ref[res, pl.ds(0, run), :]]
            else:
                k_rows = (pl.ds((n - 1) * run, 2 * run) if static_n
                          else pl.ds(pl.multiple_of((n - 1) * run, run), 2 * run))
                ks.append(kc_ref[res, k_rows, :])
                vs.append(vc_ref[res, k_rows, :])
        cat = lambda parts: parts[0] if len(parts) == 1 else jnp.concatenate(parts, axis=0)
        variant = first_chunk if (static_n and n == 0) else 0
        num, den, mx = _band_tile(cat(qs), cat(ks), cat(vs), bias_ref[bi, variant], head_a)
        for u in range(g):
            res = rbase + dil * u
            part = slice(u * run, (u + 1) * run)
            num_scr[bi, res, q_rows, :] = num[part]
            den_scr[bi, res, q_rows, :] = den[part]
            max_scr[bi, res, q_rows, :] = mx[part]

    for n in range(DIL_RES):
        tile(0, 1, 0, n)
    for r4 in range(4):
        for n in range(4):
            tile(1, 4, r4, n)
    for r in range(DIL_RES):
        tile(2, 16, r, 0)

    nbr = len(DIL_BRANCHES)

    def merge(res, carry):
        ms = [max_scr[bi, res] for bi in range(nbr)]
        m_all = functools.reduce(jnp.maximum, ms)
        es = [jnp.exp2(mb - m_all) for mb in ms]
        num = sum(es[bi] * num_scr[bi, res] for bi in range(nbr))
        den = sum(es[bi] * den_scr[bi, res] for bi in range(nbr))
        o_ref[res] = num / den
        return carry

    lax.fori_loop(0, DIL_RES, merge, 0)


def _dil_call(qa, ka, va, bias):
    b, _, per, w = qa.shape
    pairs = w // LANES
    nbr = bias.shape[0]
    blk = (None, DIL_RES, DIL_BAND, LANES)
    cur = pl.BlockSpec(blk, lambda bb, p, c: (bb, 0, c, p))
    prev = pl.BlockSpec(blk, lambda bb, p, c: (bb, 0, jnp.maximum(c - 1, 0), p))
    scr = pltpu.VMEM((nbr, DIL_RES, DIL_BAND, LANES), F32)
    return pl.pallas_call(
        _dil_kernel,
        grid=(b, pairs, per // DIL_BAND),
        in_specs=[cur, prev, cur, prev, cur,
                  pl.BlockSpec((nbr, 2, None, 2 * DIL_BAND, 2 * DIL_BAND), lambda bb, p, c: (0, 0, p, 0, 0))],
        out_specs=cur,
        out_shape=jax.ShapeDtypeStruct(qa.shape, F32),
        scratch_shapes=[scr, scr, scr],
        compiler_params=pltpu.CompilerParams(dimension_semantics=("parallel", "parallel", "arbitrary"),
                                             vmem_limit_bytes=VMEM_LIMIT_BYTES),
        name="dil_attn",
    )(qa, ka, ka, va, va, bias)


def _mla_kernel(q_ref, k_ref, v_ref, o_ref, m_ref, acc_ref, s0_ref, s1_ref):
    tq = q_ref.shape[0]
    tk = MLA_TK
    qi = pl.program_id(2)
    m_ref[...] = jnp.full(m_ref.shape, NEG, F32)
    acc_ref[...] = jnp.zeros(acc_ref.shape, F32)
    ones = jnp.ones((tk, LANES), BF16)

    def scores(kblk, row0=0):
        k0 = pl.multiple_of(kblk * tk, tk)
        return _dot_nt(q_ref[row0:, :], k_ref[pl.ds(k0, tk), :])

    def update(s, kblk, row0, diagonal):
        k0 = pl.multiple_of(kblk * tk, tk)
        rows = slice(row0, tq)
        if diagonal:
            row = lax.broadcasted_iota(jnp.int32, s.shape, 0)
            col = lax.broadcasted_iota(jnp.int32, s.shape, 1)
            s = jnp.where(col <= row, s, NEG)
        m_old = m_ref[rows, :]
        m_new = jnp.maximum(m_old, jnp.max(s, axis=-1, keepdims=True))
        m_ref[rows, :] = m_new
        p = jnp.concatenate([jnp.exp2(s[:, c * LANES:(c + 1) * LANES] - m_new) for c in range(tk // LANES)],
                            axis=1).astype(BF16)
        alpha = jnp.exp2(m_old - m_new)
        vext = jnp.concatenate([v_ref[pl.ds(k0, tk), :], ones], axis=1)
        acc_ref[rows, :] = acc_ref[rows, :] * jnp.concatenate([alpha, alpha], axis=1) + _dot(p, vext)

    s0_ref[...] = scores(0)

    def body(t, carry):
        s1_ref[...] = scores(2 * t + 1)
        update(s0_ref[...], 2 * t, 0, diagonal=False)
        s0_ref[...] = scores(2 * t + 2)
        update(s1_ref[...], 2 * t + 1, 0, diagonal=False)
        return carry

    lax.fori_loop(0, qi, body, 0)
    s_last = scores(2 * qi + 1, row0=tk)
    update(s0_ref[...], 2 * qi, 0, diagonal=True)
    update(s_last, 2 * qi + 1, tk, diagonal=True)
    acc = acc_ref[...]
    o_ref[...] = acc[:, :MLA_V_DIM] / acc[:, MLA_V_DIM:]


def _mla_call(qm, km, vm, n_heads):
    b, s, _ = qm.shape
    tq = MLA_TQ
    assert tq == 2 * MLA_TK
    return pl.pallas_call(
        _mla_kernel,
        grid=(b, n_heads, s // tq),
        in_specs=[pl.BlockSpec((None, tq, MXU_DIM), lambda bb, h, i: (bb, i, h)),
                  pl.BlockSpec((None, s, MXU_DIM), lambda bb, h, i: (bb, 0, h)),
                  pl.BlockSpec((None, s, MLA_V_DIM), lambda bb, h, i: (bb, 0, h))],
        out_specs=pl.BlockSpec((None, tq, MLA_V_DIM), lambda bb, h, i: (bb, i, h)),
        out_shape=jax.ShapeDtypeStruct((b, s, n_heads * MLA_V_DIM), F32),
        scratch_shapes=[pltpu.VMEM((tq, LANES), F32), pltpu.VMEM((tq, 2 * MLA_V_DIM), F32),
                        pltpu.VMEM((tq, MLA_TK), F32), pltpu.VMEM((tq, MLA_TK), F32)],
        compiler_params=pltpu.CompilerParams(dimension_semantics=("parallel", "parallel", "arbitrary"),
                                             vmem_limit_bytes=VMEM_LIMIT_BYTES),
        name="mla_attn",
    )(qm, km, vm)


def _rope_tables(seq):
    inv_freq = ROPE_BASE ** (-np.arange(0, MLA_ROPE, 2, dtype=np.float64) / MLA_ROPE)
    ang = np.arange(seq, dtype=np.float64)[:, None] * inv_freq[None, :]
    cos, sin = np.cos(ang), np.sin(ang)
    cos_t = np.concatenate([cos, cos] * (LANES // MLA_ROPE), axis=1)
    sin_t = np.concatenate([-sin, sin] * (LANES // MLA_ROPE), axis=1)
    return jnp.asarray(cos_t, F32), jnp.asarray(sin_t, F32)


def _pad_lanes(v, width):
    return jnp.pad(v, ((0, 0), (0, width - v.shape[1])))


def kernel(x, ffn1_norm, ffn1_w_gate, ffn1_w_up, ffn1_w_down, mix_norm, w_in, dil_q_norm, dil_k_norm,
           rel_bias, mla_q_a_norm, mla_w_q_b, mla_kv_a_norm, mla_w_kv_b, mla_q_norm, mla_k_norm,
           out_norm_dil, out_norm_mla, w_out, ffn2_norm, ffn2_w_gate, ffn2_w_up, ffn2_w_down):
    batch, seq, d_model = x.shape
    depth = w_in.shape[0]
    dil_width = out_norm_dil.shape[1]
    dil_heads = dil_width // DIL_HEAD_DIM
    q_rank = mla_q_a_norm.shape[1]
    kv_rank = mla_kv_a_norm.shape[1]
    qk_dim = MLA_NOPE + MLA_ROPE
    n_mla = mla_w_q_b.shape[2] // qk_dim
    assert rel_bias.shape == (dil_heads, REL_BUCKETS)
    assert w_in.shape[2] == 3 * dil_width + q_rank + kv_rank + MLA_ROPE
    assert seq % DIL_CHUNK == 0 and seq % MLA_TQ == 0 and seq % ROW_TILE == 0
    assert all(win // dil == DIL_BAND and DIL_RES % dil == 0 for win, dil in DIL_BRANCHES)

    cos_t, sin_t = _rope_tables(seq)
    buckets, prev = _band_tables()
    dil_bias = _dil_bias_call(rel_bias, jnp.asarray(buckets), jnp.asarray(prev))
    dil_bias = dil_bias.reshape(dil_bias.shape[0], 2, dil_heads // 2, 2 * DIL_BAND, 2 * DIL_BAND)
    eh = jnp.asarray(np.kron(np.eye(dil_heads), np.ones((DIL_HEAD_DIM, DIL_HEAD_DIM))), BF16)
    perm_np = _residue_major_perm(ROW_TILE)
    perm, unperm = jnp.asarray(perm_np, BF16), jnp.asarray(perm_np.T, BF16)

    x2d = x.reshape(batch * seq, d_model)
    row = lambda v: v.reshape(1, -1)
    for l in range(depth):
        x2d = _ffn_call(x2d, row(ffn1_norm[l]), ffn1_w_gate[l], ffn1_w_up[l], ffn1_w_down[l])

        win = _pad_lanes(w_in[l], w_in.shape[2] + LANES - MLA_ROPE).astype(BF16)
        wqb = mla_w_q_b[l].reshape(q_rank, n_mla, qk_dim)
        wqb = jnp.concatenate(
            [wqb[:, :, :MLA_NOPE].reshape(q_rank, n_mla * MLA_NOPE),
             jnp.pad(wqb[:, :, MLA_NOPE:], ((0, 0), (0, 0), (0, LANES - MLA_ROPE))).reshape(q_rank, n_mla * LANES)],
            axis=1).astype(BF16)
        wkvb = mla_w_kv_b[l].reshape(kv_rank, n_mla, MLA_NOPE + MLA_V_DIM)
        wkvb = jnp.concatenate([wkvb[:, :, :MLA_NOPE].reshape(kv_rank, n_mla * MLA_NOPE),
                                wkvb[:, :, MLA_NOPE:].reshape(kv_rank, n_mla * MLA_V_DIM)], axis=1).astype(BF16)
        gqa = row(jnp.tile(dil_q_norm[l], dil_heads)) * (LOG2E * DIL_HEAD_DIM ** -0.5)
        gka = row(jnp.tile(dil_k_norm[l], dil_heads))
        gq = row(mla_q_norm[l]) * (LOG2E * qk_dim ** -0.5)
        gk = row(mla_k_norm[l])
        gqn, gqr = gq[:, :MLA_NOPE], _pad_lanes(gq[:, MLA_NOPE:], LANES)
        gkn, gkr = gk[:, :MLA_NOPE], _pad_lanes(gk[:, MLA_NOPE:], LANES)

        consts = [row(mix_norm[l]), win, eh, perm, gqa, gka, row(mla_q_a_norm[l]), wqb,
                  row(mla_kv_a_norm[l]), wkvb, gqn, gqr, gkn, gkr]
        qa, ka, va, qm, km, vm = _proj_call(x2d, batch, seq, consts, cos_t, sin_t,
                                            n_mla_heads=n_mla, dil_width=dil_width)

        shp = lambda a: a.reshape(batch, seq, a.shape[1])
        o_dil = _dil_call(qa, ka, va, dil_bias)
        o_mla = _mla_call(shp(qm), shp(km), shp(vm), n_mla)

        x2d = _out_ffn_call(x2d, seq, o_dil, o_mla.reshape(batch * seq, -1), unperm,
                            row(out_norm_dil[l]), row(out_norm_mla[l]), w_out[l].astype(BF16),
                            row(ffn2_norm[l]), ffn2_w_gate[l], ffn2_w_up[l], ffn2_w_down[l])
    return x2d.reshape(batch, seq, d_model)
```

```python
import functools

import numpy as np
import jax
import jax.numpy as jnp
from jax import lax
from jax.experimental import pallas as pl
from jax.experimental.pallas import tpu as pltpu

F32 = jnp.float32
BF16 = jnp.bfloat16

DIL_HEAD_DIM = 64
DIL_BRANCHES = ((128, 1), (512, 4), (2048, 16))
DIL_BAND = 128
DIL_RES = 16
MLA_NOPE = 128
MLA_ROPE = 64
MLA_V_DIM = 128
ROPE_BASE = 10000.0
REL_BUCKETS = 32
REL_MAX_DIST = 2048
FFN_RESID = 0.5
EPS = 1e-6

LANES = 128
SUBLANES = 8
MXU_DIM = 256
VMEM_LIMIT_BYTES = 56 * 1024 * 1024

NEG = -1e30
LOG2E = 1.4426950408889634

ROW_TILE = 512
FFN_TF = 256
DIL_CHUNK = DIL_BAND * DIL_RES
MLA_TQ = 2048
MLA_TK = 512


def _dot(a, b):
    return jnp.dot(a, b, preferred_element_type=F32)


def _dot_nt(a, b):
    return lax.dot_general(a, b, (((1,), (1,)), ((), ())), preferred_element_type=F32)


def _rms(x, g):
    return x * lax.rsqrt(jnp.mean(x * x, axis=-1, keepdims=True) + EPS) * g


def _residue_major_perm(rows):
    per = rows // DIL_RES
    dst = np.arange(rows)
    src = DIL_RES * (dst % per) + dst // per
    p = np.zeros((rows, rows), np.float32)
    p[dst, src] = 1.0
    return p


def _swiglu_residual(x, g_ref, wg_ref, wu_ref, wd_ref, act_ref):
    h = _rms(x, g_ref[...]).astype(BF16)
    d_ff = wg_ref.shape[1]
    for c in range(d_ff // FFN_TF):
        sl = slice(c * FFN_TF, (c + 1) * FFN_TF)
        gate = _dot(h, wg_ref[:, sl].astype(BF16))
        up = _dot(h, wu_ref[:, sl].astype(BF16))
        act_ref[:, sl] = (gate * jax.nn.sigmoid(gate) * up).astype(BF16)
    return x + FFN_RESID * _dot(act_ref[...], wd_ref[...].astype(BF16))


def _ffn_kernel(x_ref, g_ref, wg_ref, wu_ref, wd_ref, o_ref, act_ref):
    o_ref[...] = _swiglu_residual(x_ref[...], g_ref, wg_ref, wu_ref, wd_ref, act_ref)


def _out_ffn_kernel(x_ref, od_ref, om_ref, gd_ref, gm_ref, wo_ref, g_ref, wg_ref, wu_ref, wd_ref,
                    o_ref, act_ref):
    wd_rows = od_ref.shape[1]
    od = _rms(od_ref[...], gd_ref[...]).astype(BF16)
    om = _rms(om_ref[...], gm_ref[...]).astype(BF16)
    x2 = x_ref[...] + _dot(od, wo_ref[:wd_rows, :]) + _dot(om, wo_ref[wd_rows:, :])
    o_ref[...] = _swiglu_residual(x2, g_ref, wg_ref, wu_ref, wd_ref, act_ref)


def _resident(shape):
    return pl.BlockSpec(shape, lambda *_: (0,) * len(shape), pipeline_mode=pl.Buffered(1))


def _residue_major_block(seq, width):
    per = ROW_TILE // DIL_RES
    blocks_per_seq = seq // ROW_TILE
    return pl.BlockSpec((None, DIL_RES, per, width), lambda i: (i // blocks_per_seq, 0, i % blocks_per_seq, 0))


def _ffn_call(x2d, g, wg, wu, wd):
    m, d = x2d.shape
    f = wg.shape[1]
    row = pl.BlockSpec((ROW_TILE, d), lambda i: (i, 0))
    return pl.pallas_call(
        _ffn_kernel,
        grid=(m // ROW_TILE,),
        in_specs=[row, _resident((1, d)), _resident((d, f)), _resident((d, f)), _resident((f, d))],
        out_specs=row,
        out_shape=jax.ShapeDtypeStruct((m, d), F32),
        scratch_shapes=[pltpu.VMEM((ROW_TILE, f), BF16)],
        compiler_params=pltpu.CompilerParams(dimension_semantics=("parallel",),
                                             vmem_limit_bytes=VMEM_LIMIT_BYTES),
        name="ffn",
    )(x2d, g, wg, wu, wd)


def _out_ffn_call(x2d, o_dil, o_mla, gd, gm, wo, g, wg, wu, wd):
    m, d = x2d.shape
    f = wg.shape[1]
    wdil, wmla = o_dil.shape[1], o_mla.shape[1]
    row = pl.BlockSpec((ROW_TILE, d), lambda i: (i, 0))
    return pl.pallas_call(
        _out_ffn_kernel,
        grid=(m // ROW_TILE,),
        in_specs=[row,
                  pl.BlockSpec((ROW_TILE, wdil), lambda i: (i, 0)),
                  pl.BlockSpec((ROW_TILE, wmla), lambda i: (i, 0)),
                  _resident((1, wdil)), _resident((1, wmla)),
                  _resident((wdil + wmla, d)),
                  _resident((1, d)), _resident((d, f)), _resident((d, f)), _resident((f, d))],
        out_specs=row,
        out_shape=jax.ShapeDtypeStruct((m, d), F32),
        scratch_shapes=[pltpu.VMEM((ROW_TILE, f), BF16)],
        compiler_params=pltpu.CompilerParams(dimension_semantics=("parallel",),
                                             vmem_limit_bytes=VMEM_LIMIT_BYTES),
        name="out_ffn",
    )(x2d, o_dil, o_mla, gd, gm, wo, g, wg, wu, wd)


def _rope(x, cos, sin_signed):
    lane = lax.broadcasted_iota(jnp.int32, x.shape, 1)
    first_half = (lane % MLA_ROPE) < (MLA_ROPE // 2)
    partner = jnp.where(first_half,
                        pltpu.roll(x, LANES - MLA_ROPE // 2, 1),
                        pltpu.roll(x, MLA_ROPE // 2, 1))
    return x * cos + partner * sin_signed


def _proj_kernel(x_ref, gmix_ref, win_ref, eh_ref, perm_ref, gqa_ref, gka_ref,
                 gcq_ref, wqb_ref, gckv_ref, wkvb_ref,
                 gqn_ref, gqr_ref, gkn_ref, gkr_ref, cos_ref, sin_ref,
                 qa_ref, ka_ref, va_ref, qm_ref, km_ref, vm_ref, *, n_mla_heads, dil_width):
    h = _rms(x_ref[...], gmix_ref[...]).astype(BF16)
    w = dil_width
    o = 3 * w
    q_rank = gcq_ref.shape[1]
    kv_rank = gckv_ref.shape[1]
    qk_dim = MLA_NOPE + MLA_ROPE
    nh = n_mla_heads
    cos = cos_ref[...]
    sin = sin_ref[...]

    pm = _dot(h, win_ref[:, o:])
    cq = pm[:, :q_rank]
    ckv = pm[:, q_rank:q_rank + kv_rank]
    kpe = pm[:, q_rank + kv_rank:]

    qb = _dot(_rms(cq, gcq_ref[...]).astype(BF16), wqb_ref[...])
    for hd in range(nh):
        qn = qb[:, hd * MLA_NOPE:(hd + 1) * MLA_NOPE]
        qr = qb[:, nh * MLA_NOPE + hd * LANES:nh * MLA_NOPE + (hd + 1) * LANES]
        ss = jnp.sum(qn * qn, axis=-1, keepdims=True) + jnp.sum(qr * qr, axis=-1, keepdims=True)
        r = lax.rsqrt(ss * (1.0 / qk_dim) + EPS)
        qm_ref[:, hd * MXU_DIM:hd * MXU_DIM + LANES] = (qn * r * gqn_ref[...]).astype(BF16)
        qm_ref[:, hd * MXU_DIM + LANES:(hd + 1) * MXU_DIM] = _rope(qr * r * gqr_ref[...], cos, sin).astype(BF16)

    kvb = _dot(_rms(ckv, gckv_ref[...]).astype(BF16), wkvb_ref[...])
    ss_pe = jnp.sum(kpe * kpe, axis=-1, keepdims=True)
    kpe_rot = _rope(kpe * gkr_ref[...], cos, sin)
    for hd in range(nh):
        kn = kvb[:, hd * MLA_NOPE:(hd + 1) * MLA_NOPE]
        ss = jnp.sum(kn * kn, axis=-1, keepdims=True) + ss_pe
        r = lax.rsqrt(ss * (1.0 / qk_dim) + EPS)
        km_ref[:, hd * MXU_DIM:hd * MXU_DIM + LANES] = (kn * r * gkn_ref[...]).astype(BF16)
        km_ref[:, hd * MXU_DIM + LANES:(hd + 1) * MXU_DIM] = (kpe_rot * r).astype(BF16)
    vm_ref[...] = kvb[:, nh * MLA_NOPE:].astype(BF16)

    def to_residue_major(val, dst):
        dst[...] = _dot(perm_ref[...], val.astype(BF16)).reshape(dst.shape)

    for c, (g_ref, dst) in enumerate(((gqa_ref, qa_ref), (gka_ref, ka_ref))):
        src = _dot(h, win_ref[:, c * w:(c + 1) * w])
        ms = _dot((src * src).astype(BF16), eh_ref[...]) * (1.0 / DIL_HEAD_DIM)
        to_residue_major(src * lax.rsqrt(ms + EPS) * g_ref[...], dst)
    to_residue_major(_dot(h, win_ref[:, 2 * w:o]), va_ref)


def _proj_call(x2d, batch, seq, consts, cos, sin, *, n_mla_heads, dil_width):
    m, d = x2d.shape
    tm = ROW_TILE
    n_seq_blocks = seq // tm
    row = lambda width: pl.BlockSpec((tm, width), lambda i: (i, 0))
    pos = pl.BlockSpec((tm, LANES), lambda i: (i % n_seq_blocks, 0))
    mla_w = n_mla_heads * MXU_DIM
    dil_shape = jax.ShapeDtypeStruct((batch, DIL_RES, seq // DIL_RES, dil_width), F32)
    return pl.pallas_call(
        functools.partial(_proj_kernel, n_mla_heads=n_mla_heads, dil_width=dil_width),
        grid=(m // tm,),
        in_specs=[row(d)] + [_resident(c.shape) for c in consts] + [pos, pos],
        out_specs=[_residue_major_block(seq, dil_width)] * 3
        + [row(mla_w), row(mla_w), row(n_mla_heads * MLA_V_DIM)],
        out_shape=[dil_shape] * 3
        + [jax.ShapeDtypeStruct((m, mla_w), BF16)] * 2
        + [jax.ShapeDtypeStruct((m, n_mla_heads * MLA_V_DIM), BF16)],
        compiler_params=pltpu.CompilerParams(dimension_semantics=("parallel",),
                                             vmem_limit_bytes=VMEM_LIMIT_BYTES),
        name="proj",
    )(x2d, *consts, cos, sin)


def _t5_bucket(dist):
    max_exact = REL_BUCKETS // 2
    d = np.maximum(dist, 1).astype(np.float32)
    large = max_exact + (np.log(d / max_exact) / np.log(REL_MAX_DIST / max_exact)
                         * (REL_BUCKETS - max_exact)).astype(np.int32)
    large = np.minimum(large, REL_BUCKETS - 1)
    return np.where(dist < max_exact, dist, large).astype(np.int32)


def _band_tables():
    rho = np.arange(DIL_BAND)
    kap = np.arange(2 * DIL_BAND)
    buckets, prev = [], []
    for _, dil in DIL_BRANCHES:
        g = DIL_RES // dil
        run_q = DIL_BAND // g
        pos_q = g * (rho % run_q) + rho // run_q
        pos_k = g * (kap % (2 * run_q)) + kap // (2 * run_q) - DIL_BAND
        delta = pos_q[:, None] - pos_k[None, :]
        valid = (delta >= 0) & (delta <= DIL_BAND)
        buckets.append(np.where(valid, _t5_bucket(np.clip(delta, 0, None) * dil), -1))
        prev.append((pos_k < 0)[None, :])
    return np.stack(buckets).astype(np.int32), np.stack(prev).astype(np.int32)


def _dil_bias_kernel(rel_ref, bucket_ref, prev_ref, o_ref):
    bucket = bucket_ref[...]
    prev = prev_ref[...] > 0
    for hd in range(o_ref.shape[1]):
        bias = jnp.full(bucket.shape, NEG, F32)
        for b in range(REL_BUCKETS):
            bias = jnp.where(bucket == b, rel_ref[hd, b] * LOG2E, bias)
        o_ref[0, hd] = bias
        o_ref[1, hd] = jnp.where(prev, NEG, bias)


def _dil_bias_call(rel_bias, buckets, prev):
    nbr = buckets.shape[0]
    nh = rel_bias.shape[0]
    tile = buckets.shape[1:]
    return pl.pallas_call(
        _dil_bias_kernel,
        grid=(nbr,),
        in_specs=[pl.BlockSpec(memory_space=pltpu.SMEM),
                  pl.BlockSpec((None,) + tile, lambda b: (b, 0, 0)),
                  pl.BlockSpec((None, 1, tile[1]), lambda b: (b, 0, 0))],
        out_specs=pl.BlockSpec((None, 2, nh) + tile, lambda b: (b, 0, 0, 0, 0)),
        out_shape=jax.ShapeDtypeStruct((nbr, 2, nh) + tile, F32),
        name="dil_bias",
    )(rel_bias, buckets, prev)


def _band_tile(q, k, v, bias, head_a):
    zero = jnp.zeros_like(q)
    q2 = jnp.concatenate([jnp.where(head_a, q, zero), jnp.where(head_a, zero, q)], axis=0).astype(BF16)
    s = _dot_nt(q2, k.astype(BF16)) + bias
    m = jnp.max(s, axis=-1, keepdims=True)
    p = jnp.exp2(s - m).astype(BF16)
    ones = jnp.ones(v.shape, BF16)
    out = _dot(p, jnp.concatenate([v.astype(BF16), ones], axis=1))
    h = DIL_BAND
    num = jnp.where(head_a, out[:h, :LANES], out[h:, :LANES])
    den = jnp.where(head_a, out[:h, LANES:], out[h:, LANES:])
    mx = jnp.where(head_a, m[:h], m[h:])
    return num, den, mx


def _dil_kernel(q_ref, kp_ref, kc_ref, vp_ref, vc_ref, bias_ref, o_ref, num_scr, den_scr, max_scr):
    first_chunk = (pl.program_id(2) == 0).astype(jnp.int32)
    head_a = lax.broadcasted_iota(jnp.int32, (1, LANES), 1) < DIL_HEAD_DIM

    def tile(bi, dil, rbase, n):
        g = DIL_RES // dil
        run = DIL_BAND // g
        static_n = isinstance(n, int)
        q_rows = pl.ds(n * run, run) if static_n else pl.ds(pl.multiple_of(n * run, run), run)
        qs, ks, vs = [], [], []
        for u in range(g):
            res = rbase + dil * u
            qs.append(q_ref[res, q_rows, :])
            if static_n and n == 0:
                ks += [kp_ref[res, pl.ds(DIL_BAND - run, run), :], kc_ref[res, pl.ds(0, run), :]]
                vs += [vp_ref[res, pl.ds(DIL_BAND - run, run), :], vc_ref[res, pl.ds(0, run), :]]
            else:
                k_rows = (pl.ds((n - 1) * run, 2 * run) if static_n
                          else pl.ds(pl.multiple_of((n - 1) * run, run), 2 * run))
                ks.append(kc_ref[res, k_rows, :])
                vs.append(vc_ref[res, k_rows, :])
        cat = lambda parts: parts[0] if len(parts) == 1 else jnp.concatenate(parts, axis=0)
        variant = first_chunk if (static_n and n == 0) else 0
        num, den, mx = _band_tile(cat(qs), cat(ks), cat(vs), bias_ref[bi, variant], head_a)
        for u in range(g):
            res = rbase + dil * u
            part = slice(u * run, (u + 1) * run)
            num_scr[bi, res, q_rows, :] = num[part]
            den_scr[bi, res, q_rows, :] = den[part]
            max_scr[bi, res, q_rows, :] = mx[part]

    for n in range(DIL_RES):
        tile(0, 1, 0, n)
    for r4 in range(4):
        for n in range(4):
            tile(1, 4, r4, n)
    for r in range(DIL_RES):
        tile(2, 16, r, 0)

    nbr = len(DIL_BRANCHES)

    def merge(res, carry):
        ms = [max_scr[bi, res] for bi in range(nbr)]
        m_all = functools.reduce(jnp.maximum, ms)
        es = [jnp.exp2(mb - m_all) for mb in ms]
        num = sum(es[bi] * num_scr[bi, res] for bi in range(nbr))
        den = sum(es[bi] * den_scr[bi, res] for bi in range(nbr))
        o_ref[pl.ds(res, DIL_BAND, stride=DIL_RES), :] = num / den
        return carry

    lax.fori_loop(0, DIL_RES, merge, 0)


def _dil_call(qa, ka, va, bias):
    b, _, per, w = qa.shape
    pairs = w // LANES
    nbr = bias.shape[0]
    blk = (None, DIL_RES, DIL_BAND, LANES)
    cur = pl.BlockSpec(blk, lambda bb, p, c: (bb, 0, c, p))
    prev = pl.BlockSpec(blk, lambda bb, p, c: (bb, 0, jnp.maximum(c - 1, 0), p))
    scr = pltpu.VMEM((nbr, DIL_RES, DIL_BAND, LANES), F32)
    return pl.pallas_call(
        _dil_kernel,
        grid=(b, pairs, per // DIL_BAND),
        in_specs=[cur, prev, cur, prev, cur,
                  pl.BlockSpec((nbr, 2, None, 2 * DIL_BAND, 2 * DIL_BAND), lambda bb, p, c: (0, 0, p, 0, 0))],
        out_specs=pl.BlockSpec((None, DIL_CHUNK, LANES), lambda bb, p, c: (bb, c, p)),
        out_shape=jax.ShapeDtypeStruct((b, per * DIL_RES, w), F32),
        scratch_shapes=[scr, scr, scr],
        compiler_params=pltpu.CompilerParams(dimension_semantics=("parallel", "parallel", "arbitrary"),
                                             vmem_limit_bytes=VMEM_LIMIT_BYTES),
        name="dil_attn",
    )(qa, ka, ka, va, va, bias)


def _mla_kernel(q_ref, k_ref, v_ref, o_ref, m_ref, acc_ref, s0_ref, s1_ref):
    tq = q_ref.shape[0]
    tk = MLA_TK
    nsub = tq // tk
    qi = pl.program_id(2)
    m_ref[...] = jnp.full(m_ref.shape, NEG, F32)
    acc_ref[...] = jnp.zeros(acc_ref.shape, F32)
    ones = jnp.ones((tk, LANES), BF16)
    s_bufs = (s0_ref, s1_ref)

    def scores(kblk, row0=0):
        k0 = pl.multiple_of(kblk * tk, tk)
        return _dot_nt(q_ref[row0:, :], k_ref[pl.ds(k0, tk), :])

    def update(s, kblk, row0, diagonal):
        k0 = pl.multiple_of(kblk * tk, tk)
        rows = slice(row0, tq)
        if diagonal:
            row = lax.broadcasted_iota(jnp.int32, s.shape, 0)
            col = lax.broadcasted_iota(jnp.int32, s.shape, 1)
            s = jnp.where(col <= row, s, NEG)
        m_old = m_ref[rows, :]
        m_new = jnp.maximum(m_old, jnp.max(s, axis=-1, keepdims=True))
        m_ref[rows, :] = m_new
        p = jnp.concatenate([jnp.exp2(s[:, c * LANES:(c + 1) * LANES] - m_new) for c in range(tk // LANES)],
                            axis=1).astype(BF16)
        alpha = jnp.exp2(m_old - m_new)
        vext = jnp.concatenate([v_ref[pl.ds(k0, tk), :], ones], axis=1)
        acc_ref[rows, :] = acc_ref[rows, :] * jnp.concatenate([alpha, alpha], axis=1) + _dot(p, vext)

    s0_ref[...] = scores(0)

    def body(t, carry):
        for jj in range(nsub):
            s_bufs[(jj + 1) % 2][...] = scores(nsub * t + jj + 1)
            update(s_bufs[jj % 2][...], nsub * t + jj, 0, diagonal=False)
        return carry

    lax.fori_loop(0, qi, body, 0)
    s_cur = s0_ref[...]
    for d in range(nsub):
        s_next = scores(nsub * qi + d + 1, row0=(d + 1) * tk) if d + 1 < nsub else None
        update(s_cur, nsub * qi + d, d * tk, diagonal=True)
        s_cur = s_next
    acc = acc_ref[...]
    o_ref[...] = acc[:, :MLA_V_DIM] / acc[:, MLA_V_DIM:]


def _mla_call(qm, km, vm, n_heads):
    b, s, _ = qm.shape
    tq = MLA_TQ
    assert tq % (2 * MLA_TK) == 0
    return pl.pallas_call(
        _mla_kernel,
        grid=(b, n_heads, s // tq),
        in_specs=[pl.BlockSpec((None, tq, MXU_DIM), lambda bb, h, i: (bb, i, h)),
                  pl.BlockSpec((None, s, MXU_DIM), lambda bb, h, i: (bb, 0, h)),
                  pl.BlockSpec((None, s, MLA_V_DIM), lambda bb, h, i: (bb, 0, h))],
        out_specs=pl.BlockSpec((None, tq, MLA_V_DIM), lambda bb, h, i: (bb, i, h)),
        out_shape=jax.ShapeDtypeStruct((b, s, n_heads * MLA_V_DIM), F32),
        scratch_shapes=[pltpu.VMEM((tq, LANES), F32), pltpu.VMEM((tq, 2 * MLA_V_DIM), F32),
                        pltpu.VMEM((tq, MLA_TK), F32), pltpu.VMEM((tq, MLA_TK), F32)],
        compiler_params=pltpu.CompilerParams(dimension_semantics=("parallel", "parallel", "arbitrary"),
                                             vmem_limit_bytes=VMEM_LIMIT_BYTES),
        name="mla_attn",
    )(qm, km, vm)


def _rope_tables(seq):
    inv_freq = ROPE_BASE ** (-np.arange(0, MLA_ROPE, 2, dtype=np.float64) / MLA_ROPE)
    ang = np.arange(seq, dtype=np.float64)[:, None] * inv_freq[None, :]
    cos, sin = np.cos(ang), np.sin(ang)
    cos_t = np.concatenate([cos, cos] * (LANES // MLA_ROPE), axis=1)
    sin_t = np.concatenate([-sin, sin] * (LANES // MLA_ROPE), axis=1)
    return jnp.asarray(cos_t, F32), jnp.asarray(sin_t, F32)


def _pad_lanes(v, width):
    return jnp.pad(v, ((0, 0), (0, width - v.shape[1])))


def kernel(x, ffn1_norm, ffn1_w_gate, ffn1_w_up, ffn1_w_down, mix_norm, w_in, dil_q_norm, dil_k_norm,
           rel_bias, mla_q_a_norm, mla_w_q_b, mla_kv_a_norm, mla_w_kv_b, mla_q_norm, mla_k_norm,
           out_norm_dil, out_norm_mla, w_out, ffn2_norm, ffn2_w_gate, ffn2_w_up, ffn2_w_down):
    batch, seq, d_model = x.shape
    depth = w_in.shape[0]
    dil_width = out_norm_dil.shape[1]
    dil_heads = dil_width // DIL_HEAD_DIM
    q_rank = mla_q_a_norm.shape[1]
    kv_rank = mla_kv_a_norm.shape[1]
    qk_dim = MLA_NOPE + MLA_ROPE
    n_mla = mla_w_q_b.shape[2] // qk_dim
    assert rel_bias.shape == (dil_heads, REL_BUCKETS)
    assert w_in.shape[2] == 3 * dil_width + q_rank + kv_rank + MLA_ROPE
    assert seq % DIL_CHUNK == 0 and seq % MLA_TQ == 0 and seq % ROW_TILE == 0
    assert all(win // dil == DIL_BAND and DIL_RES % dil == 0 for win, dil in DIL_BRANCHES)

    cos_t, sin_t = _rope_tables(seq)
    buckets, prev = _band_tables()
    dil_bias = _dil_bias_call(rel_bias, jnp.asarray(buckets), jnp.asarray(prev))
    dil_bias = dil_bias.reshape(dil_bias.shape[0], 2, dil_heads // 2, 2 * DIL_BAND, 2 * DIL_BAND)
    eh = jnp.asarray(np.kron(np.eye(dil_heads), np.ones((DIL_HEAD_DIM, DIL_HEAD_DIM))), BF16)
    perm = jnp.asarray(_residue_major_perm(ROW_TILE), BF16)

    x2d = x.reshape(batch * seq, d_model)
    row = lambda v: v.reshape(1, -1)
    for l in range(depth):
        x2d = _ffn_call(x2d, row(ffn1_norm[l]), ffn1_w_gate[l], ffn1_w_up[l], ffn1_w_down[l])

        win = _pad_lanes(w_in[l], w_in.shape[2] + LANES - MLA_ROPE).astype(BF16)
        wqb = mla_w_q_b[l].reshape(q_rank, n_mla, qk_dim)
        wqb = jnp.concatenate(
            [wqb[:, :, :MLA_NOPE].reshape(q_rank, n_mla * MLA_NOPE),
             jnp.pad(wqb[:, :, MLA_NOPE:], ((0, 0), (0, 0), (0, LANES - MLA_ROPE))).reshape(q_rank, n_mla * LANES)],
            axis=1).astype(BF16)
        wkvb = mla_w_kv_b[l].reshape(kv_rank, n_mla, MLA_NOPE + MLA_V_DIM)
        wkvb = jnp.concatenate([wkvb[:, :, :MLA_NOPE].reshape(kv_rank, n_mla * MLA_NOPE),
                                wkvb[:, :, MLA_NOPE:].reshape(kv_rank, n_mla * MLA_V_DIM)], axis=1).astype(BF16)
        gqa = row(jnp.tile(dil_q_norm[l], dil_heads)) * (LOG2E * DIL_HEAD_DIM ** -0.5)
        gka = row(jnp.tile(dil_k_norm[l], dil_heads))
        gq = row(mla_q_norm[l]) * (LOG2E * qk_dim ** -0.5)
        gk = row(mla_k_norm[l])
        gqn, gqr = gq[:, :MLA_NOPE], _pad_lanes(gq[:, MLA_NOPE:], LANES)
        gkn, gkr = gk[:, :MLA_NOPE], _pad_lanes(gk[:, MLA_NOPE:], LANES)

        consts = [row(mix_norm[l]), win, eh, perm, gqa, gka, row(mla_q_a_norm[l]), wqb,
                  row(mla_kv_a_norm[l]), wkvb, gqn, gqr, gkn, gkr]
        qa, ka, va, qm, km, vm = _proj_call(x2d, batch, seq, consts, cos_t, sin_t,
                                            n_mla_heads=n_mla, dil_width=dil_width)

        shp = lambda a: a.reshape(batch, seq, a.shape[1])
        o_dil = _dil_call(qa, ka, va, dil_bias)
        o_mla = _mla_call(shp(qm), shp(km), shp(vm), n_mla)

        x2d = _out_ffn_call(x2d, o_dil.reshape(batch * seq, -1), o_mla.reshape(batch * seq, -1),
                            row(out_norm_dil[l]), row(out_norm_mla[l]), w_out[l].astype(BF16),
                            row(ffn2_norm[l]), ffn2_w_gate[l], ffn2_w_up[l], ffn2_w_down[l])
    return x2d.reshape(batch, seq, d_model)
```

```python
import functools

import numpy as np
import jax
import jax.numpy as jnp
from jax import lax
from jax.experimental import pallas as pl
from jax.experimental.pallas import tpu as pltpu

F32 = jnp.float32
BF16 = jnp.bfloat16

DIL_HEAD_DIM = 64
DIL_BRANCHES = ((128, 1), (512, 4), (2048, 16))
DIL_BAND = 128
DIL_RES = 16
MLA_NOPE = 128
MLA_ROPE = 64
MLA_V_DIM = 128
ROPE_BASE = 10000.0
REL_BUCKETS = 32
REL_MAX_DIST = 2048
FFN_RESID = 0.5
EPS = 1e-6

LANES = 128
SUBLANES = 8
MXU_DIM = 256
VMEM_LIMIT_BYTES = 56 * 1024 * 1024

NEG = -1e30
LOG2E = 1.4426950408889634

ROW_TILE = 512
FFN_TF = 256
DIL_CHUNK = DIL_BAND * DIL_RES
MLA_TQ = 2048
MLA_TK = 512


def _dot(a, b):
    return jnp.dot(a, b, preferred_element_type=F32)


def _dot_nt(a, b):
    return lax.dot_general(a, b, (((1,), (1,)), ((), ())), preferred_element_type=F32)


def _rms(x, g):
    return x * lax.rsqrt(jnp.mean(x * x, axis=-1, keepdims=True) + EPS) * g


def _residue_major_perm(rows):
    per = rows // DIL_RES
    dst = np.arange(rows)
    src = DIL_RES * (dst % per) + dst // per
    p = np.zeros((rows, rows), np.float32)
    p[dst, src] = 1.0
    return p


def _swiglu_residual(x, g_ref, wg_ref, wu_ref, wd_ref, act_ref):
    h = _rms(x, g_ref[...]).astype(BF16)
    d_ff = wg_ref.shape[1]
    for c in range(d_ff // FFN_TF):
        sl = slice(c * FFN_TF, (c + 1) * FFN_TF)
        gate = _dot(h, wg_ref[:, sl].astype(BF16))
        up = _dot(h, wu_ref[:, sl].astype(BF16))
        act_ref[:, sl] = (gate * jax.nn.sigmoid(gate) * up).astype(BF16)
    return x + FFN_RESID * _dot(act_ref[...], wd_ref[...].astype(BF16))


def _ffn_kernel(x_ref, g_ref, wg_ref, wu_ref, wd_ref, o_ref, act_ref):
    o_ref[...] = _swiglu_residual(x_ref[...], g_ref, wg_ref, wu_ref, wd_ref, act_ref)


def _out_ffn_kernel(x_ref, od_ref, om_ref, gd_ref, gm_ref, wo_ref, g_ref, wg_ref, wu_ref, wd_ref,
                    o_ref, act_ref):
    wd_rows = od_ref.shape[1]
    od = _rms(od_ref[...], gd_ref[...]).astype(BF16)
    om = _rms(om_ref[...], gm_ref[...]).astype(BF16)
    x2 = x_ref[...] + _dot(od, wo_ref[:wd_rows, :]) + _dot(om, wo_ref[wd_rows:, :])
    o_ref[...] = _swiglu_residual(x2, g_ref, wg_ref, wu_ref, wd_ref, act_ref)


def _resident(shape):
    return pl.BlockSpec(shape, lambda *_: (0,) * len(shape), pipeline_mode=pl.Buffered(1))


def _residue_major_block(seq, width):
    per = ROW_TILE // DIL_RES
    blocks_per_seq = seq // ROW_TILE
    return pl.BlockSpec((None, DIL_RES, per, width), lambda i: (i // blocks_per_seq, 0, i % blocks_per_seq, 0))


def _ffn_call(x2d, g, wg, wu, wd):
    m, d = x2d.shape
    f = wg.shape[1]
    row = pl.BlockSpec((ROW_TILE, d), lambda i: (i, 0))
    return pl.pallas_call(
        _ffn_kernel,
        grid=(m // ROW_TILE,),
        in_specs=[row, _resident((1, d)), _resident((d, f)), _resident((d, f)), _resident((f, d))],
        out_specs=row,
        out_shape=jax.ShapeDtypeStruct((m, d), F32),
        scratch_shapes=[pltpu.VMEM((ROW_TILE, f), BF16)],
        compiler_params=pltpu.CompilerParams(dimension_semantics=("parallel",),
                                             vmem_limit_bytes=VMEM_LIMIT_BYTES),
        name="ffn",
    )(x2d, g, wg, wu, wd)


def _out_ffn_call(x2d, o_dil, o_mla, gd, gm, wo, g, wg, wu, wd):
    m, d = x2d.shape
    f = wg.shape[1]
    wdil, wmla = o_dil.shape[1], o_mla.shape[1]
    row = pl.BlockSpec((ROW_TILE, d), lambda i: (i, 0))
    return pl.pallas_call(
        _out_ffn_kernel,
        grid=(m // ROW_TILE,),
        in_specs=[row,
                  pl.BlockSpec((ROW_TILE, wdil), lambda i: (i, 0)),
                  pl.BlockSpec((ROW_TILE, wmla), lambda i: (i, 0)),
                  _resident((1, wdil)), _resident((1, wmla)),
                  _resident((wdil + wmla, d)),
                  _resident((1, d)), _resident((d, f)), _resident((d, f)), _resident((f, d))],
        out_specs=row,
        out_shape=jax.ShapeDtypeStruct((m, d), F32),
        scratch_shapes=[pltpu.VMEM((ROW_TILE, f), BF16)],
        compiler_params=pltpu.CompilerParams(dimension_semantics=("parallel",),
                                             vmem_limit_bytes=VMEM_LIMIT_BYTES),
        name="out_ffn",
    )(x2d, o_dil, o_mla, gd, gm, wo, g, wg, wu, wd)


def _rope(x, cos, sin_signed):
    lane = lax.broadcasted_iota(jnp.int32, x.shape, 1)
    first_half = (lane % MLA_ROPE) < (MLA_ROPE // 2)
    partner = jnp.where(first_half,
                        pltpu.roll(x, LANES - MLA_ROPE // 2, 1),
                        pltpu.roll(x, MLA_ROPE // 2, 1))
    return x * cos + partner * sin_signed


def _proj_kernel(x_ref, gmix_ref, win_ref, wkpe_ref, eh_ref, perm_ref, gqa_ref, gka_ref,
                 gcq_ref, wqb_ref, gckv_ref, wkvb_ref,
                 gqn_ref, gqr_ref, gkn_ref, gkr_ref, cos_ref, sin_ref,
                 qa_ref, ka_ref, va_ref, qm_ref, km_ref, vm_ref, *, n_mla_heads, dil_width):
    h = _rms(x_ref[...], gmix_ref[...]).astype(BF16)
    w = dil_width
    o = 3 * w
    q_rank = gcq_ref.shape[1]
    kv_rank = gckv_ref.shape[1]
    qk_dim = MLA_NOPE + MLA_ROPE
    nh = n_mla_heads
    cos = cos_ref[...]
    sin = sin_ref[...]

    w_mla = jnp.concatenate([win_ref[:, o:o + q_rank + kv_rank].astype(BF16), wkpe_ref[...]], axis=1)
    pm = _dot(h, w_mla)
    cq = pm[:, :q_rank]
    ckv = pm[:, q_rank:q_rank + kv_rank]
    kpe = pm[:, q_rank + kv_rank:]

    qb = _dot(_rms(cq, gcq_ref[...]).astype(BF16), wqb_ref[...])
    for hd in range(nh):
        qn = qb[:, hd * MLA_NOPE:(hd + 1) * MLA_NOPE]
        qr = qb[:, nh * MLA_NOPE + hd * LANES:nh * MLA_NOPE + (hd + 1) * LANES]
        ss = jnp.sum(qn * qn, axis=-1, keepdims=True) + jnp.sum(qr * qr, axis=-1, keepdims=True)
        r = lax.rsqrt(ss * (1.0 / qk_dim) + EPS)
        qm_ref[:, hd * MXU_DIM:hd * MXU_DIM + LANES] = (qn * r * gqn_ref[...]).astype(BF16)
        qm_ref[:, hd * MXU_DIM + LANES:(hd + 1) * MXU_DIM] = _rope(qr * r * gqr_ref[...], cos, sin).astype(BF16)

    kvb = _dot(_rms(ckv, gckv_ref[...]).astype(BF16), wkvb_ref[...])
    ss_pe = jnp.sum(kpe * kpe, axis=-1, keepdims=True)
    kpe_rot = _rope(kpe * gkr_ref[...], cos, sin)
    for hd in range(nh):
        kn = kvb[:, hd * MLA_NOPE:(hd + 1) * MLA_NOPE]
        ss = jnp.sum(kn * kn, axis=-1, keepdims=True) + ss_pe
        r = lax.rsqrt(ss * (1.0 / qk_dim) + EPS)
        km_ref[:, hd * MXU_DIM:hd * MXU_DIM + LANES] = (kn * r * gkn_ref[...]).astype(BF16)
        km_ref[:, hd * MXU_DIM + LANES:(hd + 1) * MXU_DIM] = (kpe_rot * r).astype(BF16)
    vm_ref[...] = kvb[:, nh * MLA_NOPE:].astype(BF16)

    hp = _dot(perm_ref[...], h).astype(BF16)
    for c, (g_ref, dst) in enumerate(((gqa_ref, qa_ref), (gka_ref, ka_ref))):
        src = _dot(hp, win_ref[:, c * w:(c + 1) * w].astype(BF16))
        ms = _dot((src * src).astype(BF16), eh_ref[...]) * (1.0 / DIL_HEAD_DIM)
        dst[...] = (src * lax.rsqrt(ms + EPS) * g_ref[...]).reshape(dst.shape)
    va_ref[...] = _dot(hp, win_ref[:, 2 * w:o].astype(BF16)).reshape(va_ref.shape)


def _proj_call(x2d, batch, seq, consts, cos, sin, *, n_mla_heads, dil_width):
    m, d = x2d.shape
    tm = ROW_TILE
    n_seq_blocks = seq // tm
    row = lambda width: pl.BlockSpec((tm, width), lambda i: (i, 0))
    pos = pl.BlockSpec((tm, LANES), lambda i: (i % n_seq_blocks, 0))
    mla_w = n_mla_heads * MXU_DIM
    dil_shape = jax.ShapeDtypeStruct((batch, DIL_RES, seq // DIL_RES, dil_width), F32)
    return pl.pallas_call(
        functools.partial(_proj_kernel, n_mla_heads=n_mla_heads, dil_width=dil_width),
        grid=(m // tm,),
        in_specs=[row(d)] + [_resident(c.shape) for c in consts] + [pos, pos],
        out_specs=[_residue_major_block(seq, dil_width)] * 3
        + [row(mla_w), row(mla_w), row(n_mla_heads * MLA_V_DIM)],
        out_shape=[dil_shape] * 3
        + [jax.ShapeDtypeStruct((m, mla_w), BF16)] * 2
        + [jax.ShapeDtypeStruct((m, n_mla_heads * MLA_V_DIM), BF16)],
        compiler_params=pltpu.CompilerParams(dimension_semantics=("parallel",),
                                             vmem_limit_bytes=VMEM_LIMIT_BYTES),
        name="proj",
    )(x2d, *consts, cos, sin)


def _t5_bucket(dist):
    max_exact = REL_BUCKETS // 2
    d = np.maximum(dist, 1).astype(np.float32)
    large = max_exact + (np.log(d / max_exact) / np.log(REL_MAX_DIST / max_exact)
                         * (REL_BUCKETS - max_exact)).astype(np.int32)
    large = np.minimum(large, REL_BUCKETS - 1)
    return np.where(dist < max_exact, dist, large).astype(np.int32)


def _band_tables():
    rho = np.arange(DIL_BAND)
    kap = np.arange(2 * DIL_BAND)
    buckets, prev = [], []
    for _, dil in DIL_BRANCHES:
        g = DIL_RES // dil
        run_q = DIL_BAND // g
        pos_q = g * (rho % run_q) + rho // run_q
        pos_k = g * (kap % (2 * run_q)) + kap // (2 * run_q) - DIL_BAND
        delta = pos_q[:, None] - pos_k[None, :]
        valid = (delta >= 0) & (delta <= DIL_BAND)
        buckets.append(np.where(valid, _t5_bucket(np.clip(delta, 0, None) * dil), -1))
        prev.append((pos_k < 0)[None, :])
    return np.stack(buckets).astype(np.int32), np.stack(prev).astype(np.int32)


def _dil_bias_kernel(rel_ref, bucket_ref, prev_ref, o_ref):
    bucket = bucket_ref[...]
    prev = prev_ref[...] > 0
    for hd in range(o_ref.shape[1]):
        bias = jnp.full(bucket.shape, NEG, F32)
        for b in range(REL_BUCKETS):
            bias = jnp.where(bucket == b, rel_ref[hd, b] * LOG2E, bias)
        o_ref[0, hd] = bias
        o_ref[1, hd] = jnp.where(prev, NEG, bias)


def _dil_bias_call(rel_bias, buckets, prev):
    nbr = buckets.shape[0]
    nh = rel_bias.shape[0]
    tile = buckets.shape[1:]
    return pl.pallas_call(
        _dil_bias_kernel,
        grid=(nbr,),
        in_specs=[pl.BlockSpec(memory_space=pltpu.SMEM),
                  pl.BlockSpec((None,) + tile, lambda b: (b, 0, 0)),
                  pl.BlockSpec((None, 1, tile[1]), lambda b: (b, 0, 0))],
        out_specs=pl.BlockSpec((None, 2, nh) + tile, lambda b: (b, 0, 0, 0, 0)),
        out_shape=jax.ShapeDtypeStruct((nbr, 2, nh) + tile, F32),
        name="dil_bias",
    )(rel_bias, buckets, prev)


def _band_tile(q, k, v, bias, head_a):
    zero = jnp.zeros_like(q)
    q2 = jnp.concatenate([jnp.where(head_a, q, zero), jnp.where(head_a, zero, q)], axis=0).astype(BF16)
    s = _dot_nt(q2, k.astype(BF16)) + bias
    m = jnp.max(s, axis=-1, keepdims=True)
    p = jnp.exp2(s - m).astype(BF16)
    ones = jnp.ones(v.shape, BF16)
    out = _dot(p, jnp.concatenate([v.astype(BF16), ones], axis=1))
    h = DIL_BAND
    num = jnp.where(head_a, out[:h, :LANES], out[h:, :LANES])
    den = jnp.where(head_a, out[:h, LANES:], out[h:, LANES:])
    mx = jnp.where(head_a, m[:h], m[h:])
    return num, den, mx


def _dil_kernel(q_ref, kp_ref, kc_ref, vp_ref, vc_ref, bias_ref, o_ref, num_scr, den_scr, max_scr):
    first_chunk = (pl.program_id(2) == 0).astype(jnp.int32)
    head_a = lax.broadcasted_iota(jnp.int32, (1, LANES), 1) < DIL_HEAD_DIM

    def tile(bi, dil, rbase, n):
        g = DIL_RES // dil
        run = DIL_BAND // g
        static_n = isinstance(n, int)
        q_rows = pl.ds(n * run, run) if static_n else pl.ds(pl.multiple_of(n * run, run), run)
        qs, ks, vs = [], [], []
        for u in range(g):
            res = rbase + dil * u
            qs.append(q_ref[res, q_rows, :])
            if static_n and n == 0:
                ks += [kp_ref[res, pl.ds(DIL_BAND - run, run), :], kc_ref[res, pl.ds(0, run), :]]
                vs += [vp_ref[res, pl.ds(DIL_BAND - run, run), :], vc_ref[res, pl.ds(0, run), :]]
            else:
                k_rows = (pl.ds((n - 1) * run, 2 * run) if static_n
                          else pl.ds(pl.multiple_of((n - 1) * run, run), 2 * run))
                ks.append(kc_ref[res, k_rows, :])
                vs.append(vc_ref[res, k_rows, :])
        cat = lambda parts: parts[0] if len(parts) == 1 else jnp.concatenate(parts, axis=0)
        variant = first_chunk if (static_n and n == 0) else 0
        num, den, mx = _band_tile(cat(qs), cat(ks), cat(vs), bias_ref[bi, variant], head_a)
        if g == 1:
            return num, den, mx
        for u in range(g):
            res = rbase + dil * u
            part = slice(u * run, (u + 1) * run)
            num_scr[bi, res, q_rows, :] = num[part]
            den_scr[bi, res, q_rows, :] = den[part]
            max_scr[bi, res, q_rows, :] = mx[part]

    for n in range(DIL_RES):
        tile(0, 1, 0, n)
    for r4 in range(4):
        for n in range(4):
            tile(1, 4, r4, n)
    stored = len(DIL_BRANCHES) - 1
    for r in range(DIL_RES):
        num_r, den_r, max_r = tile(stored, DIL_RES, r, 0)
        ms = [max_scr[bi, r] for bi in range(stored)] + [max_r]
        nums = [num_scr[bi, r] for bi in range(stored)] + [num_r]
        dens = [den_scr[bi, r] for bi in range(stored)] + [den_r]
        m_all = functools.reduce(jnp.maximum, ms)
        es = [jnp.exp2(mb - m_all) for mb in ms]
        num = sum(e * x for e, x in zip(es, nums))
        den = sum(e * x for e, x in zip(es, dens))
        o_ref[pl.ds(r, DIL_BAND, stride=DIL_RES), :] = num / den


def _dil_call(qa, ka, va, bias):
    b, _, per, w = qa.shape
    pairs = w // LANES
    nbr = bias.shape[0]
    blk = (None, DIL_RES, DIL_BAND, LANES)
    cur = pl.BlockSpec(blk, lambda bb, p, c: (bb, 0, c, p))
    prev = pl.BlockSpec(blk, lambda bb, p, c: (bb, 0, jnp.maximum(c - 1, 0), p))
    scr = pltpu.VMEM((nbr - 1, DIL_RES, DIL_BAND, LANES), F32)
    return pl.pallas_call(
        _dil_kernel,
        grid=(b, pairs, per // DIL_BAND),
        in_specs=[cur, prev, cur, prev, cur,
                  pl.BlockSpec((nbr, 2, None, 2 * DIL_BAND, 2 * DIL_BAND), lambda bb, p, c: (0, 0, p, 0, 0))],
        out_specs=pl.BlockSpec((None, DIL_CHUNK, LANES), lambda bb, p, c: (bb, c, p)),
        out_shape=jax.ShapeDtypeStruct((b, per * DIL_RES, w), F32),
        scratch_shapes=[scr, scr, scr],
        compiler_params=pltpu.CompilerParams(dimension_semantics=("parallel", "parallel", "arbitrary"),
                                             vmem_limit_bytes=VMEM_LIMIT_BYTES),
        name="dil_attn",
    )(qa, ka, ka, va, va, bias)


def _mla_kernel(q_ref, k_ref, v_ref, o_ref, m_ref, acc_ref, s0_ref, s1_ref):
    tq = q_ref.shape[0]
    tk = MLA_TK
    nsub = tq // tk
    qi = pl.program_id(2)
    m_ref[...] = jnp.full(m_ref.shape, NEG, F32)
    acc_ref[...] = jnp.zeros(acc_ref.shape, F32)
    ones = jnp.ones((tk, LANES), BF16)
    s_bufs = (s0_ref, s1_ref)

    def scores(kblk, row0=0):
        k0 = pl.multiple_of(kblk * tk, tk)
        return _dot_nt(q_ref[row0:, :], k_ref[pl.ds(k0, tk), :])

    def update(s, kblk, row0, diagonal):
        k0 = pl.multiple_of(kblk * tk, tk)
        rows = slice(row0, tq)
        if diagonal:
            row = lax.broadcasted_iota(jnp.int32, s.shape, 0)
            col = lax.broadcasted_iota(jnp.int32, s.shape, 1)
            s = jnp.where(col <= row, s, NEG)
        m_old = m_ref[rows, :]
        m_new = jnp.maximum(m_old, jnp.max(s, axis=-1, keepdims=True))
        m_ref[rows, :] = m_new
        p = jnp.concatenate([jnp.exp2(s[:, c * LANES:(c + 1) * LANES] - m_new) for c in range(tk // LANES)],
                            axis=1).astype(BF16)
        alpha = jnp.exp2(m_old - m_new)
        vext = jnp.concatenate([v_ref[pl.ds(k0, tk), :], ones], axis=1)
        acc_ref[rows, :] = acc_ref[rows, :] * jnp.concatenate([alpha, alpha], axis=1) + _dot(p, vext)

    s0_ref[...] = scores(0)

    def body(t, carry):
        for jj in range(nsub):
            s_bufs[(jj + 1) % 2][...] = scores(nsub * t + jj + 1)
            update(s_bufs[jj % 2][...], nsub * t + jj, 0, diagonal=False)
        return carry

    lax.fori_loop(0, qi, body, 0)
    s_cur = s0_ref[...]
    for d in range(nsub):
        s_next = scores(nsub * qi + d + 1, row0=(d + 1) * tk) if d + 1 < nsub else None
        update(s_cur, nsub * qi + d, d * tk, diagonal=True)
        s_cur = s_next
    acc = acc_ref[...]
    o_ref[...] = acc[:, :MLA_V_DIM] / acc[:, MLA_V_DIM:]


def _mla_call(qm, km, vm, n_heads):
    b, s, _ = qm.shape
    tq = MLA_TQ
    assert tq % (2 * MLA_TK) == 0
    return pl.pallas_call(
        _mla_kernel,
        grid=(b, n_heads, s // tq),
        in_specs=[pl.BlockSpec((None, tq, MXU_DIM), lambda bb, h, i: (bb, i, h)),
                  pl.BlockSpec((None, s, MXU_DIM), lambda bb, h, i: (bb, 0, h)),
                  pl.BlockSpec((None, s, MLA_V_DIM), lambda bb, h, i: (bb, 0, h))],
        out_specs=pl.BlockSpec((None, tq, MLA_V_DIM), lambda bb, h, i: (bb, i, h)),
        out_shape=jax.ShapeDtypeStruct((b, s, n_heads * MLA_V_DIM), F32),
        scratch_shapes=[pltpu.VMEM((tq, LANES), F32), pltpu.VMEM((tq, 2 * MLA_V_DIM), F32),
                        pltpu.VMEM((tq, MLA_TK), F32), pltpu.VMEM((tq, MLA_TK), F32)],
        compiler_params=pltpu.CompilerParams(dimension_semantics=("parallel", "parallel", "arbitrary"),
                                             vmem_limit_bytes=VMEM_LIMIT_BYTES),
        name="mla_attn",
    )(qm, km, vm)


def _rope_tables(seq):
    inv_freq = ROPE_BASE ** (-np.arange(0, MLA_ROPE, 2, dtype=np.float64) / MLA_ROPE)
    ang = np.arange(seq, dtype=np.float64)[:, None] * inv_freq[None, :]
    cos, sin = np.cos(ang), np.sin(ang)
    cos_t = np.concatenate([cos, cos] * (LANES // MLA_ROPE), axis=1)
    sin_t = np.concatenate([-sin, sin] * (LANES // MLA_ROPE), axis=1)
    return jnp.asarray(cos_t, F32), jnp.asarray(sin_t, F32)


def _pad_lanes(v, width):
    return jnp.pad(v, ((0, 0), (0, width - v.shape[1])))


def kernel(x, ffn1_norm, ffn1_w_gate, ffn1_w_up, ffn1_w_down, mix_norm, w_in, dil_q_norm, dil_k_norm,
           rel_bias, mla_q_a_norm, mla_w_q_b, mla_kv_a_norm, mla_w_kv_b, mla_q_norm, mla_k_norm,
           out_norm_dil, out_norm_mla, w_out, ffn2_norm, ffn2_w_gate, ffn2_w_up, ffn2_w_down):
    batch, seq, d_model = x.shape
    depth = w_in.shape[0]
    dil_width = out_norm_dil.shape[1]
    dil_heads = dil_width // DIL_HEAD_DIM
    q_rank = mla_q_a_norm.shape[1]
    kv_rank = mla_kv_a_norm.shape[1]
    qk_dim = MLA_NOPE + MLA_ROPE
    n_mla = mla_w_q_b.shape[2] // qk_dim
    assert rel_bias.shape == (dil_heads, REL_BUCKETS)
    assert w_in.shape[2] == 3 * dil_width + q_rank + kv_rank + MLA_ROPE
    assert seq % DIL_CHUNK == 0 and seq % MLA_TQ == 0 and seq % ROW_TILE == 0
    assert all(win // dil == DIL_BAND and DIL_RES % dil == 0 for win, dil in DIL_BRANCHES)
    assert DIL_BRANCHES[-1][1] == DIL_RES

    cos_t, sin_t = _rope_tables(seq)
    buckets, prev = _band_tables()
    dil_bias = _dil_bias_call(rel_bias, jnp.asarray(buckets), jnp.asarray(prev))
    dil_bias = dil_bias.reshape(dil_bias.shape[0], 2, dil_heads // 2, 2 * DIL_BAND, 2 * DIL_BAND)
    eh = jnp.asarray(np.kron(np.eye(dil_heads), np.ones((DIL_HEAD_DIM, DIL_HEAD_DIM))), BF16)
    perm = jnp.asarray(_residue_major_perm(ROW_TILE), BF16)

    x2d = x.reshape(batch * seq, d_model)
    row = lambda v: v.reshape(1, -1)
    for l in range(depth):
        x2d = _ffn_call(x2d, row(ffn1_norm[l]), ffn1_w_gate[l], ffn1_w_up[l], ffn1_w_down[l])

        wkpe = _pad_lanes(w_in[l][:, w_in.shape[2] - MLA_ROPE:], LANES).astype(BF16)
        wqb = mla_w_q_b[l].reshape(q_rank, n_mla, qk_dim)
        wqb = jnp.concatenate(
            [wqb[:, :, :MLA_NOPE].reshape(q_rank, n_mla * MLA_NOPE),
             jnp.pad(wqb[:, :, MLA_NOPE:], ((0, 0), (0, 0), (0, LANES - MLA_ROPE))).reshape(q_rank, n_mla * LANES)],
            axis=1).astype(BF16)
        wkvb = mla_w_kv_b[l].reshape(kv_rank, n_mla, MLA_NOPE + MLA_V_DIM)
        wkvb = jnp.concatenate([wkvb[:, :, :MLA_NOPE].reshape(kv_rank, n_mla * MLA_NOPE),
                                wkvb[:, :, MLA_NOPE:].reshape(kv_rank, n_mla * MLA_V_DIM)], axis=1).astype(BF16)
        gqa = row(jnp.tile(dil_q_norm[l], dil_heads)) * (LOG2E * DIL_HEAD_DIM ** -0.5)
        gka = row(jnp.tile(dil_k_norm[l], dil_heads))
        gq = row(mla_q_norm[l]) * (LOG2E * qk_dim ** -0.5)
        gk = row(mla_k_norm[l])
        gqn, gqr = gq[:, :MLA_NOPE], _pad_lanes(gq[:, MLA_NOPE:], LANES)
        gkn, gkr = gk[:, :MLA_NOPE], _pad_lanes(gk[:, MLA_NOPE:], LANES)

        consts = [row(mix_norm[l]), w_in[l], wkpe, eh, perm, gqa, gka, row(mla_q_a_norm[l]), wqb,
                  row(mla_kv_a_norm[l]), wkvb, gqn, gqr, gkn, gkr]
        qa, ka, va, qm, km, vm = _proj_call(x2d, batch, seq, consts, cos_t, sin_t,
                                            n_mla_heads=n_mla, dil_width=dil_width)

        shp = lambda a: a.reshape(batch, seq, a.shape[1])
        o_dil = _dil_call(qa, ka, va, dil_bias)
        o_mla = _mla_call(shp(qm), shp(km), shp(vm), n_mla)

        x2d = _out_ffn_call(x2d, o_dil.reshape(batch * seq, -1), o_mla.reshape(batch * seq, -1),
                            row(out_norm_dil[l]), row(out_norm_mla[l]), w_out[l].astype(BF16),
                            row(ffn2_norm[l]), ffn2_w_gate[l], ffn2_w_up[l], ffn2_w_down[l])
    return x2d.reshape(batch, seq, d_model)
```

```python
import functools

import numpy as np
import jax
import jax.numpy as jnp
from jax import lax
from jax.experimental import pallas as pl
from jax.experimental.pallas import tpu as pltpu

F32 = jnp.float32
BF16 = jnp.bfloat16

DIL_HEAD_DIM = 64
DIL_BRANCHES = ((128, 1), (512, 4), (2048, 16))
DIL_BAND = 128
DIL_RES = 16
MLA_NOPE = 128
MLA_ROPE = 64
MLA_V_DIM = 128
ROPE_BASE = 10000.0
REL_BUCKETS = 32
REL_MAX_DIST = 2048
FFN_RESID = 0.5
EPS = 1e-6

LANES = 128
SUBLANES = 8
MXU_DIM = 256
VMEM_LIMIT_BYTES = 56 * 1024 * 1024

NEG = -1e30
LOG2E = 1.4426950408889634

ROW_TILE = 512
FFN_TF = 256
DIL_CHUNK = DIL_BAND * DIL_RES
MLA_TQ = 2048
MLA_TK = 512


def _dot(a, b):
    return jnp.dot(a, b, preferred_element_type=F32)


def _dot_nt(a, b):
    return lax.dot_general(a, b, (((1,), (1,)), ((), ())), preferred_element_type=F32)


def _rms(x, g):
    return x * lax.rsqrt(jnp.mean(x * x, axis=-1, keepdims=True) + EPS) * g


def _residue_major_perm(rows):
    per = rows // DIL_RES
    dst = np.arange(rows)
    src = DIL_RES * (dst % per) + dst // per
    p = np.zeros((rows, rows), np.float32)
    p[dst, src] = 1.0
    return p


def _swiglu_residual(x, g_ref, wg_ref, wu_ref, wd_ref, act_ref):
    h = _rms(x, g_ref[...]).astype(BF16)
    d_ff = wg_ref.shape[1]
    for c in range(d_ff // FFN_TF):
        sl = slice(c * FFN_TF, (c + 1) * FFN_TF)
        gate = _dot(h, wg_ref[:, sl].astype(BF16))
        up = _dot(h, wu_ref[:, sl].astype(BF16))
        act_ref[:, sl] = (gate * jax.nn.sigmoid(gate) * up).astype(BF16)
    return x + FFN_RESID * _dot(act_ref[...], wd_ref[...].astype(BF16))


def _ffn_kernel(x_ref, g_ref, wg_ref, wu_ref, wd_ref, o_ref, act_ref):
    o_ref[...] = _swiglu_residual(x_ref[...], g_ref, wg_ref, wu_ref, wd_ref, act_ref)


def _out_ffn_kernel(x_ref, od_ref, om_ref, gd_ref, gm_ref, wo_ref, g_ref, wg_ref, wu_ref, wd_ref,
                    o_ref, act_ref):
    wd_rows = od_ref.shape[1]
    od = _rms(od_ref[...], gd_ref[...]).astype(BF16)
    om = _rms(om_ref[...], gm_ref[...]).astype(BF16)
    x2 = x_ref[...] + _dot(od, wo_ref[:wd_rows, :]) + _dot(om, wo_ref[wd_rows:, :])
    o_ref[...] = _swiglu_residual(x2, g_ref, wg_ref, wu_ref, wd_ref, act_ref)


def _resident(shape):
    return pl.BlockSpec(shape, lambda *_: (0,) * len(shape), pipeline_mode=pl.Buffered(1))


def _residue_major_block(seq, width):
    per = ROW_TILE // DIL_RES
    blocks_per_seq = seq // ROW_TILE
    return pl.BlockSpec((None, DIL_RES, per, width), lambda i: (i // blocks_per_seq, 0, i % blocks_per_seq, 0))


def _ffn_call(x2d, g, wg, wu, wd):
    m, d = x2d.shape
    f = wg.shape[1]
    row = pl.BlockSpec((ROW_TILE, d), lambda i: (i, 0))
    return pl.pallas_call(
        _ffn_kernel,
        grid=(m // ROW_TILE,),
        in_specs=[row, _resident((1, d)), _resident((d, f)), _resident((d, f)), _resident((f, d))],
        out_specs=row,
        out_shape=jax.ShapeDtypeStruct((m, d), F32),
        scratch_shapes=[pltpu.VMEM((ROW_TILE, f), BF16)],
        compiler_params=pltpu.CompilerParams(dimension_semantics=("parallel",),
                                             vmem_limit_bytes=VMEM_LIMIT_BYTES),
        name="ffn",
    )(x2d, g, wg, wu, wd)


def _out_ffn_call(x2d, o_dil, o_mla, gd, gm, wo, g, wg, wu, wd):
    m, d = x2d.shape
    f = wg.shape[1]
    wdil, wmla = o_dil.shape[1], o_mla.shape[1]
    row = pl.BlockSpec((ROW_TILE, d), lambda i: (i, 0))
    return pl.pallas_call(
        _out_ffn_kernel,
        grid=(m // ROW_TILE,),
        in_specs=[row,
                  pl.BlockSpec((ROW_TILE, wdil), lambda i: (i, 0)),
                  pl.BlockSpec((ROW_TILE, wmla), lambda i: (i, 0)),
                  _resident((1, wdil)), _resident((1, wmla)),
                  _resident((wdil + wmla, d)),
                  _resident((1, d)), _resident((d, f)), _resident((d, f)), _resident((f, d))],
        out_specs=row,
        out_shape=jax.ShapeDtypeStruct((m, d), F32),
        scratch_shapes=[pltpu.VMEM((ROW_TILE, f), BF16)],
        compiler_params=pltpu.CompilerParams(dimension_semantics=("parallel",),
                                             vmem_limit_bytes=VMEM_LIMIT_BYTES),
        name="out_ffn",
    )(x2d, o_dil, o_mla, gd, gm, wo, g, wg, wu, wd)


def _rope(x, cos, sin_signed):
    lane = lax.broadcasted_iota(jnp.int32, x.shape, 1)
    first_half = (lane % MLA_ROPE) < (MLA_ROPE // 2)
    partner = jnp.where(first_half,
                        pltpu.roll(x, LANES - MLA_ROPE // 2, 1),
                        pltpu.roll(x, MLA_ROPE // 2, 1))
    return x * cos + partner * sin_signed


def _proj_kernel(x_ref, gmix_ref, win_ref, wkpe_ref, eh_ref, perm_ref, gqa_ref, gka_ref,
                 gcq_ref, wqb_ref, gckv_ref, wkvb_ref,
                 gqn_ref, gqr_ref, gkn_ref, gkr_ref, cos_ref, sin_ref,
                 qa_ref, ka_ref, va_ref, qm_ref, km_ref, vm_ref, *, n_mla_heads, dil_width):
    h = _rms(x_ref[...], gmix_ref[...]).astype(BF16)
    w = dil_width
    o = 3 * w
    q_rank = gcq_ref.shape[1]
    kv_rank = gckv_ref.shape[1]
    qk_dim = MLA_NOPE + MLA_ROPE
    nh = n_mla_heads
    cos = cos_ref[...]
    sin = sin_ref[...]

    w_mla = jnp.concatenate([win_ref[:, o:o + q_rank + kv_rank].astype(BF16), wkpe_ref[...]], axis=1)
    pm = _dot(h, w_mla)
    cq = pm[:, :q_rank]
    ckv = pm[:, q_rank:q_rank + kv_rank]
    kpe = pm[:, q_rank + kv_rank:]

    qb = _dot(_rms(cq, gcq_ref[...]).astype(BF16), wqb_ref[...])
    for hd in range(nh):
        qn = qb[:, hd * MLA_NOPE:(hd + 1) * MLA_NOPE]
        qr = qb[:, nh * MLA_NOPE + hd * LANES:nh * MLA_NOPE + (hd + 1) * LANES]
        ss = jnp.sum(qn * qn, axis=-1, keepdims=True) + jnp.sum(qr * qr, axis=-1, keepdims=True)
        r = lax.rsqrt(ss * (1.0 / qk_dim) + EPS)
        qm_ref[:, hd * MXU_DIM:hd * MXU_DIM + LANES] = (qn * r * gqn_ref[...]).astype(BF16)
        qm_ref[:, hd * MXU_DIM + LANES:(hd + 1) * MXU_DIM] = _rope(qr * r * gqr_ref[...], cos, sin).astype(BF16)

    kvb = _dot(_rms(ckv, gckv_ref[...]).astype(BF16), wkvb_ref[...])
    ss_pe = jnp.sum(kpe * kpe, axis=-1, keepdims=True)
    kpe_rot = _rope(kpe * gkr_ref[...], cos, sin)
    for hd in range(nh):
        kn = kvb[:, hd * MLA_NOPE:(hd + 1) * MLA_NOPE]
        ss = jnp.sum(kn * kn, axis=-1, keepdims=True) + ss_pe
        r = lax.rsqrt(ss * (1.0 / qk_dim) + EPS)
        km_ref[:, hd * MXU_DIM:hd * MXU_DIM + LANES] = (kn * r * gkn_ref[...]).astype(BF16)
        km_ref[:, hd * MXU_DIM + LANES:(hd + 1) * MXU_DIM] = (kpe_rot * r).astype(BF16)
    vm_ref[...] = kvb[:, nh * MLA_NOPE:].astype(BF16)

    hp = _dot(perm_ref[...], h).astype(BF16)
    for c, (g_ref, dst) in enumerate(((gqa_ref, qa_ref), (gka_ref, ka_ref))):
        src = _dot(hp, win_ref[:, c * w:(c + 1) * w].astype(BF16))
        sq = (src * src).astype(BF16)
        ms = jnp.concatenate([_dot(sq[:, c0:c0 + MXU_DIM], eh_ref[...]) for c0 in range(0, w, MXU_DIM)],
                             axis=1) * (1.0 / DIL_HEAD_DIM)
        dst[...] = (src * lax.rsqrt(ms + EPS) * g_ref[...]).reshape(dst.shape)
    va_ref[...] = _dot(hp, win_ref[:, 2 * w:o].astype(BF16)).reshape(va_ref.shape)


def _proj_call(x2d, batch, seq, consts, cos, sin, *, n_mla_heads, dil_width):
    m, d = x2d.shape
    tm = ROW_TILE
    n_seq_blocks = seq // tm
    row = lambda width: pl.BlockSpec((tm, width), lambda i: (i, 0))
    pos = pl.BlockSpec((tm, LANES), lambda i: (i % n_seq_blocks, 0))
    mla_w = n_mla_heads * MXU_DIM
    dil_shape = jax.ShapeDtypeStruct((batch, DIL_RES, seq // DIL_RES, dil_width), F32)
    return pl.pallas_call(
        functools.partial(_proj_kernel, n_mla_heads=n_mla_heads, dil_width=dil_width),
        grid=(m // tm,),
        in_specs=[row(d)] + [_resident(c.shape) for c in consts] + [pos, pos],
        out_specs=[_residue_major_block(seq, dil_width)] * 3
        + [row(mla_w), row(mla_w), row(n_mla_heads * MLA_V_DIM)],
        out_shape=[dil_shape] * 3
        + [jax.ShapeDtypeStruct((m, mla_w), BF16)] * 2
        + [jax.ShapeDtypeStruct((m, n_mla_heads * MLA_V_DIM), BF16)],
        compiler_params=pltpu.CompilerParams(dimension_semantics=("parallel",),
                                             vmem_limit_bytes=VMEM_LIMIT_BYTES),
        name="proj",
    )(x2d, *consts, cos, sin)


def _t5_bucket(dist):
    max_exact = REL_BUCKETS // 2
    d = np.maximum(dist, 1).astype(np.float32)
    large = max_exact + (np.log(d / max_exact) / np.log(REL_MAX_DIST / max_exact)
                         * (REL_BUCKETS - max_exact)).astype(np.int32)
    large = np.minimum(large, REL_BUCKETS - 1)
    return np.where(dist < max_exact, dist, large).astype(np.int32)


def _band_tables():
    rho = np.arange(DIL_BAND)
    kap = np.arange(2 * DIL_BAND)
    buckets, prev = [], []
    for _, dil in DIL_BRANCHES:
        g = DIL_RES // dil
        run_q = DIL_BAND // g
        pos_q = g * (rho % run_q) + rho // run_q
        pos_k = g * (kap % (2 * run_q)) + kap // (2 * run_q) - DIL_BAND
        delta = pos_q[:, None] - pos_k[None, :]
        valid = (delta >= 0) & (delta <= DIL_BAND)
        buckets.append(np.where(valid, _t5_bucket(np.clip(delta, 0, None) * dil), -1))
        prev.append((pos_k < 0)[None, :])
    return np.stack(buckets).astype(np.int32), np.stack(prev).astype(np.int32)


def _dil_bias_kernel(rel_ref, bucket_ref, prev_ref, o_ref):
    bucket = bucket_ref[...]
    prev = prev_ref[...] > 0
    for hd in range(o_ref.shape[1]):
        bias = jnp.full(bucket.shape, NEG, F32)
        for b in range(REL_BUCKETS):
            bias = jnp.where(bucket == b, rel_ref[hd, b] * LOG2E, bias)
        o_ref[0, hd] = bias
        o_ref[1, hd] = jnp.where(prev, NEG, bias)


def _dil_bias_call(rel_bias, buckets, prev):
    nbr = buckets.shape[0]
    nh = rel_bias.shape[0]
    tile = buckets.shape[1:]
    return pl.pallas_call(
        _dil_bias_kernel,
        grid=(nbr,),
        in_specs=[pl.BlockSpec(memory_space=pltpu.SMEM),
                  pl.BlockSpec((None,) + tile, lambda b: (b, 0, 0)),
                  pl.BlockSpec((None, 1, tile[1]), lambda b: (b, 0, 0))],
        out_specs=pl.BlockSpec((None, 2, nh) + tile, lambda b: (b, 0, 0, 0, 0)),
        out_shape=jax.ShapeDtypeStruct((nbr, 2, nh) + tile, F32),
        name="dil_bias",
    )(rel_bias, buckets, prev)


def _band_tile(q, k, v, bias, head_a):
    zero = jnp.zeros_like(q)
    q2 = jnp.concatenate([jnp.where(head_a, q, zero), jnp.where(head_a, zero, q)], axis=0).astype(BF16)
    s = _dot_nt(q2, k.astype(BF16)) + bias
    m = jnp.max(s, axis=-1, keepdims=True)
    p = jnp.exp2(s - m).astype(BF16)
    ones = jnp.ones(v.shape, BF16)
    out = _dot(p, jnp.concatenate([v.astype(BF16), ones], axis=1))
    h = DIL_BAND
    num = jnp.where(head_a, out[:h, :LANES], out[h:, :LANES])
    den = jnp.where(head_a, out[:h, LANES:], out[h:, LANES:])
    mx = jnp.where(head_a, m[:h], m[h:])
    return num, den, mx


def _dil_kernel(q_ref, kp_ref, kc_ref, vp_ref, vc_ref, bias_ref, o_ref, num_scr, den_scr, max_scr):
    first_chunk = (pl.program_id(2) == 0).astype(jnp.int32)
    head_a = lax.broadcasted_iota(jnp.int32, (1, LANES), 1) < DIL_HEAD_DIM

    def tile(bi, dil, rbase, n):
        g = DIL_RES // dil
        run = DIL_BAND // g
        static_n = isinstance(n, int)
        q_rows = pl.ds(n * run, run) if static_n else pl.ds(pl.multiple_of(n * run, run), run)
        qs, ks, vs = [], [], []
        for u in range(g):
            res = rbase + dil * u
            qs.append(q_ref[res, q_rows, :])
            if static_n and n == 0:
                ks += [kp_ref[res, pl.ds(DIL_BAND - run, run), :], kc_ref[res, pl.ds(0, run), :]]
                vs += [vp_ref[res, pl.ds(DIL_BAND - run, run), :], vc_ref[res, pl.ds(0, run), :]]
            else:
                k_rows = (pl.ds((n - 1) * run, 2 * run) if static_n
                          else pl.ds(pl.multiple_of((n - 1) * run, run), 2 * run))
                ks.append(kc_ref[res, k_rows, :])
                vs.append(vc_ref[res, k_rows, :])
        cat = lambda parts: parts[0] if len(parts) == 1 else jnp.concatenate(parts, axis=0)
        variant = first_chunk if (static_n and n == 0) else 0
        num, den, mx = _band_tile(cat(qs), cat(ks), cat(vs), bias_ref[bi, variant], head_a)
        if g == 1:
            return num, den, mx
        for u in range(g):
            res = rbase + dil * u
            part = slice(u * run, (u + 1) * run)
            num_scr[bi, res, q_rows, :] = num[part]
            den_scr[bi, res, q_rows, :] = den[part]
            max_scr[bi, res, q_rows, :] = mx[part]

    for n in range(DIL_RES):
        tile(0, 1, 0, n)
    for r4 in range(4):
        for n in range(4):
            tile(1, 4, r4, n)
    stored = len(DIL_BRANCHES) - 1
    for r in range(DIL_RES):
        num_r, den_r, max_r = tile(stored, DIL_RES, r, 0)
        ms = [max_scr[bi, r] for bi in range(stored)] + [max_r]
        nums = [num_scr[bi, r] for bi in range(stored)] + [num_r]
        dens = [den_scr[bi, r] for bi in range(stored)] + [den_r]
        m_all = functools.reduce(jnp.maximum, ms)
        es = [jnp.exp2(mb - m_all) for mb in ms]
        num = sum(e * x for e, x in zip(es, nums))
        den = sum(e * x for e, x in zip(es, dens))
        o_ref[pl.ds(r, DIL_BAND, stride=DIL_RES), :] = num / den


def _dil_call(qa, ka, va, bias):
    b, _, per, w = qa.shape
    pairs = w // LANES
    nbr = bias.shape[0]
    blk = (None, DIL_RES, DIL_BAND, LANES)
    cur = pl.BlockSpec(blk, lambda bb, p, c: (bb, 0, c, p))
    prev = pl.BlockSpec(blk, lambda bb, p, c: (bb, 0, jnp.maximum(c - 1, 0), p))
    scr = pltpu.VMEM((nbr - 1, DIL_RES, DIL_BAND, LANES), F32)
    return pl.pallas_call(
        _dil_kernel,
        grid=(b, pairs, per // DIL_BAND),
        in_specs=[cur, prev, cur, prev, cur,
                  pl.BlockSpec((nbr, 2, None, 2 * DIL_BAND, 2 * DIL_BAND), lambda bb, p, c: (0, 0, p, 0, 0))],
        out_specs=pl.BlockSpec((None, DIL_CHUNK, LANES), lambda bb, p, c: (bb, c, p)),
        out_shape=jax.ShapeDtypeStruct((b, per * DIL_RES, w), F32),
        scratch_shapes=[scr, scr, scr],
        compiler_params=pltpu.CompilerParams(dimension_semantics=("parallel", "parallel", "arbitrary"),
                                             vmem_limit_bytes=VMEM_LIMIT_BYTES),
        name="dil_attn",
    )(qa, ka, ka, va, va, bias)


def _mla_kernel(q_ref, k_ref, v_ref, o_ref, m_ref, acc_ref, s0_ref, s1_ref):
    tq = m_ref.shape[0]
    tk = MLA_TK
    nsub = tq // tk
    qi = pl.program_id(2)
    q0 = pl.multiple_of(qi * tq, tq)
    m_ref[...] = jnp.full(m_ref.shape, NEG, F32)
    acc_ref[...] = jnp.zeros(acc_ref.shape, F32)
    ones = jnp.ones((tk, LANES), BF16)
    s_bufs = (s0_ref, s1_ref)

    def scores(kblk, row0=0, tile_start=q0):
        k0 = pl.multiple_of(kblk * tk, tk)
        return _dot_nt(q_ref[pl.ds(tile_start + row0, tq - row0), :], k_ref[pl.ds(k0, tk), :])

    def update(s, kblk, row0, diagonal):
        k0 = pl.multiple_of(kblk * tk, tk)
        rows = slice(row0, tq)
        if diagonal:
            row = lax.broadcasted_iota(jnp.int32, s.shape, 0)
            col = lax.broadcasted_iota(jnp.int32, s.shape, 1)
            s = jnp.where(col <= row, s, NEG)
        m_old = m_ref[rows, :]
        m_new = jnp.maximum(m_old, jnp.max(s, axis=-1, keepdims=True))
        m_ref[rows, :] = m_new
        p = jnp.concatenate([jnp.exp2(s[:, c * LANES:(c + 1) * LANES] - m_new) for c in range(tk // LANES)],
                            axis=1).astype(BF16)
        alpha = jnp.exp2(m_old - m_new)
        vext = jnp.concatenate([v_ref[pl.ds(k0, tk), :], ones], axis=1)
        acc_ref[rows, :] = acc_ref[rows, :] * jnp.concatenate([alpha, alpha], axis=1) + _dot(p, vext)

    @pl.when(qi == 0)
    def _():
        s0_ref[...] = scores(0)

    def body(t, carry):
        for jj in range(nsub):
            s_bufs[(jj + 1) % 2][...] = scores(nsub * t + jj + 1)
            update(s_bufs[jj % 2][...], nsub * t + jj, 0, diagonal=False)
        return carry

    lax.fori_loop(0, qi, body, 0)
    next_start = pl.multiple_of(jnp.minimum(qi + 1, pl.num_programs(2) - 1) * tq, tq)
    s_cur = s0_ref[...]
    for d in range(nsub):
        if d + 1 < nsub:
            s_next = scores(nsub * qi + d + 1, row0=(d + 1) * tk)
        else:
            s_next = scores(0, tile_start=next_start)
        update(s_cur, nsub * qi + d, d * tk, diagonal=True)
        s_cur = s_next
    s0_ref[...] = s_cur
    acc = acc_ref[...]
    o_ref[...] = acc[:, :MLA_V_DIM] / acc[:, MLA_V_DIM:]


def _mla_call(qm, km, vm, n_heads):
    b, s, _ = qm.shape
    tq = MLA_TQ
    assert tq % (2 * MLA_TK) == 0
    return pl.pallas_call(
        _mla_kernel,
        grid=(b, n_heads, s // tq),
        in_specs=[pl.BlockSpec((None, s, MXU_DIM), lambda bb, h, i: (bb, 0, h)),
                  pl.BlockSpec((None, s, MXU_DIM), lambda bb, h, i: (bb, 0, h)),
                  pl.BlockSpec((None, s, MLA_V_DIM), lambda bb, h, i: (bb, 0, h))],
        out_specs=pl.BlockSpec((None, tq, MLA_V_DIM), lambda bb, h, i: (bb, i, h)),
        out_shape=jax.ShapeDtypeStruct((b, s, n_heads * MLA_V_DIM), F32),
        scratch_shapes=[pltpu.VMEM((tq, LANES), F32), pltpu.VMEM((tq, 2 * MLA_V_DIM), F32),
                        pltpu.VMEM((tq, MLA_TK), F32), pltpu.VMEM((tq, MLA_TK), F32)],
        compiler_params=pltpu.CompilerParams(dimension_semantics=("arbitrary", "arbitrary", "arbitrary"),
                                             vmem_limit_bytes=VMEM_LIMIT_BYTES),
        name="mla_attn",
    )(qm, km, vm)


def _rope_tables(seq):
    inv_freq = ROPE_BASE ** (-np.arange(0, MLA_ROPE, 2, dtype=np.float64) / MLA_ROPE)
    ang = np.arange(seq, dtype=np.float64)[:, None] * inv_freq[None, :]
    cos, sin = np.cos(ang), np.sin(ang)
    cos_t = np.concatenate([cos, cos] * (LANES // MLA_ROPE), axis=1)
    sin_t = np.concatenate([-sin, sin] * (LANES // MLA_ROPE), axis=1)
    return jnp.asarray(cos_t, F32), jnp.asarray(sin_t, F32)


def _pad_lanes(v, width):
    return jnp.pad(v, ((0, 0), (0, width - v.shape[1])))


def kernel(x, ffn1_norm, ffn1_w_gate, ffn1_w_up, ffn1_w_down, mix_norm, w_in, dil_q_norm, dil_k_norm,
           rel_bias, mla_q_a_norm, mla_w_q_b, mla_kv_a_norm, mla_w_kv_b, mla_q_norm, mla_k_norm,
           out_norm_dil, out_norm_mla, w_out, ffn2_norm, ffn2_w_gate, ffn2_w_up, ffn2_w_down):
    batch, seq, d_model = x.shape
    depth = w_in.shape[0]
    dil_width = out_norm_dil.shape[1]
    dil_heads = dil_width // DIL_HEAD_DIM
    q_rank = mla_q_a_norm.shape[1]
    kv_rank = mla_kv_a_norm.shape[1]
    qk_dim = MLA_NOPE + MLA_ROPE
    n_mla = mla_w_q_b.shape[2] // qk_dim
    assert rel_bias.shape == (dil_heads, REL_BUCKETS)
    assert w_in.shape[2] == 3 * dil_width + q_rank + kv_rank + MLA_ROPE
    assert seq % DIL_CHUNK == 0 and seq % MLA_TQ == 0 and seq % ROW_TILE == 0
    assert all(win // dil == DIL_BAND and DIL_RES % dil == 0 for win, dil in DIL_BRANCHES)
    assert DIL_BRANCHES[-1][1] == DIL_RES

    cos_t, sin_t = _rope_tables(seq)
    buckets, prev = _band_tables()
    dil_bias = _dil_bias_call(rel_bias, jnp.asarray(buckets), jnp.asarray(prev))
    dil_bias = dil_bias.reshape(dil_bias.shape[0], 2, dil_heads // 2, 2 * DIL_BAND, 2 * DIL_BAND)
    eh = jnp.asarray(np.kron(np.eye(MXU_DIM // DIL_HEAD_DIM), np.ones((DIL_HEAD_DIM, DIL_HEAD_DIM))), BF16)
    perm = jnp.asarray(_residue_major_perm(ROW_TILE), BF16)

    x2d = x.reshape(batch * seq, d_model)
    row = lambda v: v.reshape(1, -1)
    for l in range(depth):
        x2d = _ffn_call(x2d, row(ffn1_norm[l]), ffn1_w_gate[l], ffn1_w_up[l], ffn1_w_down[l])

        wkpe = _pad_lanes(w_in[l][:, w_in.shape[2] - MLA_ROPE:], LANES).astype(BF16)
        wqb = mla_w_q_b[l].reshape(q_rank, n_mla, qk_dim)
        wqb = jnp.concatenate(
            [wqb[:, :, :MLA_NOPE].reshape(q_rank, n_mla * MLA_NOPE),
             jnp.pad(wqb[:, :, MLA_NOPE:], ((0, 0), (0, 0), (0, LANES - MLA_ROPE))).reshape(q_rank, n_mla * LANES)],
            axis=1).astype(BF16)
        wkvb = mla_w_kv_b[l].reshape(kv_rank, n_mla, MLA_NOPE + MLA_V_DIM)
        wkvb = jnp.concatenate([wkvb[:, :, :MLA_NOPE].reshape(kv_rank, n_mla * MLA_NOPE),
                                wkvb[:, :, MLA_NOPE:].reshape(kv_rank, n_mla * MLA_V_DIM)], axis=1).astype(BF16)
        gqa = row(jnp.tile(dil_q_norm[l], dil_heads)) * (LOG2E * DIL_HEAD_DIM ** -0.5)
        gka = row(jnp.tile(dil_k_norm[l], dil_heads))
        gq = row(mla_q_norm[l]) * (LOG2E * qk_dim ** -0.5)
        gk = row(mla_k_norm[l])
        gqn, gqr = gq[:, :MLA_NOPE], _pad_lanes(gq[:, MLA_NOPE:], LANES)
        gkn, gkr = gk[:, :MLA_NOPE], _pad_lanes(gk[:, MLA_NOPE:], LANES)

        consts = [row(mix_norm[l]), w_in[l], wkpe, eh, perm, gqa, gka, row(mla_q_a_norm[l]), wqb,
                  row(mla_kv_a_norm[l]), wkvb, gqn, gqr, gkn, gkr]
        qa, ka, va, qm, km, vm = _proj_call(x2d, batch, seq, consts, cos_t, sin_t,
                                            n_mla_heads=n_mla, dil_width=dil_width)

        shp = lambda a: a.reshape(batch, seq, a.shape[1])
        o_dil = _dil_call(qa, ka, va, dil_bias)
        o_mla = _mla_call(shp(qm), shp(km), shp(vm), n_mla)

        x2d = _out_ffn_call(x2d, o_dil.reshape(batch * seq, -1), o_mla.reshape(batch * seq, -1),
                            row(out_norm_dil[l]), row(out_norm_mla[l]), w_out[l].astype(BF16),
                            row(ffn2_norm[l]), ffn2_w_gate[l], ffn2_w_up[l], ffn2_w_down[l])
    return x2d.reshape(batch, seq, d_model)
```

```python
import functools

import numpy as np
import jax
import jax.numpy as jnp
from jax import lax
from jax.experimental import pallas as pl
from jax.experimental.pallas import tpu as pltpu

F32 = jnp.float32
BF16 = jnp.bfloat16

DIL_HEAD_DIM = 64
DIL_BRANCHES = ((128, 1), (512, 4), (2048, 16))
DIL_BAND = 128
DIL_RES = 16
MLA_NOPE = 128
MLA_ROPE = 64
MLA_V_DIM = 128
ROPE_BASE = 10000.0
REL_BUCKETS = 32
REL_MAX_DIST = 2048
FFN_RESID = 0.5
EPS = 1e-6

LANES = 128
SUBLANES = 8
MXU_DIM = 256
VMEM_LIMIT_BYTES = 56 * 1024 * 1024

NEG = -1e30
LOG2E = 1.4426950408889634

ROW_TILE = 512
FFN_TF = 256
DIL_CHUNK = DIL_BAND * DIL_RES
MLA_TQ = 2048
MLA_TK = 512


def _dot(a, b):
    return jnp.dot(a, b, preferred_element_type=F32)


def _dot_nt(a, b):
    return lax.dot_general(a, b, (((1,), (1,)), ((), ())), preferred_element_type=F32)


def _rms(x, g):
    return x * lax.rsqrt(jnp.mean(x * x, axis=-1, keepdims=True) + EPS) * g


def _residue_major_perm(rows):
    per = rows // DIL_RES
    dst = np.arange(rows)
    src = DIL_RES * (dst % per) + dst // per
    p = np.zeros((rows, rows), np.float32)
    p[dst, src] = 1.0
    return p


def _swiglu_residual(x, g_ref, wg_ref, wu_ref, wd_ref, act_ref):
    h = _rms(x, g_ref[...]).astype(BF16)
    d_ff = wg_ref.shape[1]
    for c in range(d_ff // FFN_TF):
        sl = slice(c * FFN_TF, (c + 1) * FFN_TF)
        gate = _dot(h, wg_ref[:, sl].astype(BF16))
        up = _dot(h, wu_ref[:, sl].astype(BF16))
        act_ref[:, sl] = (gate * jax.nn.sigmoid(gate) * up).astype(BF16)
    return x + FFN_RESID * _dot(act_ref[...], wd_ref[...].astype(BF16))


def _ffn_kernel(x_ref, g_ref, wg_ref, wu_ref, wd_ref, o_ref, act_ref):
    o_ref[...] = _swiglu_residual(x_ref[...], g_ref, wg_ref, wu_ref, wd_ref, act_ref)


def _out_ffn_kernel(x_ref, od_ref, om_ref, gd_ref, gm_ref, wo_ref, g_ref, wg_ref, wu_ref, wd_ref,
                    o_ref, act_ref):
    wd_rows = od_ref.shape[1]
    od = _rms(od_ref[...], gd_ref[...]).astype(BF16)
    om = _rms(om_ref[...], gm_ref[...]).astype(BF16)
    x2 = x_ref[...] + _dot(od, wo_ref[:wd_rows, :]) + _dot(om, wo_ref[wd_rows:, :])
    o_ref[...] = _swiglu_residual(x2, g_ref, wg_ref, wu_ref, wd_ref, act_ref)


def _resident(shape):
    return pl.BlockSpec(shape, lambda *_: (0,) * len(shape), pipeline_mode=pl.Buffered(1))


def _residue_major_block(seq, width):
    per = ROW_TILE // DIL_RES
    blocks_per_seq = seq // ROW_TILE
    return pl.BlockSpec((None, DIL_RES, per, width), lambda i: (i // blocks_per_seq, 0, i % blocks_per_seq, 0))


def _ffn_call(x2d, g, wg, wu, wd):
    m, d = x2d.shape
    f = wg.shape[1]
    row = pl.BlockSpec((ROW_TILE, d), lambda i: (i, 0))
    return pl.pallas_call(
        _ffn_kernel,
        grid=(m // ROW_TILE,),
        in_specs=[row, _resident((1, d)), _resident((d, f)), _resident((d, f)), _resident((f, d))],
        out_specs=row,
        out_shape=jax.ShapeDtypeStruct((m, d), F32),
        scratch_shapes=[pltpu.VMEM((ROW_TILE, f), BF16)],
        compiler_params=pltpu.CompilerParams(dimension_semantics=("parallel",),
                                             vmem_limit_bytes=VMEM_LIMIT_BYTES),
        name="ffn",
    )(x2d, g, wg, wu, wd)


def _out_ffn_call(x2d, o_dil, o_mla, gd, gm, wo, g, wg, wu, wd):
    m, d = x2d.shape
    f = wg.shape[1]
    wdil, wmla = o_dil.shape[1], o_mla.shape[1]
    row = pl.BlockSpec((ROW_TILE, d), lambda i: (i, 0))
    return pl.pallas_call(
        _out_ffn_kernel,
        grid=(m // ROW_TILE,),
        in_specs=[row,
                  pl.BlockSpec((ROW_TILE, wdil), lambda i: (i, 0)),
                  pl.BlockSpec((ROW_TILE, wmla), lambda i: (i, 0)),
                  _resident((1, wdil)), _resident((1, wmla)),
                  _resident((wdil + wmla, d)),
                  _resident((1, d)), _resident((d, f)), _resident((d, f)), _resident((f, d))],
        out_specs=row,
        out_shape=jax.ShapeDtypeStruct((m, d), F32),
        scratch_shapes=[pltpu.VMEM((ROW_TILE, f), BF16)],
        compiler_params=pltpu.CompilerParams(dimension_semantics=("parallel",),
                                             vmem_limit_bytes=VMEM_LIMIT_BYTES),
        name="out_ffn",
    )(x2d, o_dil, o_mla, gd, gm, wo, g, wg, wu, wd)


def _rope(x, cos, sin_signed):
    lane = lax.broadcasted_iota(jnp.int32, x.shape, 1)
    first_half = (lane % MLA_ROPE) < (MLA_ROPE // 2)
    partner = jnp.where(first_half,
                        pltpu.roll(x, LANES - MLA_ROPE // 2, 1),
                        pltpu.roll(x, MLA_ROPE // 2, 1))
    return x * cos + partner * sin_signed


def _proj_kernel(x_ref, gmix_ref, wint_ref, wkpet_ref, eh_ref, perm_ref, gqa_ref, gka_ref,
                 gcq_ref, wqb_ref, gckv_ref, wkvb_ref,
                 gqn_ref, gqr_ref, gkn_ref, gkr_ref, cos_ref, sin_ref,
                 qa_ref, ka_ref, va_ref, qm_ref, km_ref, vm_ref, *, n_mla_heads, dil_width):
    h = _rms(x_ref[...], gmix_ref[...]).astype(BF16)
    w = dil_width
    o = 3 * w
    q_rank = gcq_ref.shape[1]
    kv_rank = gckv_ref.shape[1]
    qk_dim = MLA_NOPE + MLA_ROPE
    nh = n_mla_heads
    cos = cos_ref[...]
    sin = sin_ref[...]

    w_mla = jnp.concatenate([wint_ref[o:o + q_rank + kv_rank, :].astype(BF16), wkpet_ref[...]], axis=0)
    pm = _dot_nt(h, w_mla)
    cq = pm[:, :q_rank]
    ckv = pm[:, q_rank:q_rank + kv_rank]
    kpe = pm[:, q_rank + kv_rank:]

    qb = _dot(_rms(cq, gcq_ref[...]).astype(BF16), wqb_ref[...])
    for hd in range(nh):
        qn = qb[:, hd * MLA_NOPE:(hd + 1) * MLA_NOPE]
        qr = qb[:, nh * MLA_NOPE + hd * LANES:nh * MLA_NOPE + (hd + 1) * LANES]
        ss = jnp.sum(qn * qn, axis=-1, keepdims=True) + jnp.sum(qr * qr, axis=-1, keepdims=True)
        r = lax.rsqrt(ss * (1.0 / qk_dim) + EPS)
        qm_ref[:, hd * MXU_DIM:hd * MXU_DIM + LANES] = (qn * r * gqn_ref[...]).astype(BF16)
        qm_ref[:, hd * MXU_DIM + LANES:(hd + 1) * MXU_DIM] = _rope(qr * r * gqr_ref[...], cos, sin).astype(BF16)

    kvb = _dot(_rms(ckv, gckv_ref[...]).astype(BF16), wkvb_ref[...])
    ss_pe = jnp.sum(kpe * kpe, axis=-1, keepdims=True)
    kpe_rot = _rope(kpe * gkr_ref[...], cos, sin)
    for hd in range(nh):
        kn = kvb[:, hd * MLA_NOPE:(hd + 1) * MLA_NOPE]
        ss = jnp.sum(kn * kn, axis=-1, keepdims=True) + ss_pe
        r = lax.rsqrt(ss * (1.0 / qk_dim) + EPS)
        km_ref[:, hd * MXU_DIM:hd * MXU_DIM + LANES] = (kn * r * gkn_ref[...]).astype(BF16)
        km_ref[:, hd * MXU_DIM + LANES:(hd + 1) * MXU_DIM] = (kpe_rot * r).astype(BF16)
    vm_ref[...] = kvb[:, nh * MLA_NOPE:].astype(BF16)

    hp = _dot(perm_ref[...], h).astype(BF16)
    for c, (g_ref, dst) in enumerate(((gqa_ref, qa_ref), (gka_ref, ka_ref))):
        src = _dot_nt(hp, wint_ref[c * w:(c + 1) * w, :].astype(BF16))
        sq = (src * src).astype(BF16)
        ms = jnp.concatenate([_dot(sq[:, c0:c0 + MXU_DIM], eh_ref[...]) for c0 in range(0, w, MXU_DIM)],
                             axis=1) * (1.0 / DIL_HEAD_DIM)
        dst[...] = (src * lax.rsqrt(ms + EPS) * g_ref[...]).reshape(dst.shape)
    va_ref[...] = _dot_nt(hp, wint_ref[2 * w:o, :].astype(BF16)).reshape(va_ref.shape)


def _proj_call(x2d, batch, seq, consts, cos, sin, *, n_mla_heads, dil_width):
    m, d = x2d.shape
    tm = ROW_TILE
    n_seq_blocks = seq // tm
    row = lambda width: pl.BlockSpec((tm, width), lambda i: (i, 0))
    pos = pl.BlockSpec((tm, LANES), lambda i: (i % n_seq_blocks, 0))
    mla_w = n_mla_heads * MXU_DIM
    dil_shape = jax.ShapeDtypeStruct((batch, DIL_RES, seq // DIL_RES, dil_width), F32)
    return pl.pallas_call(
        functools.partial(_proj_kernel, n_mla_heads=n_mla_heads, dil_width=dil_width),
        grid=(m // tm,),
        in_specs=[row(d)] + [_resident(c.shape) for c in consts] + [pos, pos],
        out_specs=[_residue_major_block(seq, dil_width)] * 3
        + [row(mla_w), row(mla_w), row(n_mla_heads * MLA_V_DIM)],
        out_shape=[dil_shape] * 3
        + [jax.ShapeDtypeStruct((m, mla_w), BF16)] * 2
        + [jax.ShapeDtypeStruct((m, n_mla_heads * MLA_V_DIM), BF16)],
        compiler_params=pltpu.CompilerParams(dimension_semantics=("parallel",),
                                             vmem_limit_bytes=VMEM_LIMIT_BYTES),
        name="proj",
    )(x2d, *consts, cos, sin)


def _t5_bucket(dist):
    max_exact = REL_BUCKETS // 2
    d = np.maximum(dist, 1).astype(np.float32)
    large = max_exact + (np.log(d / max_exact) / np.log(REL_MAX_DIST / max_exact)
                         * (REL_BUCKETS - max_exact)).astype(np.int32)
    large = np.minimum(large, REL_BUCKETS - 1)
    return np.where(dist < max_exact, dist, large).astype(np.int32)


def _band_tables():
    rho = np.arange(DIL_BAND)
    kap = np.arange(2 * DIL_BAND)
    buckets, prev = [], []
    for _, dil in DIL_BRANCHES:
        g = DIL_RES // dil
        run_q = DIL_BAND // g
        pos_q = g * (rho % run_q) + rho // run_q
        pos_k = g * (kap % (2 * run_q)) + kap // (2 * run_q) - DIL_BAND
        delta = pos_q[:, None] - pos_k[None, :]
        valid = (delta >= 0) & (delta <= DIL_BAND)
        buckets.append(np.where(valid, _t5_bucket(np.clip(delta, 0, None) * dil), -1))
        prev.append((pos_k < 0)[None, :])
    return np.stack(buckets).astype(np.int32), np.stack(prev).astype(np.int32)


def _dil_bias_kernel(rel_ref, bucket_ref, prev_ref, o_ref):
    bucket = bucket_ref[...]
    prev = prev_ref[...] > 0
    for hd in range(o_ref.shape[1]):
        bias = jnp.full(bucket.shape, NEG, F32)
        for b in range(REL_BUCKETS):
            bias = jnp.where(bucket == b, rel_ref[hd, b] * LOG2E, bias)
        o_ref[0, hd] = bias
        o_ref[1, hd] = jnp.where(prev, NEG, bias)


def _dil_bias_call(rel_bias, buckets, prev):
    nbr = buckets.shape[0]
    nh = rel_bias.shape[0]
    tile = buckets.shape[1:]
    return pl.pallas_call(
        _dil_bias_kernel,
        grid=(nbr,),
        in_specs=[pl.BlockSpec(memory_space=pltpu.SMEM),
                  pl.BlockSpec((None,) + tile, lambda b: (b, 0, 0)),
                  pl.BlockSpec((None, 1, tile[1]), lambda b: (b, 0, 0))],
        out_specs=pl.BlockSpec((None, 2, nh) + tile, lambda b: (b, 0, 0, 0, 0)),
        out_shape=jax.ShapeDtypeStruct((nbr, 2, nh) + tile, F32),
        name="dil_bias",
    )(rel_bias, buckets, prev)


def _band_tile(q, k, v, bias, head_a):
    zero = jnp.zeros_like(q)
    q2 = jnp.concatenate([jnp.where(head_a, q, zero), jnp.where(head_a, zero, q)], axis=0).astype(BF16)
    s = _dot_nt(q2, k.astype(BF16)) + bias
    m = jnp.max(s, axis=-1, keepdims=True)
    p = jnp.exp2(s - m).astype(BF16)
    ones = jnp.ones(v.shape, BF16)
    out = _dot(p, jnp.concatenate([v.astype(BF16), ones], axis=1))
    h = DIL_BAND
    num = jnp.where(head_a, out[:h, :LANES], out[h:, :LANES])
    den = jnp.where(head_a, out[:h, LANES:], out[h:, LANES:])
    mx = jnp.where(head_a, m[:h], m[h:])
    return num, den, mx


def _dil_kernel(q_ref, kp_ref, kc_ref, vp_ref, vc_ref, bias_ref, o_ref, num_scr, den_scr, max_scr):
    first_chunk = (pl.program_id(2) == 0).astype(jnp.int32)
    head_a = lax.broadcasted_iota(jnp.int32, (1, LANES), 1) < DIL_HEAD_DIM

    def tile(bi, dil, rbase, n):
        g = DIL_RES // dil
        run = DIL_BAND // g
        static_n = isinstance(n, int)
        q_rows = pl.ds(n * run, run) if static_n else pl.ds(pl.multiple_of(n * run, run), run)
        qs, ks, vs = [], [], []
        for u in range(g):
            res = rbase + dil * u
            qs.append(q_ref[res, q_rows, :])
            if static_n and n == 0:
                ks += [kp_ref[res, pl.ds(DIL_BAND - run, run), :], kc_ref[res, pl.ds(0, run), :]]
                vs += [vp_ref[res, pl.ds(DIL_BAND - run, run), :], vc_ref[res, pl.ds(0, run), :]]
            else:
                k_rows = (pl.ds((n - 1) * run, 2 * run) if static_n
                          else pl.ds(pl.multiple_of((n - 1) * run, run), 2 * run))
                ks.append(kc_ref[res, k_rows, :])
                vs.append(vc_ref[res, k_rows, :])
        cat = lambda parts: parts[0] if len(parts) == 1 else jnp.concatenate(parts, axis=0)
        variant = first_chunk if (static_n and n == 0) else 0
        num, den, mx = _band_tile(cat(qs), cat(ks), cat(vs), bias_ref[bi, variant], head_a)
        if g == 1:
            return num, den, mx
        for u in range(g):
            res = rbase + dil * u
            part = slice(u * run, (u + 1) * run)
            num_scr[bi, res, q_rows, :] = num[part]
            den_scr[bi, res, q_rows, :] = den[part]
            max_scr[bi, res, q_rows, :] = mx[part]

    for n in range(DIL_RES):
        tile(0, 1, 0, n)
    for r4 in range(4):
        for n in range(4):
            tile(1, 4, r4, n)
    stored = len(DIL_BRANCHES) - 1
    for r in range(DIL_RES):
        num_r, den_r, max_r = tile(stored, DIL_RES, r, 0)
        ms = [max_scr[bi, r] for bi in range(stored)] + [max_r]
        nums = [num_scr[bi, r] for bi in range(stored)] + [num_r]
        dens = [den_scr[bi, r] for bi in range(stored)] + [den_r]
        m_all = functools.reduce(jnp.maximum, ms)
        es = [jnp.exp2(mb - m_all) for mb in ms]
        num = sum(e * x for e, x in zip(es, nums))
        den = sum(e * x for e, x in zip(es, dens))
        o_ref[pl.ds(r, DIL_BAND, stride=DIL_RES), :] = num / den


def _dil_call(qa, ka, va, bias):
    b, _, per, w = qa.shape
    pairs = w // LANES
    nbr = bias.shape[0]
    blk = (None, DIL_RES, DIL_BAND, LANES)
    cur = pl.BlockSpec(blk, lambda bb, p, c: (bb, 0, c, p))
    prev = pl.BlockSpec(blk, lambda bb, p, c: (bb, 0, jnp.maximum(c - 1, 0), p))
    scr = pltpu.VMEM((nbr - 1, DIL_RES, DIL_BAND, LANES), F32)
    return pl.pallas_call(
        _dil_kernel,
        grid=(b, pairs, per // DIL_BAND),
        in_specs=[cur, prev, cur, prev, cur,
                  pl.BlockSpec((nbr, 2, None, 2 * DIL_BAND, 2 * DIL_BAND), lambda bb, p, c: (0, 0, p, 0, 0))],
        out_specs=pl.BlockSpec((None, DIL_CHUNK, LANES), lambda bb, p, c: (bb, c, p)),
        out_shape=jax.ShapeDtypeStruct((b, per * DIL_RES, w), F32),
        scratch_shapes=[scr, scr, scr],
        compiler_params=pltpu.CompilerParams(dimension_semantics=("parallel", "parallel", "arbitrary"),
                                             vmem_limit_bytes=VMEM_LIMIT_BYTES),
        name="dil_attn",
    )(qa, ka, ka, va, va, bias)


def _mla_kernel(q_ref, k_ref, v_ref, o_ref, m_ref, acc_ref, s0_ref, s1_ref):
    tq = q_ref.shape[0]
    tk = MLA_TK
    nsub = tq // tk
    qi = pl.program_id(2)
    m_ref[...] = jnp.full(m_ref.shape, NEG, F32)
    acc_ref[...] = jnp.zeros(acc_ref.shape, F32)
    ones = jnp.ones((tk, LANES), BF16)
    s_bufs = (s0_ref, s1_ref)

    def scores(kblk, row0=0):
        k0 = pl.multiple_of(kblk * tk, tk)
        return _dot_nt(q_ref[row0:, :], k_ref[pl.ds(k0, tk), :])

    def update(s, kblk, row0, diagonal):
        k0 = pl.multiple_of(kblk * tk, tk)
        rows = slice(row0, tq)
        if diagonal:
            row = lax.broadcasted_iota(jnp.int32, s.shape, 0)
            col = lax.broadcasted_iota(jnp.int32, s.shape, 1)
            s = jnp.where(col <= row, s, NEG)
        m_old = m_ref[rows, :]
        m_new = jnp.maximum(m_old, jnp.max(s, axis=-1, keepdims=True))
        m_ref[rows, :] = m_new
        p = jnp.concatenate([jnp.exp2(s[:, c * LANES:(c + 1) * LANES] - m_new) for c in range(tk // LANES)],
                            axis=1).astype(BF16)
        alpha = jnp.exp2(m_old - m_new)
        vext = jnp.concatenate([v_ref[pl.ds(k0, tk), :], ones], axis=1)
        acc_ref[rows, :] = acc_ref[rows, :] * jnp.concatenate([alpha, alpha], axis=1) + _dot(p, vext)

    s0_ref[...] = scores(0)

    def body(t, carry):
        for jj in range(nsub):
            s_bufs[(jj + 1) % 2][...] = scores(nsub * t + jj + 1)
            update(s_bufs[jj % 2][...], nsub * t + jj, 0, diagonal=False)
        return carry

    lax.fori_loop(0, qi, body, 0)
    s_cur = s0_ref[...]
    for d in range(nsub):
        s_next = scores(nsub * qi + d + 1, row0=(d + 1) * tk) if d + 1 < nsub else None
        update(s_cur, nsub * qi + d, d * tk, diagonal=True)
        s_cur = s_next
    acc = acc_ref[...]
    o_ref[...] = acc[:, :MLA_V_DIM] / acc[:, MLA_V_DIM:]


def _mla_call(qm, km, vm, n_heads):
    b, s, _ = qm.shape
    tq = MLA_TQ
    assert tq % (2 * MLA_TK) == 0
    return pl.pallas_call(
        _mla_kernel,
        grid=(b, n_heads, s // tq),
        in_specs=[pl.BlockSpec((None, tq, MXU_DIM), lambda bb, h, i: (bb, i, h)),
                  pl.BlockSpec((None, s, MXU_DIM), lambda bb, h, i: (bb, 0, h)),
                  pl.BlockSpec((None, s, MLA_V_DIM), lambda bb, h, i: (bb, 0, h))],
        out_specs=pl.BlockSpec((None, tq, MLA_V_DIM), lambda bb, h, i: (bb, i, h)),
        out_shape=jax.ShapeDtypeStruct((b, s, n_heads * MLA_V_DIM), F32),
        scratch_shapes=[pltpu.VMEM((tq, LANES), F32), pltpu.VMEM((tq, 2 * MLA_V_DIM), F32),
                        pltpu.VMEM((tq, MLA_TK), F32), pltpu.VMEM((tq, MLA_TK), F32)],
        compiler_params=pltpu.CompilerParams(dimension_semantics=("parallel", "parallel", "arbitrary"),
                                             vmem_limit_bytes=VMEM_LIMIT_BYTES),
        name="mla_attn",
    )(qm, km, vm)


def _rope_tables(seq):
    inv_freq = ROPE_BASE ** (-np.arange(0, MLA_ROPE, 2, dtype=np.float64) / MLA_ROPE)
    ang = np.arange(seq, dtype=np.float64)[:, None] * inv_freq[None, :]
    cos, sin = np.cos(ang), np.sin(ang)
    cos_t = np.concatenate([cos, cos] * (LANES // MLA_ROPE), axis=1)
    sin_t = np.concatenate([-sin, sin] * (LANES // MLA_ROPE), axis=1)
    return jnp.asarray(cos_t, F32), jnp.asarray(sin_t, F32)


def _pad_lanes(v, width):
    return jnp.pad(v, ((0, 0), (0, width - v.shape[1])))


def kernel(x, ffn1_norm, ffn1_w_gate, ffn1_w_up, ffn1_w_down, mix_norm, w_in, dil_q_norm, dil_k_norm,
           rel_bias, mla_q_a_norm, mla_w_q_b, mla_kv_a_norm, mla_w_kv_b, mla_q_norm, mla_k_norm,
           out_norm_dil, out_norm_mla, w_out, ffn2_norm, ffn2_w_gate, ffn2_w_up, ffn2_w_down):
    batch, seq, d_model = x.shape
    depth = w_in.shape[0]
    dil_width = out_norm_dil.shape[1]
    dil_heads = dil_width // DIL_HEAD_DIM
    q_rank = mla_q_a_norm.shape[1]
    kv_rank = mla_kv_a_norm.shape[1]
    qk_dim = MLA_NOPE + MLA_ROPE
    n_mla = mla_w_q_b.shape[2] // qk_dim
    assert rel_bias.shape == (dil_heads, REL_BUCKETS)
    assert w_in.shape[2] == 3 * dil_width + q_rank + kv_rank + MLA_ROPE
    assert seq % DIL_CHUNK == 0 and seq % MLA_TQ == 0 and seq % ROW_TILE == 0
    assert all(win // dil == DIL_BAND and DIL_RES % dil == 0 for win, dil in DIL_BRANCHES)
    assert DIL_BRANCHES[-1][1] == DIL_RES

    cos_t, sin_t = _rope_tables(seq)
    buckets, prev = _band_tables()
    dil_bias = _dil_bias_call(rel_bias, jnp.asarray(buckets), jnp.asarray(prev))
    dil_bias = dil_bias.reshape(dil_bias.shape[0], 2, dil_heads // 2, 2 * DIL_BAND, 2 * DIL_BAND)
    eh = jnp.asarray(np.kron(np.eye(MXU_DIM // DIL_HEAD_DIM), np.ones((DIL_HEAD_DIM, DIL_HEAD_DIM))), BF16)
    perm = jnp.asarray(_residue_major_perm(ROW_TILE), BF16)

    x2d = x.reshape(batch * seq, d_model)
    row = lambda v: v.reshape(1, -1)
    for l in range(depth):
        x2d = _ffn_call(x2d, row(ffn1_norm[l]), ffn1_w_gate[l], ffn1_w_up[l], ffn1_w_down[l])

        w_in_t = jnp.swapaxes(w_in[l], 0, 1)
        wkpe_t = jnp.pad(w_in_t[w_in.shape[2] - MLA_ROPE:], ((0, LANES - MLA_ROPE), (0, 0))).astype(BF16)
        wqb = mla_w_q_b[l].reshape(q_rank, n_mla, qk_dim)
        wqb = jnp.concatenate(
            [wqb[:, :, :MLA_NOPE].reshape(q_rank, n_mla * MLA_NOPE),
             jnp.pad(wqb[:, :, MLA_NOPE:], ((0, 0), (0, 0), (0, LANES - MLA_ROPE))).reshape(q_rank, n_mla * LANES)],
            axis=1).astype(BF16)
        wkvb = mla_w_kv_b[l].reshape(kv_rank, n_mla, MLA_NOPE + MLA_V_DIM)
        wkvb = jnp.concatenate([wkvb[:, :, :MLA_NOPE].reshape(kv_rank, n_mla * MLA_NOPE),
                                wkvb[:, :, MLA_NOPE:].reshape(kv_rank, n_mla * MLA_V_DIM)], axis=1).astype(BF16)
        gqa = row(jnp.tile(dil_q_norm[l], dil_heads)) * (LOG2E * DIL_HEAD_DIM ** -0.5)
        gka = row(jnp.tile(dil_k_norm[l], dil_heads))
        gq = row(mla_q_norm[l]) * (LOG2E * qk_dim ** -0.5)
        gk = row(mla_k_norm[l])
        gqn, gqr = gq[:, :MLA_NOPE], _pad_lanes(gq[:, MLA_NOPE:], LANES)
        gkn, gkr = gk[:, :MLA_NOPE], _pad_lanes(gk[:, MLA_NOPE:], LANES)

        consts = [row(mix_norm[l]), w_in_t, wkpe_t, eh, perm, gqa, gka, row(mla_q_a_norm[l]), wqb,
                  row(mla_kv_a_norm[l]), wkvb, gqn, gqr, gkn, gkr]
        qa, ka, va, qm, km, vm = _proj_call(x2d, batch, seq, consts, cos_t, sin_t,
                                            n_mla_heads=n_mla, dil_width=dil_width)

        shp = lambda a: a.reshape(batch, seq, a.shape[1])
        o_dil = _dil_call(qa, ka, va, dil_bias)
        o_mla = _mla_call(shp(qm), shp(km), shp(vm), n_mla)

        x2d = _out_ffn_call(x2d, o_dil.reshape(batch * seq, -1), o_mla.reshape(batch * seq, -1),
                            row(out_norm_dil[l]), row(out_norm_mla[l]), w_out[l].astype(BF16),
                            row(ffn2_norm[l]), ffn2_w_gate[l], ffn2_w_up[l], ffn2_w_down[l])
    return x2d.reshape(batch, seq, d_model)
```

```python
import functools

import numpy as np
import jax
import jax.numpy as jnp
from jax import lax
from jax.experimental import pallas as pl
from jax.experimental.pallas import tpu as pltpu

F32 = jnp.float32
BF16 = jnp.bfloat16

DIL_HEAD_DIM = 64
DIL_BRANCHES = ((128, 1), (512, 4), (2048, 16))
DIL_BAND = 128
DIL_RES = 16
MLA_NOPE = 128
MLA_ROPE = 64
MLA_V_DIM = 128
ROPE_BASE = 10000.0
REL_BUCKETS = 32
REL_MAX_DIST = 2048
FFN_RESID = 0.5
EPS = 1e-6

LANES = 128
MXU_DIM = 256
VMEM_LIMIT_BYTES = 56 * 1024 * 1024
VMEM_LIMIT_FFN_BYTES = 60 * 1024 * 1024

NEG = -1e30
LOG2E = 1.4426950408889634

ROW_TILE = 512
FFN_STEP_ROWS = 2 * ROW_TILE
FFN_TF = 256
DIL_CHUNK = DIL_BAND * DIL_RES
MLA_TQ = 2048
MLA_TK = 512


def _dot(a, b):
    return jnp.dot(a, b, preferred_element_type=F32)


def _dot_nt(a, b):
    return lax.dot_general(a, b, (((1,), (1,)), ((), ())), preferred_element_type=F32)


def _rms(x, g):
    return x * lax.rsqrt(jnp.mean(x * x, axis=-1, keepdims=True) + EPS) * g


def _residue_major_perm(rows):
    per = rows // DIL_RES
    dst = np.arange(rows)
    src = DIL_RES * (dst % per) + dst // per
    p = np.zeros((rows, rows), np.float32)
    p[dst, src] = 1.0
    return p


def _swiglu_residual(x, g_ref, wg_ref, wu_ref, wd_ref, act_ref):
    h = _rms(x, g_ref[...]).astype(BF16)
    d_ff = wg_ref.shape[1]
    for c in range(d_ff // FFN_TF):
        sl = slice(c * FFN_TF, (c + 1) * FFN_TF)
        gate = _dot(h, wg_ref[:, sl].astype(BF16))
        up = _dot(h, wu_ref[:, sl].astype(BF16))
        act_ref[:, sl] = (gate * jax.nn.sigmoid(gate) * up).astype(BF16)
    return x + FFN_RESID * _dot(act_ref[...], wd_ref[...].astype(BF16))


def _ffn_kernel(x_ref, g_ref, wg_ref, wu_ref, wd_ref, o_ref, act_ref):
    for r0 in range(0, x_ref.shape[0], ROW_TILE):
        rows = slice(r0, r0 + ROW_TILE)
        o_ref[rows, :] = _swiglu_residual(x_ref[rows, :], g_ref, wg_ref, wu_ref, wd_ref, act_ref)


def _out_ffn_kernel(x_ref, od_ref, om_ref, gd_ref, gm_ref, wo_ref, g_ref, wg_ref, wu_ref, wd_ref,
                    o_ref, act_ref):
    wd_rows = od_ref.shape[1]
    od = _rms(od_ref[...], gd_ref[...]).astype(BF16)
    om = _rms(om_ref[...], gm_ref[...]).astype(BF16)
    x2 = x_ref[...] + _dot(od, wo_ref[:wd_rows, :]) + _dot(om, wo_ref[wd_rows:, :])
    o_ref[...] = _swiglu_residual(x2, g_ref, wg_ref, wu_ref, wd_ref, act_ref)


def _resident(shape):
    return pl.BlockSpec(shape, lambda *_: (0,) * len(shape), pipeline_mode=pl.Buffered(1))


def _residue_major_block(seq, width):
    per = ROW_TILE // DIL_RES
    blocks_per_seq = seq // ROW_TILE
    return pl.BlockSpec((None, DIL_RES, per, width), lambda i: (i // blocks_per_seq, 0, i % blocks_per_seq, 0))


def _ffn_call(x2d, g, wg, wu, wd):
    m, d = x2d.shape
    f = wg.shape[1]
    row = pl.BlockSpec((FFN_STEP_ROWS, d), lambda i: (i, 0))
    return pl.pallas_call(
        _ffn_kernel,
        grid=(m // FFN_STEP_ROWS,),
        in_specs=[row, _resident((1, d)), _resident((d, f)), _resident((d, f)), _resident((f, d))],
        out_specs=row,
        out_shape=jax.ShapeDtypeStruct((m, d), F32),
        scratch_shapes=[pltpu.VMEM((ROW_TILE, f), BF16)],
        compiler_params=pltpu.CompilerParams(dimension_semantics=("parallel",),
                                             vmem_limit_bytes=VMEM_LIMIT_FFN_BYTES),
        name="ffn",
    )(x2d, g, wg, wu, wd)


def _out_ffn_call(x2d, o_dil, o_mla, gd, gm, wo, g, wg, wu, wd):
    m, d = x2d.shape
    f = wg.shape[1]
    wdil, wmla = o_dil.shape[1], o_mla.shape[1]
    row = pl.BlockSpec((ROW_TILE, d), lambda i: (i, 0))
    return pl.pallas_call(
        _out_ffn_kernel,
        grid=(m // ROW_TILE,),
        in_specs=[row,
                  pl.BlockSpec((ROW_TILE, wdil), lambda i: (i, 0)),
                  pl.BlockSpec((ROW_TILE, wmla), lambda i: (i, 0)),
                  _resident((1, wdil)), _resident((1, wmla)),
                  _resident((wdil + wmla, d)),
                  _resident((1, d)), _resident((d, f)), _resident((d, f)), _resident((f, d))],
        out_specs=row,
        out_shape=jax.ShapeDtypeStruct((m, d), F32),
        scratch_shapes=[pltpu.VMEM((ROW_TILE, f), BF16)],
        compiler_params=pltpu.CompilerParams(dimension_semantics=("parallel",),
                                             vmem_limit_bytes=VMEM_LIMIT_BYTES),
        name="out_ffn",
    )(x2d, o_dil, o_mla, gd, gm, wo, g, wg, wu, wd)


def _rope(x, cos, sin_signed):
    lane = lax.broadcasted_iota(jnp.int32, x.shape, 1)
    first_half = (lane % MLA_ROPE) < (MLA_ROPE // 2)
    partner = jnp.where(first_half,
                        pltpu.roll(x, LANES - MLA_ROPE // 2, 1),
                        pltpu.roll(x, MLA_ROPE // 2, 1))
    return x * cos + partner * sin_signed


def _proj_kernel(x_ref, gmix_ref, wint_ref, wkpet_ref, eh_ref, perm_ref, gqa_ref, gka_ref,
                 gcq_ref, wqb_ref, gckv_ref, wkvb_ref,
                 gqn_ref, gqr_ref, gkn_ref, gkr_ref, cos_ref, sin_ref,
                 qa_ref, ka_ref, va_ref, qm_ref, km_ref, vm_ref, *, n_mla_heads, dil_width):
    h = _rms(x_ref[...], gmix_ref[...]).astype(BF16)
    w = dil_width
    o = 3 * w
    q_rank = gcq_ref.shape[1]
    kv_rank = gckv_ref.shape[1]
    qk_dim = MLA_NOPE + MLA_ROPE
    nh = n_mla_heads
    cos = cos_ref[...]
    sin = sin_ref[...]

    w_mla = jnp.concatenate([wint_ref[o:o + q_rank + kv_rank, :].astype(BF16), wkpet_ref[...]], axis=0)
    pm = _dot_nt(h, w_mla)
    cq = pm[:, :q_rank]
    ckv = pm[:, q_rank:q_rank + kv_rank]
    kpe = pm[:, q_rank + kv_rank:]

    qb = _dot(_rms(cq, gcq_ref[...]).astype(BF16), wqb_ref[...])
    for hd in range(nh):
        qn = qb[:, hd * MLA_NOPE:(hd + 1) * MLA_NOPE]
        qr = qb[:, nh * MLA_NOPE + hd * LANES:nh * MLA_NOPE + (hd + 1) * LANES]
        ss = jnp.sum(qn * qn, axis=-1, keepdims=True) + jnp.sum(qr * qr, axis=-1, keepdims=True)
        r = lax.rsqrt(ss * (1.0 / qk_dim) + EPS)
        qm_ref[:, hd * MXU_DIM:hd * MXU_DIM + LANES] = (qn * r * gqn_ref[...]).astype(BF16)
        qm_ref[:, hd * MXU_DIM + LANES:(hd + 1) * MXU_DIM] = _rope(qr * r * gqr_ref[...], cos, sin).astype(BF16)

    kvb = _dot(_rms(ckv, gckv_ref[...]).astype(BF16), wkvb_ref[...])
    ss_pe = jnp.sum(kpe * kpe, axis=-1, keepdims=True)
    kpe_rot = _rope(kpe * gkr_ref[...], cos, sin)
    for hd in range(nh):
        kn = kvb[:, hd * MLA_NOPE:(hd + 1) * MLA_NOPE]
        ss = jnp.sum(kn * kn, axis=-1, keepdims=True) + ss_pe
        r = lax.rsqrt(ss * (1.0 / qk_dim) + EPS)
        km_ref[:, hd * MXU_DIM:hd * MXU_DIM + LANES] = (kn * r * gkn_ref[...]).astype(BF16)
        km_ref[:, hd * MXU_DIM + LANES:(hd + 1) * MXU_DIM] = (kpe_rot * r).astype(BF16)
    vm_ref[...] = kvb[:, nh * MLA_NOPE:].astype(BF16)

    hp = _dot(perm_ref[...], h).astype(BF16)
    for c, (g_ref, dst) in enumerate(((gqa_ref, qa_ref), (gka_ref, ka_ref))):
        src = _dot_nt(hp, wint_ref[c * w:(c + 1) * w, :].astype(BF16))
        sq = (src * src).astype(BF16)
        ms = jnp.concatenate([_dot(sq[:, c0:c0 + MXU_DIM], eh_ref[...]) for c0 in range(0, w, MXU_DIM)],
                             axis=1) * (1.0 / DIL_HEAD_DIM)
        dst[...] = (src * lax.rsqrt(ms + EPS) * g_ref[...]).reshape(dst.shape)
    va_ref[...] = _dot_nt(hp, wint_ref[2 * w:o, :].astype(BF16)).reshape(va_ref.shape)


def _proj_call(x2d, batch, seq, consts, cos, sin, *, n_mla_heads, dil_width):
    m, d = x2d.shape
    tm = ROW_TILE
    n_seq_blocks = seq // tm
    row = lambda width: pl.BlockSpec((tm, width), lambda i: (i, 0))
    pos = pl.BlockSpec((tm, LANES), lambda i: (i % n_seq_blocks, 0))
    mla_w = n_mla_heads * MXU_DIM
    dil_shape = jax.ShapeDtypeStruct((batch, DIL_RES, seq // DIL_RES, dil_width), F32)
    return pl.pallas_call(
        functools.partial(_proj_kernel, n_mla_heads=n_mla_heads, dil_width=dil_width),
        grid=(m // tm,),
        in_specs=[row(d)] + [_resident(c.shape) for c in consts] + [pos, pos],
        out_specs=[_residue_major_block(seq, dil_width)] * 3
        + [row(mla_w), row(mla_w), row(n_mla_heads * MLA_V_DIM)],
        out_shape=[dil_shape] * 3
        + [jax.ShapeDtypeStruct((m, mla_w), BF16)] * 2
        + [jax.ShapeDtypeStruct((m, n_mla_heads * MLA_V_DIM), BF16)],
        compiler_params=pltpu.CompilerParams(dimension_semantics=("parallel",),
                                             vmem_limit_bytes=VMEM_LIMIT_BYTES),
        name="proj",
    )(x2d, *consts, cos, sin)


def _t5_bucket(dist):
    max_exact = REL_BUCKETS // 2
    d = np.maximum(dist, 1).astype(np.float32)
    large = max_exact + (np.log(d / max_exact) / np.log(REL_MAX_DIST / max_exact)
                         * (REL_BUCKETS - max_exact)).astype(np.int32)
    large = np.minimum(large, REL_BUCKETS - 1)
    return np.where(dist < max_exact, dist, large).astype(np.int32)


def _band_tables():
    rho = np.arange(DIL_BAND)
    kap = np.arange(2 * DIL_BAND)
    buckets, prev = [], []
    for _, dil in DIL_BRANCHES:
        g = DIL_RES // dil
        run_q = DIL_BAND // g
        pos_q = g * (rho % run_q) + rho // run_q
        pos_k = g * (kap % (2 * run_q)) + kap // (2 * run_q) - DIL_BAND
        delta = pos_q[:, None] - pos_k[None, :]
        valid = (delta >= 0) & (delta <= DIL_BAND)
        buckets.append(np.where(valid, _t5_bucket(np.clip(delta, 0, None) * dil), -1))
        prev.append((pos_k < 0)[None, :])
    return np.stack(buckets).astype(np.int32), np.stack(prev).astype(np.int32)


def _dil_bias_kernel(rel_ref, bucket_ref, prev_ref, o_ref, *, present):
    nh = o_ref.shape[2]
    rb = 16
    scaled = {}
    for bi, buckets_here in enumerate(present):
        prev = prev_ref[bi] > 0
        for r0 in range(0, DIL_BAND, rb):
            bucket = bucket_ref[bi, r0:r0 + rb, :]
            accs = [jnp.full(bucket.shape, NEG, F32)] * nh
            for b in buckets_here:
                hit = bucket == b
                for hd in range(nh):
                    if (hd, b) not in scaled:
                        scaled[hd, b] = rel_ref[hd, b] * LOG2E
                accs = [jnp.where(hit, scaled[hd, b], acc) for hd, acc in enumerate(accs)]
            for hd in range(nh):
                o_ref[bi, 0, hd, r0:r0 + rb, :] = accs[hd]
                o_ref[bi, 1, hd, r0:r0 + rb, :] = jnp.where(prev, NEG, accs[hd])


def _dil_bias_call(rel_bias, buckets, prev):
    nbr = buckets.shape[0]
    nh = rel_bias.shape[0]
    tile = buckets.shape[1:]
    present = tuple(tuple(int(b) for b in np.unique(t) if b >= 0) for t in buckets)
    whole = lambda shape: pl.BlockSpec(shape, lambda i: (0,) * len(shape))
    out_shape = (nbr, 2, nh) + tile
    return pl.pallas_call(
        functools.partial(_dil_bias_kernel, present=present),
        grid=(1,),
        in_specs=[pl.BlockSpec(memory_space=pltpu.SMEM), whole(buckets.shape), whole(prev.shape)],
        out_specs=whole(out_shape),
        out_shape=jax.ShapeDtypeStruct(out_shape, F32),
        name="dil_bias",
    )(rel_bias, jnp.asarray(buckets), jnp.asarray(prev))


def _band_tile(q, k, v, bias, head_a):
    zero = jnp.zeros_like(q)
    q2 = jnp.concatenate([jnp.where(head_a, q, zero), jnp.where(head_a, zero, q)], axis=0).astype(BF16)
    s = _dot_nt(q2, k.astype(BF16)) + bias
    m = jnp.max(s, axis=-1, keepdims=True)
    p = jnp.exp2(s - m).astype(BF16)
    ones = jnp.ones(v.shape, BF16)
    out = _dot(p, jnp.concatenate([v.astype(BF16), ones], axis=1))
    h = DIL_BAND
    num = jnp.where(head_a, out[:h, :LANES], out[h:, :LANES])
    den = jnp.where(head_a, out[:h, LANES:], out[h:, LANES:])
    mx = jnp.where(head_a, m[:h], m[h:])
    return num, den, mx


def _dil_kernel(q_ref, kp_ref, kc_ref, vp_ref, vc_ref, bias_ref, o_ref, num_scr, den_scr, max_scr):
    first_chunk = (pl.program_id(2) == 0).astype(jnp.int32)
    head_a = lax.broadcasted_iota(jnp.int32, (1, LANES), 1) < DIL_HEAD_DIM

    def tile(bi, dil, rbase, n):
        g = DIL_RES // dil
        run = DIL_BAND // g
        static_n = isinstance(n, int)
        q_rows = pl.ds(n * run, run) if static_n else pl.ds(pl.multiple_of(n * run, run), run)
        qs, ks, vs = [], [], []
        for u in range(g):
            res = rbase + dil * u
            qs.append(q_ref[res, q_rows, :])
            if static_n and n == 0:
                ks += [kp_ref[res, pl.ds(DIL_BAND - run, run), :], kc_ref[res, pl.ds(0, run), :]]
                vs += [vp_ref[res, pl.ds(DIL_BAND - run, run), :], vc_ref[res, pl.ds(0, run), :]]
            else:
                k_rows = (pl.ds((n - 1) * run, 2 * run) if static_n
                          else pl.ds(pl.multiple_of((n - 1) * run, run), 2 * run))
                ks.append(kc_ref[res, k_rows, :])
                vs.append(vc_ref[res, k_rows, :])
        cat = lambda parts: parts[0] if len(parts) == 1 else jnp.concatenate(parts, axis=0)
        variant = first_chunk if (static_n and n == 0) else 0
        num, den, mx = _band_tile(cat(qs), cat(ks), cat(vs), bias_ref[bi, variant], head_a)
        if g == 1:
            return num, den, mx
        for u in range(g):
            res = rbase + dil * u
            part = slice(u * run, (u + 1) * run)
            num_scr[bi, res, q_rows, :] = num[part]
            den_scr[bi, res, q_rows, :] = den[part]
            max_scr[bi, res, q_rows, :] = mx[part]

    for n in range(DIL_RES):
        tile(0, 1, 0, n)
    for r4 in range(4):
        for n in range(4):
            tile(1, 4, r4, n)
    stored = len(DIL_BRANCHES) - 1
    for r in range(DIL_RES):
        num_r, den_r, max_r = tile(stored, DIL_RES, r, 0)
        ms = [max_scr[bi, r] for bi in range(stored)] + [max_r]
        nums = [num_scr[bi, r] for bi in range(stored)] + [num_r]
        dens = [den_scr[bi, r] for bi in range(stored)] + [den_r]
        m_all = functools.reduce(jnp.maximum, ms)
        es = [jnp.exp2(mb - m_all) for mb in ms]
        num = sum(e * x for e, x in zip(es, nums))
        den = sum(e * x for e, x in zip(es, dens))
        o_ref[pl.ds(r, DIL_BAND, stride=DIL_RES), :] = num / den


def _dil_call(qa, ka, va, bias):
    b, _, per, w = qa.shape
    pairs = w // LANES
    nbr = bias.shape[0]
    blk = (None, DIL_RES, DIL_BAND, LANES)
    cur = pl.BlockSpec(blk, lambda bb, p, c: (bb, 0, c, p))
    prev = pl.BlockSpec(blk, lambda bb, p, c: (bb, 0, jnp.maximum(c - 1, 0), p))
    scr = pltpu.VMEM((nbr - 1, DIL_RES, DIL_BAND, LANES), F32)
    return pl.pallas_call(
        _dil_kernel,
        grid=(b, pairs, per // DIL_BAND),
        in_specs=[cur, prev, cur, prev, cur,
                  pl.BlockSpec((nbr, 2, None, 2 * DIL_BAND, 2 * DIL_BAND), lambda bb, p, c: (0, 0, p, 0, 0))],
        out_specs=pl.BlockSpec((None, DIL_CHUNK, LANES), lambda bb, p, c: (bb, c, p)),
        out_shape=jax.ShapeDtypeStruct((b, per * DIL_RES, w), F32),
        scratch_shapes=[scr, scr, scr],
        compiler_params=pltpu.CompilerParams(dimension_semantics=("parallel", "parallel", "arbitrary"),
                                             vmem_limit_bytes=VMEM_LIMIT_BYTES),
        name="dil_attn",
    )(qa, ka, ka, va, va, bias)


def _mla_kernel(q_ref, k_ref, v_ref, o_ref, m_ref, acc_ref, s0_ref, s1_ref):
    tq = q_ref.shape[0]
    tk = MLA_TK
    nsub = tq // tk
    qi = pl.program_id(2)
    m_ref[...] = jnp.full(m_ref.shape, NEG, F32)
    acc_ref[...] = jnp.zeros(acc_ref.shape, F32)
    ones = jnp.ones((tk, LANES), BF16)
    s_bufs = (s0_ref, s1_ref)

    def scores(kblk, row0=0):
        k0 = pl.multiple_of(kblk * tk, tk)
        return _dot_nt(q_ref[row0:, :], k_ref[pl.ds(k0, tk), :])

    def update(s, kblk, row0, diagonal):
        k0 = pl.multiple_of(kblk * tk, tk)
        rows = slice(row0, tq)
        if diagonal:
            row = lax.broadcasted_iota(jnp.int32, s.shape, 0)
            col = lax.broadcasted_iota(jnp.int32, s.shape, 1)
            s = jnp.where(col <= row, s, NEG)
        m_old = m_ref[rows, :]
        m_new = jnp.maximum(m_old, jnp.max(s, axis=-1, keepdims=True))
        m_ref[rows, :] = m_new
        p = jnp.concatenate([jnp.exp2(s[:, c * LANES:(c + 1) * LANES] - m_new) for c in range(tk // LANES)],
                            axis=1).astype(BF16)
        alpha = jnp.exp2(m_old - m_new)
        vext = jnp.concatenate([v_ref[pl.ds(k0, tk), :], ones], axis=1)
        acc_ref[rows, :] = acc_ref[rows, :] * jnp.concatenate([alpha, alpha], axis=1) + _dot(p, vext)

    s0_ref[...] = scores(0)

    def body(t, carry):
        for jj in range(nsub):
            s_bufs[(jj + 1) % 2][...] = scores(nsub * t + jj + 1)
            update(s_bufs[jj % 2][...], nsub * t + jj, 0, diagonal=False)
        return carry

    lax.fori_loop(0, qi, body, 0)
    s_cur = s0_ref[...]
    for d in range(nsub):
        s_next = scores(nsub * qi + d + 1, row0=(d + 1) * tk) if d + 1 < nsub else None
        update(s_cur, nsub * qi + d, d * tk, diagonal=True)
        s_cur = s_next
    acc = acc_ref[...]
    o_ref[...] = acc[:, :MLA_V_DIM] / acc[:, MLA_V_DIM:]


def _mla_call(qm, km, vm, n_heads):
    b, s, _ = qm.shape
    tq = MLA_TQ
    assert tq % (2 * MLA_TK) == 0
    return pl.pallas_call(
        _mla_kernel,
        grid=(b, n_heads, s // tq),
        in_specs=[pl.BlockSpec((None, tq, MXU_DIM), lambda bb, h, i: (bb, i, h)),
                  pl.BlockSpec((None, s, MXU_DIM), lambda bb, h, i: (bb, 0, h)),
                  pl.BlockSpec((None, s, MLA_V_DIM), lambda bb, h, i: (bb, 0, h))],
        out_specs=pl.BlockSpec((None, tq, MLA_V_DIM), lambda bb, h, i: (bb, i, h)),
        out_shape=jax.ShapeDtypeStruct((b, s, n_heads * MLA_V_DIM), F32),
        scratch_shapes=[pltpu.VMEM((tq, LANES), F32), pltpu.VMEM((tq, 2 * MLA_V_DIM), F32),
                        pltpu.VMEM((tq, MLA_TK), F32), pltpu.VMEM((tq, MLA_TK), F32)],
        compiler_params=pltpu.CompilerParams(dimension_semantics=("parallel", "parallel", "arbitrary"),
                                             vmem_limit_bytes=VMEM_LIMIT_BYTES),
        name="mla_attn",
    )(qm, km, vm)


def _rope_tables(seq):
    inv_freq = ROPE_BASE ** (-np.arange(0, MLA_ROPE, 2, dtype=np.float64) / MLA_ROPE)
    ang = np.arange(seq, dtype=np.float64)[:, None] * inv_freq[None, :]
    cos, sin = np.cos(ang), np.sin(ang)
    cos_t = np.concatenate([cos, cos] * (LANES // MLA_ROPE), axis=1)
    sin_t = np.concatenate([-sin, sin] * (LANES // MLA_ROPE), axis=1)
    return jnp.asarray(cos_t, F32), jnp.asarray(sin_t, F32)


def _pad_lanes(v, width):
    return jnp.pad(v, ((0, 0), (0, width - v.shape[1])))


def kernel(x, ffn1_norm, ffn1_w_gate, ffn1_w_up, ffn1_w_down, mix_norm, w_in, dil_q_norm, dil_k_norm,
           rel_bias, mla_q_a_norm, mla_w_q_b, mla_kv_a_norm, mla_w_kv_b, mla_q_norm, mla_k_norm,
           out_norm_dil, out_norm_mla, w_out, ffn2_norm, ffn2_w_gate, ffn2_w_up, ffn2_w_down):
    batch, seq, d_model = x.shape
    depth = w_in.shape[0]
    dil_width = out_norm_dil.shape[1]
    dil_heads = dil_width // DIL_HEAD_DIM
    q_rank = mla_q_a_norm.shape[1]
    kv_rank = mla_kv_a_norm.shape[1]
    qk_dim = MLA_NOPE + MLA_ROPE
    n_mla = mla_w_q_b.shape[2] // qk_dim
    assert rel_bias.shape == (dil_heads, REL_BUCKETS)
    assert w_in.shape[2] == 3 * dil_width + q_rank + kv_rank + MLA_ROPE
    assert seq % DIL_CHUNK == 0 and seq % MLA_TQ == 0 and seq % ROW_TILE == 0
    assert (batch * seq) % FFN_STEP_ROWS == 0
    assert all(win // dil == DIL_BAND and DIL_RES % dil == 0 for win, dil in DIL_BRANCHES)
    assert DIL_BRANCHES[-1][1] == DIL_RES

    cos_t, sin_t = _rope_tables(seq)
    buckets, prev = _band_tables()
    dil_bias = _dil_bias_call(rel_bias, buckets, prev)
    dil_bias = dil_bias.reshape(dil_bias.shape[0], 2, dil_heads // 2, 2 * DIL_BAND, 2 * DIL_BAND)
    eh = jnp.asarray(np.kron(np.eye(MXU_DIM // DIL_HEAD_DIM), np.ones((DIL_HEAD_DIM, DIL_HEAD_DIM))), BF16)
    perm = jnp.asarray(_residue_major_perm(ROW_TILE), BF16)

    x2d = x.reshape(batch * seq, d_model)
    row = lambda v: v.reshape(1, -1)
    for l in range(depth):
        x2d = _ffn_call(x2d, row(ffn1_norm[l]), ffn1_w_gate[l], ffn1_w_up[l], ffn1_w_down[l])

        w_in_t = jnp.swapaxes(w_in[l], 0, 1)
        wkpe_t = jnp.pad(w_in_t[w_in.shape[2] - MLA_ROPE:], ((0, LANES - MLA_ROPE), (0, 0))).astype(BF16)
        wqb = mla_w_q_b[l].reshape(q_rank, n_mla, qk_dim)
        wqb = jnp.concatenate(
            [wqb[:, :, :MLA_NOPE].reshape(q_rank, n_mla * MLA_NOPE),
             jnp.pad(wqb[:, :, MLA_NOPE:], ((0, 0), (0, 0), (0, LANES - MLA_ROPE))).reshape(q_rank, n_mla * LANES)],
            axis=1).astype(BF16)
        wkvb = mla_w_kv_b[l].reshape(kv_rank, n_mla, MLA_NOPE + MLA_V_DIM)
        wkvb = jnp.concatenate([wkvb[:, :, :MLA_NOPE].reshape(kv_rank, n_mla * MLA_NOPE),
                                wkvb[:, :, MLA_NOPE:].reshape(kv_rank, n_mla * MLA_V_DIM)], axis=1).astype(BF16)
        gqa = row(jnp.tile(dil_q_norm[l], dil_heads)) * (LOG2E * DIL_HEAD_DIM ** -0.5)
        gka = row(jnp.tile(dil_k_norm[l], dil_heads))
        gq = row(mla_q_norm[l]) * (LOG2E * qk_dim ** -0.5)
        gk = row(mla_k_norm[l])
        gqn, gqr = gq[:, :MLA_NOPE], _pad_lanes(gq[:, MLA_NOPE:], LANES)
        gkn, gkr = gk[:, :MLA_NOPE], _pad_lanes(gk[:, MLA_NOPE:], LANES)

        consts = [row(mix_norm[l]), w_in_t, wkpe_t, eh, perm, gqa, gka, row(mla_q_a_norm[l]), wqb,
                  row(mla_kv_a_norm[l]), wkvb, gqn, gqr, gkn, gkr]
        qa, ka, va, qm, km, vm = _proj_call(x2d, batch, seq, consts, cos_t, sin_t,
                                            n_mla_heads=n_mla, dil_width=dil_width)

        shp = lambda a: a.reshape(batch, seq, a.shape[1])
        o_dil = _dil_call(qa, ka, va, dil_bias)
        o_mla = _mla_call(shp(qm), shp(km), shp(vm), n_mla)

        x2d = _out_ffn_call(x2d, o_dil.reshape(batch * seq, -1), o_mla.reshape(batch * seq, -1),
                            row(out_norm_dil[l]), row(out_norm_mla[l]), w_out[l].astype(BF16),
                            row(ffn2_norm[l]), ffn2_w_gate[l], ffn2_w_up[l], ffn2_w_down[l])
    return x2d.reshape(batch, seq, d_model)
```

```python
import functools

import numpy as np
import jax
import jax.numpy as jnp
from jax import lax
from jax.experimental import pallas as pl
from jax.experimental.pallas import tpu as pltpu

F32 = jnp.float32
BF16 = jnp.bfloat16

DIL_HEAD_DIM = 64
DIL_BRANCHES = ((128, 1), (512, 4), (2048, 16))
DIL_BAND = 128
DIL_RES = 16
MLA_NOPE = 128
MLA_ROPE = 64
MLA_V_DIM = 128
ROPE_BASE = 10000.0
REL_BUCKETS = 32
REL_MAX_DIST = 2048
FFN_RESID = 0.5
EPS = 1e-6

LANES = 128
MXU_DIM = 256
VMEM_LIMIT_BYTES = 56 * 1024 * 1024
VMEM_LIMIT_FFN_BYTES = 60 * 1024 * 1024

NEG = float("-inf")
LOG2E = 1.4426950408889634

ROW_TILE = 512
FFN_STEP_ROWS = 2 * ROW_TILE
FFN_TF = 256
DIL_CHUNK = DIL_BAND * DIL_RES
MLA_TQ = 2048
MLA_TK = 512


def _dot(a, b):
    return jnp.dot(a, b, preferred_element_type=F32)


def _dot_nt(a, b):
    return lax.dot_general(a, b, (((1,), (1,)), ((), ())), preferred_element_type=F32)


def _rms(x, g):
    return x * lax.rsqrt(jnp.mean(x * x, axis=-1, keepdims=True) + EPS) * g


def _residue_major_perm(rows):
    per = rows // DIL_RES
    dst = np.arange(rows)
    src = DIL_RES * (dst % per) + dst // per
    p = np.zeros((rows, rows), np.float32)
    p[dst, src] = 1.0
    return p


def _swiglu_residual(x, g_ref, wg_ref, wu_ref, wd_ref, act_ref):
    h = _rms(x, g_ref[...]).astype(BF16)
    d_ff = wg_ref.shape[1]
    for c in range(d_ff // FFN_TF):
        sl = slice(c * FFN_TF, (c + 1) * FFN_TF)
        gate = _dot(h, wg_ref[:, sl].astype(BF16))
        up = _dot(h, wu_ref[:, sl].astype(BF16))
        act_ref[:, sl] = (gate * jax.nn.sigmoid(gate) * up).astype(BF16)
    return x + FFN_RESID * _dot(act_ref[...], wd_ref[...].astype(BF16))


def _ffn_kernel(x_ref, g_ref, wg_ref, wu_ref, wd_ref, o_ref, act_ref):
    for r0 in range(0, x_ref.shape[0], ROW_TILE):
        rows = slice(r0, r0 + ROW_TILE)
        o_ref[rows, :] = _swiglu_residual(x_ref[rows, :], g_ref, wg_ref, wu_ref, wd_ref, act_ref)


def _out_ffn_kernel(x_ref, od_ref, om_ref, gd_ref, gm_ref, wo_ref, g_ref, wg_ref, wu_ref, wd_ref,
                    o_ref, act_ref):
    wd_rows = od_ref.shape[1]
    od = _rms(od_ref[...], gd_ref[...]).astype(BF16)
    om = _rms(om_ref[...], gm_ref[...]).astype(BF16)
    x2 = x_ref[...] + _dot(od, wo_ref[:wd_rows, :]) + _dot(om, wo_ref[wd_rows:, :])
    o_ref[...] = _swiglu_residual(x2, g_ref, wg_ref, wu_ref, wd_ref, act_ref)


def _resident(shape):
    return pl.BlockSpec(shape, lambda *_: (0,) * len(shape), pipeline_mode=pl.Buffered(1))


def _residue_major_block(seq, width):
    per = ROW_TILE // DIL_RES
    blocks_per_seq = seq // ROW_TILE
    return pl.BlockSpec((None, DIL_RES, per, width), lambda i: (i // blocks_per_seq, 0, i % blocks_per_seq, 0))


def _ffn_call(x2d, g, wg, wu, wd):
    m, d = x2d.shape
    f = wg.shape[1]
    row = pl.BlockSpec((FFN_STEP_ROWS, d), lambda i: (i, 0))
    return pl.pallas_call(
        _ffn_kernel,
        grid=(m // FFN_STEP_ROWS,),
        in_specs=[row, _resident((1, d)), _resident((d, f)), _resident((d, f)), _resident((f, d))],
        out_specs=row,
        out_shape=jax.ShapeDtypeStruct((m, d), F32),
        scratch_shapes=[pltpu.VMEM((ROW_TILE, f), BF16)],
        compiler_params=pltpu.CompilerParams(dimension_semantics=("parallel",),
                                             vmem_limit_bytes=VMEM_LIMIT_FFN_BYTES),
        name="ffn",
    )(x2d, g, wg, wu, wd)


def _out_ffn_call(x2d, o_dil, o_mla, gd, gm, wo, g, wg, wu, wd):
    m, d = x2d.shape
    f = wg.shape[1]
    wdil, wmla = o_dil.shape[1], o_mla.shape[1]
    row = pl.BlockSpec((ROW_TILE, d), lambda i: (i, 0))
    return pl.pallas_call(
        _out_ffn_kernel,
        grid=(m // ROW_TILE,),
        in_specs=[row,
                  pl.BlockSpec((ROW_TILE, wdil), lambda i: (i, 0)),
                  pl.BlockSpec((ROW_TILE, wmla), lambda i: (i, 0)),
                  _resident((1, wdil)), _resident((1, wmla)),
                  _resident((wdil + wmla, d)),
                  _resident((1, d)), _resident((d, f)), _resident((d, f)), _resident((f, d))],
        out_specs=row,
        out_shape=jax.ShapeDtypeStruct((m, d), F32),
        scratch_shapes=[pltpu.VMEM((ROW_TILE, f), BF16)],
        compiler_params=pltpu.CompilerParams(dimension_semantics=("parallel",),
                                             vmem_limit_bytes=VMEM_LIMIT_BYTES),
        name="out_ffn",
    )(x2d, o_dil, o_mla, gd, gm, wo, g, wg, wu, wd)


def _rope(x, cos, sin_signed):
    lane = lax.broadcasted_iota(jnp.int32, x.shape, 1)
    first_half = (lane % MLA_ROPE) < (MLA_ROPE // 2)
    partner = jnp.where(first_half,
                        pltpu.roll(x, LANES - MLA_ROPE // 2, 1),
                        pltpu.roll(x, MLA_ROPE // 2, 1))
    return x * cos + partner * sin_signed


def _proj_kernel(x_ref, gmix_ref, wint_ref, wkpet_ref, eh_ref, perm_ref, gqa_ref, gka_ref,
                 gcq_ref, wqb_ref, gckv_ref, wkvb_ref,
                 gqn_ref, gqr_ref, gkn_ref, gkr_ref, cos_ref, sin_ref,
                 qa_ref, ka_ref, va_ref, qm_ref, km_ref, vm_ref, *, n_mla_heads, dil_width):
    h = _rms(x_ref[...], gmix_ref[...]).astype(BF16)
    w = dil_width
    o = 3 * w
    q_rank = gcq_ref.shape[1]
    kv_rank = gckv_ref.shape[1]
    qk_dim = MLA_NOPE + MLA_ROPE
    nh = n_mla_heads
    cos = cos_ref[...]
    sin = sin_ref[...]

    w_mla = jnp.concatenate([wint_ref[o:o + q_rank + kv_rank, :].astype(BF16), wkpet_ref[...]], axis=0)
    pm = _dot_nt(h, w_mla)
    cq = pm[:, :q_rank]
    ckv = pm[:, q_rank:q_rank + kv_rank]
    kpe = pm[:, q_rank + kv_rank:]

    qb = _dot(_rms(cq, gcq_ref[...]).astype(BF16), wqb_ref[...])
    for hd in range(nh):
        qn = qb[:, hd * MLA_NOPE:(hd + 1) * MLA_NOPE]
        qr = qb[:, nh * MLA_NOPE + hd * LANES:nh * MLA_NOPE + (hd + 1) * LANES]
        ss = jnp.sum(qn * qn, axis=-1, keepdims=True) + jnp.sum(qr * qr, axis=-1, keepdims=True)
        r = lax.rsqrt(ss * (1.0 / qk_dim) + EPS)
        qm_ref[:, hd * MXU_DIM:hd * MXU_DIM + LANES] = (qn * r * gqn_ref[...]).astype(BF16)
        qm_ref[:, hd * MXU_DIM + LANES:(hd + 1) * MXU_DIM] = _rope(qr * r * gqr_ref[...], cos, sin).astype(BF16)

    kvb = _dot(_rms(ckv, gckv_ref[...]).astype(BF16), wkvb_ref[...])
    ss_pe = jnp.sum(kpe * kpe, axis=-1, keepdims=True)
    kpe_rot = _rope(kpe * gkr_ref[...], cos, sin)
    for hd in range(nh):
        kn = kvb[:, hd * MLA_NOPE:(hd + 1) * MLA_NOPE]
        ss = jnp.sum(kn * kn, axis=-1, keepdims=True) + ss_pe
        r = lax.rsqrt(ss * (1.0 / qk_dim) + EPS)
        km_ref[:, hd * MXU_DIM:hd * MXU_DIM + LANES] = (kn * r * gkn_ref[...]).astype(BF16)
        km_ref[:, hd * MXU_DIM + LANES:(hd + 1) * MXU_DIM] = (kpe_rot * r).astype(BF16)
    vm_ref[...] = kvb[:, nh * MLA_NOPE:].astype(BF16)

    hp = _dot(perm_ref[...], h).astype(BF16)
    for c, (g_ref, dst) in enumerate(((gqa_ref, qa_ref), (gka_ref, ka_ref))):
        src = _dot_nt(hp, wint_ref[c * w:(c + 1) * w, :].astype(BF16))
        sq = (src * src).astype(BF16)
        ms = jnp.concatenate([_dot(sq[:, c0:c0 + MXU_DIM], eh_ref[...]) for c0 in range(0, w, MXU_DIM)],
                             axis=1) * (1.0 / DIL_HEAD_DIM)
        dst[...] = (src * lax.rsqrt(ms + EPS) * g_ref[...]).reshape(dst.shape)
    va_ref[...] = _dot_nt(hp, wint_ref[2 * w:o, :].astype(BF16)).reshape(va_ref.shape)


def _proj_call(x2d, batch, seq, consts, cos, sin, *, n_mla_heads, dil_width):
    m, d = x2d.shape
    tm = ROW_TILE
    n_seq_blocks = seq // tm
    row = lambda width: pl.BlockSpec((tm, width), lambda i: (i, 0))
    pos = pl.BlockSpec((tm, LANES), lambda i: (i % n_seq_blocks, 0))
    mla_w = n_mla_heads * MXU_DIM
    dil_shape = jax.ShapeDtypeStruct((batch, DIL_RES, seq // DIL_RES, dil_width), F32)
    return pl.pallas_call(
        functools.partial(_proj_kernel, n_mla_heads=n_mla_heads, dil_width=dil_width),
        grid=(m // tm,),
        in_specs=[row(d)] + [_resident(c.shape) for c in consts] + [pos, pos],
        out_specs=[_residue_major_block(seq, dil_width)] * 3
        + [row(mla_w), row(mla_w), row(n_mla_heads * MLA_V_DIM)],
        out_shape=[dil_shape] * 3
        + [jax.ShapeDtypeStruct((m, mla_w), BF16)] * 2
        + [jax.ShapeDtypeStruct((m, n_mla_heads * MLA_V_DIM), BF16)],
        compiler_params=pltpu.CompilerParams(dimension_semantics=("parallel",),
                                             vmem_limit_bytes=VMEM_LIMIT_BYTES),
        name="proj",
    )(x2d, *consts, cos, sin)


def _t5_bucket(dist):
    max_exact = REL_BUCKETS // 2
    d = np.maximum(dist, 1).astype(np.float32)
    large = max_exact + (np.log(d / max_exact) / np.log(REL_MAX_DIST / max_exact)
                         * (REL_BUCKETS - max_exact)).astype(np.int32)
    large = np.minimum(large, REL_BUCKETS - 1)
    return np.where(dist < max_exact, dist, large).astype(np.int32)


def _band_tables():
    rho = np.arange(DIL_BAND)
    kap = np.arange(2 * DIL_BAND)
    buckets, prev = [], []
    for _, dil in DIL_BRANCHES:
        g = DIL_RES // dil
        run_q = DIL_BAND // g
        pos_q = g * (rho % run_q) + rho // run_q
        pos_k = g * (kap % (2 * run_q)) + kap // (2 * run_q) - DIL_BAND
        delta = pos_q[:, None] - pos_k[None, :]
        valid = (delta >= 0) & (delta <= DIL_BAND)
        buckets.append(np.where(valid, _t5_bucket(np.clip(delta, 0, None) * dil), -1))
        prev.append((pos_k < 0)[None, :])
    return np.stack(buckets).astype(np.int32), np.stack(prev).astype(np.int32)


def _dil_bias_kernel(rel_ref, bucket_ref, prev_ref, o_ref, *, present):
    nh = o_ref.shape[2]
    rb = 16
    scaled = {}
    for bi, buckets_here in enumerate(present):
        prev = prev_ref[bi] > 0
        for r0 in range(0, DIL_BAND, rb):
            bucket = bucket_ref[bi, r0:r0 + rb, :]
            accs = [jnp.full(bucket.shape, NEG, F32)] * nh
            for b in buckets_here:
                hit = bucket == b
                for hd in range(nh):
                    if (hd, b) not in scaled:
                        scaled[hd, b] = rel_ref[hd, b] * LOG2E
                accs = [jnp.where(hit, scaled[hd, b], acc) for hd, acc in enumerate(accs)]
            for hd in range(nh):
                o_ref[bi, 0, hd, r0:r0 + rb, :] = accs[hd]
                o_ref[bi, 1, hd, r0:r0 + rb, :] = jnp.where(prev, NEG, accs[hd])


def _dil_bias_call(rel_bias, buckets, prev):
    nbr = buckets.shape[0]
    nh = rel_bias.shape[0]
    tile = buckets.shape[1:]
    present = tuple(tuple(int(b) for b in np.unique(t) if b >= 0) for t in buckets)
    whole = lambda shape: pl.BlockSpec(shape, lambda i: (0,) * len(shape))
    out_shape = (nbr, 2, nh) + tile
    return pl.pallas_call(
        functools.partial(_dil_bias_kernel, present=present),
        grid=(1,),
        in_specs=[pl.BlockSpec(memory_space=pltpu.SMEM), whole(buckets.shape), whole(prev.shape)],
        out_specs=whole(out_shape),
        out_shape=jax.ShapeDtypeStruct(out_shape, F32),
        name="dil_bias",
    )(rel_bias, jnp.asarray(buckets), jnp.asarray(prev))


def _band_tile(q, k, v, bias, head_a):
    zero = jnp.zeros_like(q)
    q2 = jnp.concatenate([jnp.where(head_a, q, zero), jnp.where(head_a, zero, q)], axis=0).astype(BF16)
    s = _dot_nt(q2, k.astype(BF16)) + bias
    m = jnp.max(s, axis=-1, keepdims=True)
    p = jnp.exp2(s - m).astype(BF16)
    ones = jnp.ones(v.shape, BF16)
    out = _dot(p, jnp.concatenate([v.astype(BF16), ones], axis=1))
    h = DIL_BAND
    num = jnp.where(head_a, out[:h, :LANES], out[h:, :LANES])
    den = jnp.where(head_a, out[:h, LANES:], out[h:, LANES:])
    mx = jnp.where(head_a, m[:h], m[h:])
    return num, den, mx


def _dil_kernel(q_ref, kp_ref, kc_ref, vp_ref, vc_ref, bias_ref, o_ref, num_scr, den_scr, max_scr):
    first_chunk = (pl.program_id(2) == 0).astype(jnp.int32)
    head_a = lax.broadcasted_iota(jnp.int32, (1, LANES), 1) < DIL_HEAD_DIM

    def tile(bi, dil, rbase, n):
        g = DIL_RES // dil
        run = DIL_BAND // g
        static_n = isinstance(n, int)
        q_rows = pl.ds(n * run, run) if static_n else pl.ds(pl.multiple_of(n * run, run), run)
        qs, ks, vs = [], [], []
        for u in range(g):
            res = rbase + dil * u
            qs.append(q_ref[res, q_rows, :])
            if static_n and n == 0:
                ks += [kp_ref[res, pl.ds(DIL_BAND - run, run), :], kc_ref[res, pl.ds(0, run), :]]
                vs += [vp_ref[res, pl.ds(DIL_BAND - run, run), :], vc_ref[res, pl.ds(0, run), :]]
            else:
                k_rows = (pl.ds((n - 1) * run, 2 * run) if static_n
                          else pl.ds(pl.multiple_of((n - 1) * run, run), 2 * run))
                ks.append(kc_ref[res, k_rows, :])
                vs.append(vc_ref[res, k_rows, :])
        cat = lambda parts: parts[0] if len(parts) == 1 else jnp.concatenate(parts, axis=0)
        variant = first_chunk if (static_n and n == 0) else 0
        num, den, mx = _band_tile(cat(qs), cat(ks), cat(vs), bias_ref[bi, variant], head_a)
        if g == 1:
            return num, den, mx
        for u in range(g):
            res = rbase + dil * u
            part = slice(u * run, (u + 1) * run)
            num_scr[bi, res, q_rows, :] = num[part]
            den_scr[bi, res, q_rows, :] = den[part]
            max_scr[bi, res, q_rows, :] = mx[part]

    for n in range(DIL_RES):
        tile(0, 1, 0, n)
    for r4 in range(4):
        for n in range(4):
            tile(1, 4, r4, n)
    stored = len(DIL_BRANCHES) - 1
    for r in range(DIL_RES):
        num_r, den_r, max_r = tile(stored, DIL_RES, r, 0)
        ms = [max_scr[bi, r] for bi in range(stored)] + [max_r]
        nums = [num_scr[bi, r] for bi in range(stored)] + [num_r]
        dens = [den_scr[bi, r] for bi in range(stored)] + [den_r]
        m_all = functools.reduce(jnp.maximum, ms)
        es = [jnp.exp2(mb - m_all) for mb in ms]
        num = sum(e * x for e, x in zip(es, nums))
        den = sum(e * x for e, x in zip(es, dens))
        o_ref[pl.ds(r, DIL_BAND, stride=DIL_RES), :] = num / den


def _dil_call(qa, ka, va, bias):
    b, _, per, w = qa.shape
    pairs = w // LANES
    nbr = bias.shape[0]
    blk = (None, DIL_RES, DIL_BAND, LANES)
    cur = pl.BlockSpec(blk, lambda bb, p, c: (bb, 0, c, p))
    prev = pl.BlockSpec(blk, lambda bb, p, c: (bb, 0, jnp.maximum(c - 1, 0), p))
    scr = pltpu.VMEM((nbr - 1, DIL_RES, DIL_BAND, LANES), F32)
    return pl.pallas_call(
        _dil_kernel,
        grid=(b, pairs, per // DIL_BAND),
        in_specs=[cur, prev, cur, prev, cur,
                  pl.BlockSpec((nbr, 2, None, 2 * DIL_BAND, 2 * DIL_BAND), lambda bb, p, c: (0, 0, p, 0, 0))],
        out_specs=pl.BlockSpec((None, DIL_CHUNK, LANES), lambda bb, p, c: (bb, c, p)),
        out_shape=jax.ShapeDtypeStruct((b, per * DIL_RES, w), F32),
        scratch_shapes=[scr, scr, scr],
        compiler_params=pltpu.CompilerParams(dimension_semantics=("parallel", "parallel", "arbitrary"),
                                             vmem_limit_bytes=VMEM_LIMIT_BYTES),
        name="dil_attn",
    )(qa, ka, ka, va, va, bias)


def _mla_kernel(q_ref, k_ref, v_ref, o_ref, m_ref, acc_ref, s0_ref, s1_ref):
    tq = q_ref.shape[0]
    tk = MLA_TK
    nsub = tq // tk
    qi = pl.program_id(2)
    m_ref[...] = jnp.full(m_ref.shape, NEG, F32)
    acc_ref[...] = jnp.zeros(acc_ref.shape, F32)
    ones = jnp.ones((tk, LANES), BF16)
    s_bufs = (s0_ref, s1_ref)

    def scores(kblk, row0=0):
        k0 = pl.multiple_of(kblk * tk, tk)
        return _dot_nt(q_ref[row0:, :], k_ref[pl.ds(k0, tk), :])

    def update(s, kblk, row0, diagonal):
        k0 = pl.multiple_of(kblk * tk, tk)
        rows = slice(row0, tq)
        if diagonal:
            row = lax.broadcasted_iota(jnp.int32, s.shape, 0)
            col = lax.broadcasted_iota(jnp.int32, s.shape, 1)
            s = jnp.where(col <= row, s, NEG)
        m_old = m_ref[rows, :]
        m_new = jnp.maximum(m_old, jnp.max(s, axis=-1, keepdims=True))
        m_ref[rows, :] = m_new
        p = jnp.concatenate([jnp.exp2(s[:, c * LANES:(c + 1) * LANES] - m_new) for c in range(tk // LANES)],
                            axis=1).astype(BF16)
        alpha = jnp.exp2(m_old - m_new)
        vext = jnp.concatenate([v_ref[pl.ds(k0, tk), :], ones], axis=1)
        acc_ref[rows, :] = acc_ref[rows, :] * jnp.concatenate([alpha, alpha], axis=1) + _dot(p, vext)

    s0_ref[...] = scores(0)

    def body(t, carry):
        for jj in range(nsub):
            s_bufs[(jj + 1) % 2][...] = scores(nsub * t + jj + 1)
            update(s_bufs[jj % 2][...], nsub * t + jj, 0, diagonal=False)
        return carry

    lax.fori_loop(0, qi, body, 0)
    s_cur = s0_ref[...]
    for d in range(nsub):
        s_next = scores(nsub * qi + d + 1, row0=(d + 1) * tk) if d + 1 < nsub else None
        update(s_cur, nsub * qi + d, d * tk, diagonal=True)
        s_cur = s_next
    acc = acc_ref[...]
    o_ref[...] = acc[:, :MLA_V_DIM] / acc[:, MLA_V_DIM:]


def _mla_call(qm, km, vm, n_heads):
    b, s, _ = qm.shape
    tq = MLA_TQ
    assert tq % (2 * MLA_TK) == 0
    return pl.pallas_call(
        _mla_kernel,
        grid=(b, n_heads, s // tq),
        in_specs=[pl.BlockSpec((None, tq, MXU_DIM), lambda bb, h, i: (bb, i, h)),
                  pl.BlockSpec((None, s, MXU_DIM), lambda bb, h, i: (bb, 0, h)),
                  pl.BlockSpec((None, s, MLA_V_DIM), lambda bb, h, i: (bb, 0, h))],
        out_specs=pl.BlockSpec((None, tq, MLA_V_DIM), lambda bb, h, i: (bb, i, h)),
        out_shape=jax.ShapeDtypeStruct((b, s, n_heads * MLA_V_DIM), F32),
        scratch_shapes=[pltpu.VMEM((tq, LANES), F32), pltpu.VMEM((tq, 2 * MLA_V_DIM), F32),
                        pltpu.VMEM((tq, MLA_TK), F32), pltpu.VMEM((tq, MLA_TK), F32)],
        compiler_params=pltpu.CompilerParams(dimension_semantics=("parallel", "parallel", "arbitrary"),
                                             vmem_limit_bytes=VMEM_LIMIT_BYTES),
        name="mla_attn",
    )(qm, km, vm)


def _rope_tables(seq):
    inv_freq = ROPE_BASE ** (-np.arange(0, MLA_ROPE, 2, dtype=np.float64) / MLA_ROPE)
    ang = np.arange(seq, dtype=np.float64)[:, None] * inv_freq[None, :]
    cos, sin = np.cos(ang), np.sin(ang)
    cos_t = np.concatenate([cos, cos] * (LANES // MLA_ROPE), axis=1)
    sin_t = np.concatenate([-sin, sin] * (LANES // MLA_ROPE), axis=1)
    return jnp.asarray(cos_t, F32), jnp.asarray(sin_t, F32)


def _pad_lanes(v, width):
    return jnp.pad(v, ((0, 0), (0, width - v.shape[1])))


def kernel(x, ffn1_norm, ffn1_w_gate, ffn1_w_up, ffn1_w_down, mix_norm, w_in, dil_q_norm, dil_k_norm,
           rel_bias, mla_q_a_norm, mla_w_q_b, mla_kv_a_norm, mla_w_kv_b, mla_q_norm, mla_k_norm,
           out_norm_dil, out_norm_mla, w_out, ffn2_norm, ffn2_w_gate, ffn2_w_up, ffn2_w_down):
    batch, seq, d_model = x.shape
    depth = w_in.shape[0]
    dil_width = out_norm_dil.shape[1]
    dil_heads = dil_width // DIL_HEAD_DIM
    q_rank = mla_q_a_norm.shape[1]
    kv_rank = mla_kv_a_norm.shape[1]
    qk_dim = MLA_NOPE + MLA_ROPE
    n_mla = mla_w_q_b.shape[2] // qk_dim
    assert rel_bias.shape == (dil_heads, REL_BUCKETS)
    assert w_in.shape[2] == 3 * dil_width + q_rank + kv_rank + MLA_ROPE
    assert seq % DIL_CHUNK == 0 and seq % MLA_TQ == 0 and seq % ROW_TILE == 0
    assert (batch * seq) % FFN_STEP_ROWS == 0
    assert all(win // dil == DIL_BAND and DIL_RES % dil == 0 for win, dil in DIL_BRANCHES)
    assert DIL_BRANCHES[-1][1] == DIL_RES

    cos_t, sin_t = _rope_tables(seq)
    buckets, prev = _band_tables()
    dil_bias = _dil_bias_call(rel_bias, buckets, prev)
    dil_bias = dil_bias.reshape(dil_bias.shape[0], 2, dil_heads // 2, 2 * DIL_BAND, 2 * DIL_BAND)
    eh = jnp.asarray(np.kron(np.eye(MXU_DIM // DIL_HEAD_DIM), np.ones((DIL_HEAD_DIM, DIL_HEAD_DIM))), BF16)
    perm = jnp.asarray(_residue_major_perm(ROW_TILE), BF16)

    x2d = x.reshape(batch * seq, d_model)
    row = lambda v: v.reshape(1, -1)
    for l in range(depth):
        x2d = _ffn_call(x2d, row(ffn1_norm[l]), ffn1_w_gate[l], ffn1_w_up[l], ffn1_w_down[l])

        w_in_t = jnp.swapaxes(w_in[l], 0, 1)
        wkpe_t = jnp.pad(w_in_t[w_in.shape[2] - MLA_ROPE:], ((0, LANES - MLA_ROPE), (0, 0))).astype(BF16)
        wqb = mla_w_q_b[l].reshape(q_rank, n_mla, qk_dim)
        wqb = jnp.concatenate(
            [wqb[:, :, :MLA_NOPE].reshape(q_rank, n_mla * MLA_NOPE),
             jnp.pad(wqb[:, :, MLA_NOPE:], ((0, 0), (0, 0), (0, LANES - MLA_ROPE))).reshape(q_rank, n_mla * LANES)],
            axis=1).astype(BF16)
        wkvb = mla_w_kv_b[l].reshape(kv_rank, n_mla, MLA_NOPE + MLA_V_DIM)
        wkvb = jnp.concatenate([wkvb[:, :, :MLA_NOPE].reshape(kv_rank, n_mla * MLA_NOPE),
                                wkvb[:, :, MLA_NOPE:].reshape(kv_rank, n_mla * MLA_V_DIM)], axis=1).astype(BF16)
        gqa = row(jnp.tile(dil_q_norm[l], dil_heads)) * (LOG2E * DIL_HEAD_DIM ** -0.5)
        gka = row(jnp.tile(dil_k_norm[l], dil_heads))
        gq = row(mla_q_norm[l]) * (LOG2E * qk_dim ** -0.5)
        gk = row(mla_k_norm[l])
        gqn, gqr = gq[:, :MLA_NOPE], _pad_lanes(gq[:, MLA_NOPE:], LANES)
        gkn, gkr = gk[:, :MLA_NOPE], _pad_lanes(gk[:, MLA_NOPE:], LANES)

        consts = [row(mix_norm[l]), w_in_t, wkpe_t, eh, perm, gqa, gka, row(mla_q_a_norm[l]), wqb,
                  row(mla_kv_a_norm[l]), wkvb, gqn, gqr, gkn, gkr]
        qa, ka, va, qm, km, vm = _proj_call(x2d, batch, seq, consts, cos_t, sin_t,
                                            n_mla_heads=n_mla, dil_width=dil_width)

        shp = lambda a: a.reshape(batch, seq, a.shape[1])
        o_dil = _dil_call(qa, ka, va, dil_bias)
        o_mla = _mla_call(shp(qm), shp(km), shp(vm), n_mla)

        x2d = _out_ffn_call(x2d, o_dil.reshape(batch * seq, -1), o_mla.reshape(batch * seq, -1),
                            row(out_norm_dil[l]), row(out_norm_mla[l]), w_out[l].astype(BF16),
                            row(ffn2_norm[l]), ffn2_w_gate[l], ffn2_w_up[l], ffn2_w_down[l])
    return x2d.reshape(batch, seq, d_model)
```

```python
import functools

import numpy as np
import jax
import jax.numpy as jnp
from jax import lax
from jax.experimental import pallas as pl
from jax.experimental.pallas import tpu as pltpu

F32 = jnp.float32
BF16 = jnp.bfloat16

DIL_HEAD_DIM = 64
DIL_BRANCHES = ((128, 1), (512, 4), (2048, 16))
DIL_BAND = 128
DIL_RES = 16
MLA_NOPE = 128
MLA_ROPE = 64
MLA_V_DIM = 128
ROPE_BASE = 10000.0
REL_BUCKETS = 32
REL_MAX_DIST = 2048
FFN_RESID = 0.5
EPS = 1e-6

LANES = 128
MXU_DIM = 256
VMEM_LIMIT_BYTES = 56 * 1024 * 1024
VMEM_LIMIT_FFN_BYTES = 60 * 1024 * 1024

NEG = float("-inf")
LOG2E = 1.4426950408889634

ROW_TILE = 512
FFN_STEP_ROWS = 2 * ROW_TILE
FFN_TF = 256
DIL_CHUNK = DIL_BAND * DIL_RES
MLA_TQ = 2048
MLA_TK = 512


def _dot(a, b):
    return jnp.dot(a, b, preferred_element_type=F32)


def _dot_nt(a, b):
    return lax.dot_general(a, b, (((1,), (1,)), ((), ())), preferred_element_type=F32)


def _rms(x, g):
    return x * lax.rsqrt(jnp.mean(x * x, axis=-1, keepdims=True) + EPS) * g


def _residue_major_perm(rows):
    per = rows // DIL_RES
    dst = np.arange(rows)
    src = DIL_RES * (dst % per) + dst // per
    p = np.zeros((rows, rows), np.float32)
    p[dst, src] = 1.0
    return p


def _swiglu_residual(x, g_ref, wg_ref, wu_ref, wd_ref, act_ref):
    h = _rms(x, g_ref[...]).astype(BF16)
    d_ff = wg_ref.shape[1]
    for c in range(d_ff // FFN_TF):
        sl = slice(c * FFN_TF, (c + 1) * FFN_TF)
        gate = _dot(h, wg_ref[:, sl].astype(BF16))
        up = _dot(h, wu_ref[:, sl].astype(BF16))
        act_ref[:, sl] = (gate * jax.nn.sigmoid(gate) * up).astype(BF16)
    return x + FFN_RESID * _dot(act_ref[...], wd_ref[...].astype(BF16))


def _ffn_kernel(x_ref, g_ref, wg_ref, wu_ref, wd_ref, o_ref, act_ref):
    for r0 in range(0, x_ref.shape[0], ROW_TILE):
        rows = slice(r0, r0 + ROW_TILE)
        o_ref[rows, :] = _swiglu_residual(x_ref[rows, :], g_ref, wg_ref, wu_ref, wd_ref, act_ref)


def _out_ffn_kernel(x_ref, od_ref, om_ref, gd_ref, gm_ref, wo_ref, g_ref, wg_ref, wu_ref, wd_ref,
                    o_ref, act_ref):
    wd_rows = od_ref.shape[1]
    od = _rms(od_ref[...], gd_ref[...]).astype(BF16)
    om = _rms(om_ref[...], gm_ref[...]).astype(BF16)
    x2 = x_ref[...] + _dot(od, wo_ref[:wd_rows, :]) + _dot(om, wo_ref[wd_rows:, :])
    o_ref[...] = _swiglu_residual(x2, g_ref, wg_ref, wu_ref, wd_ref, act_ref)


def _resident(shape):
    return pl.BlockSpec(shape, lambda *_: (0,) * len(shape), pipeline_mode=pl.Buffered(1))


def _residue_major_block(seq, width):
    per = ROW_TILE // DIL_RES
    blocks_per_seq = seq // ROW_TILE
    return pl.BlockSpec((None, DIL_RES, per, width), lambda i: (i // blocks_per_seq, 0, i % blocks_per_seq, 0))


def _ffn_call(x2d, g, wg, wu, wd):
    m, d = x2d.shape
    f = wg.shape[1]
    row = pl.BlockSpec((FFN_STEP_ROWS, d), lambda i: (i, 0))
    return pl.pallas_call(
        _ffn_kernel,
        grid=(m // FFN_STEP_ROWS,),
        in_specs=[row, _resident((1, d)), _resident((d, f)), _resident((d, f)), _resident((f, d))],
        out_specs=row,
        out_shape=jax.ShapeDtypeStruct((m, d), F32),
        scratch_shapes=[pltpu.VMEM((ROW_TILE, f), BF16)],
        compiler_params=pltpu.CompilerParams(dimension_semantics=("parallel",),
                                             vmem_limit_bytes=VMEM_LIMIT_FFN_BYTES),
        name="ffn",
    )(x2d, g, wg, wu, wd)


def _out_ffn_call(x2d, o_dil, o_mla, gd, gm, wo, g, wg, wu, wd):
    m, d = x2d.shape
    f = wg.shape[1]
    wdil, wmla = o_dil.shape[1], o_mla.shape[1]
    row = pl.BlockSpec((ROW_TILE, d), lambda i: (i, 0))
    return pl.pallas_call(
        _out_ffn_kernel,
        grid=(m // ROW_TILE,),
        in_specs=[row,
                  pl.BlockSpec((ROW_TILE, wdil), lambda i: (i, 0)),
                  pl.BlockSpec((ROW_TILE, wmla), lambda i: (i, 0)),
                  _resident((1, wdil)), _resident((1, wmla)),
                  _resident((wdil + wmla, d)),
                  _resident((1, d)), _resident((d, f)), _resident((d, f)), _resident((f, d))],
        out_specs=row,
        out_shape=jax.ShapeDtypeStruct((m, d), F32),
        scratch_shapes=[pltpu.VMEM((ROW_TILE, f), BF16)],
        compiler_params=pltpu.CompilerParams(dimension_semantics=("parallel",),
                                             vmem_limit_bytes=VMEM_LIMIT_BYTES),
        name="out_ffn",
    )(x2d, o_dil, o_mla, gd, gm, wo, g, wg, wu, wd)


def _rope(x, cos, sin_signed):
    lane = lax.broadcasted_iota(jnp.int32, x.shape, 1)
    first_half = (lane % MLA_ROPE) < (MLA_ROPE // 2)
    partner = jnp.where(first_half,
                        pltpu.roll(x, LANES - MLA_ROPE // 2, 1),
                        pltpu.roll(x, MLA_ROPE // 2, 1))
    return x * cos + partner * sin_signed


def _proj_kernel(x_ref, gmix_ref, wint_ref, wkpet_ref, eh_ref, perm_ref, gqa_ref, gka_ref,
                 gcq_ref, wqb_ref, gckv_ref, wkvb_ref,
                 gqn_ref, gqr_ref, gkn_ref, gkr_ref, cos_ref, sin_ref,
                 qa_ref, ka_ref, va_ref, qm_ref, km_ref, vm_ref, *, n_mla_heads, dil_width):
    h = _rms(x_ref[...], gmix_ref[...]).astype(BF16)
    w = dil_width
    o = 3 * w
    q_rank = gcq_ref.shape[1]
    kv_rank = gckv_ref.shape[1]
    qk_dim = MLA_NOPE + MLA_ROPE
    nh = n_mla_heads
    cos = cos_ref[...]
    sin = sin_ref[...]

    w_mla = jnp.concatenate([wint_ref[o:o + q_rank + kv_rank, :].astype(BF16), wkpet_ref[...]], axis=0)
    pm = _dot_nt(h, w_mla)
    cq = pm[:, :q_rank]
    ckv = pm[:, q_rank:q_rank + kv_rank]
    kpe = pm[:, q_rank + kv_rank:]

    qb = _dot(_rms(cq, gcq_ref[...]).astype(BF16), wqb_ref[...])
    for hd in range(nh):
        qn = qb[:, hd * MLA_NOPE:(hd + 1) * MLA_NOPE]
        qr = qb[:, nh * MLA_NOPE + hd * LANES:nh * MLA_NOPE + (hd + 1) * LANES]
        ss = jnp.sum(qn * qn, axis=-1, keepdims=True) + jnp.sum(qr * qr, axis=-1, keepdims=True)
        r = lax.rsqrt(ss * (1.0 / qk_dim) + EPS)
        qm_ref[:, hd * MXU_DIM:hd * MXU_DIM + LANES] = (qn * r * gqn_ref[...]).astype(BF16)
        qm_ref[:, hd * MXU_DIM + LANES:(hd + 1) * MXU_DIM] = _rope(qr * r * gqr_ref[...], cos, sin).astype(BF16)

    kvb = _dot(_rms(ckv, gckv_ref[...]).astype(BF16), wkvb_ref[...])
    ss_pe = jnp.sum(kpe * kpe, axis=-1, keepdims=True)
    kpe_rot = _rope(kpe * gkr_ref[...], cos, sin)
    for hd in range(nh):
        kn = kvb[:, hd * MLA_NOPE:(hd + 1) * MLA_NOPE]
        ss = jnp.sum(kn * kn, axis=-1, keepdims=True) + ss_pe
        r = lax.rsqrt(ss * (1.0 / qk_dim) + EPS)
        km_ref[:, hd * MXU_DIM:hd * MXU_DIM + LANES] = (kn * r * gkn_ref[...]).astype(BF16)
        km_ref[:, hd * MXU_DIM + LANES:(hd + 1) * MXU_DIM] = (kpe_rot * r).astype(BF16)
    vm_ref[...] = kvb[:, nh * MLA_NOPE:].astype(BF16)

    hp = _dot(perm_ref[...], h).astype(BF16)
    for c, (g_ref, dst) in enumerate(((gqa_ref, qa_ref), (gka_ref, ka_ref))):
        src = _dot_nt(hp, wint_ref[c * w:(c + 1) * w, :].astype(BF16))
        sq = (src * src).astype(BF16)
        ms = jnp.concatenate([_dot(sq[:, c0:c0 + MXU_DIM], eh_ref[...]) for c0 in range(0, w, MXU_DIM)],
                             axis=1) * (1.0 / DIL_HEAD_DIM)
        dst[...] = (src * lax.rsqrt(ms + EPS) * g_ref[...]).reshape(dst.shape)
    va_ref[...] = _dot_nt(hp, wint_ref[2 * w:o, :].astype(BF16)).reshape(va_ref.shape)


def _proj_call(x2d, batch, seq, consts, cos, sin, *, n_mla_heads, dil_width):
    m, d = x2d.shape
    tm = ROW_TILE
    n_seq_blocks = seq // tm
    row = lambda width: pl.BlockSpec((tm, width), lambda i: (i, 0))
    pos = pl.BlockSpec((tm, LANES), lambda i: (i % n_seq_blocks, 0))
    mla_w = n_mla_heads * MXU_DIM
    dil_shape = jax.ShapeDtypeStruct((batch, DIL_RES, seq // DIL_RES, dil_width), F32)
    return pl.pallas_call(
        functools.partial(_proj_kernel, n_mla_heads=n_mla_heads, dil_width=dil_width),
        grid=(m // tm,),
        in_specs=[row(d)] + [_resident(c.shape) for c in consts] + [pos, pos],
        out_specs=[_residue_major_block(seq, dil_width)] * 3
        + [row(mla_w), row(mla_w), row(n_mla_heads * MLA_V_DIM)],
        out_shape=[dil_shape] * 3
        + [jax.ShapeDtypeStruct((m, mla_w), BF16)] * 2
        + [jax.ShapeDtypeStruct((m, n_mla_heads * MLA_V_DIM), BF16)],
        compiler_params=pltpu.CompilerParams(dimension_semantics=("parallel",),
                                             vmem_limit_bytes=VMEM_LIMIT_BYTES),
        name="proj",
    )(x2d, *consts, cos, sin)


def _t5_bucket(dist):
    max_exact = REL_BUCKETS // 2
    d = np.maximum(dist, 1).astype(np.float32)
    large = max_exact + (np.log(d / max_exact) / np.log(REL_MAX_DIST / max_exact)
                         * (REL_BUCKETS - max_exact)).astype(np.int32)
    large = np.minimum(large, REL_BUCKETS - 1)
    return np.where(dist < max_exact, dist, large).astype(np.int32)


def _band_tables():
    rho = np.arange(DIL_BAND)
    kap = np.arange(2 * DIL_BAND)
    buckets, prev = [], []
    for _, dil in DIL_BRANCHES:
        g = DIL_RES // dil
        run_q = DIL_BAND // g
        pos_q = g * (rho % run_q) + rho // run_q
        pos_k = g * (kap % (2 * run_q)) + kap // (2 * run_q) - DIL_BAND
        delta = pos_q[:, None] - pos_k[None, :]
        valid = (delta >= 0) & (delta <= DIL_BAND)
        buckets.append(np.where(valid, _t5_bucket(np.clip(delta, 0, None) * dil), -1))
        prev.append((pos_k < 0)[None, :])
    return np.stack(buckets).astype(np.int32), np.stack(prev).astype(np.int32)


def _dil_bias_kernel(rel_ref, bucket_ref, prev_ref, o_ref, *, present):
    nh = o_ref.shape[2]
    rb = 16
    scaled = {}
    for bi, buckets_here in enumerate(present):
        prev = prev_ref[bi] > 0
        for r0 in range(0, DIL_BAND, rb):
            bucket = bucket_ref[bi, r0:r0 + rb, :]
            accs = [jnp.full(bucket.shape, NEG, F32)] * nh
            for b in buckets_here:
                hit = bucket == b
                for hd in range(nh):
                    if (hd, b) not in scaled:
                        scaled[hd, b] = rel_ref[hd, b] * LOG2E
                accs = [jnp.where(hit, scaled[hd, b], acc) for hd, acc in enumerate(accs)]
            for hd in range(nh):
                o_ref[bi, 0, hd, r0:r0 + rb, :] = accs[hd]
                o_ref[bi, 1, hd, r0:r0 + rb, :] = jnp.where(prev, NEG, accs[hd])


def _dil_bias_call(rel_bias, buckets, prev):
    nbr = buckets.shape[0]
    nh = rel_bias.shape[0]
    tile = buckets.shape[1:]
    present = tuple(tuple(int(b) for b in np.unique(t) if b >= 0) for t in buckets)
    whole = lambda shape: pl.BlockSpec(shape, lambda i: (0,) * len(shape))
    out_shape = (nbr, 2, nh) + tile
    return pl.pallas_call(
        functools.partial(_dil_bias_kernel, present=present),
        grid=(1,),
        in_specs=[pl.BlockSpec(memory_space=pltpu.SMEM), whole(buckets.shape), whole(prev.shape)],
        out_specs=whole(out_shape),
        out_shape=jax.ShapeDtypeStruct(out_shape, F32),
        name="dil_bias",
    )(rel_bias, jnp.asarray(buckets), jnp.asarray(prev))


def _band_tile(q, k, v, bias, head_a):
    zero = jnp.zeros_like(q)
    q2 = jnp.concatenate([jnp.where(head_a, q, zero), jnp.where(head_a, zero, q)], axis=0).astype(BF16)
    s = _dot_nt(q2, k.astype(BF16)) + bias
    m = jnp.max(s, axis=-1, keepdims=True)
    p = jnp.exp2(s - m).astype(BF16)
    ones = jnp.ones(v.shape, BF16)
    out = _dot(p, jnp.concatenate([v.astype(BF16), ones], axis=1))
    h = DIL_BAND
    num = jnp.where(head_a, out[:h, :LANES], out[h:, :LANES])
    den = jnp.where(head_a, out[:h, LANES:], out[h:, LANES:])
    mx = jnp.where(head_a, m[:h], m[h:])
    return num, den, mx


def _dil_kernel(q_ref, kp_ref, kc_ref, vp_ref, vc_ref, bias_ref, o_ref, num_scr, den_scr, max_scr):
    first_chunk = (pl.program_id(2) == 0).astype(jnp.int32)
    head_a = lax.broadcasted_iota(jnp.int32, (1, LANES), 1) < DIL_HEAD_DIM

    def tile(bi, dil, rbase, n):
        g = DIL_RES // dil
        run = DIL_BAND // g
        static_n = isinstance(n, int)
        q_rows = pl.ds(n * run, run) if static_n else pl.ds(pl.multiple_of(n * run, run), run)
        qs, ks, vs = [], [], []
        for u in range(g):
            res = rbase + dil * u
            qs.append(q_ref[res, q_rows, :])
            if static_n and n == 0:
                ks += [kp_ref[res, pl.ds(DIL_BAND - run, run), :], kc_ref[res, pl.ds(0, run), :]]
                vs += [vp_ref[res, pl.ds(DIL_BAND - run, run), :], vc_ref[res, pl.ds(0, run), :]]
            else:
                k_rows = (pl.ds((n - 1) * run, 2 * run) if static_n
                          else pl.ds(pl.multiple_of((n - 1) * run, run), 2 * run))
                ks.append(kc_ref[res, k_rows, :])
                vs.append(vc_ref[res, k_rows, :])
        cat = lambda parts: parts[0] if len(parts) == 1 else jnp.concatenate(parts, axis=0)
        variant = first_chunk if (static_n and n == 0) else 0
        num, den, mx = _band_tile(cat(qs), cat(ks), cat(vs), bias_ref[bi, variant], head_a)
        if g == 1:
            return num, den, mx
        for u in range(g):
            res = rbase + dil * u
            part = slice(u * run, (u + 1) * run)
            num_scr[bi, res, q_rows, :] = num[part]
            den_scr[bi, res, q_rows, :] = den[part]
            max_scr[bi, res, q_rows, :] = mx[part]

    for n in range(DIL_RES):
        tile(0, 1, 0, n)
    for r4 in range(4):
        for n in range(4):
            tile(1, 4, r4, n)
    stored = len(DIL_BRANCHES) - 1
    for r in range(DIL_RES):
        num_r, den_r, max_r = tile(stored, DIL_RES, r, 0)
        ms = [max_scr[bi, r] for bi in range(stored)] + [max_r]
        nums = [num_scr[bi, r] for bi in range(stored)] + [num_r]
        dens = [den_scr[bi, r] for bi in range(stored)] + [den_r]
        m_all = functools.reduce(jnp.maximum, ms)
        es = [jnp.exp2(mb - m_all) for mb in ms]
        num = sum(e * x for e, x in zip(es, nums))
        den = sum(e * x for e, x in zip(es, dens))
        o_ref[pl.ds(r, DIL_BAND, stride=DIL_RES), :] = num / den


def _dil_call(qa, ka, va, bias):
    b, _, per, w = qa.shape
    pairs = w // LANES
    nbr = bias.shape[0]
    blk = (None, DIL_RES, DIL_BAND, LANES)
    cur = pl.BlockSpec(blk, lambda bb, p, c: (bb, 0, c, p))
    prev = pl.BlockSpec(blk, lambda bb, p, c: (bb, 0, jnp.maximum(c - 1, 0), p))
    scr = pltpu.VMEM((nbr - 1, DIL_RES, DIL_BAND, LANES), F32)
    return pl.pallas_call(
        _dil_kernel,
        grid=(b, pairs, per // DIL_BAND),
        in_specs=[cur, prev, cur, prev, cur,
                  pl.BlockSpec((nbr, 2, None, 2 * DIL_BAND, 2 * DIL_BAND), lambda bb, p, c: (0, 0, p, 0, 0))],
        out_specs=pl.BlockSpec((None, DIL_CHUNK, LANES), lambda bb, p, c: (bb, c, p)),
        out_shape=jax.ShapeDtypeStruct((b, per * DIL_RES, w), F32),
        scratch_shapes=[scr, scr, scr],
        compiler_params=pltpu.CompilerParams(dimension_semantics=("parallel", "parallel", "arbitrary"),
                                             vmem_limit_bytes=VMEM_LIMIT_BYTES),
        name="dil_attn",
    )(qa, ka, ka, va, va, bias)


def _mla_kernel(q_ref, k_ref, v_ref, o_ref, m_ref, acc_ref, s0_ref, s1_ref):
    tq = m_ref.shape[0]
    tk = MLA_TK
    nsub = tq // tk
    ones = jnp.ones((tk, LANES), BF16)
    s_bufs = (s0_ref, s1_ref)

    def q_tile(qi):
        q_lo = qi * tq
        m_ref[...] = jnp.full(m_ref.shape, NEG, F32)
        acc_ref[...] = jnp.zeros(acc_ref.shape, F32)

        def scores(kblk, row0=0):
            k0 = pl.multiple_of(kblk * tk, tk)
            return _dot_nt(q_ref[q_lo + row0:q_lo + tq, :], k_ref[pl.ds(k0, tk), :])

        def update(s, kblk, row0, diagonal):
            k0 = pl.multiple_of(kblk * tk, tk)
            rows = slice(row0, tq)
            if diagonal:
                row = lax.broadcasted_iota(jnp.int32, s.shape, 0)
                col = lax.broadcasted_iota(jnp.int32, s.shape, 1)
                s = jnp.where(col <= row, s, NEG)
            m_old = m_ref[rows, :]
            m_new = jnp.maximum(m_old, jnp.max(s, axis=-1, keepdims=True))
            m_ref[rows, :] = m_new
            p = jnp.concatenate([jnp.exp2(s[:, c * LANES:(c + 1) * LANES] - m_new) for c in range(tk // LANES)],
                                axis=1).astype(BF16)
            alpha = jnp.exp2(m_old - m_new)
            vext = jnp.concatenate([v_ref[pl.ds(k0, tk), :], ones], axis=1)
            acc_ref[rows, :] = acc_ref[rows, :] * jnp.concatenate([alpha, alpha], axis=1) + _dot(p, vext)

        s0_ref[...] = scores(0)

        def body(t, carry):
            for jj in range(nsub):
                s_bufs[(jj + 1) % 2][...] = scores(nsub * t + jj + 1)
                update(s_bufs[jj % 2][...], nsub * t + jj, 0, diagonal=False)
            return carry

        if qi:
            lax.fori_loop(0, qi, body, 0)
        s_cur = s0_ref[...]
        for d in range(nsub):
            s_next = scores(nsub * qi + d + 1, row0=(d + 1) * tk) if d + 1 < nsub else None
            update(s_cur, nsub * qi + d, d * tk, diagonal=True)
            s_cur = s_next
        acc = acc_ref[...]
        o_ref[q_lo:q_lo + tq, :] = acc[:, :MLA_V_DIM] / acc[:, MLA_V_DIM:]

    for qi in range(q_ref.shape[0] // tq):
        q_tile(qi)


def _mla_call(qm, km, vm, n_heads):
    b, s, _ = qm.shape
    tq = MLA_TQ
    assert tq % (2 * MLA_TK) == 0
    seq_block = lambda width: pl.BlockSpec((None, s, width), lambda bb, h: (bb, 0, h))
    return pl.pallas_call(
        _mla_kernel,
        grid=(b, n_heads),
        in_specs=[seq_block(MXU_DIM), seq_block(MXU_DIM), seq_block(MLA_V_DIM)],
        out_specs=seq_block(MLA_V_DIM),
        out_shape=jax.ShapeDtypeStruct((b, s, n_heads * MLA_V_DIM), F32),
        scratch_shapes=[pltpu.VMEM((tq, LANES), F32), pltpu.VMEM((tq, 2 * MLA_V_DIM), F32),
                        pltpu.VMEM((tq, MLA_TK), F32), pltpu.VMEM((tq, MLA_TK), F32)],
        compiler_params=pltpu.CompilerParams(dimension_semantics=("parallel", "parallel"),
                                             vmem_limit_bytes=VMEM_LIMIT_BYTES),
        name="mla_attn",
    )(qm, km, vm)


def _rope_tables(seq):
    inv_freq = ROPE_BASE ** (-np.arange(0, MLA_ROPE, 2, dtype=np.float64) / MLA_ROPE)
    ang = np.arange(seq, dtype=np.float64)[:, None] * inv_freq[None, :]
    cos, sin = np.cos(ang), np.sin(ang)
    cos_t = np.concatenate([cos, cos] * (LANES // MLA_ROPE), axis=1)
    sin_t = np.concatenate([-sin, sin] * (LANES // MLA_ROPE), axis=1)
    return jnp.asarray(cos_t, F32), jnp.asarray(sin_t, F32)


def _pad_lanes(v, width):
    return jnp.pad(v, ((0, 0), (0, width - v.shape[1])))


def kernel(x, ffn1_norm, ffn1_w_gate, ffn1_w_up, ffn1_w_down, mix_norm, w_in, dil_q_norm, dil_k_norm,
           rel_bias, mla_q_a_norm, mla_w_q_b, mla_kv_a_norm, mla_w_kv_b, mla_q_norm, mla_k_norm,
           out_norm_dil, out_norm_mla, w_out, ffn2_norm, ffn2_w_gate, ffn2_w_up, ffn2_w_down):
    batch, seq, d_model = x.shape
    depth = w_in.shape[0]
    dil_width = out_norm_dil.shape[1]
    dil_heads = dil_width // DIL_HEAD_DIM
    q_rank = mla_q_a_norm.shape[1]
    kv_rank = mla_kv_a_norm.shape[1]
    qk_dim = MLA_NOPE + MLA_ROPE
    n_mla = mla_w_q_b.shape[2] // qk_dim
    assert rel_bias.shape == (dil_heads, REL_BUCKETS)
    assert w_in.shape[2] == 3 * dil_width + q_rank + kv_rank + MLA_ROPE
    assert seq % DIL_CHUNK == 0 and seq % MLA_TQ == 0 and seq % ROW_TILE == 0
    assert (batch * seq) % FFN_STEP_ROWS == 0
    assert all(win // dil == DIL_BAND and DIL_RES % dil == 0 for win, dil in DIL_BRANCHES)
    assert DIL_BRANCHES[-1][1] == DIL_RES

    cos_t, sin_t = _rope_tables(seq)
    buckets, prev = _band_tables()
    dil_bias = _dil_bias_call(rel_bias, buckets, prev)
    dil_bias = dil_bias.reshape(dil_bias.shape[0], 2, dil_heads // 2, 2 * DIL_BAND, 2 * DIL_BAND)
    eh = jnp.asarray(np.kron(np.eye(MXU_DIM // DIL_HEAD_DIM), np.ones((DIL_HEAD_DIM, DIL_HEAD_DIM))), BF16)
    perm = jnp.asarray(_residue_major_perm(ROW_TILE), BF16)

    x2d = x.reshape(batch * seq, d_model)
    row = lambda v: v.reshape(1, -1)
    for l in range(depth):
        x2d = _ffn_call(x2d, row(ffn1_norm[l]), ffn1_w_gate[l], ffn1_w_up[l], ffn1_w_down[l])

        w_in_t = jnp.swapaxes(w_in[l], 0, 1)
        wkpe_t = jnp.pad(w_in_t[w_in.shape[2] - MLA_ROPE:], ((0, LANES - MLA_ROPE), (0, 0))).astype(BF16)
        wqb = mla_w_q_b[l].reshape(q_rank, n_mla, qk_dim)
        wqb = jnp.concatenate(
            [wqb[:, :, :MLA_NOPE].reshape(q_rank, n_mla * MLA_NOPE),
             jnp.pad(wqb[:, :, MLA_NOPE:], ((0, 0), (0, 0), (0, LANES - MLA_ROPE))).reshape(q_rank, n_mla * LANES)],
            axis=1).astype(BF16)
        wkvb = mla_w_kv_b[l].reshape(kv_rank, n_mla, MLA_NOPE + MLA_V_DIM)
        wkvb = jnp.concatenate([wkvb[:, :, :MLA_NOPE].reshape(kv_rank, n_mla * MLA_NOPE),
                                wkvb[:, :, MLA_NOPE:].reshape(kv_rank, n_mla * MLA_V_DIM)], axis=1).astype(BF16)
        gqa = row(jnp.tile(dil_q_norm[l], dil_heads)) * (LOG2E * DIL_HEAD_DIM ** -0.5)
        gka = row(jnp.tile(dil_k_norm[l], dil_heads))
        gq = row(mla_q_norm[l]) * (LOG2E * qk_dim ** -0.5)
        gk = row(mla_k_norm[l])
        gqn, gqr = gq[:, :MLA_NOPE], _pad_lanes(gq[:, MLA_NOPE:], LANES)
        gkn, gkr = gk[:, :MLA_NOPE], _pad_lanes(gk[:, MLA_NOPE:], LANES)

        consts = [row(mix_norm[l]), w_in_t, wkpe_t, eh, perm, gqa, gka, row(mla_q_a_norm[l]), wqb,
                  row(mla_kv_a_norm[l]), wkvb, gqn, gqr, gkn, gkr]
        qa, ka, va, qm, km, vm = _proj_call(x2d, batch, seq, consts, cos_t, sin_t,
                                            n_mla_heads=n_mla, dil_width=dil_width)

        shp = lambda a: a.reshape(batch, seq, a.shape[1])
        o_dil = _dil_call(qa, ka, va, dil_bias)
        o_mla = _mla_call(shp(qm), shp(km), shp(vm), n_mla)

        x2d = _out_ffn_call(x2d, o_dil.reshape(batch * seq, -1), o_mla.reshape(batch * seq, -1),
                            row(out_norm_dil[l]), row(out_norm_mla[l]), w_out[l].astype(BF16),
                            row(ffn2_norm[l]), ffn2_w_gate[l], ffn2_w_up[l], ffn2_w_down[l])
    return x2d.reshape(batch, seq, d_model)
```

```python
import functools

import numpy as np
import jax
import jax.numpy as jnp
from jax import lax
from jax.experimental import pallas as pl
from jax.experimental.pallas import tpu as pltpu

F32 = jnp.float32
BF16 = jnp.bfloat16

DIL_HEAD_DIM = 64
DIL_BRANCHES = ((128, 1), (512, 4), (2048, 16))
DIL_BAND = 128
DIL_RES = 16
MLA_NOPE = 128
MLA_ROPE = 64
MLA_V_DIM = 128
ROPE_BASE = 10000.0
REL_BUCKETS = 32
REL_MAX_DIST = 2048
FFN_RESID = 0.5
EPS = 1e-6

LANES = 128
MXU_DIM = 256
VMEM_LIMIT_BYTES = 56 * 1024 * 1024
VMEM_LIMIT_FFN_BYTES = 60 * 1024 * 1024

NEG = float("-inf")
LOG2E = 1.4426950408889634

ROW_TILE = 512
FFN_STEP_ROWS = 2 * ROW_TILE
FFN_TF = 256
DIL_CHUNK = DIL_BAND * DIL_RES
MLA_TQ = 2048
MLA_TK = 512


def _dot(a, b):
    return jnp.dot(a, b, preferred_element_type=F32)


def _dot_nt(a, b):
    return lax.dot_general(a, b, (((1,), (1,)), ((), ())), preferred_element_type=F32)


def _rms(x, g):
    return x * lax.rsqrt(jnp.mean(x * x, axis=-1, keepdims=True) + EPS) * g


def _residue_major_perm(rows):
    per = rows // DIL_RES
    dst = np.arange(rows)
    src = DIL_RES * (dst % per) + dst // per
    p = np.zeros((rows, rows), np.float32)
    p[dst, src] = 1.0
    return p


def _swiglu_residual(x, g_ref, wg_ref, wu_ref, wd_ref, act_ref):
    h = _rms(x, g_ref[...]).astype(BF16)
    d_ff = wg_ref.shape[1]
    for c in range(d_ff // FFN_TF):
        sl = slice(c * FFN_TF, (c + 1) * FFN_TF)
        gate = _dot(h, wg_ref[:, sl].astype(BF16))
        up = _dot(h, wu_ref[:, sl].astype(BF16))
        act_ref[:, sl] = (gate * jax.nn.sigmoid(gate) * up).astype(BF16)
    return x + FFN_RESID * _dot(act_ref[...], wd_ref[...].astype(BF16))


def _ffn_kernel(x_ref, g_ref, wg_ref, wu_ref, wd_ref, o_ref, act_ref):
    for r0 in range(0, x_ref.shape[0], ROW_TILE):
        rows = slice(r0, r0 + ROW_TILE)
        o_ref[rows, :] = _swiglu_residual(x_ref[rows, :], g_ref, wg_ref, wu_ref, wd_ref, act_ref)


def _out_ffn_kernel(x_ref, od_ref, om_ref, gd_ref, gm_ref, wo_ref, g_ref, wg_ref, wu_ref, wd_ref,
                    o_ref, act_ref):
    wd_rows = od_ref.shape[1]
    od = _rms(od_ref[...], gd_ref[...]).astype(BF16)
    om = _rms(om_ref[...], gm_ref[...]).astype(BF16)
    x2 = x_ref[...] + _dot(od, wo_ref[:wd_rows, :]) + _dot(om, wo_ref[wd_rows:, :])
    o_ref[...] = _swiglu_residual(x2, g_ref, wg_ref, wu_ref, wd_ref, act_ref)


def _resident(shape):
    return pl.BlockSpec(shape, lambda *_: (0,) * len(shape), pipeline_mode=pl.Buffered(1))


def _residue_major_block(seq, width):
    per = ROW_TILE // DIL_RES
    blocks_per_seq = seq // ROW_TILE
    return pl.BlockSpec((None, DIL_RES, per, width), lambda i: (i // blocks_per_seq, 0, i % blocks_per_seq, 0))


def _ffn_call(x2d, g, wg, wu, wd):
    m, d = x2d.shape
    f = wg.shape[1]
    row = pl.BlockSpec((FFN_STEP_ROWS, d), lambda i: (i, 0))
    return pl.pallas_call(
        _ffn_kernel,
        grid=(m // FFN_STEP_ROWS,),
        in_specs=[row, _resident((1, d)), _resident((d, f)), _resident((d, f)), _resident((f, d))],
        out_specs=row,
        out_shape=jax.ShapeDtypeStruct((m, d), F32),
        scratch_shapes=[pltpu.VMEM((ROW_TILE, f), BF16)],
        compiler_params=pltpu.CompilerParams(dimension_semantics=("parallel",),
                                             vmem_limit_bytes=VMEM_LIMIT_FFN_BYTES),
        name="ffn",
    )(x2d, g, wg, wu, wd)


def _out_ffn_call(x2d, o_dil, o_mla, gd, gm, wo, g, wg, wu, wd):
    m, d = x2d.shape
    f = wg.shape[1]
    wdil, wmla = o_dil.shape[1], o_mla.shape[1]
    row = pl.BlockSpec((ROW_TILE, d), lambda i: (i, 0))
    return pl.pallas_call(
        _out_ffn_kernel,
        grid=(m // ROW_TILE,),
        in_specs=[row,
                  pl.BlockSpec((ROW_TILE, wdil), lambda i: (i, 0)),
                  pl.BlockSpec((ROW_TILE, wmla), lambda i: (i, 0)),
                  _resident((1, wdil)), _resident((1, wmla)),
                  _resident((wdil + wmla, d)),
                  _resident((1, d)), _resident((d, f)), _resident((d, f)), _resident((f, d))],
        out_specs=row,
        out_shape=jax.ShapeDtypeStruct((m, d), F32),
        scratch_shapes=[pltpu.VMEM((ROW_TILE, f), BF16)],
        compiler_params=pltpu.CompilerParams(dimension_semantics=("parallel",),
                                             vmem_limit_bytes=VMEM_LIMIT_BYTES),
        name="out_ffn",
    )(x2d, o_dil, o_mla, gd, gm, wo, g, wg, wu, wd)


def _rope(x, cos, sin_signed):
    lane = lax.broadcasted_iota(jnp.int32, x.shape, 1)
    first_half = (lane % MLA_ROPE) < (MLA_ROPE // 2)
    partner = jnp.where(first_half,
                        pltpu.roll(x, LANES - MLA_ROPE // 2, 1),
                        pltpu.roll(x, MLA_ROPE // 2, 1))
    return x * cos + partner * sin_signed


def _proj_kernel(x_ref, gmix_ref, wint_ref, wkpet_ref, eh_ref, perm_ref, gqa_ref, gka_ref,
                 gcq_ref, wqb_ref, gckv_ref, wkvb_ref,
                 gqn_ref, gqr_ref, gkn_ref, gkr_ref, cos_ref, sin_ref,
                 qa_ref, ka_ref, va_ref, qm_ref, km_ref, vm_ref, *, n_mla_heads, dil_width):
    h = _rms(x_ref[...], gmix_ref[...]).astype(BF16)
    w = dil_width
    o = 3 * w
    q_rank = gcq_ref.shape[1]
    kv_rank = gckv_ref.shape[1]
    qk_dim = MLA_NOPE + MLA_ROPE
    nh = n_mla_heads
    cos = cos_ref[...]
    sin = sin_ref[...]

    w_mla = jnp.concatenate([wint_ref[o:o + q_rank + kv_rank, :].astype(BF16), wkpet_ref[...]], axis=0)
    pm = _dot_nt(h, w_mla)
    cq = pm[:, :q_rank]
    ckv = pm[:, q_rank:q_rank + kv_rank]
    kpe = pm[:, q_rank + kv_rank:]

    qb = _dot(_rms(cq, gcq_ref[...]).astype(BF16), wqb_ref[...])
    for hd in range(nh):
        qn = qb[:, hd * MLA_NOPE:(hd + 1) * MLA_NOPE]
        qr = qb[:, nh * MLA_NOPE + hd * LANES:nh * MLA_NOPE + (hd + 1) * LANES]
        ss = jnp.sum(qn * qn, axis=-1, keepdims=True) + jnp.sum(qr * qr, axis=-1, keepdims=True)
        r = lax.rsqrt(ss * (1.0 / qk_dim) + EPS)
        qm_ref[:, hd * MXU_DIM:hd * MXU_DIM + LANES] = (qn * r * gqn_ref[...]).astype(BF16)
        qm_ref[:, hd * MXU_DIM + LANES:(hd + 1) * MXU_DIM] = _rope(qr * r * gqr_ref[...], cos, sin).astype(BF16)

    kvb = _dot(_rms(ckv, gckv_ref[...]).astype(BF16), wkvb_ref[...])
    ss_pe = jnp.sum(kpe * kpe, axis=-1, keepdims=True)
    kpe_rot = _rope(kpe * gkr_ref[...], cos, sin)
    for hd in range(nh):
        kn = kvb[:, hd * MLA_NOPE:(hd + 1) * MLA_NOPE]
        ss = jnp.sum(kn * kn, axis=-1, keepdims=True) + ss_pe
        r = lax.rsqrt(ss * (1.0 / qk_dim) + EPS)
        km_ref[:, hd * MXU_DIM:hd * MXU_DIM + LANES] = (kn * r * gkn_ref[...]).astype(BF16)
        km_ref[:, hd * MXU_DIM + LANES:(hd + 1) * MXU_DIM] = (kpe_rot * r).astype(BF16)
    vm_ref[...] = kvb[:, nh * MLA_NOPE:].astype(BF16)

    hp = _dot(perm_ref[...], h).astype(BF16)
    for c, (g_ref, dst) in enumerate(((gqa_ref, qa_ref), (gka_ref, ka_ref))):
        src = _dot_nt(hp, wint_ref[c * w:(c + 1) * w, :].astype(BF16))
        sq = (src * src).astype(BF16)
        ms = jnp.concatenate([_dot(sq[:, c0:c0 + MXU_DIM], eh_ref[...]) for c0 in range(0, w, MXU_DIM)],
                             axis=1) * (1.0 / DIL_HEAD_DIM)
        dst[...] = (src * lax.rsqrt(ms + EPS) * g_ref[...]).reshape(dst.shape)
    va_ref[...] = _dot_nt(hp, wint_ref[2 * w:o, :].astype(BF16)).reshape(va_ref.shape)


def _proj_call(x2d, batch, seq, consts, cos, sin, *, n_mla_heads, dil_width):
    m, d = x2d.shape
    tm = ROW_TILE
    n_seq_blocks = seq // tm
    row = lambda width: pl.BlockSpec((tm, width), lambda i: (i, 0))
    pos = pl.BlockSpec((tm, LANES), lambda i: (i % n_seq_blocks, 0))
    mla_w = n_mla_heads * MXU_DIM
    dil_shape = jax.ShapeDtypeStruct((batch, DIL_RES, seq // DIL_RES, dil_width), F32)
    return pl.pallas_call(
        functools.partial(_proj_kernel, n_mla_heads=n_mla_heads, dil_width=dil_width),
        grid=(m // tm,),
        in_specs=[row(d)] + [_resident(c.shape) for c in consts] + [pos, pos],
        out_specs=[_residue_major_block(seq, dil_width)] * 3
        + [row(mla_w), row(mla_w), row(n_mla_heads * MLA_V_DIM)],
        out_shape=[dil_shape] * 3
        + [jax.ShapeDtypeStruct((m, mla_w), BF16)] * 2
        + [jax.ShapeDtypeStruct((m, n_mla_heads * MLA_V_DIM), BF16)],
        compiler_params=pltpu.CompilerParams(dimension_semantics=("parallel",),
                                             vmem_limit_bytes=VMEM_LIMIT_BYTES),
        name="proj",
    )(x2d, *consts, cos, sin)


def _t5_bucket(dist):
    max_exact = REL_BUCKETS // 2
    d = np.maximum(dist, 1).astype(np.float32)
    large = max_exact + (np.log(d / max_exact) / np.log(REL_MAX_DIST / max_exact)
                         * (REL_BUCKETS - max_exact)).astype(np.int32)
    large = np.minimum(large, REL_BUCKETS - 1)
    return np.where(dist < max_exact, dist, large).astype(np.int32)


def _band_tables():
    rho = np.arange(DIL_BAND)
    kap = np.arange(2 * DIL_BAND)
    buckets, prev = [], []
    for _, dil in DIL_BRANCHES:
        g = DIL_RES // dil
        run_q = DIL_BAND // g
        pos_q = g * (rho % run_q) + rho // run_q
        pos_k = g * (kap % (2 * run_q)) + kap // (2 * run_q) - DIL_BAND
        delta = pos_q[:, None] - pos_k[None, :]
        valid = (delta >= 0) & (delta <= DIL_BAND)
        buckets.append(np.where(valid, _t5_bucket(np.clip(delta, 0, None) * dil), -1))
        prev.append((pos_k < 0)[None, :])
    return np.stack(buckets).astype(np.int32), np.stack(prev).astype(np.int32)


def _dil_bias_kernel(rel_ref, bucket_ref, prev_ref, o_ref, *, present):
    nh = o_ref.shape[2]
    rb = 16
    scaled = {}
    for bi, buckets_here in enumerate(present):
        prev = prev_ref[bi] > 0
        for r0 in range(0, DIL_BAND, rb):
            bucket = bucket_ref[bi, r0:r0 + rb, :]
            accs = [jnp.full(bucket.shape, NEG, F32)] * nh
            for b in buckets_here:
                hit = bucket == b
                for hd in range(nh):
                    if (hd, b) not in scaled:
                        scaled[hd, b] = rel_ref[hd, b] * LOG2E
                accs = [jnp.where(hit, scaled[hd, b], acc) for hd, acc in enumerate(accs)]
            for hd in range(nh):
                o_ref[bi, 0, hd, r0:r0 + rb, :] = accs[hd]
                o_ref[bi, 1, hd, r0:r0 + rb, :] = jnp.where(prev, NEG, accs[hd])


def _dil_bias_call(rel_bias, buckets, prev):
    nbr = buckets.shape[0]
    nh = rel_bias.shape[0]
    tile = buckets.shape[1:]
    present = tuple(tuple(int(b) for b in np.unique(t) if b >= 0) for t in buckets)
    whole = lambda shape: pl.BlockSpec(shape, lambda i: (0,) * len(shape))
    out_shape = (nbr, 2, nh) + tile
    return pl.pallas_call(
        functools.partial(_dil_bias_kernel, present=present),
        grid=(1,),
        in_specs=[pl.BlockSpec(memory_space=pltpu.SMEM), whole(buckets.shape), whole(prev.shape)],
        out_specs=whole(out_shape),
        out_shape=jax.ShapeDtypeStruct(out_shape, F32),
        name="dil_bias",
    )(rel_bias, jnp.asarray(buckets), jnp.asarray(prev))


def _band_tile(q, k, v, bias, head_a):
    zero = jnp.zeros_like(q)
    q2 = jnp.concatenate([jnp.where(head_a, q, zero), jnp.where(head_a, zero, q)], axis=0).astype(BF16)
    s = _dot_nt(q2, k.astype(BF16)) + bias
    m = jnp.max(s, axis=-1, keepdims=True)
    p = jnp.exp2(s - m).astype(BF16)
    ones = jnp.ones(v.shape, BF16)
    out = _dot(p, jnp.concatenate([v.astype(BF16), ones], axis=1))
    h = DIL_BAND
    num = jnp.where(head_a, out[:h, :LANES], out[h:, :LANES])
    den = jnp.where(head_a, out[:h, LANES:], out[h:, LANES:])
    mx = jnp.where(head_a, m[:h], m[h:])
    return num, den, mx


def _dil_kernel(q_ref, kp_ref, kc_ref, vp_ref, vc_ref, bias_ref, o_ref, num_scr, den_scr, max_scr):
    first_chunk = (pl.program_id(2) == 0).astype(jnp.int32)
    head_a = lax.broadcasted_iota(jnp.int32, (1, LANES), 1) < DIL_HEAD_DIM

    def tile(bi, dil, rbase, n):
        g = DIL_RES // dil
        run = DIL_BAND // g
        static_n = isinstance(n, int)
        q_rows = pl.ds(n * run, run) if static_n else pl.ds(pl.multiple_of(n * run, run), run)
        qs, ks, vs = [], [], []
        for u in range(g):
            res = rbase + dil * u
            qs.append(q_ref[res, q_rows, :])
            if static_n and n == 0:
                ks += [kp_ref[res, pl.ds(DIL_BAND - run, run), :], kc_ref[res, pl.ds(0, run), :]]
                vs += [vp_ref[res, pl.ds(DIL_BAND - run, run), :], vc_ref[res, pl.ds(0, run), :]]
            else:
                k_rows = (pl.ds((n - 1) * run, 2 * run) if static_n
                          else pl.ds(pl.multiple_of((n - 1) * run, run), 2 * run))
                ks.append(kc_ref[res, k_rows, :])
                vs.append(vc_ref[res, k_rows, :])
        cat = lambda parts: parts[0] if len(parts) == 1 else jnp.concatenate(parts, axis=0)
        variant = first_chunk if (static_n and n == 0) else 0
        num, den, mx = _band_tile(cat(qs), cat(ks), cat(vs), bias_ref[bi, variant], head_a)
        if g == 1:
            return num, den, mx
        for u in range(g):
            res = rbase + dil * u
            part = slice(u * run, (u + 1) * run)
            num_scr[bi, res, q_rows, :] = num[part]
            den_scr[bi, res, q_rows, :] = den[part]
            max_scr[bi, res, q_rows, :] = mx[part]

    for n in range(DIL_RES):
        tile(0, 1, 0, n)
    for r4 in range(4):
        for n in range(4):
            tile(1, 4, r4, n)
    stored = len(DIL_BRANCHES) - 1
    for r in range(DIL_RES):
        num_r, den_r, max_r = tile(stored, DIL_RES, r, 0)
        ms = [max_scr[bi, r] for bi in range(stored)] + [max_r]
        nums = [num_scr[bi, r] for bi in range(stored)] + [num_r]
        dens = [den_scr[bi, r] for bi in range(stored)] + [den_r]
        m_all = functools.reduce(jnp.maximum, ms)
        es = [jnp.exp2(mb - m_all) for mb in ms]
        num = sum(e * x for e, x in zip(es, nums))
        den = sum(e * x for e, x in zip(es, dens))
        o_ref[pl.ds(r, DIL_BAND, stride=DIL_RES), :] = num / den


def _dil_call(qa, ka, va, bias):
    b, _, per, w = qa.shape
    pairs = w // LANES
    nbr = bias.shape[0]
    blk = (None, DIL_RES, DIL_BAND, LANES)
    cur = pl.BlockSpec(blk, lambda bb, p, c: (bb, 0, c, p))
    prev = pl.BlockSpec(blk, lambda bb, p, c: (bb, 0, jnp.maximum(c - 1, 0), p))
    scr = pltpu.VMEM((nbr - 1, DIL_RES, DIL_BAND, LANES), F32)
    return pl.pallas_call(
        _dil_kernel,
        grid=(b, pairs, per // DIL_BAND),
        in_specs=[cur, prev, cur, prev, cur,
                  pl.BlockSpec((nbr, 2, None, 2 * DIL_BAND, 2 * DIL_BAND), lambda bb, p, c: (0, 0, p, 0, 0))],
        out_specs=pl.BlockSpec((None, DIL_CHUNK, LANES), lambda bb, p, c: (bb, c, p)),
        out_shape=jax.ShapeDtypeStruct((b, per * DIL_RES, w), F32),
        scratch_shapes=[scr, scr, scr],
        compiler_params=pltpu.CompilerParams(dimension_semantics=("parallel", "parallel", "arbitrary"),
                                             vmem_limit_bytes=VMEM_LIMIT_BYTES),
        name="dil_attn",
    )(qa, ka, ka, va, va, bias)


def _mla_kernel(q_ref, k_ref, v_ref, o_ref, m_ref, acc_ref, s0_ref, s1_ref):
    tq = m_ref.shape[0]
    tk = MLA_TK
    nsub = tq // tk
    ones = jnp.ones((tk, LANES), BF16)
    s_bufs = (s0_ref, s1_ref)

    def q_tile(qi):
        q_lo = qi * tq
        m_ref[...] = jnp.full(m_ref.shape, NEG, F32)
        acc_ref[...] = jnp.zeros(acc_ref.shape, F32)

        def scores(kblk, row0=0):
            k0 = kblk * tk
            return _dot_nt(q_ref[q_lo + row0:q_lo + tq, :], k_ref[pl.ds(k0, tk), :])

        def update(s, kblk, row0, diagonal):
            k0 = kblk * tk
            rows = slice(row0, tq)
            if diagonal:
                row = lax.broadcasted_iota(jnp.int32, s.shape, 0)
                col = lax.broadcasted_iota(jnp.int32, s.shape, 1)
                s = jnp.where(col <= row, s, NEG)
            m_old = m_ref[rows, :]
            m_new = jnp.maximum(m_old, jnp.max(s, axis=-1, keepdims=True))
            m_ref[rows, :] = m_new
            p = jnp.concatenate([jnp.exp2(s[:, c * LANES:(c + 1) * LANES] - m_new) for c in range(tk // LANES)],
                                axis=1).astype(BF16)
            alpha = jnp.exp2(m_old - m_new)
            vext = jnp.concatenate([v_ref[pl.ds(k0, tk), :], ones], axis=1)
            acc_ref[rows, :] = acc_ref[rows, :] * jnp.concatenate([alpha, alpha], axis=1) + _dot(p, vext)

        s0_ref[...] = scores(0)

        for blk in range(nsub * qi):
            s_bufs[(blk + 1) % 2][...] = scores(blk + 1)
            update(s_bufs[blk % 2][...], blk, 0, diagonal=False)
        s_cur = s0_ref[...]
        for d in range(nsub):
            s_next = scores(nsub * qi + d + 1, row0=(d + 1) * tk) if d + 1 < nsub else None
            update(s_cur, nsub * qi + d, d * tk, diagonal=True)
            s_cur = s_next
        acc = acc_ref[...]
        o_ref[q_lo:q_lo + tq, :] = acc[:, :MLA_V_DIM] / acc[:, MLA_V_DIM:]

    for qi in range(q_ref.shape[0] // tq):
        q_tile(qi)


def _mla_call(qm, km, vm, n_heads):
    b, s, _ = qm.shape
    tq = MLA_TQ
    assert tq % (2 * MLA_TK) == 0
    seq_block = lambda width: pl.BlockSpec((None, s, width), lambda bb, h: (bb, 0, h))
    return pl.pallas_call(
        _mla_kernel,
        grid=(b, n_heads),
        in_specs=[seq_block(MXU_DIM), seq_block(MXU_DIM), seq_block(MLA_V_DIM)],
        out_specs=seq_block(MLA_V_DIM),
        out_shape=jax.ShapeDtypeStruct((b, s, n_heads * MLA_V_DIM), F32),
        scratch_shapes=[pltpu.VMEM((tq, LANES), F32), pltpu.VMEM((tq, 2 * MLA_V_DIM), F32),
                        pltpu.VMEM((tq, MLA_TK), F32), pltpu.VMEM((tq, MLA_TK), F32)],
        compiler_params=pltpu.CompilerParams(dimension_semantics=("parallel", "parallel"),
                                             vmem_limit_bytes=VMEM_LIMIT_BYTES),
        name="mla_attn",
    )(qm, km, vm)


def _rope_tables(seq):
    inv_freq = ROPE_BASE ** (-np.arange(0, MLA_ROPE, 2, dtype=np.float64) / MLA_ROPE)
    ang = np.arange(seq, dtype=np.float64)[:, None] * inv_freq[None, :]
    cos, sin = np.cos(ang), np.sin(ang)
    cos_t = np.concatenate([cos, cos] * (LANES // MLA_ROPE), axis=1)
    sin_t = np.concatenate([-sin, sin] * (LANES // MLA_ROPE), axis=1)
    return jnp.asarray(cos_t, F32), jnp.asarray(sin_t, F32)


def _pad_lanes(v, width):
    return jnp.pad(v, ((0, 0), (0, width - v.shape[1])))


def kernel(x, ffn1_norm, ffn1_w_gate, ffn1_w_up, ffn1_w_down, mix_norm, w_in, dil_q_norm, dil_k_norm,
           rel_bias, mla_q_a_norm, mla_w_q_b, mla_kv_a_norm, mla_w_kv_b, mla_q_norm, mla_k_norm,
           out_norm_dil, out_norm_mla, w_out, ffn2_norm, ffn2_w_gate, ffn2_w_up, ffn2_w_down):
    batch, seq, d_model = x.shape
    depth = w_in.shape[0]
    dil_width = out_norm_dil.shape[1]
    dil_heads = dil_width // DIL_HEAD_DIM
    q_rank = mla_q_a_norm.shape[1]
    kv_rank = mla_kv_a_norm.shape[1]
    qk_dim = MLA_NOPE + MLA_ROPE
    n_mla = mla_w_q_b.shape[2] // qk_dim
    assert rel_bias.shape == (dil_heads, REL_BUCKETS)
    assert w_in.shape[2] == 3 * dil_width + q_rank + kv_rank + MLA_ROPE
    assert seq % DIL_CHUNK == 0 and seq % MLA_TQ == 0 and seq % ROW_TILE == 0
    assert (batch * seq) % FFN_STEP_ROWS == 0
    assert all(win // dil == DIL_BAND and DIL_RES % dil == 0 for win, dil in DIL_BRANCHES)
    assert DIL_BRANCHES[-1][1] == DIL_RES

    cos_t, sin_t = _rope_tables(seq)
    buckets, prev = _band_tables()
    dil_bias = _dil_bias_call(rel_bias, buckets, prev)
    dil_bias = dil_bias.reshape(dil_bias.shape[0], 2, dil_heads // 2, 2 * DIL_BAND, 2 * DIL_BAND)
    eh = jnp.asarray(np.kron(np.eye(MXU_DIM // DIL_HEAD_DIM), np.ones((DIL_HEAD_DIM, DIL_HEAD_DIM))), BF16)
    perm = jnp.asarray(_residue_major_perm(ROW_TILE), BF16)

    x2d = x.reshape(batch * seq, d_model)
    row = lambda v: v.reshape(1, -1)
    for l in range(depth):
        x2d = _ffn_call(x2d, row(ffn1_norm[l]), ffn1_w_gate[l], ffn1_w_up[l], ffn1_w_down[l])

        w_in_t = jnp.swapaxes(w_in[l], 0, 1)
        wkpe_t = jnp.pad(w_in_t[w_in.shape[2] - MLA_ROPE:], ((0, LANES - MLA_ROPE), (0, 0))).astype(BF16)
        wqb = mla_w_q_b[l].reshape(q_rank, n_mla, qk_dim)
        wqb = jnp.concatenate(
            [wqb[:, :, :MLA_NOPE].reshape(q_rank, n_mla * MLA_NOPE),
             jnp.pad(wqb[:, :, MLA_NOPE:], ((0, 0), (0, 0), (0, LANES - MLA_ROPE))).reshape(q_rank, n_mla * LANES)],
            axis=1).astype(BF16)
        wkvb = mla_w_kv_b[l].reshape(kv_rank, n_mla, MLA_NOPE + MLA_V_DIM)
        wkvb = jnp.concatenate([wkvb[:, :, :MLA_NOPE].reshape(kv_rank, n_mla * MLA_NOPE),
                                wkvb[:, :, MLA_NOPE:].reshape(kv_rank, n_mla * MLA_V_DIM)], axis=1).astype(BF16)
        gqa = row(jnp.tile(dil_q_norm[l], dil_heads)) * (LOG2E * DIL_HEAD_DIM ** -0.5)
        gka = row(jnp.tile(dil_k_norm[l], dil_heads))
        gq = row(mla_q_norm[l]) * (LOG2E * qk_dim ** -0.5)
        gk = row(mla_k_norm[l])
        gqn, gqr = gq[:, :MLA_NOPE], _pad_lanes(gq[:, MLA_NOPE:], LANES)
        gkn, gkr = gk[:, :MLA_NOPE], _pad_lanes(gk[:, MLA_NOPE:], LANES)

        consts = [row(mix_norm[l]), w_in_t, wkpe_t, eh, perm, gqa, gka, row(mla_q_a_norm[l]), wqb,
                  row(mla_kv_a_norm[l]), wkvb, gqn, gqr, gkn, gkr]
        qa, ka, va, qm, km, vm = _proj_call(x2d, batch, seq, consts, cos_t, sin_t,
                                            n_mla_heads=n_mla, dil_width=dil_width)

        shp = lambda a: a.reshape(batch, seq, a.shape[1])
        o_dil = _dil_call(qa, ka, va, dil_bias)
        o_mla = _mla_call(shp(qm), shp(km), shp(vm), n_mla)

        x2d = _out_ffn_call(x2d, o_dil.reshape(batch * seq, -1), o_mla.reshape(batch * seq, -1),
                            row(out_norm_dil[l]), row(out_norm_mla[l]), w_out[l].astype(BF16),
                            row(ffn2_norm[l]), ffn2_w_gate[l], ffn2_w_up[l], ffn2_w_down[l])
    return x2d.reshape(batch, seq, d_model)
```

```python
import functools

import numpy as np
import jax
import jax.numpy as jnp
from jax import lax
from jax.experimental import pallas as pl
from jax.experimental.pallas import tpu as pltpu

F32 = jnp.float32
BF16 = jnp.bfloat16

DIL_HEAD_DIM = 64
DIL_BRANCHES = ((128, 1), (512, 4), (2048, 16))
DIL_BAND = 128
DIL_RES = 16
MLA_NOPE = 128
MLA_ROPE = 64
MLA_V_DIM = 128
ROPE_BASE = 10000.0
REL_BUCKETS = 32
REL_MAX_DIST = 2048
FFN_RESID = 0.5
EPS = 1e-6

LANES = 128
MXU_DIM = 256
VMEM_LIMIT_BYTES = 56 * 1024 * 1024
VMEM_LIMIT_FFN_BYTES = 60 * 1024 * 1024

NEG = float("-inf")
LOG2E = 1.4426950408889634

ROW_TILE = 512
FFN_STEP_ROWS = 2 * ROW_TILE
FFN_TF = 256
DIL_CHUNK = DIL_BAND * DIL_RES
DIL_STEP_CHUNKS = 2
MLA_TQ = 2048
MLA_TK = 512


def _dot(a, b):
    return jnp.dot(a, b, preferred_element_type=F32)


def _dot_nt(a, b):
    return lax.dot_general(a, b, (((1,), (1,)), ((), ())), preferred_element_type=F32)


def _rms(x, g):
    return x * lax.rsqrt(jnp.mean(x * x, axis=-1, keepdims=True) + EPS) * g


def _residue_major_perm(rows):
    per = rows // DIL_RES
    dst = np.arange(rows)
    src = DIL_RES * (dst % per) + dst // per
    p = np.zeros((rows, rows), np.float32)
    p[dst, src] = 1.0
    return p


def _swiglu_residual(x, g_ref, wg_ref, wu_ref, wd_ref, act_ref):
    h = _rms(x, g_ref[...]).astype(BF16)
    d_ff = wg_ref.shape[1]
    for c in range(d_ff // FFN_TF):
        sl = slice(c * FFN_TF, (c + 1) * FFN_TF)
        gate = _dot(h, wg_ref[:, sl].astype(BF16))
        up = _dot(h, wu_ref[:, sl].astype(BF16))
        act_ref[:, sl] = (gate * jax.nn.sigmoid(gate) * up).astype(BF16)
    return x + FFN_RESID * _dot(act_ref[...], wd_ref[...].astype(BF16))


def _ffn_kernel(x_ref, g_ref, wg_ref, wu_ref, wd_ref, o_ref, act_ref):
    for r0 in range(0, x_ref.shape[0], ROW_TILE):
        rows = slice(r0, r0 + ROW_TILE)
        o_ref[rows, :] = _swiglu_residual(x_ref[rows, :], g_ref, wg_ref, wu_ref, wd_ref, act_ref)


def _out_ffn_kernel(x_ref, od_ref, om_ref, gd_ref, gm_ref, wo_ref, g_ref, wg_ref, wu_ref, wd_ref,
                    o_ref, act_ref):
    wd_rows = od_ref.shape[1]
    od = _rms(od_ref[...], gd_ref[...]).astype(BF16)
    om = _rms(om_ref[...], gm_ref[...]).astype(BF16)
    x2 = x_ref[...] + _dot(od, wo_ref[:wd_rows, :]) + _dot(om, wo_ref[wd_rows:, :])
    o_ref[...] = _swiglu_residual(x2, g_ref, wg_ref, wu_ref, wd_ref, act_ref)


def _resident(shape):
    return pl.BlockSpec(shape, lambda *_: (0,) * len(shape), pipeline_mode=pl.Buffered(1))


def _residue_major_block(seq, width):
    per = ROW_TILE // DIL_RES
    blocks_per_seq = seq // ROW_TILE
    return pl.BlockSpec((None, DIL_RES, per, width), lambda i: (i // blocks_per_seq, 0, i % blocks_per_seq, 0))


def _ffn_call(x2d, g, wg, wu, wd):
    m, d = x2d.shape
    f = wg.shape[1]
    row = pl.BlockSpec((FFN_STEP_ROWS, d), lambda i: (i, 0))
    return pl.pallas_call(
        _ffn_kernel,
        grid=(m // FFN_STEP_ROWS,),
        in_specs=[row, _resident((1, d)), _resident((d, f)), _resident((d, f)), _resident((f, d))],
        out_specs=row,
        out_shape=jax.ShapeDtypeStruct((m, d), F32),
        scratch_shapes=[pltpu.VMEM((ROW_TILE, f), BF16)],
        compiler_params=pltpu.CompilerParams(dimension_semantics=("parallel",),
                                             vmem_limit_bytes=VMEM_LIMIT_FFN_BYTES),
        name="ffn",
    )(x2d, g, wg, wu, wd)


def _out_ffn_call(x2d, o_dil, o_mla, gd, gm, wo, g, wg, wu, wd):
    m, d = x2d.shape
    f = wg.shape[1]
    wdil, wmla = o_dil.shape[1], o_mla.shape[1]
    row = pl.BlockSpec((ROW_TILE, d), lambda i: (i, 0))
    return pl.pallas_call(
        _out_ffn_kernel,
        grid=(m // ROW_TILE,),
        in_specs=[row,
                  pl.BlockSpec((ROW_TILE, wdil), lambda i: (i, 0)),
                  pl.BlockSpec((ROW_TILE, wmla), lambda i: (i, 0)),
                  _resident((1, wdil)), _resident((1, wmla)),
                  _resident((wdil + wmla, d)),
                  _resident((1, d)), _resident((d, f)), _resident((d, f)), _resident((f, d))],
        out_specs=row,
        out_shape=jax.ShapeDtypeStruct((m, d), F32),
        scratch_shapes=[pltpu.VMEM((ROW_TILE, f), BF16)],
        compiler_params=pltpu.CompilerParams(dimension_semantics=("parallel",),
                                             vmem_limit_bytes=VMEM_LIMIT_BYTES),
        name="out_ffn",
    )(x2d, o_dil, o_mla, gd, gm, wo, g, wg, wu, wd)


def _rope(x, cos, sin_signed):
    lane = lax.broadcasted_iota(jnp.int32, x.shape, 1)
    first_half = (lane % MLA_ROPE) < (MLA_ROPE // 2)
    partner = jnp.where(first_half,
                        pltpu.roll(x, LANES - MLA_ROPE // 2, 1),
                        pltpu.roll(x, MLA_ROPE // 2, 1))
    return x * cos + partner * sin_signed


def _proj_kernel(x_ref, gmix_ref, wint_ref, wkpet_ref, eh_ref, perm_ref, gqa_ref, gka_ref,
                 gcq_ref, wqb_ref, gckv_ref, wkvb_ref,
                 gqn_ref, gqr_ref, gkn_ref, gkr_ref, cos_ref, sin_ref,
                 qa_ref, ka_ref, va_ref, qm_ref, km_ref, vm_ref, *, n_mla_heads, dil_width):
    h = _rms(x_ref[...], gmix_ref[...]).astype(BF16)
    w = dil_width
    o = 3 * w
    q_rank = gcq_ref.shape[1]
    kv_rank = gckv_ref.shape[1]
    qk_dim = MLA_NOPE + MLA_ROPE
    nh = n_mla_heads
    cos = cos_ref[...]
    sin = sin_ref[...]

    w_mla = jnp.concatenate([wint_ref[o:o + q_rank + kv_rank, :].astype(BF16), wkpet_ref[...]], axis=0)
    pm = _dot_nt(h, w_mla)
    cq = pm[:, :q_rank]
    ckv = pm[:, q_rank:q_rank + kv_rank]
    kpe = pm[:, q_rank + kv_rank:]

    qb = _dot(_rms(cq, gcq_ref[...]).astype(BF16), wqb_ref[...])
    for hd in range(nh):
        qn = qb[:, hd * MLA_NOPE:(hd + 1) * MLA_NOPE]
        qr = qb[:, nh * MLA_NOPE + hd * LANES:nh * MLA_NOPE + (hd + 1) * LANES]
        ss = jnp.sum(qn * qn, axis=-1, keepdims=True) + jnp.sum(qr * qr, axis=-1, keepdims=True)
        r = lax.rsqrt(ss * (1.0 / qk_dim) + EPS)
        qm_ref[:, hd * MXU_DIM:hd * MXU_DIM + LANES] = (qn * r * gqn_ref[...]).astype(BF16)
        qm_ref[:, hd * MXU_DIM + LANES:(hd + 1) * MXU_DIM] = _rope(qr * r * gqr_ref[...], cos, sin).astype(BF16)

    kvb = _dot(_rms(ckv, gckv_ref[...]).astype(BF16), wkvb_ref[...])
    ss_pe = jnp.sum(kpe * kpe, axis=-1, keepdims=True)
    kpe_rot = _rope(kpe * gkr_ref[...], cos, sin)
    for hd in range(nh):
        kn = kvb[:, hd * MLA_NOPE:(hd + 1) * MLA_NOPE]
        ss = jnp.sum(kn * kn, axis=-1, keepdims=True) + ss_pe
        r = lax.rsqrt(ss * (1.0 / qk_dim) + EPS)
        km_ref[:, hd * MXU_DIM:hd * MXU_DIM + LANES] = (kn * r * gkn_ref[...]).astype(BF16)
        km_ref[:, hd * MXU_DIM + LANES:(hd + 1) * MXU_DIM] = (kpe_rot * r).astype(BF16)
    vm_ref[...] = kvb[:, nh * MLA_NOPE:].astype(BF16)

    hp = _dot(perm_ref[...], h).astype(BF16)
    for c, (g_ref, dst) in enumerate(((gqa_ref, qa_ref), (gka_ref, ka_ref))):
        src = _dot_nt(hp, wint_ref[c * w:(c + 1) * w, :].astype(BF16))
        sq = (src * src).astype(BF16)
        ms = jnp.concatenate([_dot(sq[:, c0:c0 + MXU_DIM], eh_ref[...]) for c0 in range(0, w, MXU_DIM)],
                             axis=1) * (1.0 / DIL_HEAD_DIM)
        dst[...] = (src * lax.rsqrt(ms + EPS) * g_ref[...]).reshape(dst.shape)
    va_ref[...] = _dot_nt(hp, wint_ref[2 * w:o, :].astype(BF16)).reshape(va_ref.shape)


def _proj_call(x2d, batch, seq, consts, cos, sin, *, n_mla_heads, dil_width):
    m, d = x2d.shape
    tm = ROW_TILE
    n_seq_blocks = seq // tm
    row = lambda width: pl.BlockSpec((tm, width), lambda i: (i, 0))
    pos = pl.BlockSpec((tm, LANES), lambda i: (i % n_seq_blocks, 0))
    mla_w = n_mla_heads * MXU_DIM
    dil_shape = jax.ShapeDtypeStruct((batch, DIL_RES, seq // DIL_RES, dil_width), F32)
    return pl.pallas_call(
        functools.partial(_proj_kernel, n_mla_heads=n_mla_heads, dil_width=dil_width),
        grid=(m // tm,),
        in_specs=[row(d)] + [_resident(c.shape) for c in consts] + [pos, pos],
        out_specs=[_residue_major_block(seq, dil_width)] * 3
        + [row(mla_w), row(mla_w), row(n_mla_heads * MLA_V_DIM)],
        out_shape=[dil_shape] * 3
        + [jax.ShapeDtypeStruct((m, mla_w), BF16)] * 2
        + [jax.ShapeDtypeStruct((m, n_mla_heads * MLA_V_DIM), BF16)],
        compiler_params=pltpu.CompilerParams(dimension_semantics=("parallel",),
                                             vmem_limit_bytes=VMEM_LIMIT_BYTES),
        name="proj",
    )(x2d, *consts, cos, sin)


def _t5_bucket(dist):
    max_exact = REL_BUCKETS // 2
    d = np.maximum(dist, 1).astype(np.float32)
    large = max_exact + (np.log(d / max_exact) / np.log(REL_MAX_DIST / max_exact)
                         * (REL_BUCKETS - max_exact)).astype(np.int32)
    large = np.minimum(large, REL_BUCKETS - 1)
    return np.where(dist < max_exact, dist, large).astype(np.int32)


def _band_tables():
    rho = np.arange(DIL_BAND)
    kap = np.arange(2 * DIL_BAND)
    buckets, prev = [], []
    for _, dil in DIL_BRANCHES:
        g = DIL_RES // dil
        run_q = DIL_BAND // g
        pos_q = g * (rho % run_q) + rho // run_q
        pos_k = g * (kap % (2 * run_q)) + kap // (2 * run_q) - DIL_BAND
        delta = pos_q[:, None] - pos_k[None, :]
        valid = (delta >= 0) & (delta <= DIL_BAND)
        buckets.append(np.where(valid, _t5_bucket(np.clip(delta, 0, None) * dil), -1))
        prev.append((pos_k < 0)[None, :])
    return np.stack(buckets).astype(np.int32), np.stack(prev).astype(np.int32)


def _dil_bias_kernel(rel_ref, bucket_ref, prev_ref, o_ref, *, present):
    nh = o_ref.shape[2]
    rb = 16
    scaled = {}
    for bi, buckets_here in enumerate(present):
        prev = prev_ref[bi] > 0
        for r0 in range(0, DIL_BAND, rb):
            bucket = bucket_ref[bi, r0:r0 + rb, :]
            accs = [jnp.full(bucket.shape, NEG, F32)] * nh
            for b in buckets_here:
                hit = bucket == b
                for hd in range(nh):
                    if (hd, b) not in scaled:
                        scaled[hd, b] = rel_ref[hd, b] * LOG2E
                accs = [jnp.where(hit, scaled[hd, b], acc) for hd, acc in enumerate(accs)]
            for hd in range(nh):
                o_ref[bi, 0, hd, r0:r0 + rb, :] = accs[hd]
                o_ref[bi, 1, hd, r0:r0 + rb, :] = jnp.where(prev, NEG, accs[hd])


def _dil_bias_call(rel_bias, buckets, prev):
    nbr = buckets.shape[0]
    nh = rel_bias.shape[0]
    tile = buckets.shape[1:]
    present = tuple(tuple(int(b) for b in np.unique(t) if b >= 0) for t in buckets)
    whole = lambda shape: pl.BlockSpec(shape, lambda i: (0,) * len(shape))
    out_shape = (nbr, 2, nh) + tile
    return pl.pallas_call(
        functools.partial(_dil_bias_kernel, present=present),
        grid=(1,),
        in_specs=[pl.BlockSpec(memory_space=pltpu.SMEM), whole(buckets.shape), whole(prev.shape)],
        out_specs=whole(out_shape),
        out_shape=jax.ShapeDtypeStruct(out_shape, F32),
        name="dil_bias",
    )(rel_bias, jnp.asarray(buckets), jnp.asarray(prev))


def _band_tile(q, k, v, bias, head_a):
    zero = jnp.zeros_like(q)
    q2 = jnp.concatenate([jnp.where(head_a, q, zero), jnp.where(head_a, zero, q)], axis=0).astype(BF16)
    s = _dot_nt(q2, k.astype(BF16)) + bias
    m = jnp.max(s, axis=-1, keepdims=True)
    p = jnp.exp2(s - m).astype(BF16)
    ones = jnp.ones(v.shape, BF16)
    out = _dot(p, jnp.concatenate([v.astype(BF16), ones], axis=1))
    h = DIL_BAND
    num = jnp.where(head_a, out[:h, :LANES], out[h:, :LANES])
    den = jnp.where(head_a, out[:h, LANES:], out[h:, LANES:])
    mx = jnp.where(head_a, m[:h], m[h:])
    return num, den, mx


def _dil_kernel(q_ref, kp_ref, kc_ref, vp_ref, vc_ref, bias_ref, o_ref, num_scr, den_scr, max_scr):
    first_block = (pl.program_id(2) == 0).astype(jnp.int32)
    head_a = lax.broadcasted_iota(jnp.int32, (1, LANES), 1) < DIL_HEAD_DIM

    def tile(j, bi, dil, rbase, n):
        g = DIL_RES // dil
        run = DIL_BAND // g
        base = j * DIL_BAND
        q_rows = pl.ds(base + n * run, run)
        qs, ks, vs = [], [], []
        for u in range(g):
            res = rbase + dil * u
            qs.append(q_ref[res, q_rows, :])
            if n == 0 and j == 0:
                ks += [kp_ref[res, pl.ds(DIL_BAND - run, run), :], kc_ref[res, pl.ds(0, run), :]]
                vs += [vp_ref[res, pl.ds(DIL_BAND - run, run), :], vc_ref[res, pl.ds(0, run), :]]
            else:
                k_rows = pl.ds(base + (n - 1) * run, 2 * run)
                ks.append(kc_ref[res, k_rows, :])
                vs.append(vc_ref[res, k_rows, :])
        cat = lambda parts: parts[0] if len(parts) == 1 else jnp.concatenate(parts, axis=0)
        variant = first_block if (n == 0 and j == 0) else 0
        num, den, mx = _band_tile(cat(qs), cat(ks), cat(vs), bias_ref[bi, variant], head_a)
        if g == 1:
            return num, den, mx
        s_rows = pl.ds(n * run, run)
        for u in range(g):
            res = rbase + dil * u
            part = slice(u * run, (u + 1) * run)
            num_scr[j, bi, res, s_rows, :] = num[part]
            den_scr[j, bi, res, s_rows, :] = den[part]
            max_scr[j, bi, res, s_rows, :] = mx[part]

    stored = len(DIL_BRANCHES) - 1
    for j in range(q_ref.shape[1] // DIL_BAND):
        for n in range(DIL_RES):
            tile(j, 0, 1, 0, n)
        for r4 in range(4):
            for n in range(4):
                tile(j, 1, 4, r4, n)
        for r in range(DIL_RES):
            num_r, den_r, max_r = tile(j, stored, DIL_RES, r, 0)
            ms = [max_scr[j, bi, r] for bi in range(stored)] + [max_r]
            nums = [num_scr[j, bi, r] for bi in range(stored)] + [num_r]
            dens = [den_scr[j, bi, r] for bi in range(stored)] + [den_r]
            m_all = functools.reduce(jnp.maximum, ms)
            es = [jnp.exp2(mb - m_all) for mb in ms]
            num = sum(e * x for e, x in zip(es, nums))
            den = sum(e * x for e, x in zip(es, dens))
            o_ref[pl.ds(j * DIL_CHUNK + r, DIL_BAND, stride=DIL_RES), :] = num / den


def _dil_call(qa, ka, va, bias):
    b, _, per, w = qa.shape
    pairs = w // LANES
    nbr = bias.shape[0]
    rows = DIL_STEP_CHUNKS * DIL_BAND
    cur = pl.BlockSpec((None, DIL_RES, rows, LANES), lambda bb, p, c: (bb, 0, c, p))
    prev = pl.BlockSpec((None, DIL_RES, DIL_BAND, LANES),
                        lambda bb, p, c: (bb, 0, jnp.maximum(DIL_STEP_CHUNKS * c - 1, 0), p))
    scr = pltpu.VMEM((DIL_STEP_CHUNKS, nbr - 1, DIL_RES, DIL_BAND, LANES), F32)
    return pl.pallas_call(
        _dil_kernel,
        grid=(b, pairs, per // rows),
        in_specs=[cur, prev, cur, prev, cur,
                  pl.BlockSpec((nbr, 2, None, 2 * DIL_BAND, 2 * DIL_BAND), lambda bb, p, c: (0, 0, p, 0, 0))],
        out_specs=pl.BlockSpec((None, DIL_STEP_CHUNKS * DIL_CHUNK, LANES), lambda bb, p, c: (bb, c, p)),
        out_shape=jax.ShapeDtypeStruct((b, per * DIL_RES, w), F32),
        scratch_shapes=[scr, scr, scr],
        compiler_params=pltpu.CompilerParams(dimension_semantics=("parallel", "parallel", "arbitrary"),
                                             vmem_limit_bytes=VMEM_LIMIT_BYTES),
        name="dil_attn",
    )(qa, ka, ka, va, va, bias)


def _mla_kernel(q_ref, k_ref, v_ref, o_ref, m_ref, acc_ref, s0_ref, s1_ref):
    tq = m_ref.shape[0]
    tk = MLA_TK
    nsub = tq // tk
    ones = jnp.ones((tk, LANES), BF16)
    s_bufs = (s0_ref, s1_ref)

    def q_tile(qi):
        q_lo = qi * tq
        m_ref[...] = jnp.full(m_ref.shape, NEG, F32)
        acc_ref[...] = jnp.zeros(acc_ref.shape, F32)

        def scores(kblk, row0=0):
            k0 = pl.multiple_of(kblk * tk, tk)
            return _dot_nt(q_ref[q_lo + row0:q_lo + tq, :], k_ref[pl.ds(k0, tk), :])

        def update(s, kblk, row0, diagonal):
            k0 = pl.multiple_of(kblk * tk, tk)
            rows = slice(row0, tq)
            if diagonal:
                row = lax.broadcasted_iota(jnp.int32, s.shape, 0)
                col = lax.broadcasted_iota(jnp.int32, s.shape, 1)
                s = jnp.where(col <= row, s, NEG)
            m_old = m_ref[rows, :]
            m_new = jnp.maximum(m_old, jnp.max(s, axis=-1, keepdims=True))
            m_ref[rows, :] = m_new
            p = jnp.concatenate([jnp.exp2(s[:, c * LANES:(c + 1) * LANES] - m_new) for c in range(tk // LANES)],
                                axis=1).astype(BF16)
            alpha = jnp.exp2(m_old - m_new)
            vext = jnp.concatenate([v_ref[pl.ds(k0, tk), :], ones], axis=1)
            acc_ref[rows, :] = acc_ref[rows, :] * jnp.concatenate([alpha, alpha], axis=1) + _dot(p, vext)

        s0_ref[...] = scores(0)

        def body(t, carry):
            for jj in range(nsub):
                s_bufs[(jj + 1) % 2][...] = scores(nsub * t + jj + 1)
                update(s_bufs[jj % 2][...], nsub * t + jj, 0, diagonal=False)
            return carry

        if qi:
            lax.fori_loop(0, qi, body, 0)
        s_cur = s0_ref[...]
        for d in range(nsub):
            s_next = scores(nsub * qi + d + 1, row0=(d + 1) * tk) if d + 1 < nsub else None
            update(s_cur, nsub * qi + d, d * tk, diagonal=True)
            s_cur = s_next
        acc = acc_ref[...]
        o_ref[q_lo:q_lo + tq, :] = acc[:, :MLA_V_DIM] / acc[:, MLA_V_DIM:]

    for qi in range(q_ref.shape[0] // tq):
        q_tile(qi)


def _mla_call(qm, km, vm, n_heads):
    b, s, _ = qm.shape
    tq = MLA_TQ
    assert tq % (2 * MLA_TK) == 0
    seq_block = lambda width: pl.BlockSpec((None, s, width), lambda bb, h: (bb, 0, h))
    return pl.pallas_call(
        _mla_kernel,
        grid=(b, n_heads),
        in_specs=[seq_block(MXU_DIM), seq_block(MXU_DIM), seq_block(MLA_V_DIM)],
        out_specs=seq_block(MLA_V_DIM),
        out_shape=jax.ShapeDtypeStruct((b, s, n_heads * MLA_V_DIM), F32),
        scratch_shapes=[pltpu.VMEM((tq, LANES), F32), pltpu.VMEM((tq, 2 * MLA_V_DIM), F32),
                        pltpu.VMEM((tq, MLA_TK), F32), pltpu.VMEM((tq, MLA_TK), F32)],
        compiler_params=pltpu.CompilerParams(dimension_semantics=("parallel", "parallel"),
                                             vmem_limit_bytes=VMEM_LIMIT_BYTES),
        name="mla_attn",
    )(qm, km, vm)


def _rope_tables(seq):
    inv_freq = ROPE_BASE ** (-np.arange(0, MLA_ROPE, 2, dtype=np.float64) / MLA_ROPE)
    ang = np.arange(seq, dtype=np.float64)[:, None] * inv_freq[None, :]
    cos, sin = np.cos(ang), np.sin(ang)
    cos_t = np.concatenate([cos, cos] * (LANES // MLA_ROPE), axis=1)
    sin_t = np.concatenate([-sin, sin] * (LANES // MLA_ROPE), axis=1)
    return jnp.asarray(cos_t, F32), jnp.asarray(sin_t, F32)


def _pad_lanes(v, width):
    return jnp.pad(v, ((0, 0), (0, width - v.shape[1])))


def kernel(x, ffn1_norm, ffn1_w_gate, ffn1_w_up, ffn1_w_down, mix_norm, w_in, dil_q_norm, dil_k_norm,
           rel_bias, mla_q_a_norm, mla_w_q_b, mla_kv_a_norm, mla_w_kv_b, mla_q_norm, mla_k_norm,
           out_norm_dil, out_norm_mla, w_out, ffn2_norm, ffn2_w_gate, ffn2_w_up, ffn2_w_down):
    batch, seq, d_model = x.shape
    depth = w_in.shape[0]
    dil_width = out_norm_dil.shape[1]
    dil_heads = dil_width // DIL_HEAD_DIM
    q_rank = mla_q_a_norm.shape[1]
    kv_rank = mla_kv_a_norm.shape[1]
    qk_dim = MLA_NOPE + MLA_ROPE
    n_mla = mla_w_q_b.shape[2] // qk_dim
    assert rel_bias.shape == (dil_heads, REL_BUCKETS)
    assert w_in.shape[2] == 3 * dil_width + q_rank + kv_rank + MLA_ROPE
    assert seq % (DIL_STEP_CHUNKS * DIL_CHUNK) == 0 and seq % MLA_TQ == 0 and seq % ROW_TILE == 0
    assert (batch * seq) % FFN_STEP_ROWS == 0
    assert all(win // dil == DIL_BAND and DIL_RES % dil == 0 for win, dil in DIL_BRANCHES)
    assert DIL_BRANCHES[-1][1] == DIL_RES

    cos_t, sin_t = _rope_tables(seq)
    buckets, prev = _band_tables()
    dil_bias = _dil_bias_call(rel_bias, buckets, prev)
    dil_bias = dil_bias.reshape(dil_bias.shape[0], 2, dil_heads // 2, 2 * DIL_BAND, 2 * DIL_BAND)
    eh = jnp.asarray(np.kron(np.eye(MXU_DIM // DIL_HEAD_DIM), np.ones((DIL_HEAD_DIM, DIL_HEAD_DIM))), BF16)
    perm = jnp.asarray(_residue_major_perm(ROW_TILE), BF16)

    x2d = x.reshape(batch * seq, d_model)
    row = lambda v: v.reshape(1, -1)
    for l in range(depth):
        x2d = _ffn_call(x2d, row(ffn1_norm[l]), ffn1_w_gate[l], ffn1_w_up[l], ffn1_w_down[l])

        w_in_t = jnp.swapaxes(w_in[l], 0, 1)
        wkpe_t = jnp.pad(w_in_t[w_in.shape[2] - MLA_ROPE:], ((0, LANES - MLA_ROPE), (0, 0))).astype(BF16)
        wqb = mla_w_q_b[l].reshape(q_rank, n_mla, qk_dim)
        wqb = jnp.concatenate(
            [wqb[:, :, :MLA_NOPE].reshape(q_rank, n_mla * MLA_NOPE),
             jnp.pad(wqb[:, :, MLA_NOPE:], ((0, 0), (0, 0), (0, LANES - MLA_ROPE))).reshape(q_rank, n_mla * LANES)],
            axis=1).astype(BF16)
        wkvb = mla_w_kv_b[l].reshape(kv_rank, n_mla, MLA_NOPE + MLA_V_DIM)
        wkvb = jnp.concatenate([wkvb[:, :, :MLA_NOPE].reshape(kv_rank, n_mla * MLA_NOPE),
                                wkvb[:, :, MLA_NOPE:].reshape(kv_rank, n_mla * MLA_V_DIM)], axis=1).astype(BF16)
        gqa = row(jnp.tile(dil_q_norm[l], dil_heads)) * (LOG2E * DIL_HEAD_DIM ** -0.5)
        gka = row(jnp.tile(dil_k_norm[l], dil_heads))
        gq = row(mla_q_norm[l]) * (LOG2E * qk_dim ** -0.5)
        gk = row(mla_k_norm[l])
        gqn, gqr = gq[:, :MLA_NOPE], _pad_lanes(gq[:, MLA_NOPE:], LANES)
        gkn, gkr = gk[:, :MLA_NOPE], _pad_lanes(gk[:, MLA_NOPE:], LANES)

        consts = [row(mix_norm[l]), w_in_t, wkpe_t, eh, perm, gqa, gka, row(mla_q_a_norm[l]), wqb,
                  row(mla_kv_a_norm[l]), wkvb, gqn, gqr, gkn, gkr]
        qa, ka, va, qm, km, vm = _proj_call(x2d, batch, seq, consts, cos_t, sin_t,
                                            n_mla_heads=n_mla, dil_width=dil_width)

        shp = lambda a: a.reshape(batch, seq, a.shape[1])
        o_dil = _dil_call(qa, ka, va, dil_bias)
        o_mla = _mla_call(shp(qm), shp(km), shp(vm), n_mla)

        x2d = _out_ffn_call(x2d, o_dil.reshape(batch * seq, -1), o_mla.reshape(batch * seq, -1),
                            row(out_norm_dil[l]), row(out_norm_mla[l]), w_out[l].astype(BF16),
                            row(ffn2_norm[l]), ffn2_w_gate[l], ffn2_w_up[l], ffn2_w_down[l])
    return x2d.reshape(batch, seq, d_model)
```

```python
import functools

import numpy as np
import jax
import jax.numpy as jnp
from jax import lax
from jax.experimental import pallas as pl
from jax.experimental.pallas import tpu as pltpu

F32 = jnp.float32
BF16 = jnp.bfloat16

DIL_HEAD_DIM = 64
DIL_BRANCHES = ((128, 1), (512, 4), (2048, 16))
DIL_BAND = 128
DIL_RES = 16
MLA_NOPE = 128
MLA_ROPE = 64
MLA_V_DIM = 128
ROPE_BASE = 10000.0
REL_BUCKETS = 32
REL_MAX_DIST = 2048
FFN_RESID = 0.5
EPS = 1e-6

LANES = 128
MXU_DIM = 256
VMEM_LIMIT_BYTES = 56 * 1024 * 1024
VMEM_LIMIT_FFN_BYTES = 60 * 1024 * 1024

NEG = float("-inf")
LOG2E = 1.4426950408889634

ROW_TILE = 512
FFN_STEP_ROWS = 2 * ROW_TILE
FFN_TF = 256
DIL_CHUNK = DIL_BAND * DIL_RES
DIL_STEP_CHUNKS = 4
DIL_SCRATCH_SETS = 2
MLA_TQ = 2048
MLA_TK = 512


def _dot(a, b):
    return jnp.dot(a, b, preferred_element_type=F32)


def _dot_nt(a, b):
    return lax.dot_general(a, b, (((1,), (1,)), ((), ())), preferred_element_type=F32)


def _rms(x, g):
    return x * lax.rsqrt(jnp.mean(x * x, axis=-1, keepdims=True) + EPS) * g


def _residue_major_perm(rows):
    per = rows // DIL_RES
    dst = np.arange(rows)
    src = DIL_RES * (dst % per) + dst // per
    p = np.zeros((rows, rows), np.float32)
    p[dst, src] = 1.0
    return p


def _swiglu_residual(x, g_ref, wg_ref, wu_ref, wd_ref, act_ref):
    h = _rms(x, g_ref[...]).astype(BF16)
    d_ff = wg_ref.shape[1]
    for c in range(d_ff // FFN_TF):
        sl = slice(c * FFN_TF, (c + 1) * FFN_TF)
        gate = _dot(h, wg_ref[:, sl].astype(BF16))
        up = _dot(h, wu_ref[:, sl].astype(BF16))
        act_ref[:, sl] = (gate * jax.nn.sigmoid(gate) * up).astype(BF16)
    return x + FFN_RESID * _dot(act_ref[...], wd_ref[...].astype(BF16))


def _ffn_kernel(x_ref, g_ref, wg_ref, wu_ref, wd_ref, o_ref, act_ref):
    for r0 in range(0, x_ref.shape[0], ROW_TILE):
        rows = slice(r0, r0 + ROW_TILE)
        o_ref[rows, :] = _swiglu_residual(x_ref[rows, :], g_ref, wg_ref, wu_ref, wd_ref, act_ref)


def _out_ffn_kernel(x_ref, od_ref, om_ref, gd_ref, gm_ref, wo_ref, g_ref, wg_ref, wu_ref, wd_ref,
                    o_ref, act_ref):
    wd_rows = od_ref.shape[1]
    od = _rms(od_ref[...], gd_ref[...]).astype(BF16)
    om = _rms(om_ref[...], gm_ref[...]).astype(BF16)
    x2 = x_ref[...] + _dot(od, wo_ref[:wd_rows, :]) + _dot(om, wo_ref[wd_rows:, :])
    o_ref[...] = _swiglu_residual(x2, g_ref, wg_ref, wu_ref, wd_ref, act_ref)


def _resident(shape):
    return pl.BlockSpec(shape, lambda *_: (0,) * len(shape), pipeline_mode=pl.Buffered(1))


def _residue_major_block(seq, width):
    per = ROW_TILE // DIL_RES
    blocks_per_seq = seq // ROW_TILE
    return pl.BlockSpec((None, DIL_RES, per, width), lambda i: (i // blocks_per_seq, 0, i % blocks_per_seq, 0))


def _ffn_call(x2d, g, wg, wu, wd):
    m, d = x2d.shape
    f = wg.shape[1]
    row = pl.BlockSpec((FFN_STEP_ROWS, d), lambda i: (i, 0))
    return pl.pallas_call(
        _ffn_kernel,
        grid=(m // FFN_STEP_ROWS,),
        in_specs=[row, _resident((1, d)), _resident((d, f)), _resident((d, f)), _resident((f, d))],
        out_specs=row,
        out_shape=jax.ShapeDtypeStruct((m, d), F32),
        scratch_shapes=[pltpu.VMEM((ROW_TILE, f), BF16)],
        compiler_params=pltpu.CompilerParams(dimension_semantics=("parallel",),
                                             vmem_limit_bytes=VMEM_LIMIT_FFN_BYTES),
        name="ffn",
    )(x2d, g, wg, wu, wd)


def _out_ffn_call(x2d, o_dil, o_mla, gd, gm, wo, g, wg, wu, wd):
    m, d = x2d.shape
    f = wg.shape[1]
    wdil, wmla = o_dil.shape[1], o_mla.shape[1]
    row = pl.BlockSpec((ROW_TILE, d), lambda i: (i, 0))
    return pl.pallas_call(
        _out_ffn_kernel,
        grid=(m // ROW_TILE,),
        in_specs=[row,
                  pl.BlockSpec((ROW_TILE, wdil), lambda i: (i, 0)),
                  pl.BlockSpec((ROW_TILE, wmla), lambda i: (i, 0)),
                  _resident((1, wdil)), _resident((1, wmla)),
                  _resident((wdil + wmla, d)),
                  _resident((1, d)), _resident((d, f)), _resident((d, f)), _resident((f, d))],
        out_specs=row,
        out_shape=jax.ShapeDtypeStruct((m, d), F32),
        scratch_shapes=[pltpu.VMEM((ROW_TILE, f), BF16)],
        compiler_params=pltpu.CompilerParams(dimension_semantics=("parallel",),
                                             vmem_limit_bytes=VMEM_LIMIT_BYTES),
        name="out_ffn",
    )(x2d, o_dil, o_mla, gd, gm, wo, g, wg, wu, wd)


def _rope(x, cos, sin_signed):
    lane = lax.broadcasted_iota(jnp.int32, x.shape, 1)
    first_half = (lane % MLA_ROPE) < (MLA_ROPE // 2)
    partner = jnp.where(first_half,
                        pltpu.roll(x, LANES - MLA_ROPE // 2, 1),
                        pltpu.roll(x, MLA_ROPE // 2, 1))
    return x * cos + partner * sin_signed


def _proj_kernel(x_ref, gmix_ref, wint_ref, wkpet_ref, eh_ref, perm_ref, gqa_ref, gka_ref,
                 gcq_ref, wqb_ref, gckv_ref, wkvb_ref,
                 gqn_ref, gqr_ref, gkn_ref, gkr_ref, cos_ref, sin_ref,
                 qa_ref, ka_ref, va_ref, qm_ref, km_ref, vm_ref, *, n_mla_heads, dil_width):
    h = _rms(x_ref[...], gmix_ref[...]).astype(BF16)
    w = dil_width
    o = 3 * w
    q_rank = gcq_ref.shape[1]
    kv_rank = gckv_ref.shape[1]
    qk_dim = MLA_NOPE + MLA_ROPE
    nh = n_mla_heads
    cos = cos_ref[...]
    sin = sin_ref[...]

    w_mla = jnp.concatenate([wint_ref[o:o + q_rank + kv_rank, :].astype(BF16), wkpet_ref[...]], axis=0)
    pm = _dot_nt(h, w_mla)
    cq = pm[:, :q_rank]
    ckv = pm[:, q_rank:q_rank + kv_rank]
    kpe = pm[:, q_rank + kv_rank:]

    qb = _dot(_rms(cq, gcq_ref[...]).astype(BF16), wqb_ref[...])
    for hd in range(nh):
        qn = qb[:, hd * MLA_NOPE:(hd + 1) * MLA_NOPE]
        qr = qb[:, nh * MLA_NOPE + hd * LANES:nh * MLA_NOPE + (hd + 1) * LANES]
        ss = jnp.sum(qn * qn, axis=-1, keepdims=True) + jnp.sum(qr * qr, axis=-1, keepdims=True)
        r = lax.rsqrt(ss * (1.0 / qk_dim) + EPS)
        qm_ref[:, hd * MXU_DIM:hd * MXU_DIM + LANES] = (qn * r * gqn_ref[...]).astype(BF16)
        qm_ref[:, hd * MXU_DIM + LANES:(hd + 1) * MXU_DIM] = _rope(qr * r * gqr_ref[...], cos, sin).astype(BF16)

    kvb = _dot(_rms(ckv, gckv_ref[...]).astype(BF16), wkvb_ref[...])
    ss_pe = jnp.sum(kpe * kpe, axis=-1, keepdims=True)
    kpe_rot = _rope(kpe * gkr_ref[...], cos, sin)
    for hd in range(nh):
        kn = kvb[:, hd * MLA_NOPE:(hd + 1) * MLA_NOPE]
        ss = jnp.sum(kn * kn, axis=-1, keepdims=True) + ss_pe
        r = lax.rsqrt(ss * (1.0 / qk_dim) + EPS)
        km_ref[:, hd * MXU_DIM:hd * MXU_DIM + LANES] = (kn * r * gkn_ref[...]).astype(BF16)
        km_ref[:, hd * MXU_DIM + LANES:(hd + 1) * MXU_DIM] = (kpe_rot * r).astype(BF16)
    vm_ref[...] = kvb[:, nh * MLA_NOPE:].astype(BF16)

    hp = _dot(perm_ref[...], h).astype(BF16)
    for c, (g_ref, dst) in enumerate(((gqa_ref, qa_ref), (gka_ref, ka_ref))):
        src = _dot_nt(hp, wint_ref[c * w:(c + 1) * w, :].astype(BF16))
        sq = (src * src).astype(BF16)
        ms = jnp.concatenate([_dot(sq[:, c0:c0 + MXU_DIM], eh_ref[...]) for c0 in range(0, w, MXU_DIM)],
                             axis=1) * (1.0 / DIL_HEAD_DIM)
        dst[...] = (src * lax.rsqrt(ms + EPS) * g_ref[...]).reshape(dst.shape)
    va_ref[...] = _dot_nt(hp, wint_ref[2 * w:o, :].astype(BF16)).reshape(va_ref.shape)


def _proj_call(x2d, batch, seq, consts, cos, sin, *, n_mla_heads, dil_width):
    m, d = x2d.shape
    tm = ROW_TILE
    n_seq_blocks = seq // tm
    row = lambda width: pl.BlockSpec((tm, width), lambda i: (i, 0))
    pos = pl.BlockSpec((tm, LANES), lambda i: (i % n_seq_blocks, 0))
    mla_w = n_mla_heads * MXU_DIM
    dil_shape = jax.ShapeDtypeStruct((batch, DIL_RES, seq // DIL_RES, dil_width), F32)
    return pl.pallas_call(
        functools.partial(_proj_kernel, n_mla_heads=n_mla_heads, dil_width=dil_width),
        grid=(m // tm,),
        in_specs=[row(d)] + [_resident(c.shape) for c in consts] + [pos, pos],
        out_specs=[_residue_major_block(seq, dil_width)] * 3
        + [row(mla_w), row(mla_w), row(n_mla_heads * MLA_V_DIM)],
        out_shape=[dil_shape] * 3
        + [jax.ShapeDtypeStruct((m, mla_w), BF16)] * 2
        + [jax.ShapeDtypeStruct((m, n_mla_heads * MLA_V_DIM), BF16)],
        compiler_params=pltpu.CompilerParams(dimension_semantics=("parallel",),
                                             vmem_limit_bytes=VMEM_LIMIT_BYTES),
        name="proj",
    )(x2d, *consts, cos, sin)


def _t5_bucket(dist):
    max_exact = REL_BUCKETS // 2
    d = np.maximum(dist, 1).astype(np.float32)
    large = max_exact + (np.log(d / max_exact) / np.log(REL_MAX_DIST / max_exact)
                         * (REL_BUCKETS - max_exact)).astype(np.int32)
    large = np.minimum(large, REL_BUCKETS - 1)
    return np.where(dist < max_exact, dist, large).astype(np.int32)


def _band_tables():
    rho = np.arange(DIL_BAND)
    kap = np.arange(2 * DIL_BAND)
    buckets, prev = [], []
    for _, dil in DIL_BRANCHES:
        g = DIL_RES // dil
        run_q = DIL_BAND // g
        pos_q = g * (rho % run_q) + rho // run_q
        pos_k = g * (kap % (2 * run_q)) + kap // (2 * run_q) - DIL_BAND
        delta = pos_q[:, None] - pos_k[None, :]
        valid = (delta >= 0) & (delta <= DIL_BAND)
        buckets.append(np.where(valid, _t5_bucket(np.clip(delta, 0, None) * dil), -1))
        prev.append((pos_k < 0)[None, :])
    return np.stack(buckets).astype(np.int32), np.stack(prev).astype(np.int32)


def _dil_bias_kernel(rel_ref, bucket_ref, prev_ref, o_ref, *, present):
    nh = o_ref.shape[2]
    rb = 16
    scaled = {}
    for bi, buckets_here in enumerate(present):
        prev = prev_ref[bi] > 0
        for r0 in range(0, DIL_BAND, rb):
            bucket = bucket_ref[bi, r0:r0 + rb, :]
            accs = [jnp.full(bucket.shape, NEG, F32)] * nh
            for b in buckets_here:
                hit = bucket == b
                for hd in range(nh):
                    if (hd, b) not in scaled:
                        scaled[hd, b] = rel_ref[hd, b] * LOG2E
                accs = [jnp.where(hit, scaled[hd, b], acc) for hd, acc in enumerate(accs)]
            for hd in range(nh):
                o_ref[bi, 0, hd, r0:r0 + rb, :] = accs[hd]
                o_ref[bi, 1, hd, r0:r0 + rb, :] = jnp.where(prev, NEG, accs[hd])


def _dil_bias_call(rel_bias, buckets, prev):
    nbr = buckets.shape[0]
    nh = rel_bias.shape[0]
    tile = buckets.shape[1:]
    present = tuple(tuple(int(b) for b in np.unique(t) if b >= 0) for t in buckets)
    whole = lambda shape: pl.BlockSpec(shape, lambda i: (0,) * len(shape))
    out_shape = (nbr, 2, nh) + tile
    return pl.pallas_call(
        functools.partial(_dil_bias_kernel, present=present),
        grid=(1,),
        in_specs=[pl.BlockSpec(memory_space=pltpu.SMEM), whole(buckets.shape), whole(prev.shape)],
        out_specs=whole(out_shape),
        out_shape=jax.ShapeDtypeStruct(out_shape, F32),
        name="dil_bias",
    )(rel_bias, jnp.asarray(buckets), jnp.asarray(prev))


def _band_tile(q, k, v, bias, head_a):
    zero = jnp.zeros_like(q)
    q2 = jnp.concatenate([jnp.where(head_a, q, zero), jnp.where(head_a, zero, q)], axis=0).astype(BF16)
    s = _dot_nt(q2, k.astype(BF16)) + bias
    m = jnp.max(s, axis=-1, keepdims=True)
    p = jnp.exp2(s - m).astype(BF16)
    ones = jnp.ones(v.shape, BF16)
    out = _dot(p, jnp.concatenate([v.astype(BF16), ones], axis=1))
    h = DIL_BAND
    num = jnp.where(head_a, out[:h, :LANES], out[h:, :LANES])
    den = jnp.where(head_a, out[:h, LANES:], out[h:, LANES:])
    mx = jnp.where(head_a, m[:h], m[h:])
    return num, den, mx


def _dil_kernel(q_ref, kp_ref, kc_ref, vp_ref, vc_ref, bias_ref, o_ref, num_scr, den_scr, max_scr):
    first_block = (pl.program_id(2) == 0).astype(jnp.int32)
    head_a = lax.broadcasted_iota(jnp.int32, (1, LANES), 1) < DIL_HEAD_DIM

    def tile(j, bi, dil, rbase, n):
        g = DIL_RES // dil
        run = DIL_BAND // g
        base = j * DIL_BAND
        sj = j % num_scr.shape[0]
        q_rows = pl.ds(base + n * run, run)
        qs, ks, vs = [], [], []
        for u in range(g):
            res = rbase + dil * u
            qs.append(q_ref[res, q_rows, :])
            if n == 0 and j == 0:
                ks += [kp_ref[res, pl.ds(DIL_BAND - run, run), :], kc_ref[res, pl.ds(0, run), :]]
                vs += [vp_ref[res, pl.ds(DIL_BAND - run, run), :], vc_ref[res, pl.ds(0, run), :]]
            else:
                k_rows = pl.ds(base + (n - 1) * run, 2 * run)
                ks.append(kc_ref[res, k_rows, :])
                vs.append(vc_ref[res, k_rows, :])
        cat = lambda parts: parts[0] if len(parts) == 1 else jnp.concatenate(parts, axis=0)
        variant = first_block if (n == 0 and j == 0) else 0
        num, den, mx = _band_tile(cat(qs), cat(ks), cat(vs), bias_ref[bi, variant], head_a)
        if g == 1:
            return num, den, mx
        s_rows = pl.ds(n * run, run)
        for u in range(g):
            res = rbase + dil * u
            part = slice(u * run, (u + 1) * run)
            num_scr[sj, bi, res, s_rows, :] = num[part]
            den_scr[sj, bi, res, s_rows, :] = den[part]
            max_scr[sj, bi, res, s_rows, :] = mx[part]

    stored = len(DIL_BRANCHES) - 1
    for j in range(q_ref.shape[1] // DIL_BAND):
        for n in range(DIL_RES):
            tile(j, 0, 1, 0, n)
        for r4 in range(4):
            for n in range(4):
                tile(j, 1, 4, r4, n)
        for r in range(DIL_RES):
            num_r, den_r, max_r = tile(j, stored, DIL_RES, r, 0)
            sj = j % num_scr.shape[0]
            ms = [max_scr[sj, bi, r] for bi in range(stored)] + [max_r]
            nums = [num_scr[sj, bi, r] for bi in range(stored)] + [num_r]
            dens = [den_scr[sj, bi, r] for bi in range(stored)] + [den_r]
            m_all = functools.reduce(jnp.maximum, ms)
            es = [jnp.exp2(mb - m_all) for mb in ms]
            num = sum(e * x for e, x in zip(es, nums))
            den = sum(e * x for e, x in zip(es, dens))
            o_ref[pl.ds(j * DIL_CHUNK + r, DIL_BAND, stride=DIL_RES), :] = num / den


def _dil_call(qa, ka, va, bias):
    b, _, per, w = qa.shape
    pairs = w // LANES
    nbr = bias.shape[0]
    rows = DIL_STEP_CHUNKS * DIL_BAND
    cur = pl.BlockSpec((None, DIL_RES, rows, LANES), lambda bb, p, c: (bb, 0, c, p))
    prev = pl.BlockSpec((None, DIL_RES, DIL_BAND, LANES),
                        lambda bb, p, c: (bb, 0, jnp.maximum(DIL_STEP_CHUNKS * c - 1, 0), p))
    scr = pltpu.VMEM((min(DIL_STEP_CHUNKS, DIL_SCRATCH_SETS), nbr - 1, DIL_RES, DIL_BAND, LANES), F32)
    return pl.pallas_call(
        _dil_kernel,
        grid=(b, pairs, per // rows),
        in_specs=[cur, prev, cur, prev, cur,
                  pl.BlockSpec((nbr, 2, None, 2 * DIL_BAND, 2 * DIL_BAND), lambda bb, p, c: (0, 0, p, 0, 0))],
        out_specs=pl.BlockSpec((None, DIL_STEP_CHUNKS * DIL_CHUNK, LANES), lambda bb, p, c: (bb, c, p)),
        out_shape=jax.ShapeDtypeStruct((b, per * DIL_RES, w), F32),
        scratch_shapes=[scr, scr, scr],
        compiler_params=pltpu.CompilerParams(dimension_semantics=("parallel", "parallel", "arbitrary"),
                                             vmem_limit_bytes=VMEM_LIMIT_BYTES),
        name="dil_attn",
    )(qa, ka, ka, va, va, bias)


def _mla_kernel(q_ref, k_ref, v_ref, o_ref, m_ref, acc_ref, s0_ref, s1_ref):
    tq = m_ref.shape[0]
    tk = MLA_TK
    nsub = tq // tk
    ones = jnp.ones((tk, LANES), BF16)
    s_bufs = (s0_ref, s1_ref)

    def q_tile(qi):
        q_lo = qi * tq
        m_ref[...] = jnp.full(m_ref.shape, NEG, F32)
        acc_ref[...] = jnp.zeros(acc_ref.shape, F32)

        def scores(kblk, row0=0):
            k0 = pl.multiple_of(kblk * tk, tk)
            return _dot_nt(q_ref[q_lo + row0:q_lo + tq, :], k_ref[pl.ds(k0, tk), :])

        def update(s, kblk, row0, diagonal):
            k0 = pl.multiple_of(kblk * tk, tk)
            rows = slice(row0, tq)
            if diagonal:
                row = lax.broadcasted_iota(jnp.int32, s.shape, 0)
                col = lax.broadcasted_iota(jnp.int32, s.shape, 1)
                s = jnp.where(col <= row, s, NEG)
            m_old = m_ref[rows, :]
            m_new = jnp.maximum(m_old, jnp.max(s, axis=-1, keepdims=True))
            m_ref[rows, :] = m_new
            p = jnp.concatenate([jnp.exp2(s[:, c * LANES:(c + 1) * LANES] - m_new) for c in range(tk // LANES)],
                                axis=1).astype(BF16)
            alpha = jnp.exp2(m_old - m_new)
            vext = jnp.concatenate([v_ref[pl.ds(k0, tk), :], ones], axis=1)
            acc_ref[rows, :] = acc_ref[rows, :] * jnp.concatenate([alpha, alpha], axis=1) + _dot(p, vext)

        s0_ref[...] = scores(0)

        def body(t, carry):
            for jj in range(nsub):
                s_bufs[(jj + 1) % 2][...] = scores(nsub * t + jj + 1)
                update(s_bufs[jj % 2][...], nsub * t + jj, 0, diagonal=False)
            return carry

        if qi:
            lax.fori_loop(0, qi, body, 0)
        s_cur = s0_ref[...]
        for d in range(nsub):
            s_next = scores(nsub * qi + d + 1, row0=(d + 1) * tk) if d + 1 < nsub else None
            update(s_cur, nsub * qi + d, d * tk, diagonal=True)
            s_cur = s_next
        acc = acc_ref[...]
        o_ref[q_lo:q_lo + tq, :] = acc[:, :MLA_V_DIM] / acc[:, MLA_V_DIM:]

    for qi in range(q_ref.shape[0] // tq):
        q_tile(qi)


def _mla_call(qm, km, vm, n_heads):
    b, s, _ = qm.shape
    tq = MLA_TQ
    assert tq % (2 * MLA_TK) == 0
    seq_block = lambda width: pl.BlockSpec((None, s, width), lambda bb, h: (bb, 0, h))
    return pl.pallas_call(
        _mla_kernel,
        grid=(b, n_heads),
        in_specs=[seq_block(MXU_DIM), seq_block(MXU_DIM), seq_block(MLA_V_DIM)],
        out_specs=seq_block(MLA_V_DIM),
        out_shape=jax.ShapeDtypeStruct((b, s, n_heads * MLA_V_DIM), F32),
        scratch_shapes=[pltpu.VMEM((tq, LANES), F32), pltpu.VMEM((tq, 2 * MLA_V_DIM), F32),
                        pltpu.VMEM((tq, MLA_TK), F32), pltpu.VMEM((tq, MLA_TK), F32)],
        compiler_params=pltpu.CompilerParams(dimension_semantics=("parallel", "parallel"),
                                             vmem_limit_bytes=VMEM_LIMIT_BYTES),
        name="mla_attn",
    )(qm, km, vm)


def _rope_tables(seq):
    inv_freq = ROPE_BASE ** (-np.arange(0, MLA_ROPE, 2, dtype=np.float64) / MLA_ROPE)
    ang = np.arange(seq, dtype=np.float64)[:, None] * inv_freq[None, :]
    cos, sin = np.cos(ang), np.sin(ang)
    cos_t = np.concatenate([cos, cos] * (LANES // MLA_ROPE), axis=1)
    sin_t = np.concatenate([-sin, sin] * (LANES // MLA_ROPE), axis=1)
    return jnp.asarray(cos_t, F32), jnp.asarray(sin_t, F32)


def _pad_lanes(v, width):
    return jnp.pad(v, ((0, 0), (0, width - v.shape[1])))


def kernel(x, ffn1_norm, ffn1_w_gate, ffn1_w_up, ffn1_w_down, mix_norm, w_in, dil_q_norm, dil_k_norm,
           rel_bias, mla_q_a_norm, mla_w_q_b, mla_kv_a_norm, mla_w_kv_b, mla_q_norm, mla_k_norm,
           out_norm_dil, out_norm_mla, w_out, ffn2_norm, ffn2_w_gate, ffn2_w_up, ffn2_w_down):
    batch, seq, d_model = x.shape
    depth = w_in.shape[0]
    dil_width = out_norm_dil.shape[1]
    dil_heads = dil_width // DIL_HEAD_DIM
    q_rank = mla_q_a_norm.shape[1]
    kv_rank = mla_kv_a_norm.shape[1]
    qk_dim = MLA_NOPE + MLA_ROPE
    n_mla = mla_w_q_b.shape[2] // qk_dim
    assert rel_bias.shape == (dil_heads, REL_BUCKETS)
    assert w_in.shape[2] == 3 * dil_width + q_rank + kv_rank + MLA_ROPE
    assert seq % (DIL_STEP_CHUNKS * DIL_CHUNK) == 0 and seq % MLA_TQ == 0 and seq % ROW_TILE == 0
    assert (batch * seq) % FFN_STEP_ROWS == 0
    assert all(win // dil == DIL_BAND and DIL_RES % dil == 0 for win, dil in DIL_BRANCHES)
    assert DIL_BRANCHES[-1][1] == DIL_RES

    cos_t, sin_t = _rope_tables(seq)
    buckets, prev = _band_tables()
    dil_bias = _dil_bias_call(rel_bias, buckets, prev)
    dil_bias = dil_bias.reshape(dil_bias.shape[0], 2, dil_heads // 2, 2 * DIL_BAND, 2 * DIL_BAND)
    eh = jnp.asarray(np.kron(np.eye(MXU_DIM // DIL_HEAD_DIM), np.ones((DIL_HEAD_DIM, DIL_HEAD_DIM))), BF16)
    perm = jnp.asarray(_residue_major_perm(ROW_TILE), BF16)

    x2d = x.reshape(batch * seq, d_model)
    row = lambda v: v.reshape(1, -1)
    for l in range(depth):
        x2d = _ffn_call(x2d, row(ffn1_norm[l]), ffn1_w_gate[l], ffn1_w_up[l], ffn1_w_down[l])

        w_in_t = jnp.swapaxes(w_in[l], 0, 1)
        wkpe_t = jnp.pad(w_in_t[w_in.shape[2] - MLA_ROPE:], ((0, LANES - MLA_ROPE), (0, 0))).astype(BF16)
        wqb = mla_w_q_b[l].reshape(q_rank, n_mla, qk_dim)
        wqb = jnp.concatenate(
            [wqb[:, :, :MLA_NOPE].reshape(q_rank, n_mla * MLA_NOPE),
             jnp.pad(wqb[:, :, MLA_NOPE:], ((0, 0), (0, 0), (0, LANES - MLA_ROPE))).reshape(q_rank, n_mla * LANES)],
            axis=1).astype(BF16)
        wkvb = mla_w_kv_b[l].reshape(kv_rank, n_mla, MLA_NOPE + MLA_V_DIM)
        wkvb = jnp.concatenate([wkvb[:, :, :MLA_NOPE].reshape(kv_rank, n_mla * MLA_NOPE),
                                wkvb[:, :, MLA_NOPE:].reshape(kv_rank, n_mla * MLA_V_DIM)], axis=1).astype(BF16)
        gqa = row(jnp.tile(dil_q_norm[l], dil_heads)) * (LOG2E * DIL_HEAD_DIM ** -0.5)
        gka = row(jnp.tile(dil_k_norm[l], dil_heads))
        gq = row(mla_q_norm[l]) * (LOG2E * qk_dim ** -0.5)
        gk = row(mla_k_norm[l])
        gqn, gqr = gq[:, :MLA_NOPE], _pad_lanes(gq[:, MLA_NOPE:], LANES)
        gkn, gkr = gk[:, :MLA_NOPE], _pad_lanes(gk[:, MLA_NOPE:], LANES)

        consts = [row(mix_norm[l]), w_in_t, wkpe_t, eh, perm, gqa, gka, row(mla_q_a_norm[l]), wqb,
                  row(mla_kv_a_norm[l]), wkvb, gqn, gqr, gkn, gkr]
        qa, ka, va, qm, km, vm = _proj_call(x2d, batch, seq, consts, cos_t, sin_t,
                                            n_mla_heads=n_mla, dil_width=dil_width)

        shp = lambda a: a.reshape(batch, seq, a.shape[1])
        o_dil = _dil_call(qa, ka, va, dil_bias)
        o_mla = _mla_call(shp(qm), shp(km), shp(vm), n_mla)

        x2d = _out_ffn_call(x2d, o_dil.reshape(batch * seq, -1), o_mla.reshape(batch * seq, -1),
                            row(out_norm_dil[l]), row(out_norm_mla[l]), w_out[l].astype(BF16),
                            row(ffn2_norm[l]), ffn2_w_gate[l], ffn2_w_up[l], ffn2_w_down[l])
    return x2d.reshape(batch, seq, d_model)
```

```python
import functools

import numpy as np
import jax
import jax.numpy as jnp
from jax import lax
from jax.experimental import pallas as pl
from jax.experimental.pallas import tpu as pltpu

F32 = jnp.float32
BF16 = jnp.bfloat16

DIL_HEAD_DIM = 64
DIL_BRANCHES = ((128, 1), (512, 4), (2048, 16))
DIL_BAND = 128
DIL_RES = 16
MLA_NOPE = 128
MLA_ROPE = 64
MLA_V_DIM = 128
ROPE_BASE = 10000.0
REL_BUCKETS = 32
REL_MAX_DIST = 2048
FFN_RESID = 0.5
EPS = 1e-6

LANES = 128
MXU_DIM = 256
VMEM_LIMIT_BYTES = 56 * 1024 * 1024
VMEM_LIMIT_FFN_BYTES = 60 * 1024 * 1024

NEG = float("-inf")
LOG2E = 1.4426950408889634

ROW_TILE = 512
FFN_STEP_ROWS = ROW_TILE
FFN_TF = 256
DIL_CHUNK = DIL_BAND * DIL_RES
DIL_STEP_CHUNKS = 2
MLA_TQ = 2048
MLA_TK = 512


def _dot(a, b):
    return jnp.dot(a, b, preferred_element_type=F32)


def _dot_nt(a, b):
    return lax.dot_general(a, b, (((1,), (1,)), ((), ())), preferred_element_type=F32)


def _rms(x, g):
    return x * lax.rsqrt(jnp.mean(x * x, axis=-1, keepdims=True) + EPS) * g


def _residue_major_perm(rows):
    per = rows // DIL_RES
    dst = np.arange(rows)
    src = DIL_RES * (dst % per) + dst // per
    p = np.zeros((rows, rows), np.float32)
    p[dst, src] = 1.0
    return p


def _swiglu_residual(x, g_ref, wg_ref, wu_ref, wd_ref, act_ref):
    h = _rms(x, g_ref[...]).astype(BF16)
    d_ff = wg_ref.shape[1]
    for c in range(d_ff // FFN_TF):
        sl = slice(c * FFN_TF, (c + 1) * FFN_TF)
        gate = _dot(h, wg_ref[:, sl].astype(BF16))
        up = _dot(h, wu_ref[:, sl].astype(BF16))
        act_ref[:, sl] = (gate * jax.nn.sigmoid(gate) * up).astype(BF16)
    return x + FFN_RESID * _dot(act_ref[...], wd_ref[...].astype(BF16))


def _ffn_kernel(x_ref, g_ref, wg_ref, wu_ref, wd_ref, o_ref, act_ref):
    for r0 in range(0, x_ref.shape[0], ROW_TILE):
        rows = slice(r0, r0 + ROW_TILE)
        o_ref[rows, :] = _swiglu_residual(x_ref[rows, :], g_ref, wg_ref, wu_ref, wd_ref, act_ref)


def _out_ffn_kernel(x_ref, od_ref, om_ref, gd_ref, gm_ref, wo_ref, g_ref, wg_ref, wu_ref, wd_ref,
                    o_ref, act_ref):
    wd_rows = od_ref.shape[1]
    od = _rms(od_ref[...], gd_ref[...]).astype(BF16)
    om = _rms(om_ref[...], gm_ref[...]).astype(BF16)
    x2 = x_ref[...] + _dot(od, wo_ref[:wd_rows, :]) + _dot(om, wo_ref[wd_rows:, :])
    o_ref[...] = _swiglu_residual(x2, g_ref, wg_ref, wu_ref, wd_ref, act_ref)


def _resident(shape):
    return pl.BlockSpec(shape, lambda *_: (0,) * len(shape), pipeline_mode=pl.Buffered(1))


def _residue_major_block(seq, width):
    per = ROW_TILE // DIL_RES
    blocks_per_seq = seq // ROW_TILE
    return pl.BlockSpec((None, DIL_RES, per, width), lambda i: (i // blocks_per_seq, 0, i % blocks_per_seq, 0))


def _ffn_call(x2d, g, wg, wu, wd):
    m, d = x2d.shape
    f = wg.shape[1]
    row = pl.BlockSpec((FFN_STEP_ROWS, d), lambda i: (i, 0))
    return pl.pallas_call(
        _ffn_kernel,
        grid=(m // FFN_STEP_ROWS,),
        in_specs=[row, _resident((1, d)), _resident((d, f)), _resident((d, f)), _resident((f, d))],
        out_specs=row,
        out_shape=jax.ShapeDtypeStruct((m, d), F32),
        scratch_shapes=[pltpu.VMEM((ROW_TILE, f), BF16)],
        compiler_params=pltpu.CompilerParams(dimension_semantics=("parallel",),
                                             vmem_limit_bytes=VMEM_LIMIT_FFN_BYTES),
        name="ffn",
    )(x2d, g, wg, wu, wd)


def _out_ffn_call(x2d, o_dil, o_mla, gd, gm, wo, g, wg, wu, wd):
    m, d = x2d.shape
    f = wg.shape[1]
    wdil, wmla = o_dil.shape[1], o_mla.shape[1]
    row = pl.BlockSpec((ROW_TILE, d), lambda i: (i, 0))
    return pl.pallas_call(
        _out_ffn_kernel,
        grid=(m // ROW_TILE,),
        in_specs=[row,
                  pl.BlockSpec((ROW_TILE, wdil), lambda i: (i, 0)),
                  pl.BlockSpec((ROW_TILE, wmla), lambda i: (i, 0)),
                  _resident((1, wdil)), _resident((1, wmla)),
                  _resident((wdil + wmla, d)),
                  _resident((1, d)), _resident((d, f)), _resident((d, f)), _resident((f, d))],
        out_specs=row,
        out_shape=jax.ShapeDtypeStruct((m, d), F32),
        scratch_shapes=[pltpu.VMEM((ROW_TILE, f), BF16)],
        compiler_params=pltpu.CompilerParams(dimension_semantics=("parallel",),
                                             vmem_limit_bytes=VMEM_LIMIT_BYTES),
        name="out_ffn",
    )(x2d, o_dil, o_mla, gd, gm, wo, g, wg, wu, wd)


def _rope(x, cos, sin_signed):
    lane = lax.broadcasted_iota(jnp.int32, x.shape, 1)
    first_half = (lane % MLA_ROPE) < (MLA_ROPE // 2)
    partner = jnp.where(first_half,
                        pltpu.roll(x, LANES - MLA_ROPE // 2, 1),
                        pltpu.roll(x, MLA_ROPE // 2, 1))
    return x * cos + partner * sin_signed


def _proj_kernel(x_ref, gmix_ref, wint_ref, wkpet_ref, eh_ref, perm_ref, gqa_ref, gka_ref,
                 gcq_ref, wqb_ref, gckv_ref, wkvb_ref,
                 gqn_ref, gqr_ref, gkn_ref, gkr_ref, cos_ref, sin_ref,
                 qa_ref, ka_ref, va_ref, qm_ref, km_ref, vm_ref, *, n_mla_heads, dil_width):
    h = _rms(x_ref[...], gmix_ref[...]).astype(BF16)
    w = dil_width
    o = 3 * w
    q_rank = gcq_ref.shape[1]
    kv_rank = gckv_ref.shape[1]
    qk_dim = MLA_NOPE + MLA_ROPE
    nh = n_mla_heads
    cos = cos_ref[...]
    sin = sin_ref[...]

    w_mla = jnp.concatenate([wint_ref[o:o + q_rank + kv_rank, :].astype(BF16), wkpet_ref[...]], axis=0)
    pm = _dot_nt(h, w_mla)
    cq = pm[:, :q_rank]
    ckv = pm[:, q_rank:q_rank + kv_rank]
    kpe = pm[:, q_rank + kv_rank:]

    qb = _dot(_rms(cq, gcq_ref[...]).astype(BF16), wqb_ref[...])
    for hd in range(nh):
        qn = qb[:, hd * MLA_NOPE:(hd + 1) * MLA_NOPE]
        qr = qb[:, nh * MLA_NOPE + hd * LANES:nh * MLA_NOPE + (hd + 1) * LANES]
        ss = jnp.sum(qn * qn, axis=-1, keepdims=True) + jnp.sum(qr * qr, axis=-1, keepdims=True)
        r = lax.rsqrt(ss * (1.0 / qk_dim) + EPS)
        qm_ref[:, hd * MXU_DIM:hd * MXU_DIM + LANES] = (qn * r * gqn_ref[...]).astype(BF16)
        qm_ref[:, hd * MXU_DIM + LANES:(hd + 1) * MXU_DIM] = _rope(qr * r * gqr_ref[...], cos, sin).astype(BF16)

    kvb = _dot(_rms(ckv, gckv_ref[...]).astype(BF16), wkvb_ref[...])
    ss_pe = jnp.sum(kpe * kpe, axis=-1, keepdims=True)
    kpe_rot = _rope(kpe * gkr_ref[...], cos, sin)
    for hd in range(nh):
        kn = kvb[:, hd * MLA_NOPE:(hd + 1) * MLA_NOPE]
        ss = jnp.sum(kn * kn, axis=-1, keepdims=True) + ss_pe
        r = lax.rsqrt(ss * (1.0 / qk_dim) + EPS)
        km_ref[:, hd * MXU_DIM:hd * MXU_DIM + LANES] = (kn * r * gkn_ref[...]).astype(BF16)
        km_ref[:, hd * MXU_DIM + LANES:(hd + 1) * MXU_DIM] = (kpe_rot * r).astype(BF16)
    vm_ref[...] = kvb[:, nh * MLA_NOPE:].astype(BF16)

    hp = _dot(perm_ref[...], h).astype(BF16)
    for c, (g_ref, dst) in enumerate(((gqa_ref, qa_ref), (gka_ref, ka_ref))):
        src = _dot_nt(hp, wint_ref[c * w:(c + 1) * w, :].astype(BF16))
        sq = (src * src).astype(BF16)
        ms = jnp.concatenate([_dot(sq[:, c0:c0 + MXU_DIM], eh_ref[...]) for c0 in range(0, w, MXU_DIM)],
                             axis=1) * (1.0 / DIL_HEAD_DIM)
        dst[...] = (src * lax.rsqrt(ms + EPS) * g_ref[...]).reshape(dst.shape)
    va_ref[...] = _dot_nt(hp, wint_ref[2 * w:o, :].astype(BF16)).reshape(va_ref.shape)


def _proj_call(x2d, batch, seq, consts, cos, sin, *, n_mla_heads, dil_width):
    m, d = x2d.shape
    tm = ROW_TILE
    n_seq_blocks = seq // tm
    row = lambda width: pl.BlockSpec((tm, width), lambda i: (i, 0))
    pos = pl.BlockSpec((tm, LANES), lambda i: (i % n_seq_blocks, 0))
    mla_w = n_mla_heads * MXU_DIM
    dil_shape = jax.ShapeDtypeStruct((batch, DIL_RES, seq // DIL_RES, dil_width), F32)
    return pl.pallas_call(
        functools.partial(_proj_kernel, n_mla_heads=n_mla_heads, dil_width=dil_width),
        grid=(m // tm,),
        in_specs=[row(d)] + [_resident(c.shape) for c in consts] + [pos, pos],
        out_specs=[_residue_major_block(seq, dil_width)] * 3
        + [row(mla_w), row(mla_w), row(n_mla_heads * MLA_V_DIM)],
        out_shape=[dil_shape] * 3
        + [jax.ShapeDtypeStruct((m, mla_w), BF16)] * 2
        + [jax.ShapeDtypeStruct((m, n_mla_heads * MLA_V_DIM), BF16)],
        compiler_params=pltpu.CompilerParams(dimension_semantics=("parallel",),
                                             vmem_limit_bytes=VMEM_LIMIT_BYTES),
        name="proj",
    )(x2d, *consts, cos, sin)


def _t5_bucket(dist):
    max_exact = REL_BUCKETS // 2
    d = np.maximum(dist, 1).astype(np.float32)
    large = max_exact + (np.log(d / max_exact) / np.log(REL_MAX_DIST / max_exact)
                         * (REL_BUCKETS - max_exact)).astype(np.int32)
    large = np.minimum(large, REL_BUCKETS - 1)
    return np.where(dist < max_exact, dist, large).astype(np.int32)


def _band_tables():
    rho = np.arange(DIL_BAND)
    kap = np.arange(2 * DIL_BAND)
    buckets, prev = [], []
    for _, dil in DIL_BRANCHES:
        g = DIL_RES // dil
        run_q = DIL_BAND // g
        pos_q = g * (rho % run_q) + rho // run_q
        pos_k = g * (kap % (2 * run_q)) + kap // (2 * run_q) - DIL_BAND
        delta = pos_q[:, None] - pos_k[None, :]
        valid = (delta >= 0) & (delta <= DIL_BAND)
        buckets.append(np.where(valid, _t5_bucket(np.clip(delta, 0, None) * dil), -1))
        prev.append((pos_k < 0)[None, :])
    return np.stack(buckets).astype(np.int32), np.stack(prev).astype(np.int32)


def _dil_bias_kernel(rel_ref, bucket_ref, prev_ref, o_ref, *, present):
    nh = o_ref.shape[2]
    rb = 16
    scaled = {}
    for bi, buckets_here in enumerate(present):
        prev = prev_ref[bi] > 0
        for r0 in range(0, DIL_BAND, rb):
            bucket = bucket_ref[bi, r0:r0 + rb, :]
            accs = [jnp.full(bucket.shape, NEG, F32)] * nh
            for b in buckets_here:
                hit = bucket == b
                for hd in range(nh):
                    if (hd, b) not in scaled:
                        scaled[hd, b] = rel_ref[hd, b] * LOG2E
                accs = [jnp.where(hit, scaled[hd, b], acc) for hd, acc in enumerate(accs)]
            for hd in range(nh):
                o_ref[bi, 0, hd, r0:r0 + rb, :] = accs[hd]
                o_ref[bi, 1, hd, r0:r0 + rb, :] = jnp.where(prev, NEG, accs[hd])


def _dil_bias_call(rel_bias, buckets, prev):
    nbr = buckets.shape[0]
    nh = rel_bias.shape[0]
    tile = buckets.shape[1:]
    present = tuple(tuple(int(b) for b in np.unique(t) if b >= 0) for t in buckets)
    whole = lambda shape: pl.BlockSpec(shape, lambda i: (0,) * len(shape))
    out_shape = (nbr, 2, nh) + tile
    return pl.pallas_call(
        functools.partial(_dil_bias_kernel, present=present),
        grid=(1,),
        in_specs=[pl.BlockSpec(memory_space=pltpu.SMEM), whole(buckets.shape), whole(prev.shape)],
        out_specs=whole(out_shape),
        out_shape=jax.ShapeDtypeStruct(out_shape, F32),
        name="dil_bias",
    )(rel_bias, jnp.asarray(buckets), jnp.asarray(prev))


def _band_tile(q, k, v, bias, head_a):
    zero = jnp.zeros_like(q)
    q2 = jnp.concatenate([jnp.where(head_a, q, zero), jnp.where(head_a, zero, q)], axis=0).astype(BF16)
    s = _dot_nt(q2, k.astype(BF16)) + bias
    m = jnp.max(s, axis=-1, keepdims=True)
    p = jnp.exp2(s - m).astype(BF16)
    ones = jnp.ones(v.shape, BF16)
    out = _dot(p, jnp.concatenate([v.astype(BF16), ones], axis=1))
    h = DIL_BAND
    num = jnp.where(head_a, out[:h, :LANES], out[h:, :LANES])
    den = jnp.where(head_a, out[:h, LANES:], out[h:, LANES:])
    mx = jnp.where(head_a, m[:h], m[h:])
    return num, den, mx


def _dil_kernel(q_ref, kp_ref, kc_ref, vp_ref, vc_ref, bias_ref, o_ref, num_scr, den_scr, max_scr):
    first_block = (pl.program_id(2) == 0).astype(jnp.int32)
    head_a = lax.broadcasted_iota(jnp.int32, (1, LANES), 1) < DIL_HEAD_DIM

    def tile(j, bi, dil, rbase, n):
        g = DIL_RES // dil
        run = DIL_BAND // g
        base = j * DIL_BAND
        q_rows = pl.ds(base + n * run, run)
        qs, ks, vs = [], [], []
        for u in range(g):
            res = rbase + dil * u
            qs.append(q_ref[res, q_rows, :])
            if n == 0 and j == 0:
                ks += [kp_ref[res, pl.ds(DIL_BAND - run, run), :], kc_ref[res, pl.ds(0, run), :]]
                vs += [vp_ref[res, pl.ds(DIL_BAND - run, run), :], vc_ref[res, pl.ds(0, run), :]]
            else:
                k_rows = pl.ds(base + (n - 1) * run, 2 * run)
                ks.append(kc_ref[res, k_rows, :])
                vs.append(vc_ref[res, k_rows, :])
        cat = lambda parts: parts[0] if len(parts) == 1 else jnp.concatenate(parts, axis=0)
        variant = first_block if (n == 0 and j == 0) else 0
        num, den, mx = _band_tile(cat(qs), cat(ks), cat(vs), bias_ref[bi, variant], head_a)
        if g == 1:
            return num, den, mx
        s_rows = pl.ds(n * run, run)
        for u in range(g):
            res = rbase + dil * u
            part = slice(u * run, (u + 1) * run)
            num_scr[j, bi, res, s_rows, :] = num[part]
            den_scr[j, bi, res, s_rows, :] = den[part]
            max_scr[j, bi, res, s_rows, :] = mx[part]

    stored = len(DIL_BRANCHES) - 1
    for j in range(q_ref.shape[1] // DIL_BAND):
        for n in range(DIL_RES):
            tile(j, 0, 1, 0, n)
        for r4 in range(4):
            for n in range(4):
                tile(j, 1, 4, r4, n)
        for r in range(DIL_RES):
            num_r, den_r, max_r = tile(j, stored, DIL_RES, r, 0)
            ms = [max_scr[j, bi, r] for bi in range(stored)] + [max_r]
            nums = [num_scr[j, bi, r] for bi in range(stored)] + [num_r]
            dens = [den_scr[j, bi, r] for bi in range(stored)] + [den_r]
            m_all = functools.reduce(jnp.maximum, ms)
            es = [jnp.exp2(mb - m_all) for mb in ms]
            num = sum(e * x for e, x in zip(es, nums))
            den = sum(e * x for e, x in zip(es, dens))
            o_ref[pl.ds(j * DIL_CHUNK + r, DIL_BAND, stride=DIL_RES), :] = num / den


def _dil_call(qa, ka, va, bias):
    b, _, per, w = qa.shape
    pairs = w // LANES
    nbr = bias.shape[0]
    rows = DIL_STEP_CHUNKS * DIL_BAND
    cur = pl.BlockSpec((None, DIL_RES, rows, LANES), lambda bb, p, c: (bb, 0, c, p))
    prev = pl.BlockSpec((None, DIL_RES, DIL_BAND, LANES),
                        lambda bb, p, c: (bb, 0, jnp.maximum(DIL_STEP_CHUNKS * c - 1, 0), p))
    scr = pltpu.VMEM((DIL_STEP_CHUNKS, nbr - 1, DIL_RES, DIL_BAND, LANES), F32)
    return pl.pallas_call(
        _dil_kernel,
        grid=(b, pairs, per // rows),
        in_specs=[cur, prev, cur, prev, cur,
                  pl.BlockSpec((nbr, 2, None, 2 * DIL_BAND, 2 * DIL_BAND), lambda bb, p, c: (0, 0, p, 0, 0))],
        out_specs=pl.BlockSpec((None, DIL_STEP_CHUNKS * DIL_CHUNK, LANES), lambda bb, p, c: (bb, c, p)),
        out_shape=jax.ShapeDtypeStruct((b, per * DIL_RES, w), F32),
        scratch_shapes=[scr, scr, scr],
        compiler_params=pltpu.CompilerParams(dimension_semantics=("parallel", "parallel", "arbitrary"),
                                             vmem_limit_bytes=VMEM_LIMIT_BYTES),
        name="dil_attn",
    )(qa, ka, ka, va, va, bias)


def _mla_kernel(q_ref, k_ref, v_ref, o_ref, m_ref, acc_ref, s0_ref, s1_ref):
    tq = m_ref.shape[0]
    tk = MLA_TK
    nsub = tq // tk
    ones = jnp.ones((tk, LANES), BF16)
    s_bufs = (s0_ref, s1_ref)

    def q_tile(qi):
        q_lo = qi * tq
        m_ref[...] = jnp.full(m_ref.shape, NEG, F32)
        acc_ref[...] = jnp.zeros(acc_ref.shape, F32)

        def scores(kblk, row0=0):
            k0 = pl.multiple_of(kblk * tk, tk)
            return _dot_nt(q_ref[q_lo + row0:q_lo + tq, :], k_ref[pl.ds(k0, tk), :])

        def update(s, kblk, row0, diagonal):
            k0 = pl.multiple_of(kblk * tk, tk)
            rows = slice(row0, tq)
            if diagonal:
                row = lax.broadcasted_iota(jnp.int32, s.shape, 0)
                col = lax.broadcasted_iota(jnp.int32, s.shape, 1)
                s = jnp.where(col <= row, s, NEG)
            m_old = m_ref[rows, :]
            m_new = jnp.maximum(m_old, jnp.max(s, axis=-1, keepdims=True))
            m_ref[rows, :] = m_new
            p = jnp.concatenate([jnp.exp2(s[:, c * LANES:(c + 1) * LANES] - m_new) for c in range(tk // LANES)],
                                axis=1).astype(BF16)
            alpha = jnp.exp2(m_old - m_new)
            vext = jnp.concatenate([v_ref[pl.ds(k0, tk), :], ones], axis=1)
            acc_ref[rows, :] = acc_ref[rows, :] * jnp.concatenate([alpha, alpha], axis=1) + _dot(p, vext)

        s0_ref[...] = scores(0)

        def body(t, carry):
            for jj in range(nsub):
                s_bufs[(jj + 1) % 2][...] = scores(nsub * t + jj + 1)
                update(s_bufs[jj % 2][...], nsub * t + jj, 0, diagonal=False)
            return carry

        if qi:
            lax.fori_loop(0, qi, body, 0)
        s_cur = s0_ref[...]
        for d in range(nsub):
            s_next = scores(nsub * qi + d + 1, row0=(d + 1) * tk) if d + 1 < nsub else None
            update(s_cur, nsub * qi + d, d * tk, diagonal=True)
            s_cur = s_next
        acc = acc_ref[...]
        o_ref[q_lo:q_lo + tq, :] = acc[:, :MLA_V_DIM] / acc[:, MLA_V_DIM:]

    for qi in range(q_ref.shape[0] // tq):
        q_tile(qi)


def _mla_call(qm, km, vm, n_heads):
    b, s, _ = qm.shape
    tq = MLA_TQ
    assert tq % (2 * MLA_TK) == 0
    seq_block = lambda width: pl.BlockSpec((None, s, width), lambda bb, h: (bb, 0, h))
    return pl.pallas_call(
        _mla_kernel,
        grid=(b, n_heads),
        in_specs=[seq_block(MXU_DIM), seq_block(MXU_DIM), seq_block(MLA_V_DIM)],
        out_specs=seq_block(MLA_V_DIM),
        out_shape=jax.ShapeDtypeStruct((b, s, n_heads * MLA_V_DIM), F32),
        scratch_shapes=[pltpu.VMEM((tq, LANES), F32), pltpu.VMEM((tq, 2 * MLA_V_DIM), F32),
                        pltpu.VMEM((tq, MLA_TK), F32), pltpu.VMEM((tq, MLA_TK), F32)],
        compiler_params=pltpu.CompilerParams(dimension_semantics=("parallel", "parallel"),
                                             vmem_limit_bytes=VMEM_LIMIT_BYTES),
        name="mla_attn",
    )(qm, km, vm)


def _rope_tables(seq):
    inv_freq = ROPE_BASE ** (-np.arange(0, MLA_ROPE, 2, dtype=np.float64) / MLA_ROPE)
    ang = np.arange(seq, dtype=np.float64)[:, None] * inv_freq[None, :]
    cos, sin = np.cos(ang), np.sin(ang)
    cos_t = np.concatenate([cos, cos] * (LANES // MLA_ROPE), axis=1)
    sin_t = np.concatenate([-sin, sin] * (LANES // MLA_ROPE), axis=1)
    return jnp.asarray(cos_t, F32), jnp.asarray(sin_t, F32)


def _pad_lanes(v, width):
    return jnp.pad(v, ((0, 0), (0, width - v.shape[1])))


def kernel(x, ffn1_norm, ffn1_w_gate, ffn1_w_up, ffn1_w_down, mix_norm, w_in, dil_q_norm, dil_k_norm,
           rel_bias, mla_q_a_norm, mla_w_q_b, mla_kv_a_norm, mla_w_kv_b, mla_q_norm, mla_k_norm,
           out_norm_dil, out_norm_mla, w_out, ffn2_norm, ffn2_w_gate, ffn2_w_up, ffn2_w_down):
    batch, seq, d_model = x.shape
    depth = w_in.shape[0]
    dil_width = out_norm_dil.shape[1]
    dil_heads = dil_width // DIL_HEAD_DIM
    q_rank = mla_q_a_norm.shape[1]
    kv_rank = mla_kv_a_norm.shape[1]
    qk_dim = MLA_NOPE + MLA_ROPE
    n_mla = mla_w_q_b.shape[2] // qk_dim
    assert rel_bias.shape == (dil_heads, REL_BUCKETS)
    assert w_in.shape[2] == 3 * dil_width + q_rank + kv_rank + MLA_ROPE
    assert seq % (DIL_STEP_CHUNKS * DIL_CHUNK) == 0 and seq % MLA_TQ == 0 and seq % ROW_TILE == 0
    assert (batch * seq) % FFN_STEP_ROWS == 0
    assert all(win // dil == DIL_BAND and DIL_RES % dil == 0 for win, dil in DIL_BRANCHES)
    assert DIL_BRANCHES[-1][1] == DIL_RES

    cos_t, sin_t = _rope_tables(seq)
    buckets, prev = _band_tables()
    dil_bias = _dil_bias_call(rel_bias, buckets, prev)
    dil_bias = dil_bias.reshape(dil_bias.shape[0], 2, dil_heads // 2, 2 * DIL_BAND, 2 * DIL_BAND)
    eh = jnp.asarray(np.kron(np.eye(MXU_DIM // DIL_HEAD_DIM), np.ones((DIL_HEAD_DIM, DIL_HEAD_DIM))), BF16)
    perm = jnp.asarray(_residue_major_perm(ROW_TILE), BF16)

    x2d = x.reshape(batch * seq, d_model)
    row = lambda v: v.reshape(1, -1)
    for l in range(depth):
        x2d = _ffn_call(x2d, row(ffn1_norm[l]), ffn1_w_gate[l], ffn1_w_up[l], ffn1_w_down[l])

        w_in_t = jnp.swapaxes(w_in[l], 0, 1)
        wkpe_t = jnp.pad(w_in_t[w_in.shape[2] - MLA_ROPE:], ((0, LANES - MLA_ROPE), (0, 0))).astype(BF16)
        wqb = mla_w_q_b[l].reshape(q_rank, n_mla, qk_dim)
        wqb = jnp.concatenate(
            [wqb[:, :, :MLA_NOPE].reshape(q_rank, n_mla * MLA_NOPE),
             jnp.pad(wqb[:, :, MLA_NOPE:], ((0, 0), (0, 0), (0, LANES - MLA_ROPE))).reshape(q_rank, n_mla * LANES)],
            axis=1).astype(BF16)
        wkvb = mla_w_kv_b[l].reshape(kv_rank, n_mla, MLA_NOPE + MLA_V_DIM)
        wkvb = jnp.concatenate([wkvb[:, :, :MLA_NOPE].reshape(kv_rank, n_mla * MLA_NOPE),
                                wkvb[:, :, MLA_NOPE:].reshape(kv_rank, n_mla * MLA_V_DIM)], axis=1).astype(BF16)
        gqa = row(jnp.tile(dil_q_norm[l], dil_heads)) * (LOG2E * DIL_HEAD_DIM ** -0.5)
        gka = row(jnp.tile(dil_k_norm[l], dil_heads))
        gq = row(mla_q_norm[l]) * (LOG2E * qk_dim ** -0.5)
        gk = row(mla_k_norm[l])
        gqn, gqr = gq[:, :MLA_NOPE], _pad_lanes(gq[:, MLA_NOPE:], LANES)
        gkn, gkr = gk[:, :MLA_NOPE], _pad_lanes(gk[:, MLA_NOPE:], LANES)

        consts = [row(mix_norm[l]), w_in_t, wkpe_t, eh, perm, gqa, gka, row(mla_q_a_norm[l]), wqb,
                  row(mla_kv_a_norm[l]), wkvb, gqn, gqr, gkn, gkr]
        qa, ka, va, qm, km, vm = _proj_call(x2d, batch, seq, consts, cos_t, sin_t,
                                            n_mla_heads=n_mla, dil_width=dil_width)

        shp = lambda a: a.reshape(batch, seq, a.shape[1])
        o_dil = _dil_call(qa, ka, va, dil_bias)
        o_mla = _mla_call(shp(qm), shp(km), shp(vm), n_mla)

        x2d = _out_ffn_call(x2d, o_dil.reshape(batch * seq, -1), o_mla.reshape(batch * seq, -1),
                            row(out_norm_dil[l]), row(out_norm_mla[l]), w_out[l].astype(BF16),
                            row(ffn2_norm[l]), ffn2_w_gate[l], ffn2_w_up[l], ffn2_w_down[l])
    return x2d.reshape(batch, seq, d_model)
```

```python
import functools

import numpy as np
import jax
import jax.numpy as jnp
from jax import lax
from jax.experimental import pallas as pl
from jax.experimental.pallas import tpu as pltpu

F32 = jnp.float32
BF16 = jnp.bfloat16

DIL_HEAD_DIM = 64
DIL_BRANCHES = ((128, 1), (512, 4), (2048, 16))
DIL_BAND = 128
DIL_RES = 16
MLA_NOPE = 128
MLA_ROPE = 64
MLA_V_DIM = 128
ROPE_BASE = 10000.0
REL_BUCKETS = 32
REL_MAX_DIST = 2048
FFN_RESID = 0.5
EPS = 1e-6

LANES = 128
MXU_DIM = 256
VMEM_LIMIT_BYTES = 56 * 1024 * 1024
VMEM_LIMIT_FFN_BYTES = 60 * 1024 * 1024

NEG = float("-inf")
LOG2E = 1.4426950408889634

ROW_TILE = 512
FFN_STEP_ROWS = 2 * ROW_TILE
PROJ_STEP_ROWS = 2 * ROW_TILE
FFN_TF = 256
DIL_CHUNK = DIL_BAND * DIL_RES
DIL_STEP_CHUNKS = 2
MLA_TQ = 2048
MLA_TK = 512


def _dot(a, b):
    return jnp.dot(a, b, preferred_element_type=F32)


def _dot_nt(a, b):
    return lax.dot_general(a, b, (((1,), (1,)), ((), ())), preferred_element_type=F32)


def _rms(x, g):
    return x * lax.rsqrt(jnp.mean(x * x, axis=-1, keepdims=True) + EPS) * g


def _residue_major_perm(rows):
    per = rows // DIL_RES
    dst = np.arange(rows)
    src = DIL_RES * (dst % per) + dst // per
    p = np.zeros((rows, rows), np.float32)
    p[dst, src] = 1.0
    return p


def _swiglu_residual(x, g_ref, wg_ref, wu_ref, wd_ref, act_ref):
    h = _rms(x, g_ref[...]).astype(BF16)
    d_ff = wg_ref.shape[1]
    for c in range(d_ff // FFN_TF):
        sl = slice(c * FFN_TF, (c + 1) * FFN_TF)
        gate = _dot(h, wg_ref[:, sl].astype(BF16))
        up = _dot(h, wu_ref[:, sl].astype(BF16))
        act_ref[:, sl] = (gate * jax.nn.sigmoid(gate) * up).astype(BF16)
    return x + FFN_RESID * _dot(act_ref[...], wd_ref[...].astype(BF16))


def _ffn_kernel(x_ref, g_ref, wg_ref, wu_ref, wd_ref, o_ref, act_ref):
    for r0 in range(0, x_ref.shape[0], ROW_TILE):
        rows = slice(r0, r0 + ROW_TILE)
        o_ref[rows, :] = _swiglu_residual(x_ref[rows, :], g_ref, wg_ref, wu_ref, wd_ref, act_ref)


def _out_ffn_kernel(x_ref, od_ref, om_ref, gd_ref, gm_ref, wo_ref, g_ref, wg_ref, wu_ref, wd_ref,
                    o_ref, act_ref):
    wd_rows = od_ref.shape[1]
    od = _rms(od_ref[...], gd_ref[...]).astype(BF16)
    om = _rms(om_ref[...], gm_ref[...]).astype(BF16)
    x2 = x_ref[...] + _dot(od, wo_ref[:wd_rows, :]) + _dot(om, wo_ref[wd_rows:, :])
    o_ref[...] = _swiglu_residual(x2, g_ref, wg_ref, wu_ref, wd_ref, act_ref)


def _resident(shape):
    return pl.BlockSpec(shape, lambda *_: (0,) * len(shape), pipeline_mode=pl.Buffered(1))


def _ffn_call(x2d, g, wg, wu, wd):
    m, d = x2d.shape
    f = wg.shape[1]
    row = pl.BlockSpec((FFN_STEP_ROWS, d), lambda i: (i, 0))
    return pl.pallas_call(
        _ffn_kernel,
        grid=(m // FFN_STEP_ROWS,),
        in_specs=[row, _resident((1, d)), _resident((d, f)), _resident((d, f)), _resident((f, d))],
        out_specs=row,
        out_shape=jax.ShapeDtypeStruct((m, d), F32),
        scratch_shapes=[pltpu.VMEM((ROW_TILE, f), BF16)],
        compiler_params=pltpu.CompilerParams(dimension_semantics=("parallel",),
                                             vmem_limit_bytes=VMEM_LIMIT_FFN_BYTES),
        name="ffn",
    )(x2d, g, wg, wu, wd)


def _out_ffn_call(x2d, o_dil, o_mla, gd, gm, wo, g, wg, wu, wd):
    m, d = x2d.shape
    f = wg.shape[1]
    wdil, wmla = o_dil.shape[1], o_mla.shape[1]
    row = pl.BlockSpec((ROW_TILE, d), lambda i: (i, 0))
    return pl.pallas_call(
        _out_ffn_kernel,
        grid=(m // ROW_TILE,),
        in_specs=[row,
                  pl.BlockSpec((ROW_TILE, wdil), lambda i: (i, 0)),
                  pl.BlockSpec((ROW_TILE, wmla), lambda i: (i, 0)),
                  _resident((1, wdil)), _resident((1, wmla)),
                  _resident((wdil + wmla, d)),
                  _resident((1, d)), _resident((d, f)), _resident((d, f)), _resident((f, d))],
        out_specs=row,
        out_shape=jax.ShapeDtypeStruct((m, d), F32),
        scratch_shapes=[pltpu.VMEM((ROW_TILE, f), BF16)],
        compiler_params=pltpu.CompilerParams(dimension_semantics=("parallel",),
                                             vmem_limit_bytes=VMEM_LIMIT_BYTES),
        name="out_ffn",
    )(x2d, o_dil, o_mla, gd, gm, wo, g, wg, wu, wd)


def _rope(x, cos, sin_signed):
    lane = lax.broadcasted_iota(jnp.int32, x.shape, 1)
    first_half = (lane % MLA_ROPE) < (MLA_ROPE // 2)
    partner = jnp.where(first_half,
                        pltpu.roll(x, LANES - MLA_ROPE // 2, 1),
                        pltpu.roll(x, MLA_ROPE // 2, 1))
    return x * cos + partner * sin_signed


def _proj_kernel(x_ref, gmix_ref, wint_ref, wkpet_ref, eh_ref, perm_ref, gqa_ref, gka_ref,
                 gcq_ref, wqb_ref, gckv_ref, wkvb_ref,
                 gqn_ref, gqr_ref, gkn_ref, gkr_ref, cos_ref, sin_ref,
                 qa_ref, ka_ref, va_ref, qm_ref, km_ref, vm_ref, *, n_mla_heads, dil_width):
    sub = perm_ref.shape[0]
    per = sub // DIL_RES
    n_sub = x_ref.shape[0] // sub
    w = dil_width
    o = 3 * w
    q_rank = gcq_ref.shape[1]
    kv_rank = gckv_ref.shape[1]
    qk_dim = MLA_NOPE + MLA_ROPE
    nh = n_mla_heads
    hs = [_rms(x_ref[i * sub:(i + 1) * sub, :], gmix_ref[...]).astype(BF16) for i in range(n_sub)]

    for i, h in enumerate(hs):
        rows = slice(i * sub, (i + 1) * sub)
        cos = cos_ref[rows, :]
        sin = sin_ref[rows, :]
        w_mla = jnp.concatenate([wint_ref[o:o + q_rank + kv_rank, :].astype(BF16), wkpet_ref[...]], axis=0)
        pm = _dot_nt(h, w_mla)
        cq = pm[:, :q_rank]
        ckv = pm[:, q_rank:q_rank + kv_rank]
        kpe = pm[:, q_rank + kv_rank:]

        qb = _dot(_rms(cq, gcq_ref[...]).astype(BF16), wqb_ref[...])
        for hd in range(nh):
            qn = qb[:, hd * MLA_NOPE:(hd + 1) * MLA_NOPE]
            qr = qb[:, nh * MLA_NOPE + hd * LANES:nh * MLA_NOPE + (hd + 1) * LANES]
            ss = jnp.sum(qn * qn, axis=-1, keepdims=True) + jnp.sum(qr * qr, axis=-1, keepdims=True)
            r = lax.rsqrt(ss * (1.0 / qk_dim) + EPS)
            qm_ref[rows, hd * MXU_DIM:hd * MXU_DIM + LANES] = (qn * r * gqn_ref[...]).astype(BF16)
            qm_ref[rows, hd * MXU_DIM + LANES:(hd + 1) * MXU_DIM] = _rope(qr * r * gqr_ref[...], cos, sin).astype(BF16)

        kvb = _dot(_rms(ckv, gckv_ref[...]).astype(BF16), wkvb_ref[...])
        ss_pe = jnp.sum(kpe * kpe, axis=-1, keepdims=True)
        kpe_rot = _rope(kpe * gkr_ref[...], cos, sin)
        for hd in range(nh):
            kn = kvb[:, hd * MLA_NOPE:(hd + 1) * MLA_NOPE]
            ss = jnp.sum(kn * kn, axis=-1, keepdims=True) + ss_pe
            r = lax.rsqrt(ss * (1.0 / qk_dim) + EPS)
            km_ref[rows, hd * MXU_DIM:hd * MXU_DIM + LANES] = (kn * r * gkn_ref[...]).astype(BF16)
            km_ref[rows, hd * MXU_DIM + LANES:(hd + 1) * MXU_DIM] = (kpe_rot * r).astype(BF16)
        vm_ref[rows, :] = kvb[:, nh * MLA_NOPE:].astype(BF16)

    for i, h in enumerate(hs):
        res_rows = slice(i * per, (i + 1) * per)
        hp = _dot(perm_ref[...], h).astype(BF16)
        for c, (g_ref, dst) in enumerate(((gqa_ref, qa_ref), (gka_ref, ka_ref))):
            src = _dot_nt(hp, wint_ref[c * w:(c + 1) * w, :].astype(BF16))
            sq = (src * src).astype(BF16)
            ms = jnp.concatenate([_dot(sq[:, c0:c0 + MXU_DIM], eh_ref[...]) for c0 in range(0, w, MXU_DIM)],
                                 axis=1) * (1.0 / DIL_HEAD_DIM)
            dst[:, res_rows, :] = (src * lax.rsqrt(ms + EPS) * g_ref[...]).reshape(DIL_RES, per, w)
        va_ref[:, res_rows, :] = _dot_nt(hp, wint_ref[2 * w:o, :].astype(BF16)).reshape(DIL_RES, per, w)


def _proj_call(x2d, batch, seq, consts, cos, sin, *, n_mla_heads, dil_width):
    m, d = x2d.shape
    tm = PROJ_STEP_ROWS
    n_seq_blocks = seq // tm
    row = lambda width: pl.BlockSpec((tm, width), lambda i: (i, 0))
    pos = pl.BlockSpec((tm, LANES), lambda i: (i % n_seq_blocks, 0))
    res_major = pl.BlockSpec((None, DIL_RES, tm // DIL_RES, dil_width),
                             lambda i: (i // n_seq_blocks, 0, i % n_seq_blocks, 0))
    mla_w = n_mla_heads * MXU_DIM
    dil_shape = jax.ShapeDtypeStruct((batch, DIL_RES, seq // DIL_RES, dil_width), F32)
    return pl.pallas_call(
        functools.partial(_proj_kernel, n_mla_heads=n_mla_heads, dil_width=dil_width),
        grid=(m // tm,),
        in_specs=[row(d)] + [_resident(c.shape) for c in consts] + [pos, pos],
        out_specs=[res_major] * 3 + [row(mla_w), row(mla_w), row(n_mla_heads * MLA_V_DIM)],
        out_shape=[dil_shape] * 3
        + [jax.ShapeDtypeStruct((m, mla_w), BF16)] * 2
        + [jax.ShapeDtypeStruct((m, n_mla_heads * MLA_V_DIM), BF16)],
        compiler_params=pltpu.CompilerParams(dimension_semantics=("parallel",),
                                             vmem_limit_bytes=VMEM_LIMIT_BYTES),
        name="proj",
    )(x2d, *consts, cos, sin)


def _t5_bucket(dist):
    max_exact = REL_BUCKETS // 2
    d = np.maximum(dist, 1).astype(np.float32)
    large = max_exact + (np.log(d / max_exact) / np.log(REL_MAX_DIST / max_exact)
                         * (REL_BUCKETS - max_exact)).astype(np.int32)
    large = np.minimum(large, REL_BUCKETS - 1)
    return np.where(dist < max_exact, dist, large).astype(np.int32)


def _band_tables():
    rho = np.arange(DIL_BAND)
    kap = np.arange(2 * DIL_BAND)
    buckets, prev = [], []
    for _, dil in DIL_BRANCHES:
        g = DIL_RES // dil
        run_q = DIL_BAND // g
        pos_q = g * (rho % run_q) + rho // run_q
        pos_k = g * (kap % (2 * run_q)) + kap // (2 * run_q) - DIL_BAND
        delta = pos_q[:, None] - pos_k[None, :]
        valid = (delta >= 0) & (delta <= DIL_BAND)
        buckets.append(np.where(valid, _t5_bucket(np.clip(delta, 0, None) * dil), -1))
        prev.append((pos_k < 0)[None, :])
    return np.stack(buckets).astype(np.int32), np.stack(prev).astype(np.int32)


def _dil_bias_kernel(rel_ref, bucket_ref, prev_ref, o_ref, *, present):
    nh = o_ref.shape[2]
    rb = 16
    scaled = {}
    for bi, buckets_here in enumerate(present):
        prev = prev_ref[bi] > 0
        for r0 in range(0, DIL_BAND, rb):
            bucket = bucket_ref[bi, r0:r0 + rb, :]
            accs = [jnp.full(bucket.shape, NEG, F32)] * nh
            for b in buckets_here:
                hit = bucket == b
                for hd in range(nh):
                    if (hd, b) not in scaled:
                        scaled[hd, b] = rel_ref[hd, b] * LOG2E
                accs = [jnp.where(hit, scaled[hd, b], acc) for hd, acc in enumerate(accs)]
            for hd in range(nh):
                o_ref[bi, 0, hd, r0:r0 + rb, :] = accs[hd]
                o_ref[bi, 1, hd, r0:r0 + rb, :] = jnp.where(prev, NEG, accs[hd])


def _dil_bias_call(rel_bias, buckets, prev):
    nbr = buckets.shape[0]
    nh = rel_bias.shape[0]
    tile = buckets.shape[1:]
    present = tuple(tuple(int(b) for b in np.unique(t) if b >= 0) for t in buckets)
    whole = lambda shape: pl.BlockSpec(shape, lambda i: (0,) * len(shape))
    out_shape = (nbr, 2, nh) + tile
    return pl.pallas_call(
        functools.partial(_dil_bias_kernel, present=present),
        grid=(1,),
        in_specs=[pl.BlockSpec(memory_space=pltpu.SMEM), whole(buckets.shape), whole(prev.shape)],
        out_specs=whole(out_shape),
        out_shape=jax.ShapeDtypeStruct(out_shape, F32),
        name="dil_bias",
    )(rel_bias, jnp.asarray(buckets), jnp.asarray(prev))


def _band_tile(q, k, v, bias, head_a):
    zero = jnp.zeros_like(q)
    q2 = jnp.concatenate([jnp.where(head_a, q, zero), jnp.where(head_a, zero, q)], axis=0).astype(BF16)
    s = _dot_nt(q2, k.astype(BF16)) + bias
    m = jnp.max(s, axis=-1, keepdims=True)
    p = jnp.exp2(s - m).astype(BF16)
    ones = jnp.ones(v.shape, BF16)
    out = _dot(p, jnp.concatenate([v.astype(BF16), ones], axis=1))
    h = DIL_BAND
    num = jnp.where(head_a, out[:h, :LANES], out[h:, :LANES])
    den = jnp.where(head_a, out[:h, LANES:], out[h:, LANES:])
    mx = jnp.where(head_a, m[:h], m[h:])
    return num, den, mx


def _dil_kernel(q_ref, kp_ref, kc_ref, vp_ref, vc_ref, bias_ref, o_ref, num_scr, den_scr, max_scr):
    first_block = (pl.program_id(2) == 0).astype(jnp.int32)
    head_a = lax.broadcasted_iota(jnp.int32, (1, LANES), 1) < DIL_HEAD_DIM

    def tile(j, bi, dil, rbase, n):
        g = DIL_RES // dil
        run = DIL_BAND // g
        base = j * DIL_BAND
        q_rows = pl.ds(base + n * run, run)
        qs, ks, vs = [], [], []
        for u in range(g):
            res = rbase + dil * u
            qs.append(q_ref[res, q_rows, :])
            if n == 0 and j == 0:
                ks += [kp_ref[res, pl.ds(DIL_BAND - run, run), :], kc_ref[res, pl.ds(0, run), :]]
                vs += [vp_ref[res, pl.ds(DIL_BAND - run, run), :], vc_ref[res, pl.ds(0, run), :]]
            else:
                k_rows = pl.ds(base + (n - 1) * run, 2 * run)
                ks.append(kc_ref[res, k_rows, :])
                vs.append(vc_ref[res, k_rows, :])
        cat = lambda parts: parts[0] if len(parts) == 1 else jnp.concatenate(parts, axis=0)
        variant = first_block if (n == 0 and j == 0) else 0
        num, den, mx = _band_tile(cat(qs), cat(ks), cat(vs), bias_ref[bi, variant], head_a)
        if g == 1:
            return num, den, mx
        s_rows = pl.ds(n * run, run)
        for u in range(g):
            res = rbase + dil * u
            part = slice(u * run, (u + 1) * run)
            num_scr[j, bi, res, s_rows, :] = num[part]
            den_scr[j, bi, res, s_rows, :] = den[part]
            max_scr[j, bi, res, s_rows, :] = mx[part]

    stored = len(DIL_BRANCHES) - 1
    for j in range(q_ref.shape[1] // DIL_BAND):
        for n in range(DIL_RES):
            tile(j, 0, 1, 0, n)
        for r4 in range(4):
            for n in range(4):
                tile(j, 1, 4, r4, n)
        for r in range(DIL_RES):
            num_r, den_r, max_r = tile(j, stored, DIL_RES, r, 0)
            ms = [max_scr[j, bi, r] for bi in range(stored)] + [max_r]
            nums = [num_scr[j, bi, r] for bi in range(stored)] + [num_r]
            dens = [den_scr[j, bi, r] for bi in range(stored)] + [den_r]
            m_all = functools.reduce(jnp.maximum, ms)
            es = [jnp.exp2(mb - m_all) for mb in ms]
            num = sum(e * x for e, x in zip(es, nums))
            den = sum(e * x for e, x in zip(es, dens))
            o_ref[pl.ds(j * DIL_CHUNK + r, DIL_BAND, stride=DIL_RES), :] = num / den


def _dil_call(qa, ka, va, bias):
    b, _, per, w = qa.shape
    pairs = w // LANES
    nbr = bias.shape[0]
    rows = DIL_STEP_CHUNKS * DIL_BAND
    cur = pl.BlockSpec((None, DIL_RES, rows, LANES), lambda bb, p, c: (bb, 0, c, p))
    prev = pl.BlockSpec((None, DIL_RES, DIL_BAND, LANES),
                        lambda bb, p, c: (bb, 0, jnp.maximum(DIL_STEP_CHUNKS * c - 1, 0), p))
    scr = pltpu.VMEM((DIL_STEP_CHUNKS, nbr - 1, DIL_RES, DIL_BAND, LANES), F32)
    return pl.pallas_call(
        _dil_kernel,
        grid=(b, pairs, per // rows),
        in_specs=[cur, prev, cur, prev, cur,
                  pl.BlockSpec((nbr, 2, None, 2 * DIL_BAND, 2 * DIL_BAND), lambda bb, p, c: (0, 0, p, 0, 0))],
        out_specs=pl.BlockSpec((None, DIL_STEP_CHUNKS * DIL_CHUNK, LANES), lambda bb, p, c: (bb, c, p)),
        out_shape=jax.ShapeDtypeStruct((b, per * DIL_RES, w), F32),
        scratch_shapes=[scr, scr, scr],
        compiler_params=pltpu.CompilerParams(dimension_semantics=("parallel", "parallel", "arbitrary"),
                                             vmem_limit_bytes=VMEM_LIMIT_BYTES),
        name="dil_attn",
    )(qa, ka, ka, va, va, bias)


def _mla_kernel(q_ref, k_ref, v_ref, o_ref, m_ref, acc_ref, s0_ref, s1_ref):
    tq = m_ref.shape[0]
    tk = MLA_TK
    nsub = tq // tk
    ones = jnp.ones((tk, LANES), BF16)
    s_bufs = (s0_ref, s1_ref)

    def q_tile(qi):
        q_lo = qi * tq
        m_ref[...] = jnp.full(m_ref.shape, NEG, F32)
        acc_ref[...] = jnp.zeros(acc_ref.shape, F32)

        def scores(kblk, row0=0):
            k0 = pl.multiple_of(kblk * tk, tk)
            return _dot_nt(q_ref[q_lo + row0:q_lo + tq, :], k_ref[pl.ds(k0, tk), :])

        def update(s, kblk, row0, diagonal):
            k0 = pl.multiple_of(kblk * tk, tk)
            rows = slice(row0, tq)
            if diagonal:
                row = lax.broadcasted_iota(jnp.int32, s.shape, 0)
                col = lax.broadcasted_iota(jnp.int32, s.shape, 1)
                s = jnp.where(col <= row, s, NEG)
            m_old = m_ref[rows, :]
            m_new = jnp.maximum(m_old, jnp.max(s, axis=-1, keepdims=True))
            m_ref[rows, :] = m_new
            p = jnp.concatenate([jnp.exp2(s[:, c * LANES:(c + 1) * LANES] - m_new) for c in range(tk // LANES)],
                                axis=1).astype(BF16)
            alpha = jnp.exp2(m_old - m_new)
            vext = jnp.concatenate([v_ref[pl.ds(k0, tk), :], ones], axis=1)
            acc_ref[rows, :] = acc_ref[rows, :] * jnp.concatenate([alpha, alpha], axis=1) + _dot(p, vext)

        s0_ref[...] = scores(0)

        def body(t, carry):
            for jj in range(nsub):
                s_bufs[(jj + 1) % 2][...] = scores(nsub * t + jj + 1)
                update(s_bufs[jj % 2][...], nsub * t + jj, 0, diagonal=False)
            return carry

        if qi:
            lax.fori_loop(0, qi, body, 0)
        s_cur = s0_ref[...]
        for d in range(nsub):
            s_next = scores(nsub * qi + d + 1, row0=(d + 1) * tk) if d + 1 < nsub else None
            update(s_cur, nsub * qi + d, d * tk, diagonal=True)
            s_cur = s_next
        acc = acc_ref[...]
        o_ref[q_lo:q_lo + tq, :] = acc[:, :MLA_V_DIM] / acc[:, MLA_V_DIM:]

    for qi in range(q_ref.shape[0] // tq):
        q_tile(qi)


def _mla_call(qm, km, vm, n_heads):
    b, s, _ = qm.shape
    tq = MLA_TQ
    assert tq % (2 * MLA_TK) == 0
    seq_block = lambda width: pl.BlockSpec((None, s, width), lambda bb, h: (bb, 0, h))
    return pl.pallas_call(
        _mla_kernel,
        grid=(b, n_heads),
        in_specs=[seq_block(MXU_DIM), seq_block(MXU_DIM), seq_block(MLA_V_DIM)],
        out_specs=seq_block(MLA_V_DIM),
        out_shape=jax.ShapeDtypeStruct((b, s, n_heads * MLA_V_DIM), F32),
        scratch_shapes=[pltpu.VMEM((tq, LANES), F32), pltpu.VMEM((tq, 2 * MLA_V_DIM), F32),
                        pltpu.VMEM((tq, MLA_TK), F32), pltpu.VMEM((tq, MLA_TK), F32)],
        compiler_params=pltpu.CompilerParams(dimension_semantics=("parallel", "parallel"),
                                             vmem_limit_bytes=VMEM_LIMIT_BYTES),
        name="mla_attn",
    )(qm, km, vm)


def _rope_tables(seq):
    inv_freq = ROPE_BASE ** (-np.arange(0, MLA_ROPE, 2, dtype=np.float64) / MLA_ROPE)
    ang = np.arange(seq, dtype=np.float64)[:, None] * inv_freq[None, :]
    cos, sin = np.cos(ang), np.sin(ang)
    cos_t = np.concatenate([cos, cos] * (LANES // MLA_ROPE), axis=1)
    sin_t = np.concatenate([-sin, sin] * (LANES // MLA_ROPE), axis=1)
    return jnp.asarray(cos_t, F32), jnp.asarray(sin_t, F32)


def _pad_lanes(v, width):
    return jnp.pad(v, ((0, 0), (0, width - v.shape[1])))


def kernel(x, ffn1_norm, ffn1_w_gate, ffn1_w_up, ffn1_w_down, mix_norm, w_in, dil_q_norm, dil_k_norm,
           rel_bias, mla_q_a_norm, mla_w_q_b, mla_kv_a_norm, mla_w_kv_b, mla_q_norm, mla_k_norm,
           out_norm_dil, out_norm_mla, w_out, ffn2_norm, ffn2_w_gate, ffn2_w_up, ffn2_w_down):
    batch, seq, d_model = x.shape
    depth = w_in.shape[0]
    dil_width = out_norm_dil.shape[1]
    dil_heads = dil_width // DIL_HEAD_DIM
    q_rank = mla_q_a_norm.shape[1]
    kv_rank = mla_kv_a_norm.shape[1]
    qk_dim = MLA_NOPE + MLA_ROPE
    n_mla = mla_w_q_b.shape[2] // qk_dim
    assert rel_bias.shape == (dil_heads, REL_BUCKETS)
    assert w_in.shape[2] == 3 * dil_width + q_rank + kv_rank + MLA_ROPE
    assert seq % (DIL_STEP_CHUNKS * DIL_CHUNK) == 0 and seq % MLA_TQ == 0 and seq % ROW_TILE == 0
    assert (batch * seq) % FFN_STEP_ROWS == 0 and seq % PROJ_STEP_ROWS == 0
    assert all(win // dil == DIL_BAND and DIL_RES % dil == 0 for win, dil in DIL_BRANCHES)
    assert DIL_BRANCHES[-1][1] == DIL_RES

    cos_t, sin_t = _rope_tables(seq)
    buckets, prev = _band_tables()
    dil_bias = _dil_bias_call(rel_bias, buckets, prev)
    dil_bias = dil_bias.reshape(dil_bias.shape[0], 2, dil_heads // 2, 2 * DIL_BAND, 2 * DIL_BAND)
    eh = jnp.asarray(np.kron(np.eye(MXU_DIM // DIL_HEAD_DIM), np.ones((DIL_HEAD_DIM, DIL_HEAD_DIM))), BF16)
    perm = jnp.asarray(_residue_major_perm(ROW_TILE), BF16)

    x2d = x.reshape(batch * seq, d_model)
    row = lambda v: v.reshape(1, -1)
    for l in range(depth):
        x2d = _ffn_call(x2d, row(ffn1_norm[l]), ffn1_w_gate[l], ffn1_w_up[l], ffn1_w_down[l])

        w_in_t = jnp.swapaxes(w_in[l], 0, 1)
        wkpe_t = jnp.pad(w_in_t[w_in.shape[2] - MLA_ROPE:], ((0, LANES - MLA_ROPE), (0, 0))).astype(BF16)
        wqb = mla_w_q_b[l].reshape(q_rank, n_mla, qk_dim)
        wqb = jnp.concatenate(
            [wqb[:, :, :MLA_NOPE].reshape(q_rank, n_mla * MLA_NOPE),
             jnp.pad(wqb[:, :, MLA_NOPE:], ((0, 0), (0, 0), (0, LANES - MLA_ROPE))).reshape(q_rank, n_mla * LANES)],
            axis=1).astype(BF16)
        wkvb = mla_w_kv_b[l].reshape(kv_rank, n_mla, MLA_NOPE + MLA_V_DIM)
        wkvb = jnp.concatenate([wkvb[:, :, :MLA_NOPE].reshape(kv_rank, n_mla * MLA_NOPE),
                                wkvb[:, :, MLA_NOPE:].reshape(kv_rank, n_mla * MLA_V_DIM)], axis=1).astype(BF16)
        gqa = row(jnp.tile(dil_q_norm[l], dil_heads)) * (LOG2E * DIL_HEAD_DIM ** -0.5)
        gka = row(jnp.tile(dil_k_norm[l], dil_heads))
        gq = row(mla_q_norm[l]) * (LOG2E * qk_dim ** -0.5)
        gk = row(mla_k_norm[l])
        gqn, gqr = gq[:, :MLA_NOPE], _pad_lanes(gq[:, MLA_NOPE:], LANES)
        gkn, gkr = gk[:, :MLA_NOPE], _pad_lanes(gk[:, MLA_NOPE:], LANES)

        consts = [row(mix_norm[l]), w_in_t, wkpe_t, eh, perm, gqa, gka, row(mla_q_a_norm[l]), wqb,
                  row(mla_kv_a_norm[l]), wkvb, gqn, gqr, gkn, gkr]
        qa, ka, va, qm, km, vm = _proj_call(x2d, batch, seq, consts, cos_t, sin_t,
                                            n_mla_heads=n_mla, dil_width=dil_width)

        shp = lambda a: a.reshape(batch, seq, a.shape[1])
        o_dil = _dil_call(qa, ka, va, dil_bias)
        o_mla = _mla_call(shp(qm), shp(km), shp(vm), n_mla)

        x2d = _out_ffn_call(x2d, o_dil.reshape(batch * seq, -1), o_mla.reshape(batch * seq, -1),
                            row(out_norm_dil[l]), row(out_norm_mla[l]), w_out[l].astype(BF16),
                            row(ffn2_norm[l]), ffn2_w_gate[l], ffn2_w_up[l], ffn2_w_down[l])
    return x2d.reshape(batch, seq, d_model)
```

```python
import functools

import numpy as np
import jax
import jax.numpy as jnp
from jax import lax
from jax.experimental import pallas as pl
from jax.experimental.pallas import tpu as pltpu

F32 = jnp.float32
BF16 = jnp.bfloat16

DIL_HEAD_DIM = 64
DIL_BRANCHES = ((128, 1), (512, 4), (2048, 16))
DIL_BAND = 128
DIL_RES = 16
MLA_NOPE = 128
MLA_ROPE = 64
MLA_V_DIM = 128
ROPE_BASE = 10000.0
REL_BUCKETS = 32
REL_MAX_DIST = 2048
FFN_RESID = 0.5
EPS = 1e-6

LANES = 128
MXU_DIM = 256
VMEM_LIMIT_BYTES = 56 * 1024 * 1024
VMEM_LIMIT_FFN_BYTES = 60 * 1024 * 1024

NEG = float("-inf")
LOG2E = 1.4426950408889634

ROW_TILE = 512
FFN_STEP_ROWS = 2 * ROW_TILE
FFN_TF = 256
DIL_CHUNK = DIL_BAND * DIL_RES
DIL_STEP_CHUNKS = 2
MLA_TQ = 2048
MLA_TK = 512


def _dot(a, b):
    return jnp.dot(a, b, preferred_element_type=F32)


def _dot_nt(a, b):
    return lax.dot_general(a, b, (((1,), (1,)), ((), ())), preferred_element_type=F32)


def _rms(x, g):
    return x * lax.rsqrt(jnp.mean(x * x, axis=-1, keepdims=True) + EPS) * g


def _residue_major_perm(rows):
    per = rows // DIL_RES
    dst = np.arange(rows)
    src = DIL_RES * (dst % per) + dst // per
    p = np.zeros((rows, rows), np.float32)
    p[dst, src] = 1.0
    return p


def _swiglu_residual(x, g_ref, wg_ref, wu_ref, wd_ref, act_ref):
    h = _rms(x, g_ref[...]).astype(BF16)
    d_ff = wg_ref.shape[1]
    for c in range(d_ff // FFN_TF):
        sl = slice(c * FFN_TF, (c + 1) * FFN_TF)
        gate = _dot(h, wg_ref[:, sl].astype(BF16))
        up = _dot(h, wu_ref[:, sl].astype(BF16))
        act_ref[:, sl] = (gate * (0.5 * jnp.tanh(0.5 * gate) + 0.5) * up).astype(BF16)
    return x + FFN_RESID * _dot(act_ref[...], wd_ref[...].astype(BF16))


def _ffn_kernel(x_ref, g_ref, wg_ref, wu_ref, wd_ref, o_ref, act_ref):
    for r0 in range(0, x_ref.shape[0], ROW_TILE):
        rows = slice(r0, r0 + ROW_TILE)
        o_ref[rows, :] = _swiglu_residual(x_ref[rows, :], g_ref, wg_ref, wu_ref, wd_ref, act_ref)


def _out_ffn_kernel(x_ref, od_ref, om_ref, gd_ref, gm_ref, wo_ref, g_ref, wg_ref, wu_ref, wd_ref,
                    o_ref, act_ref):
    od = _rms(od_ref[...], gd_ref[...]).astype(BF16)
    om = _rms(om_ref[...], gm_ref[...]).astype(BF16)
    x2 = x_ref[...] + _dot(jnp.concatenate([od, om], axis=1), wo_ref[...])
    o_ref[...] = _swiglu_residual(x2, g_ref, wg_ref, wu_ref, wd_ref, act_ref)


def _resident(shape):
    return pl.BlockSpec(shape, lambda *_: (0,) * len(shape), pipeline_mode=pl.Buffered(1))


def _residue_major_block(seq, width):
    per = ROW_TILE // DIL_RES
    blocks_per_seq = seq // ROW_TILE
    return pl.BlockSpec((None, DIL_RES, per, width), lambda i: (i // blocks_per_seq, 0, i % blocks_per_seq, 0))


def _ffn_call(x2d, g, wg, wu, wd):
    m, d = x2d.shape
    f = wg.shape[1]
    row = pl.BlockSpec((FFN_STEP_ROWS, d), lambda i: (i, 0))
    return pl.pallas_call(
        _ffn_kernel,
        grid=(m // FFN_STEP_ROWS,),
        in_specs=[row, _resident((1, d)), _resident((d, f)), _resident((d, f)), _resident((f, d))],
        out_specs=row,
        out_shape=jax.ShapeDtypeStruct((m, d), F32),
        scratch_shapes=[pltpu.VMEM((ROW_TILE, f), BF16)],
        compiler_params=pltpu.CompilerParams(dimension_semantics=("parallel",),
                                             vmem_limit_bytes=VMEM_LIMIT_FFN_BYTES),
        name="ffn",
    )(x2d, g, wg, wu, wd)


def _out_ffn_call(x2d, o_dil, o_mla, gd, gm, wo, g, wg, wu, wd):
    m, d = x2d.shape
    f = wg.shape[1]
    wdil, wmla = o_dil.shape[1], o_mla.shape[1]
    row = pl.BlockSpec((ROW_TILE, d), lambda i: (i, 0))
    return pl.pallas_call(
        _out_ffn_kernel,
        grid=(m // ROW_TILE,),
        in_specs=[row,
                  pl.BlockSpec((ROW_TILE, wdil), lambda i: (i, 0)),
                  pl.BlockSpec((ROW_TILE, wmla), lambda i: (i, 0)),
                  _resident((1, wdil)), _resident((1, wmla)),
                  _resident((wdil + wmla, d)),
                  _resident((1, d)), _resident((d, f)), _resident((d, f)), _resident((f, d))],
        out_specs=row,
        out_shape=jax.ShapeDtypeStruct((m, d), F32),
        scratch_shapes=[pltpu.VMEM((ROW_TILE, f), BF16)],
        compiler_params=pltpu.CompilerParams(dimension_semantics=("parallel",),
                                             vmem_limit_bytes=VMEM_LIMIT_BYTES),
        name="out_ffn",
    )(x2d, o_dil, o_mla, gd, gm, wo, g, wg, wu, wd)


def _rope(x, cos, sin_signed):
    lane = lax.broadcasted_iota(jnp.int32, x.shape, 1)
    first_half = (lane % MLA_ROPE) < (MLA_ROPE // 2)
    partner = jnp.where(first_half,
                        pltpu.roll(x, LANES - MLA_ROPE // 2, 1),
                        pltpu.roll(x, MLA_ROPE // 2, 1))
    return x * cos + partner * sin_signed


def _proj_kernel(x_ref, gmix_ref, wint_ref, wkpet_ref, eh_ref, perm_ref, gqa_ref, gka_ref,
                 gcq_ref, wqb_ref, gckv_ref, wkvb_ref,
                 gqn_ref, gqr_ref, gkn_ref, gkr_ref, cos_ref, sin_ref,
                 qa_ref, ka_ref, va_ref, qm_ref, km_ref, vm_ref, *, n_mla_heads, dil_width):
    h = _rms(x_ref[...], gmix_ref[...]).astype(BF16)
    w = dil_width
    o = 3 * w
    q_rank = gcq_ref.shape[1]
    kv_rank = gckv_ref.shape[1]
    qk_dim = MLA_NOPE + MLA_ROPE
    nh = n_mla_heads
    cos = cos_ref[...]
    sin = sin_ref[...]

    w_mla = jnp.concatenate([wint_ref[o:o + q_rank + kv_rank, :].astype(BF16), wkpet_ref[...]], axis=0)
    pm = _dot_nt(h, w_mla)
    cq = pm[:, :q_rank]
    ckv = pm[:, q_rank:q_rank + kv_rank]
    kpe = pm[:, q_rank + kv_rank:]

    qb = _dot(_rms(cq, gcq_ref[...]).astype(BF16), wqb_ref[...])
    for hd in range(nh):
        qn = qb[:, hd * MLA_NOPE:(hd + 1) * MLA_NOPE]
        qr = qb[:, nh * MLA_NOPE + hd * LANES:nh * MLA_NOPE + (hd + 1) * LANES]
        ss = jnp.sum(qn * qn, axis=-1, keepdims=True) + jnp.sum(qr * qr, axis=-1, keepdims=True)
        r = lax.rsqrt(ss * (1.0 / qk_dim) + EPS)
        qm_ref[:, hd * MXU_DIM:hd * MXU_DIM + LANES] = (qn * r * gqn_ref[...]).astype(BF16)
        qm_ref[:, hd * MXU_DIM + LANES:(hd + 1) * MXU_DIM] = _rope(qr * r * gqr_ref[...], cos, sin).astype(BF16)

    kvb = _dot(_rms(ckv, gckv_ref[...]).astype(BF16), wkvb_ref[...])
    ss_pe = jnp.sum(kpe * kpe, axis=-1, keepdims=True)
    kpe_rot = _rope(kpe * gkr_ref[...], cos, sin)
    for hd in range(nh):
        kn = kvb[:, hd * MLA_NOPE:(hd + 1) * MLA_NOPE]
        ss = jnp.sum(kn * kn, axis=-1, keepdims=True) + ss_pe
        r = lax.rsqrt(ss * (1.0 / qk_dim) + EPS)
        km_ref[:, hd * MXU_DIM:hd * MXU_DIM + LANES] = (kn * r * gkn_ref[...]).astype(BF16)
        km_ref[:, hd * MXU_DIM + LANES:(hd + 1) * MXU_DIM] = (kpe_rot * r).astype(BF16)
    vm_ref[...] = kvb[:, nh * MLA_NOPE:].astype(BF16)

    hp = _dot(perm_ref[...], h).astype(BF16)
    for c, (g_ref, dst) in enumerate(((gqa_ref, qa_ref), (gka_ref, ka_ref))):
        src = _dot_nt(hp, wint_ref[c * w:(c + 1) * w, :].astype(BF16))
        sq = (src * src).astype(BF16)
        ms = jnp.concatenate([_dot(sq[:, c0:c0 + MXU_DIM], eh_ref[...]) for c0 in range(0, w, MXU_DIM)],
                             axis=1) * (1.0 / DIL_HEAD_DIM)
        dst[...] = (src * lax.rsqrt(ms + EPS) * g_ref[...]).reshape(dst.shape)
    va_ref[...] = _dot_nt(hp, wint_ref[2 * w:o, :].astype(BF16)).reshape(va_ref.shape)


def _proj_call(x2d, batch, seq, consts, cos, sin, *, n_mla_heads, dil_width):
    m, d = x2d.shape
    tm = ROW_TILE
    n_seq_blocks = seq // tm
    row = lambda width: pl.BlockSpec((tm, width), lambda i: (i, 0))
    pos = pl.BlockSpec((tm, LANES), lambda i: (i % n_seq_blocks, 0))
    mla_w = n_mla_heads * MXU_DIM
    dil_shape = jax.ShapeDtypeStruct((batch, DIL_RES, seq // DIL_RES, dil_width), F32)
    return pl.pallas_call(
        functools.partial(_proj_kernel, n_mla_heads=n_mla_heads, dil_width=dil_width),
        grid=(m // tm,),
        in_specs=[row(d)] + [_resident(c.shape) for c in consts] + [pos, pos],
        out_specs=[_residue_major_block(seq, dil_width)] * 3
        + [row(mla_w), row(mla_w), row(n_mla_heads * MLA_V_DIM)],
        out_shape=[dil_shape] * 3
        + [jax.ShapeDtypeStruct((m, mla_w), BF16)] * 2
        + [jax.ShapeDtypeStruct((m, n_mla_heads * MLA_V_DIM), BF16)],
        compiler_params=pltpu.CompilerParams(dimension_semantics=("parallel",),
                                             vmem_limit_bytes=VMEM_LIMIT_BYTES),
        name="proj",
    )(x2d, *consts, cos, sin)


def _t5_bucket(dist):
    max_exact = REL_BUCKETS // 2
    d = np.maximum(dist, 1).astype(np.float32)
    large = max_exact + (np.log(d / max_exact) / np.log(REL_MAX_DIST / max_exact)
                         * (REL_BUCKETS - max_exact)).astype(np.int32)
    large = np.minimum(large, REL_BUCKETS - 1)
    return np.where(dist < max_exact, dist, large).astype(np.int32)


def _band_tables():
    rho = np.arange(DIL_BAND)
    kap = np.arange(2 * DIL_BAND)
    buckets, prev = [], []
    for _, dil in DIL_BRANCHES:
        g = DIL_RES // dil
        run_q = DIL_BAND // g
        pos_q = g * (rho % run_q) + rho // run_q
        pos_k = g * (kap % (2 * run_q)) + kap // (2 * run_q) - DIL_BAND
        delta = pos_q[:, None] - pos_k[None, :]
        valid = (delta >= 0) & (delta <= DIL_BAND)
        buckets.append(np.where(valid, _t5_bucket(np.clip(delta, 0, None) * dil), -1))
        prev.append((pos_k < 0)[None, :])
    return np.stack(buckets).astype(np.int32), np.stack(prev).astype(np.int32)


def _dil_bias_kernel(rel_ref, bucket_ref, prev_ref, o_ref, *, present):
    nh = o_ref.shape[2]
    rb = 16
    scaled = {}
    for bi, buckets_here in enumerate(present):
        prev = prev_ref[bi] > 0
        for r0 in range(0, DIL_BAND, rb):
            bucket = bucket_ref[bi, r0:r0 + rb, :]
            accs = [jnp.full(bucket.shape, NEG, F32)] * nh
            for b in buckets_here:
                hit = bucket == b
                for hd in range(nh):
                    if (hd, b) not in scaled:
                        scaled[hd, b] = rel_ref[hd, b] * LOG2E
                accs = [jnp.where(hit, scaled[hd, b], acc) for hd, acc in enumerate(accs)]
            for hd in range(nh):
                o_ref[bi, 0, hd, r0:r0 + rb, :] = accs[hd]
                o_ref[bi, 1, hd, r0:r0 + rb, :] = jnp.where(prev, NEG, accs[hd])


def _dil_bias_call(rel_bias, buckets, prev):
    nbr = buckets.shape[0]
    nh = rel_bias.shape[0]
    tile = buckets.shape[1:]
    present = tuple(tuple(int(b) for b in np.unique(t) if b >= 0) for t in buckets)
    whole = lambda shape: pl.BlockSpec(shape, lambda i: (0,) * len(shape))
    out_shape = (nbr, 2, nh) + tile
    return pl.pallas_call(
        functools.partial(_dil_bias_kernel, present=present),
        grid=(1,),
        in_specs=[pl.BlockSpec(memory_space=pltpu.SMEM), whole(buckets.shape), whole(prev.shape)],
        out_specs=whole(out_shape),
        out_shape=jax.ShapeDtypeStruct(out_shape, F32),
        name="dil_bias",
    )(rel_bias, jnp.asarray(buckets), jnp.asarray(prev))


def _band_tile(q, k, v, bias, head_a):
    zero = jnp.zeros_like(q)
    q2 = jnp.concatenate([jnp.where(head_a, q, zero), jnp.where(head_a, zero, q)], axis=0).astype(BF16)
    s = _dot_nt(q2, k.astype(BF16)) + bias
    m = jnp.max(s, axis=-1, keepdims=True)
    p = jnp.exp2(s - m).astype(BF16)
    ones = jnp.ones(v.shape, BF16)
    out = _dot(p, jnp.concatenate([v.astype(BF16), ones], axis=1))
    h = DIL_BAND
    num = jnp.where(head_a, out[:h, :LANES], out[h:, :LANES])
    den = jnp.where(head_a, out[:h, LANES:], out[h:, LANES:])
    mx = jnp.where(head_a, m[:h], m[h:])
    return num, den, mx


def _dil_kernel(q_ref, kp_ref, kc_ref, vp_ref, vc_ref, bias_ref, o_ref, num_scr, den_scr, max_scr):
    first_block = (pl.program_id(2) == 0).astype(jnp.int32)
    head_a = lax.broadcasted_iota(jnp.int32, (1, LANES), 1) < DIL_HEAD_DIM

    def tile(j, bi, dil, rbase, n):
        g = DIL_RES // dil
        run = DIL_BAND // g
        base = j * DIL_BAND
        q_rows = pl.ds(base + n * run, run)
        qs, ks, vs = [], [], []
        for u in range(g):
            res = rbase + dil * u
            qs.append(q_ref[res, q_rows, :])
            if n == 0 and j == 0:
                ks += [kp_ref[res, pl.ds(DIL_BAND - run, run), :], kc_ref[res, pl.ds(0, run), :]]
                vs += [vp_ref[res, pl.ds(DIL_BAND - run, run), :], vc_ref[res, pl.ds(0, run), :]]
            else:
                k_rows = pl.ds(base + (n - 1) * run, 2 * run)
                ks.append(kc_ref[res, k_rows, :])
                vs.append(vc_ref[res, k_rows, :])
        cat = lambda parts: parts[0] if len(parts) == 1 else jnp.concatenate(parts, axis=0)
        variant = first_block if (n == 0 and j == 0) else 0
        num, den, mx = _band_tile(cat(qs), cat(ks), cat(vs), bias_ref[bi, variant], head_a)
        if g == 1:
            return num, den, mx
        s_rows = pl.ds(n * run, run)
        for u in range(g):
            res = rbase + dil * u
            part = slice(u * run, (u + 1) * run)
            num_scr[j, bi, res, s_rows, :] = num[part]
            den_scr[j, bi, res, s_rows, :] = den[part]
            max_scr[j, bi, res, s_rows, :] = mx[part]

    stored = len(DIL_BRANCHES) - 1
    for j in range(q_ref.shape[1] // DIL_BAND):
        for n in range(DIL_RES):
            tile(j, 0, 1, 0, n)
        for r4 in range(4):
            for n in range(4):
                tile(j, 1, 4, r4, n)
        for r in range(DIL_RES):
            num_r, den_r, max_r = tile(j, stored, DIL_RES, r, 0)
            ms = [max_scr[j, bi, r] for bi in range(stored)] + [max_r]
            nums = [num_scr[j, bi, r] for bi in range(stored)] + [num_r]
            dens = [den_scr[j, bi, r] for bi in range(stored)] + [den_r]
            m_all = functools.reduce(jnp.maximum, ms)
            es = [jnp.exp2(mb - m_all) for mb in ms]
            num = sum(e * x for e, x in zip(es, nums))
            den = sum(e * x for e, x in zip(es, dens))
            o_ref[pl.ds(j * DIL_CHUNK + r, DIL_BAND, stride=DIL_RES), :] = num / den


def _dil_call(qa, ka, va, bias):
    b, _, per, w = qa.shape
    pairs = w // LANES
    nbr = bias.shape[0]
    rows = DIL_STEP_CHUNKS * DIL_BAND
    cur = pl.BlockSpec((None, DIL_RES, rows, LANES), lambda bb, p, c: (bb, 0, c, p))
    prev = pl.BlockSpec((None, DIL_RES, DIL_BAND, LANES),
                        lambda bb, p, c: (bb, 0, jnp.maximum(DIL_STEP_CHUNKS * c - 1, 0), p))
    scr = pltpu.VMEM((DIL_STEP_CHUNKS, nbr - 1, DIL_RES, DIL_BAND, LANES), F32)
    return pl.pallas_call(
        _dil_kernel,
        grid=(b, pairs, per // rows),
        in_specs=[cur, prev, cur, prev, cur,
                  pl.BlockSpec((nbr, 2, None, 2 * DIL_BAND, 2 * DIL_BAND), lambda bb, p, c: (0, 0, p, 0, 0))],
        out_specs=pl.BlockSpec((None, DIL_STEP_CHUNKS * DIL_CHUNK, LANES), lambda bb, p, c: (bb, c, p)),
        out_shape=jax.ShapeDtypeStruct((b, per * DIL_RES, w), F32),
        scratch_shapes=[scr, scr, scr],
        compiler_params=pltpu.CompilerParams(dimension_semantics=("parallel", "parallel", "arbitrary"),
                                             vmem_limit_bytes=VMEM_LIMIT_BYTES),
        name="dil_attn",
    )(qa, ka, ka, va, va, bias)


def _mla_kernel(q_ref, k_ref, v_ref, o_ref, m_ref, acc_ref, s0_ref, s1_ref):
    tq = m_ref.shape[0]
    tk = MLA_TK
    nsub = tq // tk
    ones = jnp.ones((tk, LANES), BF16)
    s_bufs = (s0_ref, s1_ref)

    def q_tile(qi):
        q_lo = qi * tq
        m_ref[...] = jnp.full(m_ref.shape, NEG, F32)
        acc_ref[...] = jnp.zeros(acc_ref.shape, F32)

        def scores(kblk, row0=0):
            k0 = pl.multiple_of(kblk * tk, tk)
            return _dot_nt(q_ref[q_lo + row0:q_lo + tq, :], k_ref[pl.ds(k0, tk), :])

        def update(s, kblk, row0, diagonal):
            k0 = pl.multiple_of(kblk * tk, tk)
            rows = slice(row0, tq)
            if diagonal:
                row = lax.broadcasted_iota(jnp.int32, s.shape, 0)
                col = lax.broadcasted_iota(jnp.int32, s.shape, 1)
                s = jnp.where(col <= row, s, NEG)
            m_old = m_ref[rows, :]
            m_new = jnp.maximum(m_old, jnp.max(s, axis=-1, keepdims=True))
            m_ref[rows, :] = m_new
            p = jnp.concatenate([jnp.exp2(s[:, c * LANES:(c + 1) * LANES] - m_new) for c in range(tk // LANES)],
                                axis=1).astype(BF16)
            alpha = jnp.exp2(m_old - m_new)
            vext = jnp.concatenate([v_ref[pl.ds(k0, tk), :], ones], axis=1)
            acc_ref[rows, :] = acc_ref[rows, :] * jnp.concatenate([alpha, alpha], axis=1) + _dot(p, vext)

        s0_ref[...] = scores(0)

        def body(t, carry):
            for jj in range(nsub):
                s_bufs[(jj + 1) % 2][...] = scores(nsub * t + jj + 1)
                update(s_bufs[jj % 2][...], nsub * t + jj, 0, diagonal=False)
            return carry

        if qi:
            lax.fori_loop(0, qi, body, 0)
        s_cur = s0_ref[...]
        for d in range(nsub):
            s_next = scores(nsub * qi + d + 1, row0=(d + 1) * tk) if d + 1 < nsub else None
            update(s_cur, nsub * qi + d, d * tk, diagonal=True)
            s_cur = s_next
        acc = acc_ref[...]
        o_ref[q_lo:q_lo + tq, :] = acc[:, :MLA_V_DIM] / acc[:, MLA_V_DIM:]

    for qi in range(q_ref.shape[0] // tq):
        q_tile(qi)


def _mla_call(qm, km, vm, n_heads):
    b, s, _ = qm.shape
    tq = MLA_TQ
    assert tq % (2 * MLA_TK) == 0
    seq_block = lambda width: pl.BlockSpec((None, s, width), lambda bb, h: (bb, 0, h))
    return pl.pallas_call(
        _mla_kernel,
        grid=(b, n_heads),
        in_specs=[seq_block(MXU_DIM), seq_block(MXU_DIM), seq_block(MLA_V_DIM)],
        out_specs=seq_block(MLA_V_DIM),
        out_shape=jax.ShapeDtypeStruct((b, s, n_heads * MLA_V_DIM), F32),
        scratch_shapes=[pltpu.VMEM((tq, LANES), F32), pltpu.VMEM((tq, 2 * MLA_V_DIM), F32),
                        pltpu.VMEM((tq, MLA_TK), F32), pltpu.VMEM((tq, MLA_TK), F32)],
        compiler_params=pltpu.CompilerParams(dimension_semantics=("parallel", "parallel"),
                                             vmem_limit_bytes=VMEM_LIMIT_BYTES),
        name="mla_attn",
    )(qm, km, vm)


def _rope_tables(seq):
    inv_freq = ROPE_BASE ** (-np.arange(0, MLA_ROPE, 2, dtype=np.float64) / MLA_ROPE)
    ang = np.arange(seq, dtype=np.float64)[:, None] * inv_freq[None, :]
    cos, sin = np.cos(ang), np.sin(ang)
    cos_t = np.concatenate([cos, cos] * (LANES // MLA_ROPE), axis=1)
    sin_t = np.concatenate([-sin, sin] * (LANES // MLA_ROPE), axis=1)
    return jnp.asarray(cos_t, F32), jnp.asarray(sin_t, F32)


def _pad_lanes(v, width):
    return jnp.pad(v, ((0, 0), (0, width - v.shape[1])))


def kernel(x, ffn1_norm, ffn1_w_gate, ffn1_w_up, ffn1_w_down, mix_norm, w_in, dil_q_norm, dil_k_norm,
           rel_bias, mla_q_a_norm, mla_w_q_b, mla_kv_a_norm, mla_w_kv_b, mla_q_norm, mla_k_norm,
           out_norm_dil, out_norm_mla, w_out, ffn2_norm, ffn2_w_gate, ffn2_w_up, ffn2_w_down):
    batch, seq, d_model = x.shape
    depth = w_in.shape[0]
    dil_width = out_norm_dil.shape[1]
    dil_heads = dil_width // DIL_HEAD_DIM
    q_rank = mla_q_a_norm.shape[1]
    kv_rank = mla_kv_a_norm.shape[1]
    qk_dim = MLA_NOPE + MLA_ROPE
    n_mla = mla_w_q_b.shape[2] // qk_dim
    assert rel_bias.shape == (dil_heads, REL_BUCKETS)
    assert w_in.shape[2] == 3 * dil_width + q_rank + kv_rank + MLA_ROPE
    assert seq % (DIL_STEP_CHUNKS * DIL_CHUNK) == 0 and seq % MLA_TQ == 0 and seq % ROW_TILE == 0
    assert (batch * seq) % FFN_STEP_ROWS == 0
    assert all(win // dil == DIL_BAND and DIL_RES % dil == 0 for win, dil in DIL_BRANCHES)
    assert DIL_BRANCHES[-1][1] == DIL_RES

    cos_t, sin_t = _rope_tables(seq)
    buckets, prev = _band_tables()
    dil_bias = _dil_bias_call(rel_bias, buckets, prev)
    dil_bias = dil_bias.reshape(dil_bias.shape[0], 2, dil_heads // 2, 2 * DIL_BAND, 2 * DIL_BAND)
    eh = jnp.asarray(np.kron(np.eye(MXU_DIM // DIL_HEAD_DIM), np.ones((DIL_HEAD_DIM, DIL_HEAD_DIM))), BF16)
    perm = jnp.asarray(_residue_major_perm(ROW_TILE), BF16)

    x2d = x.reshape(batch * seq, d_model)
    row = lambda v: v.reshape(1, -1)
    for l in range(depth):
        x2d = _ffn_call(x2d, row(ffn1_norm[l]), ffn1_w_gate[l], ffn1_w_up[l], ffn1_w_down[l])

        w_in_t = jnp.swapaxes(w_in[l], 0, 1)
        wkpe_t = jnp.pad(w_in_t[w_in.shape[2] - MLA_ROPE:], ((0, LANES - MLA_ROPE), (0, 0))).astype(BF16)
        wqb = mla_w_q_b[l].reshape(q_rank, n_mla, qk_dim)
        wqb = jnp.concatenate(
            [wqb[:, :, :MLA_NOPE].reshape(q_rank, n_mla * MLA_NOPE),
             jnp.pad(wqb[:, :, MLA_NOPE:], ((0, 0), (0, 0), (0, LANES - MLA_ROPE))).reshape(q_rank, n_mla * LANES)],
            axis=1).astype(BF16)
        wkvb = mla_w_kv_b[l].reshape(kv_rank, n_mla, MLA_NOPE + MLA_V_DIM)
        wkvb = jnp.concatenate([wkvb[:, :, :MLA_NOPE].reshape(kv_rank, n_mla * MLA_NOPE),
                                wkvb[:, :, MLA_NOPE:].reshape(kv_rank, n_mla * MLA_V_DIM)], axis=1).astype(BF16)
        gqa = row(jnp.tile(dil_q_norm[l], dil_heads)) * (LOG2E * DIL_HEAD_DIM ** -0.5)
        gka = row(jnp.tile(dil_k_norm[l], dil_heads))
        gq = row(mla_q_norm[l]) * (LOG2E * qk_dim ** -0.5)
        gk = row(mla_k_norm[l])
        gqn, gqr = gq[:, :MLA_NOPE], _pad_lanes(gq[:, MLA_NOPE:], LANES)
        gkn, gkr = gk[:, :MLA_NOPE], _pad_lanes(gk[:, MLA_NOPE:], LANES)

        consts = [row(mix_norm[l]), w_in_t, wkpe_t, eh, perm, gqa, gka, row(mla_q_a_norm[l]), wqb,
                  row(mla_kv_a_norm[l]), wkvb, gqn, gqr, gkn, gkr]
        qa, ka, va, qm, km, vm = _proj_call(x2d, batch, seq, consts, cos_t, sin_t,
                                            n_mla_heads=n_mla, dil_width=dil_width)

        shp = lambda a: a.reshape(batch, seq, a.shape[1])
        o_dil = _dil_call(qa, ka, va, dil_bias)
        o_mla = _mla_call(shp(qm), shp(km), shp(vm), n_mla)

        x2d = _out_ffn_call(x2d, o_dil.reshape(batch * seq, -1), o_mla.reshape(batch * seq, -1),
                            row(out_norm_dil[l]), row(out_norm_mla[l]), w_out[l].astype(BF16),
                            row(ffn2_norm[l]), ffn2_w_gate[l], ffn2_w_up[l], ffn2_w_down[l])
    return x2d.reshape(batch, seq, d_model)
```

```python
import functools

import numpy as np
import jax
import jax.numpy as jnp
from jax import lax
from jax.experimental import pallas as pl
from jax.experimental.pallas import tpu as pltpu

F32 = jnp.float32
BF16 = jnp.bfloat16

DIL_HEAD_DIM = 64
DIL_BRANCHES = ((128, 1), (512, 4), (2048, 16))
DIL_BAND = 128
DIL_RES = 16
MLA_NOPE = 128
MLA_ROPE = 64
MLA_V_DIM = 128
ROPE_BASE = 10000.0
REL_BUCKETS = 32
REL_MAX_DIST = 2048
FFN_RESID = 0.5
EPS = 1e-6

LANES = 128
MXU_DIM = 256
VMEM_LIMIT_BYTES = 56 * 1024 * 1024

NEG = float("-inf")
LOG2E = 1.4426950408889634

ROW_TILE = 512
FFN_TF = 256
DIL_CHUNK = DIL_BAND * DIL_RES
DIL_STEP_CHUNKS = 2
MLA_TQ = 2048
MLA_TK = 512


def _dot(a, b):
    return jnp.dot(a, b, preferred_element_type=F32)


def _dot_nt(a, b):
    return lax.dot_general(a, b, (((1,), (1,)), ((), ())), preferred_element_type=F32)


def _rms(x, g):
    return x * lax.rsqrt(jnp.mean(x * x, axis=-1, keepdims=True) + EPS) * g


def _residue_major_perm(rows):
    per = rows // DIL_RES
    dst = np.arange(rows)
    src = DIL_RES * (dst % per) + dst // per
    p = np.zeros((rows, rows), np.float32)
    p[dst, src] = 1.0
    return p


def _swiglu_residual(x, g_ref, wg_ref, wu_ref, wd_ref, act_ref, wait_chunk=None):
    h = _rms(x, g_ref[...]).astype(BF16)
    d_ff = wg_ref.shape[1]
    for c in range(d_ff // FFN_TF):
        sl = slice(c * FFN_TF, (c + 1) * FFN_TF)
        if wait_chunk is not None:
            wait_chunk(0, c)
            wait_chunk(1, c)
        gate = _dot(h, wg_ref[:, sl].astype(BF16))
        up = _dot(h, wu_ref[:, sl].astype(BF16))
        act_ref[:, sl] = (gate * (0.5 * jnp.tanh(0.5 * gate) + 0.5) * up).astype(BF16)
    if wait_chunk is not None:
        for c in range(d_ff // FFN_TF):
            wait_chunk(2, c)
    return x + FFN_RESID * _dot(act_ref[...], wd_ref[...].astype(BF16))


def _ffn_kernel(x_ref, g_ref, wg_hbm, wu_hbm, wd_hbm, o_ref, wg_ref, wu_ref, wd_ref, act_ref, sem):
    n_chunks = wg_ref.shape[1] // FFN_TF

    def chunk_copy(kind, c):
        sl = pl.ds(c * FFN_TF, FFN_TF)
        if kind == 2:
            return pltpu.make_async_copy(wd_hbm.at[sl, :], wd_ref.at[sl, :], sem.at[kind, c])
        src, dst = ((wg_hbm, wg_ref), (wu_hbm, wu_ref))[kind]
        return pltpu.make_async_copy(src.at[:, sl], dst.at[:, sl], sem.at[kind, c])

    first = pl.program_id(0) == 0

    @pl.when(first)
    def _():
        for c in range(n_chunks):
            chunk_copy(0, c).start()
            chunk_copy(1, c).start()
        for c in range(n_chunks):
            chunk_copy(2, c).start()
        o_ref[...] = _swiglu_residual(x_ref[...], g_ref, wg_ref, wu_ref, wd_ref, act_ref,
                                      wait_chunk=lambda kind, c: chunk_copy(kind, c).wait())

    @pl.when(jnp.logical_not(first))
    def _():
        o_ref[...] = _swiglu_residual(x_ref[...], g_ref, wg_ref, wu_ref, wd_ref, act_ref)


def _out_ffn_kernel(x_ref, od_ref, om_ref, gd_ref, gm_ref, wo_ref, g_ref, wg_ref, wu_ref, wd_ref,
                    o_ref, act_ref):
    od = _rms(od_ref[...], gd_ref[...]).astype(BF16)
    om = _rms(om_ref[...], gm_ref[...]).astype(BF16)
    x2 = x_ref[...] + _dot(jnp.concatenate([od, om], axis=1), wo_ref[...])
    o_ref[...] = _swiglu_residual(x2, g_ref, wg_ref, wu_ref, wd_ref, act_ref)


def _resident(shape):
    return pl.BlockSpec(shape, lambda *_: (0,) * len(shape), pipeline_mode=pl.Buffered(1))


def _residue_major_block(seq, width):
    per = ROW_TILE // DIL_RES
    blocks_per_seq = seq // ROW_TILE
    return pl.BlockSpec((None, DIL_RES, per, width), lambda i: (i // blocks_per_seq, 0, i % blocks_per_seq, 0))


def _ffn_call(x2d, g, wg, wu, wd):
    m, d = x2d.shape
    f = wg.shape[1]
    row = pl.BlockSpec((ROW_TILE, d), lambda i: (i, 0))
    hbm = pl.BlockSpec(memory_space=pl.ANY)
    return pl.pallas_call(
        _ffn_kernel,
        grid=(m // ROW_TILE,),
        in_specs=[row, _resident((1, d)), hbm, hbm, hbm],
        out_specs=row,
        out_shape=jax.ShapeDtypeStruct((m, d), F32),
        scratch_shapes=[pltpu.VMEM((d, f), F32), pltpu.VMEM((d, f), F32), pltpu.VMEM((f, d), F32),
                        pltpu.VMEM((ROW_TILE, f), BF16), pltpu.SemaphoreType.DMA((3, f // FFN_TF))],
        compiler_params=pltpu.CompilerParams(dimension_semantics=("arbitrary",),
                                             vmem_limit_bytes=VMEM_LIMIT_BYTES),
        name="ffn",
    )(x2d, g, wg, wu, wd)


def _out_ffn_call(x2d, o_dil, o_mla, gd, gm, wo, g, wg, wu, wd):
    m, d = x2d.shape
    f = wg.shape[1]
    wdil, wmla = o_dil.shape[1], o_mla.shape[1]
    row = pl.BlockSpec((ROW_TILE, d), lambda i: (i, 0))
    return pl.pallas_call(
        _out_ffn_kernel,
        grid=(m // ROW_TILE,),
        in_specs=[row,
                  pl.BlockSpec((ROW_TILE, wdil), lambda i: (i, 0)),
                  pl.BlockSpec((ROW_TILE, wmla), lambda i: (i, 0)),
                  _resident((1, wdil)), _resident((1, wmla)),
                  _resident((wdil + wmla, d)),
                  _resident((1, d)), _resident((d, f)), _resident((d, f)), _resident((f, d))],
        out_specs=row,
        out_shape=jax.ShapeDtypeStruct((m, d), F32),
        scratch_shapes=[pltpu.VMEM((ROW_TILE, f), BF16)],
        compiler_params=pltpu.CompilerParams(dimension_semantics=("parallel",),
                                             vmem_limit_bytes=VMEM_LIMIT_BYTES),
        name="out_ffn",
    )(x2d, o_dil, o_mla, gd, gm, wo, g, wg, wu, wd)


def _rope(x, cos, sin_signed):
    lane = lax.broadcasted_iota(jnp.int32, x.shape, 1)
    first_half = (lane % MLA_ROPE) < (MLA_ROPE // 2)
    partner = jnp.where(first_half,
                        pltpu.roll(x, LANES - MLA_ROPE // 2, 1),
                        pltpu.roll(x, MLA_ROPE // 2, 1))
    return x * cos + partner * sin_signed


def _proj_kernel(x_ref, gmix_ref, wint_ref, wkpet_ref, eh_ref, perm_ref, gqa_ref, gka_ref,
                 gcq_ref, wqb_ref, gckv_ref, wkvb_ref,
                 gqn_ref, gqr_ref, gkn_ref, gkr_ref, cos_ref, sin_ref,
                 qa_ref, ka_ref, va_ref, qm_ref, km_ref, vm_ref, *, n_mla_heads, dil_width):
    h = _rms(x_ref[...], gmix_ref[...]).astype(BF16)
    w = dil_width
    o = 3 * w
    q_rank = gcq_ref.shape[1]
    kv_rank = gckv_ref.shape[1]
    qk_dim = MLA_NOPE + MLA_ROPE
    nh = n_mla_heads
    cos = cos_ref[...]
    sin = sin_ref[...]

    w_mla = jnp.concatenate([wint_ref[o:o + q_rank + kv_rank, :].astype(BF16), wkpet_ref[...]], axis=0)
    pm = _dot_nt(h, w_mla)
    cq = pm[:, :q_rank]
    ckv = pm[:, q_rank:q_rank + kv_rank]
    kpe = pm[:, q_rank + kv_rank:]

    qb = _dot(_rms(cq, gcq_ref[...]).astype(BF16), wqb_ref[...])
    for hd in range(nh):
        qn = qb[:, hd * MLA_NOPE:(hd + 1) * MLA_NOPE]
        qr = qb[:, nh * MLA_NOPE + hd * LANES:nh * MLA_NOPE + (hd + 1) * LANES]
        ss = jnp.sum(qn * qn, axis=-1, keepdims=True) + jnp.sum(qr * qr, axis=-1, keepdims=True)
        r = lax.rsqrt(ss * (1.0 / qk_dim) + EPS)
        qm_ref[:, hd * MXU_DIM:hd * MXU_DIM + LANES] = (qn * r * gqn_ref[...]).astype(BF16)
        qm_ref[:, hd * MXU_DIM + LANES:(hd + 1) * MXU_DIM] = _rope(qr * r * gqr_ref[...], cos, sin).astype(BF16)

    kvb = _dot(_rms(ckv, gckv_ref[...]).astype(BF16), wkvb_ref[...])
    ss_pe = jnp.sum(kpe * kpe, axis=-1, keepdims=True)
    kpe_rot = _rope(kpe * gkr_ref[...], cos, sin)
    for hd in range(nh):
        kn = kvb[:, hd * MLA_NOPE:(hd + 1) * MLA_NOPE]
        ss = jnp.sum(kn * kn, axis=-1, keepdims=True) + ss_pe
        r = lax.rsqrt(ss * (1.0 / qk_dim) + EPS)
        km_ref[:, hd * MXU_DIM:hd * MXU_DIM + LANES] = (kn * r * gkn_ref[...]).astype(BF16)
        km_ref[:, hd * MXU_DIM + LANES:(hd + 1) * MXU_DIM] = (kpe_rot * r).astype(BF16)
    vm_ref[...] = kvb[:, nh * MLA_NOPE:].astype(BF16)

    hp = _dot(perm_ref[...], h).astype(BF16)
    for c, (g_ref, dst) in enumerate(((gqa_ref, qa_ref), (gka_ref, ka_ref))):
        src = _dot_nt(hp, wint_ref[c * w:(c + 1) * w, :].astype(BF16))
        sq = (src * src).astype(BF16)
        ms = jnp.concatenate([_dot(sq[:, c0:c0 + MXU_DIM], eh_ref[...]) for c0 in range(0, w, MXU_DIM)],
                             axis=1) * (1.0 / DIL_HEAD_DIM)
        dst[...] = (src * lax.rsqrt(ms + EPS) * g_ref[...]).reshape(dst.shape)
    va_ref[...] = _dot_nt(hp, wint_ref[2 * w:o, :].astype(BF16)).reshape(va_ref.shape)


def _proj_call(x2d, batch, seq, consts, cos, sin, *, n_mla_heads, dil_width):
    m, d = x2d.shape
    tm = ROW_TILE
    n_seq_blocks = seq // tm
    row = lambda width: pl.BlockSpec((tm, width), lambda i: (i, 0))
    pos = pl.BlockSpec((tm, LANES), lambda i: (i % n_seq_blocks, 0))
    mla_w = n_mla_heads * MXU_DIM
    dil_shape = jax.ShapeDtypeStruct((batch, DIL_RES, seq // DIL_RES, dil_width), F32)
    return pl.pallas_call(
        functools.partial(_proj_kernel, n_mla_heads=n_mla_heads, dil_width=dil_width),
        grid=(m // tm,),
        in_specs=[row(d)] + [_resident(c.shape) for c in consts] + [pos, pos],
        out_specs=[_residue_major_block(seq, dil_width)] * 3
        + [row(mla_w), row(mla_w), row(n_mla_heads * MLA_V_DIM)],
        out_shape=[dil_shape] * 3
        + [jax.ShapeDtypeStruct((m, mla_w), BF16)] * 2
        + [jax.ShapeDtypeStruct((m, n_mla_heads * MLA_V_DIM), BF16)],
        compiler_params=pltpu.CompilerParams(dimension_semantics=("parallel",),
                                             vmem_limit_bytes=VMEM_LIMIT_BYTES),
        name="proj",
    )(x2d, *consts, cos, sin)


def _t5_bucket(dist):
    max_exact = REL_BUCKETS // 2
    d = np.maximum(dist, 1).astype(np.float32)
    large = max_exact + (np.log(d / max_exact) / np.log(REL_MAX_DIST / max_exact)
                         * (REL_BUCKETS - max_exact)).astype(np.int32)
    large = np.minimum(large, REL_BUCKETS - 1)
    return np.where(dist < max_exact, dist, large).astype(np.int32)


def _band_tables():
    rho = np.arange(DIL_BAND)
    kap = np.arange(2 * DIL_BAND)
    buckets, prev = [], []
    for _, dil in DIL_BRANCHES:
        g = DIL_RES // dil
        run_q = DIL_BAND // g
        pos_q = g * (rho % run_q) + rho // run_q
        pos_k = g * (kap % (2 * run_q)) + kap // (2 * run_q) - DIL_BAND
        delta = pos_q[:, None] - pos_k[None, :]
        valid = (delta >= 0) & (delta <= DIL_BAND)
        buckets.append(np.where(valid, _t5_bucket(np.clip(delta, 0, None) * dil), -1))
        prev.append((pos_k < 0)[None, :])
    return np.stack(buckets).astype(np.int32), np.stack(prev).astype(np.int32)


def _dil_bias_kernel(rel_ref, bucket_ref, prev_ref, o_ref, *, present):
    nh = o_ref.shape[2]
    rb = 16
    scaled = {}
    for bi, buckets_here in enumerate(present):
        prev = prev_ref[bi] > 0
        for r0 in range(0, DIL_BAND, rb):
            bucket = bucket_ref[bi, r0:r0 + rb, :]
            accs = [jnp.full(bucket.shape, NEG, F32)] * nh
            for b in buckets_here:
                hit = bucket == b
                for hd in range(nh):
                    if (hd, b) not in scaled:
                        scaled[hd, b] = rel_ref[hd, b] * LOG2E
                accs = [jnp.where(hit, scaled[hd, b], acc) for hd, acc in enumerate(accs)]
            for hd in range(nh):
                o_ref[bi, 0, hd, r0:r0 + rb, :] = accs[hd]
                o_ref[bi, 1, hd, r0:r0 + rb, :] = jnp.where(prev, NEG, accs[hd])


def _dil_bias_call(rel_bias, buckets, prev):
    nbr = buckets.shape[0]
    nh = rel_bias.shape[0]
    tile = buckets.shape[1:]
    present = tuple(tuple(int(b) for b in np.unique(t) if b >= 0) for t in buckets)
    whole = lambda shape: pl.BlockSpec(shape, lambda i: (0,) * len(shape))
    out_shape = (nbr, 2, nh) + tile
    return pl.pallas_call(
        functools.partial(_dil_bias_kernel, present=present),
        grid=(1,),
        in_specs=[pl.BlockSpec(memory_space=pltpu.SMEM), whole(buckets.shape), whole(prev.shape)],
        out_specs=whole(out_shape),
        out_shape=jax.ShapeDtypeStruct(out_shape, F32),
        name="dil_bias",
    )(rel_bias, jnp.asarray(buckets), jnp.asarray(prev))


def _band_tile(q, k, v, bias, head_a):
    zero = jnp.zeros_like(q)
    q2 = jnp.concatenate([jnp.where(head_a, q, zero), jnp.where(head_a, zero, q)], axis=0).astype(BF16)
    s = _dot_nt(q2, k.astype(BF16)) + bias
    m = jnp.max(s, axis=-1, keepdims=True)
    p = jnp.exp2(s - m).astype(BF16)
    ones = jnp.ones(v.shape, BF16)
    out = _dot(p, jnp.concatenate([v.astype(BF16), ones], axis=1))
    h = DIL_BAND
    num = jnp.where(head_a, out[:h, :LANES], out[h:, :LANES])
    den = jnp.where(head_a, out[:h, LANES:], out[h:, LANES:])
    mx = jnp.where(head_a, m[:h], m[h:])
    return num, den, mx


def _dil_kernel(q_ref, kp_ref, kc_ref, vp_ref, vc_ref, bias_ref, o_ref, num_scr, den_scr, max_scr):
    first_block = (pl.program_id(2) == 0).astype(jnp.int32)
    head_a = lax.broadcasted_iota(jnp.int32, (1, LANES), 1) < DIL_HEAD_DIM

    def tile(j, bi, dil, rbase, n):
        g = DIL_RES // dil
        run = DIL_BAND // g
        base = j * DIL_BAND
        q_rows = pl.ds(base + n * run, run)
        qs, ks, vs = [], [], []
        for u in range(g):
            res = rbase + dil * u
            qs.append(q_ref[res, q_rows, :])
            if n == 0 and j == 0:
                ks += [kp_ref[res, pl.ds(DIL_BAND - run, run), :], kc_ref[res, pl.ds(0, run), :]]
                vs += [vp_ref[res, pl.ds(DIL_BAND - run, run), :], vc_ref[res, pl.ds(0, run), :]]
            else:
                k_rows = pl.ds(base + (n - 1) * run, 2 * run)
                ks.append(kc_ref[res, k_rows, :])
                vs.append(vc_ref[res, k_rows, :])
        cat = lambda parts: parts[0] if len(parts) == 1 else jnp.concatenate(parts, axis=0)
        variant = first_block if (n == 0 and j == 0) else 0
        num, den, mx = _band_tile(cat(qs), cat(ks), cat(vs), bias_ref[bi, variant], head_a)
        if g == 1:
            return num, den, mx
        s_rows = pl.ds(n * run, run)
        for u in range(g):
            res = rbase + dil * u
            part = slice(u * run, (u + 1) * run)
            num_scr[j, bi, res, s_rows, :] = num[part]
            den_scr[j, bi, res, s_rows, :] = den[part]
            max_scr[j, bi, res, s_rows, :] = mx[part]

    stored = len(DIL_BRANCHES) - 1
    for j in range(q_ref.shape[1] // DIL_BAND):
        for n in range(DIL_RES):
            tile(j, 0, 1, 0, n)
        for r4 in range(4):
            for n in range(4):
                tile(j, 1, 4, r4, n)
        for r in range(DIL_RES):
            num_r, den_r, max_r = tile(j, stored, DIL_RES, r, 0)
            ms = [max_scr[j, bi, r] for bi in range(stored)] + [max_r]
            nums = [num_scr[j, bi, r] for bi in range(stored)] + [num_r]
            dens = [den_scr[j, bi, r] for bi in range(stored)] + [den_r]
            m_all = functools.reduce(jnp.maximum, ms)
            es = [jnp.exp2(mb - m_all) for mb in ms]
            num = sum(e * x for e, x in zip(es, nums))
            den = sum(e * x for e, x in zip(es, dens))
            o_ref[pl.ds(j * DIL_CHUNK + r, DIL_BAND, stride=DIL_RES), :] = num / den


def _dil_call(qa, ka, va, bias):
    b, _, per, w = qa.shape
    pairs = w // LANES
    nbr = bias.shape[0]
    rows = DIL_STEP_CHUNKS * DIL_BAND
    cur = pl.BlockSpec((None, DIL_RES, rows, LANES), lambda bb, p, c: (bb, 0, c, p))
    prev = pl.BlockSpec((None, DIL_RES, DIL_BAND, LANES),
                        lambda bb, p, c: (bb, 0, jnp.maximum(DIL_STEP_CHUNKS * c - 1, 0), p))
    scr = pltpu.VMEM((DIL_STEP_CHUNKS, nbr - 1, DIL_RES, DIL_BAND, LANES), F32)
    return pl.pallas_call(
        _dil_kernel,
        grid=(b, pairs, per // rows),
        in_specs=[cur, prev, cur, prev, cur,
                  pl.BlockSpec((nbr, 2, None, 2 * DIL_BAND, 2 * DIL_BAND), lambda bb, p, c: (0, 0, p, 0, 0))],
        out_specs=pl.BlockSpec((None, DIL_STEP_CHUNKS * DIL_CHUNK, LANES), lambda bb, p, c: (bb, c, p)),
        out_shape=jax.ShapeDtypeStruct((b, per * DIL_RES, w), F32),
        scratch_shapes=[scr, scr, scr],
        compiler_params=pltpu.CompilerParams(dimension_semantics=("parallel", "parallel", "arbitrary"),
                                             vmem_limit_bytes=VMEM_LIMIT_BYTES),
        name="dil_attn",
    )(qa, ka, ka, va, va, bias)


def _mla_kernel(q_ref, k_ref, v_ref, o_ref, m_ref, acc_ref, s0_ref, s1_ref):
    tq = m_ref.shape[0]
    tk = MLA_TK
    nsub = tq // tk
    ones = jnp.ones((tk, LANES), BF16)
    s_bufs = (s0_ref, s1_ref)

    def q_tile(qi):
        q_lo = qi * tq
        m_ref[...] = jnp.full(m_ref.shape, NEG, F32)
        acc_ref[...] = jnp.zeros(acc_ref.shape, F32)

        def scores(kblk, row0=0):
            k0 = pl.multiple_of(kblk * tk, tk)
            return _dot_nt(q_ref[q_lo + row0:q_lo + tq, :], k_ref[pl.ds(k0, tk), :])

        def update(s, kblk, row0, diagonal):
            k0 = pl.multiple_of(kblk * tk, tk)
            rows = slice(row0, tq)
            if diagonal:
                row = lax.broadcasted_iota(jnp.int32, s.shape, 0)
                col = lax.broadcasted_iota(jnp.int32, s.shape, 1)
                s = jnp.where(col <= row, s, NEG)
            m_old = m_ref[rows, :]
            m_new = jnp.maximum(m_old, jnp.max(s, axis=-1, keepdims=True))
            m_ref[rows, :] = m_new
            p = jnp.concatenate([jnp.exp2(s[:, c * LANES:(c + 1) * LANES] - m_new) for c in range(tk // LANES)],
                                axis=1).astype(BF16)
            alpha = jnp.exp2(m_old - m_new)
            vext = jnp.concatenate([v_ref[pl.ds(k0, tk), :], ones], axis=1)
            acc_ref[rows, :] = acc_ref[rows, :] * jnp.concatenate([alpha, alpha], axis=1) + _dot(p, vext)

        s0_ref[...] = scores(0)

        def body(t, carry):
            for jj in range(nsub):
                s_bufs[(jj + 1) % 2][...] = scores(nsub * t + jj + 1)
                update(s_bufs[jj % 2][...], nsub * t + jj, 0, diagonal=False)
            return carry

        if qi:
            lax.fori_loop(0, qi, body, 0)
        s_cur = s0_ref[...]
        for d in range(nsub):
            s_next = scores(nsub * qi + d + 1, row0=(d + 1) * tk) if d + 1 < nsub else None
            update(s_cur, nsub * qi + d, d * tk, diagonal=True)
            s_cur = s_next
        acc = acc_ref[...]
        o_ref[q_lo:q_lo + tq, :] = acc[:, :MLA_V_DIM] / acc[:, MLA_V_DIM:]

    for qi in range(q_ref.shape[0] // tq):
        q_tile(qi)


def _mla_call(qm, km, vm, n_heads):
    b, s, _ = qm.shape
    tq = MLA_TQ
    assert tq % (2 * MLA_TK) == 0
    seq_block = lambda width: pl.BlockSpec((None, s, width), lambda bb, h: (bb, 0, h))
    return pl.pallas_call(
        _mla_kernel,
        grid=(b, n_heads),
        in_specs=[seq_block(MXU_DIM), seq_block(MXU_DIM), seq_block(MLA_V_DIM)],
        out_specs=seq_block(MLA_V_DIM),
        out_shape=jax.ShapeDtypeStruct((b, s, n_heads * MLA_V_DIM), F32),
        scratch_shapes=[pltpu.VMEM((tq, LANES), F32), pltpu.VMEM((tq, 2 * MLA_V_DIM), F32),
                        pltpu.VMEM((tq, MLA_TK), F32), pltpu.VMEM((tq, MLA_TK), F32)],
        compiler_params=pltpu.CompilerParams(dimension_semantics=("parallel", "parallel"),
                                             vmem_limit_bytes=VMEM_LIMIT_BYTES),
        name="mla_attn",
    )(qm, km, vm)


def _rope_tables(seq):
    inv_freq = ROPE_BASE ** (-np.arange(0, MLA_ROPE, 2, dtype=np.float64) / MLA_ROPE)
    ang = np.arange(seq, dtype=np.float64)[:, None] * inv_freq[None, :]
    cos, sin = np.cos(ang), np.sin(ang)
    cos_t = np.concatenate([cos, cos] * (LANES // MLA_ROPE), axis=1)
    sin_t = np.concatenate([-sin, sin] * (LANES // MLA_ROPE), axis=1)
    return jnp.asarray(cos_t, F32), jnp.asarray(sin_t, F32)


def _pad_lanes(v, width):
    return jnp.pad(v, ((0, 0), (0, width - v.shape[1])))


def kernel(x, ffn1_norm, ffn1_w_gate, ffn1_w_up, ffn1_w_down, mix_norm, w_in, dil_q_norm, dil_k_norm,
           rel_bias, mla_q_a_norm, mla_w_q_b, mla_kv_a_norm, mla_w_kv_b, mla_q_norm, mla_k_norm,
           out_norm_dil, out_norm_mla, w_out, ffn2_norm, ffn2_w_gate, ffn2_w_up, ffn2_w_down):
    batch, seq, d_model = x.shape
    depth = w_in.shape[0]
    dil_width = out_norm_dil.shape[1]
    dil_heads = dil_width // DIL_HEAD_DIM
    q_rank = mla_q_a_norm.shape[1]
    kv_rank = mla_kv_a_norm.shape[1]
    qk_dim = MLA_NOPE + MLA_ROPE
    n_mla = mla_w_q_b.shape[2] // qk_dim
    assert rel_bias.shape == (dil_heads, REL_BUCKETS)
    assert w_in.shape[2] == 3 * dil_width + q_rank + kv_rank + MLA_ROPE
    assert seq % (DIL_STEP_CHUNKS * DIL_CHUNK) == 0 and seq % MLA_TQ == 0 and seq % ROW_TILE == 0
    assert all(win // dil == DIL_BAND and DIL_RES % dil == 0 for win, dil in DIL_BRANCHES)
    assert DIL_BRANCHES[-1][1] == DIL_RES

    cos_t, sin_t = _rope_tables(seq)
    buckets, prev = _band_tables()
    dil_bias = _dil_bias_call(rel_bias, buckets, prev)
    dil_bias = dil_bias.reshape(dil_bias.shape[0], 2, dil_heads // 2, 2 * DIL_BAND, 2 * DIL_BAND)
    eh = jnp.asarray(np.kron(np.eye(MXU_DIM // DIL_HEAD_DIM), np.ones((DIL_HEAD_DIM, DIL_HEAD_DIM))), BF16)
    perm = jnp.asarray(_residue_major_perm(ROW_TILE), BF16)

    x2d = x.reshape(batch * seq, d_model)
    row = lambda v: v.reshape(1, -1)
    for l in range(depth):
        x2d = _ffn_call(x2d, row(ffn1_norm[l]), ffn1_w_gate[l], ffn1_w_up[l], ffn1_w_down[l])

        w_in_t = jnp.swapaxes(w_in[l], 0, 1)
        wkpe_t = jnp.pad(w_in_t[w_in.shape[2] - MLA_ROPE:], ((0, LANES - MLA_ROPE), (0, 0))).astype(BF16)
        wqb = mla_w_q_b[l].reshape(q_rank, n_mla, qk_dim)
        wqb = jnp.concatenate(
            [wqb[:, :, :MLA_NOPE].reshape(q_rank, n_mla * MLA_NOPE),
             jnp.pad(wqb[:, :, MLA_NOPE:], ((0, 0), (0, 0), (0, LANES - MLA_ROPE))).reshape(q_rank, n_mla * LANES)],
            axis=1).astype(BF16)
        wkvb = mla_w_kv_b[l].reshape(kv_rank, n_mla, MLA_NOPE + MLA_V_DIM)
        wkvb = jnp.concatenate([wkvb[:, :, :MLA_NOPE].reshape(kv_rank, n_mla * MLA_NOPE),
                                wkvb[:, :, MLA_NOPE:].reshape(kv_rank, n_mla * MLA_V_DIM)], axis=1).astype(BF16)
        gqa = row(jnp.tile(dil_q_norm[l], dil_heads)) * (LOG2E * DIL_HEAD_DIM ** -0.5)
        gka = row(jnp.tile(dil_k_norm[l], dil_heads))
        gq = row(mla_q_norm[l]) * (LOG2E * qk_dim ** -0.5)
        gk = row(mla_k_norm[l])
        gqn, gqr = gq[:, :MLA_NOPE], _pad_lanes(gq[:, MLA_NOPE:], LANES)
        gkn, gkr = gk[:, :MLA_NOPE], _pad_lanes(gk[:, MLA_NOPE:], LANES)

        consts = [row(mix_norm[l]), w_in_t, wkpe_t, eh, perm, gqa, gka, row(mla_q_a_norm[l]), wqb,
                  row(mla_kv_a_norm[l]), wkvb, gqn, gqr, gkn, gkr]
        qa, ka, va, qm, km, vm = _proj_call(x2d, batch, seq, consts, cos_t, sin_t,
                                            n_mla_heads=n_mla, dil_width=dil_width)

        shp = lambda a: a.reshape(batch, seq, a.shape[1])
        o_dil = _dil_call(qa, ka, va, dil_bias)
        o_mla = _mla_call(shp(qm), shp(km), shp(vm), n_mla)

        x2d = _out_ffn_call(x2d, o_dil.reshape(batch * seq, -1), o_mla.reshape(batch * seq, -1),
                            row(out_norm_dil[l]), row(out_norm_mla[l]), w_out[l].astype(BF16),
                            row(ffn2_norm[l]), ffn2_w_gate[l], ffn2_w_up[l], ffn2_w_down[l])
    return x2d.reshape(batch, seq, d_model)
```

```python
import functools

import numpy as np
import jax
import jax.numpy as jnp
from jax import lax
from jax.experimental import pallas as pl
from jax.experimental.pallas import tpu as pltpu

F32 = jnp.float32
BF16 = jnp.bfloat16

DIL_HEAD_DIM = 64
DIL_BRANCHES = ((128, 1), (512, 4), (2048, 16))
DIL_BAND = 128
DIL_RES = 16
MLA_NOPE = 128
MLA_ROPE = 64
MLA_V_DIM = 128
ROPE_BASE = 10000.0
REL_BUCKETS = 32
REL_MAX_DIST = 2048
FFN_RESID = 0.5
EPS = 1e-6

LANES = 128
MXU_DIM = 256
VMEM_LIMIT_BYTES = 56 * 1024 * 1024

NEG = float("-inf")
LOG2E = 1.4426950408889634

ROW_TILE = 512
FFN_TF = 256
DIL_CHUNK = DIL_BAND * DIL_RES
DIL_STEP_CHUNKS = 2
MLA_TQ = 2048
MLA_TK = 512


def _dot(a, b):
    return jnp.dot(a, b, preferred_element_type=F32)


def _dot_nt(a, b):
    return lax.dot_general(a, b, (((1,), (1,)), ((), ())), preferred_element_type=F32)


def _rms(x, g):
    return x * lax.rsqrt(jnp.mean(x * x, axis=-1, keepdims=True) + EPS) * g


def _residue_major_perm(rows):
    per = rows // DIL_RES
    dst = np.arange(rows)
    src = DIL_RES * (dst % per) + dst // per
    p = np.zeros((rows, rows), np.float32)
    p[dst, src] = 1.0
    return p


def _swiglu_residual(x, g_ref, wg_ref, wu_ref, wd_ref, act_ref, wait_chunk=None):
    h = _rms(x, g_ref[...]).astype(BF16)
    d_ff = wg_ref.shape[1]
    for c in range(d_ff // FFN_TF):
        sl = slice(c * FFN_TF, (c + 1) * FFN_TF)
        if wait_chunk is not None:
            wait_chunk(0, c)
            wait_chunk(1, c)
        gate = _dot(h, wg_ref[:, sl].astype(BF16))
        up = _dot(h, wu_ref[:, sl].astype(BF16))
        act_ref[:, sl] = (gate * (0.5 * jnp.tanh(0.5 * gate) + 0.5) * up).astype(BF16)
    if wait_chunk is not None:
        for c in range(d_ff // FFN_TF):
            wait_chunk(2, c)
    return x + FFN_RESID * _dot(act_ref[...], wd_ref[...].astype(BF16))


def _streamed_swiglu(make_x, o_ref, g_ref, w_hbm, w_vmem, act_ref, sem):
    wg_ref, wu_ref, wd_ref = w_vmem
    n_chunks = wg_ref.shape[1] // FFN_TF

    def chunk_copy(kind, c):
        sl = pl.ds(c * FFN_TF, FFN_TF)
        if kind == 2:
            return pltpu.make_async_copy(w_hbm[kind].at[sl, :], wd_ref.at[sl, :], sem.at[kind, c])
        return pltpu.make_async_copy(w_hbm[kind].at[:, sl], w_vmem[kind].at[:, sl], sem.at[kind, c])

    first = pl.program_id(0) == 0

    @pl.when(first)
    def _():
        for c in range(n_chunks):
            chunk_copy(0, c).start()
            chunk_copy(1, c).start()
        for c in range(n_chunks):
            chunk_copy(2, c).start()
        o_ref[...] = _swiglu_residual(make_x(), g_ref, wg_ref, wu_ref, wd_ref, act_ref,
                                      wait_chunk=lambda kind, c: chunk_copy(kind, c).wait())

    @pl.when(jnp.logical_not(first))
    def _():
        o_ref[...] = _swiglu_residual(make_x(), g_ref, wg_ref, wu_ref, wd_ref, act_ref)


def _ffn_kernel(x_ref, g_ref, wg_hbm, wu_hbm, wd_hbm, o_ref, wg_ref, wu_ref, wd_ref, act_ref, sem):
    _streamed_swiglu(lambda: x_ref[...], o_ref, g_ref, (wg_hbm, wu_hbm, wd_hbm), (wg_ref, wu_ref, wd_ref),
                     act_ref, sem)


def _out_ffn_kernel(x_ref, od_ref, om_ref, gd_ref, gm_ref, wo_ref, g_ref, wg_hbm, wu_hbm, wd_hbm,
                    o_ref, wg_ref, wu_ref, wd_ref, act_ref, sem):
    def mixer_residual():
        od = _rms(od_ref[...], gd_ref[...]).astype(BF16)
        om = _rms(om_ref[...], gm_ref[...]).astype(BF16)
        return x_ref[...] + _dot(jnp.concatenate([od, om], axis=1), wo_ref[...])

    _streamed_swiglu(mixer_residual, o_ref, g_ref, (wg_hbm, wu_hbm, wd_hbm), (wg_ref, wu_ref, wd_ref),
                     act_ref, sem)


def _ffn_weight_scratch(d, f):
    return [pltpu.VMEM((d, f), F32), pltpu.VMEM((d, f), F32), pltpu.VMEM((f, d), F32),
            pltpu.VMEM((ROW_TILE, f), BF16), pltpu.SemaphoreType.DMA((3, f // FFN_TF))]


def _resident(shape):
    return pl.BlockSpec(shape, lambda *_: (0,) * len(shape), pipeline_mode=pl.Buffered(1))


def _residue_major_block(seq, width):
    per = ROW_TILE // DIL_RES
    blocks_per_seq = seq // ROW_TILE
    return pl.BlockSpec((None, DIL_RES, per, width), lambda i: (i // blocks_per_seq, 0, i % blocks_per_seq, 0))


def _ffn_call(x2d, g, wg, wu, wd):
    m, d = x2d.shape
    f = wg.shape[1]
    row = pl.BlockSpec((ROW_TILE, d), lambda i: (i, 0))
    hbm = pl.BlockSpec(memory_space=pl.ANY)
    return pl.pallas_call(
        _ffn_kernel,
        grid=(m // ROW_TILE,),
        in_specs=[row, _resident((1, d)), hbm, hbm, hbm],
        out_specs=row,
        out_shape=jax.ShapeDtypeStruct((m, d), F32),
        scratch_shapes=_ffn_weight_scratch(d, f),
        compiler_params=pltpu.CompilerParams(dimension_semantics=("arbitrary",),
                                             vmem_limit_bytes=VMEM_LIMIT_BYTES),
        name="ffn",
    )(x2d, g, wg, wu, wd)


def _out_ffn_call(x2d, o_dil, o_mla, gd, gm, wo, g, wg, wu, wd):
    m, d = x2d.shape
    f = wg.shape[1]
    wdil, wmla = o_dil.shape[1], o_mla.shape[1]
    row = pl.BlockSpec((ROW_TILE, d), lambda i: (i, 0))
    hbm = pl.BlockSpec(memory_space=pl.ANY)
    return pl.pallas_call(
        _out_ffn_kernel,
        grid=(m // ROW_TILE,),
        in_specs=[row,
                  pl.BlockSpec((ROW_TILE, wdil), lambda i: (i, 0)),
                  pl.BlockSpec((ROW_TILE, wmla), lambda i: (i, 0)),
                  _resident((1, wdil)), _resident((1, wmla)),
                  _resident((wdil + wmla, d)),
                  _resident((1, d)), hbm, hbm, hbm],
        out_specs=row,
        out_shape=jax.ShapeDtypeStruct((m, d), F32),
        scratch_shapes=_ffn_weight_scratch(d, f),
        compiler_params=pltpu.CompilerParams(dimension_semantics=("arbitrary",),
                                             vmem_limit_bytes=VMEM_LIMIT_BYTES),
        name="out_ffn",
    )(x2d, o_dil, o_mla, gd, gm, wo, g, wg, wu, wd)


def _rope(x, cos, sin_signed):
    lane = lax.broadcasted_iota(jnp.int32, x.shape, 1)
    first_half = (lane % MLA_ROPE) < (MLA_ROPE // 2)
    partner = jnp.where(first_half,
                        pltpu.roll(x, LANES - MLA_ROPE // 2, 1),
                        pltpu.roll(x, MLA_ROPE // 2, 1))
    return x * cos + partner * sin_signed


def _proj_kernel(x_ref, gmix_ref, wint_ref, wkpet_ref, eh_ref, perm_ref, gqa_ref, gka_ref,
                 gcq_ref, wqb_ref, gckv_ref, wkvb_ref,
                 gqn_ref, gqr_ref, gkn_ref, gkr_ref, cos_ref, sin_ref,
                 qa_ref, ka_ref, va_ref, qm_ref, km_ref, vm_ref, *, n_mla_heads, dil_width):
    h = _rms(x_ref[...], gmix_ref[...]).astype(BF16)
    w = dil_width
    o = 3 * w
    q_rank = gcq_ref.shape[1]
    kv_rank = gckv_ref.shape[1]
    qk_dim = MLA_NOPE + MLA_ROPE
    nh = n_mla_heads
    cos = cos_ref[...]
    sin = sin_ref[...]

    w_mla = jnp.concatenate([wint_ref[o:o + q_rank + kv_rank, :].astype(BF16), wkpet_ref[...]], axis=0)
    pm = _dot_nt(h, w_mla)
    cq = pm[:, :q_rank]
    ckv = pm[:, q_rank:q_rank + kv_rank]
    kpe = pm[:, q_rank + kv_rank:]

    qb = _dot(_rms(cq, gcq_ref[...]).astype(BF16), wqb_ref[...])
    for hd in range(nh):
        qn = qb[:, hd * MLA_NOPE:(hd + 1) * MLA_NOPE]
        qr = qb[:, nh * MLA_NOPE + hd * LANES:nh * MLA_NOPE + (hd + 1) * LANES]
        ss = jnp.sum(qn * qn, axis=-1, keepdims=True) + jnp.sum(qr * qr, axis=-1, keepdims=True)
        r = lax.rsqrt(ss * (1.0 / qk_dim) + EPS)
        qm_ref[:, hd * MXU_DIM:hd * MXU_DIM + LANES] = (qn * r * gqn_ref[...]).astype(BF16)
        qm_ref[:, hd * MXU_DIM + LANES:(hd + 1) * MXU_DIM] = _rope(qr * r * gqr_ref[...], cos, sin).astype(BF16)

    kvb = _dot(_rms(ckv, gckv_ref[...]).astype(BF16), wkvb_ref[...])
    ss_pe = jnp.sum(kpe * kpe, axis=-1, keepdims=True)
    kpe_rot = _rope(kpe * gkr_ref[...], cos, sin)
    for hd in range(nh):
        kn = kvb[:, hd * MLA_NOPE:(hd + 1) * MLA_NOPE]
        ss = jnp.sum(kn * kn, axis=-1, keepdims=True) + ss_pe
        r = lax.rsqrt(ss * (1.0 / qk_dim) + EPS)
        km_ref[:, hd * MXU_DIM:hd * MXU_DIM + LANES] = (kn * r * gkn_ref[...]).astype(BF16)
        km_ref[:, hd * MXU_DIM + LANES:(hd + 1) * MXU_DIM] = (kpe_rot * r).astype(BF16)
    vm_ref[...] = kvb[:, nh * MLA_NOPE:].astype(BF16)

    hp = _dot(perm_ref[...], h).astype(BF16)
    for c, (g_ref, dst) in enumerate(((gqa_ref, qa_ref), (gka_ref, ka_ref))):
        src = _dot_nt(hp, wint_ref[c * w:(c + 1) * w, :].astype(BF16))
        sq = (src * src).astype(BF16)
        ms = jnp.concatenate([_dot(sq[:, c0:c0 + MXU_DIM], eh_ref[...]) for c0 in range(0, w, MXU_DIM)],
                             axis=1) * (1.0 / DIL_HEAD_DIM)
        dst[...] = (src * lax.rsqrt(ms + EPS) * g_ref[...]).reshape(dst.shape)
    va_ref[...] = _dot_nt(hp, wint_ref[2 * w:o, :].astype(BF16)).reshape(va_ref.shape)


def _proj_call(x2d, batch, seq, consts, cos, sin, *, n_mla_heads, dil_width):
    m, d = x2d.shape
    tm = ROW_TILE
    n_seq_blocks = seq // tm
    row = lambda width: pl.BlockSpec((tm, width), lambda i: (i, 0))
    pos = pl.BlockSpec((tm, LANES), lambda i: (i % n_seq_blocks, 0))
    mla_w = n_mla_heads * MXU_DIM
    dil_shape = jax.ShapeDtypeStruct((batch, DIL_RES, seq // DIL_RES, dil_width), F32)
    return pl.pallas_call(
        functools.partial(_proj_kernel, n_mla_heads=n_mla_heads, dil_width=dil_width),
        grid=(m // tm,),
        in_specs=[row(d)] + [_resident(c.shape) for c in consts] + [pos, pos],
        out_specs=[_residue_major_block(seq, dil_width)] * 3
        + [row(mla_w), row(mla_w), row(n_mla_heads * MLA_V_DIM)],
        out_shape=[dil_shape] * 3
        + [jax.ShapeDtypeStruct((m, mla_w), BF16)] * 2
        + [jax.ShapeDtypeStruct((m, n_mla_heads * MLA_V_DIM), BF16)],
        compiler_params=pltpu.CompilerParams(dimension_semantics=("parallel",),
                                             vmem_limit_bytes=VMEM_LIMIT_BYTES),
        name="proj",
    )(x2d, *consts, cos, sin)


def _t5_bucket(dist):
    max_exact = REL_BUCKETS // 2
    d = np.maximum(dist, 1).astype(np.float32)
    large = max_exact + (np.log(d / max_exact) / np.log(REL_MAX_DIST / max_exact)
                         * (REL_BUCKETS - max_exact)).astype(np.int32)
    large = np.minimum(large, REL_BUCKETS - 1)
    return np.where(dist < max_exact, dist, large).astype(np.int32)


def _band_tables():
    rho = np.arange(DIL_BAND)
    kap = np.arange(2 * DIL_BAND)
    buckets, prev = [], []
    for _, dil in DIL_BRANCHES:
        g = DIL_RES // dil
        run_q = DIL_BAND // g
        pos_q = g * (rho % run_q) + rho // run_q
        pos_k = g * (kap % (2 * run_q)) + kap // (2 * run_q) - DIL_BAND
        delta = pos_q[:, None] - pos_k[None, :]
        valid = (delta >= 0) & (delta <= DIL_BAND)
        buckets.append(np.where(valid, _t5_bucket(np.clip(delta, 0, None) * dil), -1))
        prev.append((pos_k < 0)[None, :])
    return np.stack(buckets).astype(np.int32), np.stack(prev).astype(np.int32)


def _dil_bias_kernel(rel_ref, bucket_ref, prev_ref, o_ref, *, present):
    nh = o_ref.shape[2]
    rb = 16
    scaled = {}
    for bi, buckets_here in enumerate(present):
        prev = prev_ref[bi] > 0
        for r0 in range(0, DIL_BAND, rb):
            bucket = bucket_ref[bi, r0:r0 + rb, :]
            accs = [jnp.full(bucket.shape, NEG, F32)] * nh
            for b in buckets_here:
                hit = bucket == b
                for hd in range(nh):
                    if (hd, b) not in scaled:
                        scaled[hd, b] = rel_ref[hd, b] * LOG2E
                accs = [jnp.where(hit, scaled[hd, b], acc) for hd, acc in enumerate(accs)]
            for hd in range(nh):
                o_ref[bi, 0, hd, r0:r0 + rb, :] = accs[hd]
                o_ref[bi, 1, hd, r0:r0 + rb, :] = jnp.where(prev, NEG, accs[hd])


def _dil_bias_call(rel_bias, buckets, prev):
    nbr = buckets.shape[0]
    nh = rel_bias.shape[0]
    tile = buckets.shape[1:]
    present = tuple(tuple(int(b) for b in np.unique(t) if b >= 0) for t in buckets)
    whole = lambda shape: pl.BlockSpec(shape, lambda i: (0,) * len(shape))
    out_shape = (nbr, 2, nh) + tile
    return pl.pallas_call(
        functools.partial(_dil_bias_kernel, present=present),
        grid=(1,),
        in_specs=[pl.BlockSpec(memory_space=pltpu.SMEM), whole(buckets.shape), whole(prev.shape)],
        out_specs=whole(out_shape),
        out_shape=jax.ShapeDtypeStruct(out_shape, F32),
        name="dil_bias",
    )(rel_bias, jnp.asarray(buckets), jnp.asarray(prev))


def _band_tile(q, k, v, bias, head_a):
    zero = jnp.zeros_like(q)
    q2 = jnp.concatenate([jnp.where(head_a, q, zero), jnp.where(head_a, zero, q)], axis=0).astype(BF16)
    s = _dot_nt(q2, k.astype(BF16)) + bias
    m = jnp.max(s, axis=-1, keepdims=True)
    p = jnp.exp2(s - m).astype(BF16)
    ones = jnp.ones(v.shape, BF16)
    out = _dot(p, jnp.concatenate([v.astype(BF16), ones], axis=1))
    h = DIL_BAND
    num = jnp.where(head_a, out[:h, :LANES], out[h:, :LANES])
    den = jnp.where(head_a, out[:h, LANES:], out[h:, LANES:])
    mx = jnp.where(head_a, m[:h], m[h:])
    return num, den, mx


def _dil_kernel(q_ref, kp_ref, kc_ref, vp_ref, vc_ref, bias_ref, o_ref, num_scr, den_scr, max_scr):
    first_block = (pl.program_id(2) == 0).astype(jnp.int32)
    head_a = lax.broadcasted_iota(jnp.int32, (1, LANES), 1) < DIL_HEAD_DIM

    def tile(j, bi, dil, rbase, n):
        g = DIL_RES // dil
        run = DIL_BAND // g
        base = j * DIL_BAND
        q_rows = pl.ds(base + n * run, run)
        qs, ks, vs = [], [], []
        for u in range(g):
            res = rbase + dil * u
            qs.append(q_ref[res, q_rows, :])
            if n == 0 and j == 0:
                ks += [kp_ref[res, pl.ds(DIL_BAND - run, run), :], kc_ref[res, pl.ds(0, run), :]]
                vs += [vp_ref[res, pl.ds(DIL_BAND - run, run), :], vc_ref[res, pl.ds(0, run), :]]
            else:
                k_rows = pl.ds(base + (n - 1) * run, 2 * run)
                ks.append(kc_ref[res, k_rows, :])
                vs.append(vc_ref[res, k_rows, :])
        cat = lambda parts: parts[0] if len(parts) == 1 else jnp.concatenate(parts, axis=0)
        variant = first_block if (n == 0 and j == 0) else 0
        num, den, mx = _band_tile(cat(qs), cat(ks), cat(vs), bias_ref[bi, variant], head_a)
        if g == 1:
            return num, den, mx
        s_rows = pl.ds(n * run, run)
        for u in range(g):
            res = rbase + dil * u
            part = slice(u * run, (u + 1) * run)
            num_scr[j, bi, res, s_rows, :] = num[part]
            den_scr[j, bi, res, s_rows, :] = den[part]
            max_scr[j, bi, res, s_rows, :] = mx[part]

    stored = len(DIL_BRANCHES) - 1
    for j in range(q_ref.shape[1] // DIL_BAND):
        for n in range(DIL_RES):
            tile(j, 0, 1, 0, n)
        for r4 in range(4):
            for n in range(4):
                tile(j, 1, 4, r4, n)
        for r in range(DIL_RES):
            num_r, den_r, max_r = tile(j, stored, DIL_RES, r, 0)
            ms = [max_scr[j, bi, r] for bi in range(stored)] + [max_r]
            nums = [num_scr[j, bi, r] for bi in range(stored)] + [num_r]
            dens = [den_scr[j, bi, r] for bi in range(stored)] + [den_r]
            m_all = functools.reduce(jnp.maximum, ms)
            es = [jnp.exp2(mb - m_all) for mb in ms]
            num = sum(e * x for e, x in zip(es, nums))
            den = sum(e * x for e, x in zip(es, dens))
            o_ref[pl.ds(j * DIL_CHUNK + r, DIL_BAND, stride=DIL_RES), :] = num / den


def _dil_call(qa, ka, va, bias):
    b, _, per, w = qa.shape
    pairs = w // LANES
    nbr = bias.shape[0]
    rows = DIL_STEP_CHUNKS * DIL_BAND
    cur = pl.BlockSpec((None, DIL_RES, rows, LANES), lambda bb, p, c: (bb, 0, c, p))
    prev = pl.BlockSpec((None, DIL_RES, DIL_BAND, LANES),
                        lambda bb, p, c: (bb, 0, jnp.maximum(DIL_STEP_CHUNKS * c - 1, 0), p))
    scr = pltpu.VMEM((DIL_STEP_CHUNKS, nbr - 1, DIL_RES, DIL_BAND, LANES), F32)
    return pl.pallas_call(
        _dil_kernel,
        grid=(b, pairs, per // rows),
        in_specs=[cur, prev, cur, prev, cur,
                  pl.BlockSpec((nbr, 2, None, 2 * DIL_BAND, 2 * DIL_BAND), lambda bb, p, c: (0, 0, p, 0, 0))],
        out_specs=pl.BlockSpec((None, DIL_STEP_CHUNKS * DIL_CHUNK, LANES), lambda bb, p, c: (bb, c, p)),
        out_shape=jax.ShapeDtypeStruct((b, per * DIL_RES, w), F32),
        scratch_shapes=[scr, scr, scr],
        compiler_params=pltpu.CompilerParams(dimension_semantics=("parallel", "parallel", "arbitrary"),
                                             vmem_limit_bytes=VMEM_LIMIT_BYTES),
        name="dil_attn",
    )(qa, ka, ka, va, va, bias)


def _mla_kernel(q_ref, k_ref, v_ref, o_ref, m_ref, acc_ref, s0_ref, s1_ref):
    tq = m_ref.shape[0]
    tk = MLA_TK
    nsub = tq // tk
    ones = jnp.ones((tk, LANES), BF16)
    s_bufs = (s0_ref, s1_ref)

    def q_tile(qi):
        q_lo = qi * tq
        m_ref[...] = jnp.full(m_ref.shape, NEG, F32)
        acc_ref[...] = jnp.zeros(acc_ref.shape, F32)

        def scores(kblk, row0=0):
            k0 = pl.multiple_of(kblk * tk, tk)
            return _dot_nt(q_ref[q_lo + row0:q_lo + tq, :], k_ref[pl.ds(k0, tk), :])

        def update(s, kblk, row0, diagonal):
            k0 = pl.multiple_of(kblk * tk, tk)
            rows = slice(row0, tq)
            if diagonal:
                row = lax.broadcasted_iota(jnp.int32, s.shape, 0)
                col = lax.broadcasted_iota(jnp.int32, s.shape, 1)
                s = jnp.where(col <= row, s, NEG)
            m_old = m_ref[rows, :]
            m_new = jnp.maximum(m_old, jnp.max(s, axis=-1, keepdims=True))
            m_ref[rows, :] = m_new
            p = jnp.concatenate([jnp.exp2(s[:, c * LANES:(c + 1) * LANES] - m_new) for c in range(tk // LANES)],
                                axis=1).astype(BF16)
            alpha = jnp.exp2(m_old - m_new)
            vext = jnp.concatenate([v_ref[pl.ds(k0, tk), :], ones], axis=1)
            acc_ref[rows, :] = acc_ref[rows, :] * jnp.concatenate([alpha, alpha], axis=1) + _dot(p, vext)

        s0_ref[...] = scores(0)

        def body(t, carry):
            for jj in range(nsub):
                s_bufs[(jj + 1) % 2][...] = scores(nsub * t + jj + 1)
                update(s_bufs[jj % 2][...], nsub * t + jj, 0, diagonal=False)
            return carry

        if qi:
            lax.fori_loop(0, qi, body, 0)
        s_cur = s0_ref[...]
        for d in range(nsub):
            s_next = scores(nsub * qi + d + 1, row0=(d + 1) * tk) if d + 1 < nsub else None
            update(s_cur, nsub * qi + d, d * tk, diagonal=True)
            s_cur = s_next
        acc = acc_ref[...]
        o_ref[q_lo:q_lo + tq, :] = acc[:, :MLA_V_DIM] / acc[:, MLA_V_DIM:]

    for qi in range(q_ref.shape[0] // tq):
        q_tile(qi)


def _mla_call(qm, km, vm, n_heads):
    b, s, _ = qm.shape
    tq = MLA_TQ
    assert tq % (2 * MLA_TK) == 0
    seq_block = lambda width: pl.BlockSpec((None, s, width), lambda bb, h: (bb, 0, h))
    return pl.pallas_call(
        _mla_kernel,
        grid=(b, n_heads),
        in_specs=[seq_block(MXU_DIM), seq_block(MXU_DIM), seq_block(MLA_V_DIM)],
        out_specs=seq_block(MLA_V_DIM),
        out_shape=jax.ShapeDtypeStruct((b, s, n_heads * MLA_V_DIM), F32),
        scratch_shapes=[pltpu.VMEM((tq, LANES), F32), pltpu.VMEM((tq, 2 * MLA_V_DIM), F32),
                        pltpu.VMEM((tq, MLA_TK), F32), pltpu.VMEM((tq, MLA_TK), F32)],
        compiler_params=pltpu.CompilerParams(dimension_semantics=("parallel", "parallel"),
                                             vmem_limit_bytes=VMEM_LIMIT_BYTES),
        name="mla_attn",
    )(qm, km, vm)


def _rope_tables(seq):
    inv_freq = ROPE_BASE ** (-np.arange(0, MLA_ROPE, 2, dtype=np.float64) / MLA_ROPE)
    ang = np.arange(seq, dtype=np.float64)[:, None] * inv_freq[None, :]
    cos, sin = np.cos(ang), np.sin(ang)
    cos_t = np.concatenate([cos, cos] * (LANES // MLA_ROPE), axis=1)
    sin_t = np.concatenate([-sin, sin] * (LANES // MLA_ROPE), axis=1)
    return jnp.asarray(cos_t, F32), jnp.asarray(sin_t, F32)


def _pad_lanes(v, width):
    return jnp.pad(v, ((0, 0), (0, width - v.shape[1])))


def kernel(x, ffn1_norm, ffn1_w_gate, ffn1_w_up, ffn1_w_down, mix_norm, w_in, dil_q_norm, dil_k_norm,
           rel_bias, mla_q_a_norm, mla_w_q_b, mla_kv_a_norm, mla_w_kv_b, mla_q_norm, mla_k_norm,
           out_norm_dil, out_norm_mla, w_out, ffn2_norm, ffn2_w_gate, ffn2_w_up, ffn2_w_down):
    batch, seq, d_model = x.shape
    depth = w_in.shape[0]
    dil_width = out_norm_dil.shape[1]
    dil_heads = dil_width // DIL_HEAD_DIM
    q_rank = mla_q_a_norm.shape[1]
    kv_rank = mla_kv_a_norm.shape[1]
    qk_dim = MLA_NOPE + MLA_ROPE
    n_mla = mla_w_q_b.shape[2] // qk_dim
    assert rel_bias.shape == (dil_heads, REL_BUCKETS)
    assert w_in.shape[2] == 3 * dil_width + q_rank + kv_rank + MLA_ROPE
    assert seq % (DIL_STEP_CHUNKS * DIL_CHUNK) == 0 and seq % MLA_TQ == 0 and seq % ROW_TILE == 0
    assert all(win // dil == DIL_BAND and DIL_RES % dil == 0 for win, dil in DIL_BRANCHES)
    assert DIL_BRANCHES[-1][1] == DIL_RES

    cos_t, sin_t = _rope_tables(seq)
    buckets, prev = _band_tables()
    dil_bias = _dil_bias_call(rel_bias, buckets, prev)
    dil_bias = dil_bias.reshape(dil_bias.shape[0], 2, dil_heads // 2, 2 * DIL_BAND, 2 * DIL_BAND)
    eh = jnp.asarray(np.kron(np.eye(MXU_DIM // DIL_HEAD_DIM), np.ones((DIL_HEAD_DIM, DIL_HEAD_DIM))), BF16)
    perm = jnp.asarray(_residue_major_perm(ROW_TILE), BF16)

    x2d = x.reshape(batch * seq, d_model)
    row = lambda v: v.reshape(1, -1)
    for l in range(depth):
        x2d = _ffn_call(x2d, row(ffn1_norm[l]), ffn1_w_gate[l], ffn1_w_up[l], ffn1_w_down[l])

        w_in_t = jnp.swapaxes(w_in[l], 0, 1)
        wkpe_t = jnp.pad(w_in_t[w_in.shape[2] - MLA_ROPE:], ((0, LANES - MLA_ROPE), (0, 0))).astype(BF16)
        wqb = mla_w_q_b[l].reshape(q_rank, n_mla, qk_dim)
        wqb = jnp.concatenate(
            [wqb[:, :, :MLA_NOPE].reshape(q_rank, n_mla * MLA_NOPE),
             jnp.pad(wqb[:, :, MLA_NOPE:], ((0, 0), (0, 0), (0, LANES - MLA_ROPE))).reshape(q_rank, n_mla * LANES)],
            axis=1).astype(BF16)
        wkvb = mla_w_kv_b[l].reshape(kv_rank, n_mla, MLA_NOPE + MLA_V_DIM)
        wkvb = jnp.concatenate([wkvb[:, :, :MLA_NOPE].reshape(kv_rank, n_mla * MLA_NOPE),
                                wkvb[:, :, MLA_NOPE:].reshape(kv_rank, n_mla * MLA_V_DIM)], axis=1).astype(BF16)
        gqa = row(jnp.tile(dil_q_norm[l], dil_heads)) * (LOG2E * DIL_HEAD_DIM ** -0.5)
        gka = row(jnp.tile(dil_k_norm[l], dil_heads))
        gq = row(mla_q_norm[l]) * (LOG2E * qk_dim ** -0.5)
        gk = row(mla_k_norm[l])
        gqn, gqr = gq[:, :MLA_NOPE], _pad_lanes(gq[:, MLA_NOPE:], LANES)
        gkn, gkr = gk[:, :MLA_NOPE], _pad_lanes(gk[:, MLA_NOPE:], LANES)

        consts = [row(mix_norm[l]), w_in_t, wkpe_t, eh, perm, gqa, gka, row(mla_q_a_norm[l]), wqb,
                  row(mla_kv_a_norm[l]), wkvb, gqn, gqr, gkn, gkr]
        qa, ka, va, qm, km, vm = _proj_call(x2d, batch, seq, consts, cos_t, sin_t,
                                            n_mla_heads=n_mla, dil_width=dil_width)

        shp = lambda a: a.reshape(batch, seq, a.shape[1])
        o_dil = _dil_call(qa, ka, va, dil_bias)
        o_mla = _mla_call(shp(qm), shp(km), shp(vm), n_mla)

        x2d = _out_ffn_call(x2d, o_dil.reshape(batch * seq, -1), o_mla.reshape(batch * seq, -1),
                            row(out_norm_dil[l]), row(out_norm_mla[l]), w_out[l].astype(BF16),
                            row(ffn2_norm[l]), ffn2_w_gate[l], ffn2_w_up[l], ffn2_w_down[l])
    return x2d.reshape(batch, seq, d_model)
```

```python
import functools

import numpy as np
import jax
import jax.numpy as jnp
from jax import lax
from jax.experimental import pallas as pl
from jax.experimental.pallas import tpu as pltpu

F32 = jnp.float32
BF16 = jnp.bfloat16

DIL_HEAD_DIM = 64
DIL_BRANCHES = ((128, 1), (512, 4), (2048, 16))
DIL_BAND = 128
DIL_RES = 16
MLA_NOPE = 128
MLA_ROPE = 64
MLA_V_DIM = 128
ROPE_BASE = 10000.0
REL_BUCKETS = 32
REL_MAX_DIST = 2048
FFN_RESID = 0.5
EPS = 1e-6

LANES = 128
MXU_DIM = 256
VMEM_LIMIT_BYTES = 56 * 1024 * 1024
VMEM_LIMIT_FFN_BYTES = 60 * 1024 * 1024

NEG = float("-inf")
LOG2E = 1.4426950408889634

ROW_TILE = 512
FFN_STEP_ROWS = 2 * ROW_TILE
FFN_TF = 256
DIL_CHUNK = DIL_BAND * DIL_RES
DIL_STEP_CHUNKS = 2
MLA_TQ = 2048
MLA_TK = 512


def _dot(a, b):
    return jnp.dot(a, b, preferred_element_type=F32)


def _dot_nt(a, b):
    return lax.dot_general(a, b, (((1,), (1,)), ((), ())), preferred_element_type=F32)


def _rms(x, g):
    return x * lax.rsqrt(jnp.mean(x * x, axis=-1, keepdims=True) + EPS) * g


def _residue_major_perm(rows):
    per = rows // DIL_RES
    dst = np.arange(rows)
    src = DIL_RES * (dst % per) + dst // per
    p = np.zeros((rows, rows), np.float32)
    p[dst, src] = 1.0
    return p


def _swiglu_residual(x, g_ref, wg_ref, wu_ref, wd_ref, act_ref, wait_chunk=None):
    h = _rms(x, g_ref[...]).astype(BF16)
    d_ff = wg_ref.shape[1]
    for c in range(d_ff // FFN_TF):
        sl = slice(c * FFN_TF, (c + 1) * FFN_TF)
        if wait_chunk is not None:
            wait_chunk(0, c)
            wait_chunk(1, c)
        gate = _dot(h, wg_ref[:, sl].astype(BF16))
        up = _dot(h, wu_ref[:, sl].astype(BF16))
        act_ref[:, sl] = (gate * (0.5 * jnp.tanh(0.5 * gate) + 0.5) * up).astype(BF16)
    if wait_chunk is not None:
        for c in range(d_ff // FFN_TF):
            wait_chunk(2, c)
    return x + FFN_RESID * _dot(act_ref[...], wd_ref[...].astype(BF16))


def _streamed_swiglu(make_x, o_ref, g_ref, w_hbm, w_vmem, act_ref, sem):
    wg_ref, wu_ref, wd_ref = w_vmem
    n_chunks = wg_ref.shape[1] // FFN_TF

    def chunk_copy(kind, c):
        sl = pl.ds(c * FFN_TF, FFN_TF)
        if kind == 2:
            return pltpu.make_async_copy(w_hbm[kind].at[sl, :], wd_ref.at[sl, :], sem.at[kind, c])
        return pltpu.make_async_copy(w_hbm[kind].at[:, sl], w_vmem[kind].at[:, sl], sem.at[kind, c])

    first = pl.program_id(0) == 0

    @pl.when(first)
    def _():
        for c in range(n_chunks):
            chunk_copy(0, c).start()
            chunk_copy(1, c).start()
        for c in range(n_chunks):
            chunk_copy(2, c).start()
        o_ref[...] = _swiglu_residual(make_x(), g_ref, wg_ref, wu_ref, wd_ref, act_ref,
                                      wait_chunk=lambda kind, c: chunk_copy(kind, c).wait())

    @pl.when(jnp.logical_not(first))
    def _():
        o_ref[...] = _swiglu_residual(make_x(), g_ref, wg_ref, wu_ref, wd_ref, act_ref)


def _ffn_kernel(x_ref, g_ref, wg_ref, wu_ref, wd_ref, o_ref, act_ref):
    for r0 in range(0, x_ref.shape[0], ROW_TILE):
        rows = slice(r0, r0 + ROW_TILE)
        o_ref[rows, :] = _swiglu_residual(x_ref[rows, :], g_ref, wg_ref, wu_ref, wd_ref, act_ref)


def _out_ffn_kernel(x_ref, od_ref, om_ref, gd_ref, gm_ref, wo_ref, g_ref, wg_hbm, wu_hbm, wd_hbm,
                    o_ref, wg_ref, wu_ref, wd_ref, act_ref, sem):
    def mixer_residual():
        od = _rms(od_ref[...], gd_ref[...]).astype(BF16)
        om = _rms(om_ref[...], gm_ref[...]).astype(BF16)
        return x_ref[...] + _dot(jnp.concatenate([od, om], axis=1), wo_ref[...])

    _streamed_swiglu(mixer_residual, o_ref, g_ref, (wg_hbm, wu_hbm, wd_hbm), (wg_ref, wu_ref, wd_ref),
                     act_ref, sem)


def _ffn_weight_scratch(d, f):
    return [pltpu.VMEM((d, f), F32), pltpu.VMEM((d, f), F32), pltpu.VMEM((f, d), F32),
            pltpu.VMEM((ROW_TILE, f), BF16), pltpu.SemaphoreType.DMA((3, f // FFN_TF))]


def _resident(shape):
    return pl.BlockSpec(shape, lambda *_: (0,) * len(shape), pipeline_mode=pl.Buffered(1))


def _residue_major_block(seq, width):
    per = ROW_TILE // DIL_RES
    blocks_per_seq = seq // ROW_TILE
    return pl.BlockSpec((None, DIL_RES, per, width), lambda i: (i // blocks_per_seq, 0, i % blocks_per_seq, 0))


def _ffn_call(x2d, g, wg, wu, wd):
    m, d = x2d.shape
    f = wg.shape[1]
    row = pl.BlockSpec((FFN_STEP_ROWS, d), lambda i: (i, 0))
    return pl.pallas_call(
        _ffn_kernel,
        grid=(m // FFN_STEP_ROWS,),
        in_specs=[row, _resident((1, d)), _resident((d, f)), _resident((d, f)), _resident((f, d))],
        out_specs=row,
        out_shape=jax.ShapeDtypeStruct((m, d), F32),
        scratch_shapes=[pltpu.VMEM((ROW_TILE, f), BF16)],
        compiler_params=pltpu.CompilerParams(dimension_semantics=("parallel",),
                                             vmem_limit_bytes=VMEM_LIMIT_FFN_BYTES),
        name="ffn",
    )(x2d, g, wg, wu, wd)


def _out_ffn_call(x2d, o_dil, o_mla, gd, gm, wo, g, wg, wu, wd):
    m, d = x2d.shape
    f = wg.shape[1]
    wdil, wmla = o_dil.shape[1], o_mla.shape[1]
    row = pl.BlockSpec((ROW_TILE, d), lambda i: (i, 0))
    hbm = pl.BlockSpec(memory_space=pl.ANY)
    return pl.pallas_call(
        _out_ffn_kernel,
        grid=(m // ROW_TILE,),
        in_specs=[row,
                  pl.BlockSpec((ROW_TILE, wdil), lambda i: (i, 0)),
                  pl.BlockSpec((ROW_TILE, wmla), lambda i: (i, 0)),
                  _resident((1, wdil)), _resident((1, wmla)),
                  _resident((wdil + wmla, d)),
                  _resident((1, d)), hbm, hbm, hbm],
        out_specs=row,
        out_shape=jax.ShapeDtypeStruct((m, d), F32),
        scratch_shapes=_ffn_weight_scratch(d, f),
        compiler_params=pltpu.CompilerParams(dimension_semantics=("arbitrary",),
                                             vmem_limit_bytes=VMEM_LIMIT_BYTES),
        name="out_ffn",
    )(x2d, o_dil, o_mla, gd, gm, wo, g, wg, wu, wd)


def _rope(x, cos, sin_signed):
    lane = lax.broadcasted_iota(jnp.int32, x.shape, 1)
    first_half = (lane % MLA_ROPE) < (MLA_ROPE // 2)
    partner = jnp.where(first_half,
                        pltpu.roll(x, LANES - MLA_ROPE // 2, 1),
                        pltpu.roll(x, MLA_ROPE // 2, 1))
    return x * cos + partner * sin_signed


def _proj_kernel(x_ref, gmix_ref, wint_ref, wkpet_ref, eh_ref, perm_ref, gqa_ref, gka_ref,
                 gcq_ref, wqb_ref, gckv_ref, wkvb_ref,
                 gqn_ref, gqr_ref, gkn_ref, gkr_ref, cos_ref, sin_ref,
                 qa_ref, ka_ref, va_ref, qm_ref, km_ref, vm_ref, *, n_mla_heads, dil_width):
    h = _rms(x_ref[...], gmix_ref[...]).astype(BF16)
    w = dil_width
    o = 3 * w
    q_rank = gcq_ref.shape[1]
    kv_rank = gckv_ref.shape[1]
    qk_dim = MLA_NOPE + MLA_ROPE
    nh = n_mla_heads
    cos = cos_ref[...]
    sin = sin_ref[...]

    w_mla = jnp.concatenate([wint_ref[o:o + q_rank + kv_rank, :].astype(BF16), wkpet_ref[...]], axis=0)
    pm = _dot_nt(h, w_mla)
    cq = pm[:, :q_rank]
    ckv = pm[:, q_rank:q_rank + kv_rank]
    kpe = pm[:, q_rank + kv_rank:]

    qb = _dot(_rms(cq, gcq_ref[...]).astype(BF16), wqb_ref[...])
    for hd in range(nh):
        qn = qb[:, hd * MLA_NOPE:(hd + 1) * MLA_NOPE]
        qr = qb[:, nh * MLA_NOPE + hd * LANES:nh * MLA_NOPE + (hd + 1) * LANES]
        ss = jnp.sum(qn * qn, axis=-1, keepdims=True) + jnp.sum(qr * qr, axis=-1, keepdims=True)
        r = lax.rsqrt(ss * (1.0 / qk_dim) + EPS)
        qm_ref[:, hd * MXU_DIM:hd * MXU_DIM + LANES] = (qn * r * gqn_ref[...]).astype(BF16)
        qm_ref[:, hd * MXU_DIM + LANES:(hd + 1) * MXU_DIM] = _rope(qr * r * gqr_ref[...], cos, sin).astype(BF16)

    kvb = _dot(_rms(ckv, gckv_ref[...]).astype(BF16), wkvb_ref[...])
    ss_pe = jnp.sum(kpe * kpe, axis=-1, keepdims=True)
    kpe_rot = _rope(kpe * gkr_ref[...], cos, sin)
    for hd in range(nh):
        kn = kvb[:, hd * MLA_NOPE:(hd + 1) * MLA_NOPE]
        ss = jnp.sum(kn * kn, axis=-1, keepdims=True) + ss_pe
        r = lax.rsqrt(ss * (1.0 / qk_dim) + EPS)
        km_ref[:, hd * MXU_DIM:hd * MXU_DIM + LANES] = (kn * r * gkn_ref[...]).astype(BF16)
        km_ref[:, hd * MXU_DIM + LANES:(hd + 1) * MXU_DIM] = (kpe_rot * r).astype(BF16)
    vm_ref[...] = kvb[:, nh * MLA_NOPE:].astype(BF16)

    hp = _dot(perm_ref[...], h).astype(BF16)
    for c, (g_ref, dst) in enumerate(((gqa_ref, qa_ref), (gka_ref, ka_ref))):
        src = _dot_nt(hp, wint_ref[c * w:(c + 1) * w, :].astype(BF16))
        sq = (src * src).astype(BF16)
        ms = jnp.concatenate([_dot(sq[:, c0:c0 + MXU_DIM], eh_ref[...]) for c0 in range(0, w, MXU_DIM)],
                             axis=1) * (1.0 / DIL_HEAD_DIM)
        dst[...] = (src * lax.rsqrt(ms + EPS) * g_ref[...]).reshape(dst.shape)
    va_ref[...] = _dot_nt(hp, wint_ref[2 * w:o, :].astype(BF16)).reshape(va_ref.shape)


def _proj_call(x2d, batch, seq, consts, cos, sin, *, n_mla_heads, dil_width):
    m, d = x2d.shape
    tm = ROW_TILE
    n_seq_blocks = seq // tm
    row = lambda width: pl.BlockSpec((tm, width), lambda i: (i, 0))
    pos = pl.BlockSpec((tm, LANES), lambda i: (i % n_seq_blocks, 0))
    mla_w = n_mla_heads * MXU_DIM
    dil_shape = jax.ShapeDtypeStruct((batch, DIL_RES, seq // DIL_RES, dil_width), F32)
    return pl.pallas_call(
        functools.partial(_proj_kernel, n_mla_heads=n_mla_heads, dil_width=dil_width),
        grid=(m // tm,),
        in_specs=[row(d)] + [_resident(c.shape) for c in consts] + [pos, pos],
        out_specs=[_residue_major_block(seq, dil_width)] * 3
        + [row(mla_w), row(mla_w), row(n_mla_heads * MLA_V_DIM)],
        out_shape=[dil_shape] * 3
        + [jax.ShapeDtypeStruct((m, mla_w), BF16)] * 2
        + [jax.ShapeDtypeStruct((m, n_mla_heads * MLA_V_DIM), BF16)],
        compiler_params=pltpu.CompilerParams(dimension_semantics=("parallel",),
                                             vmem_limit_bytes=VMEM_LIMIT_BYTES),
        name="proj",
    )(x2d, *consts, cos, sin)


def _t5_bucket(dist):
    max_exact = REL_BUCKETS // 2
    d = np.maximum(dist, 1).astype(np.float32)
    large = max_exact + (np.log(d / max_exact) / np.log(REL_MAX_DIST / max_exact)
                         * (REL_BUCKETS - max_exact)).astype(np.int32)
    large = np.minimum(large, REL_BUCKETS - 1)
    return np.where(dist < max_exact, dist, large).astype(np.int32)


def _band_tables():
    rho = np.arange(DIL_BAND)
    kap = np.arange(2 * DIL_BAND)
    buckets, prev = [], []
    for _, dil in DIL_BRANCHES:
        g = DIL_RES // dil
        run_q = DIL_BAND // g
        pos_q = g * (rho % run_q) + rho // run_q
        pos_k = g * (kap % (2 * run_q)) + kap // (2 * run_q) - DIL_BAND
        delta = pos_q[:, None] - pos_k[None, :]
        valid = (delta >= 0) & (delta <= DIL_BAND)
        buckets.append(np.where(valid, _t5_bucket(np.clip(delta, 0, None) * dil), -1))
        prev.append((pos_k < 0)[None, :])
    return np.stack(buckets).astype(np.int32), np.stack(prev).astype(np.int32)


def _dil_bias_kernel(rel_ref, bucket_ref, prev_ref, o_ref, *, present):
    nh = o_ref.shape[2]
    rb = 16
    scaled = {}
    for bi, buckets_here in enumerate(present):
        prev = prev_ref[bi] > 0
        for r0 in range(0, DIL_BAND, rb):
            bucket = bucket_ref[bi, r0:r0 + rb, :]
            accs = [jnp.full(bucket.shape, NEG, F32)] * nh
            for b in buckets_here:
                hit = bucket == b
                for hd in range(nh):
                    if (hd, b) not in scaled:
                        scaled[hd, b] = rel_ref[hd, b] * LOG2E
                accs = [jnp.where(hit, scaled[hd, b], acc) for hd, acc in enumerate(accs)]
            for hd in range(nh):
                o_ref[bi, 0, hd, r0:r0 + rb, :] = accs[hd]
                o_ref[bi, 1, hd, r0:r0 + rb, :] = jnp.where(prev, NEG, accs[hd])


def _dil_bias_call(rel_bias, buckets, prev):
    nbr = buckets.shape[0]
    nh = rel_bias.shape[0]
    tile = buckets.shape[1:]
    present = tuple(tuple(int(b) for b in np.unique(t) if b >= 0) for t in buckets)
    whole = lambda shape: pl.BlockSpec(shape, lambda i: (0,) * len(shape))
    out_shape = (nbr, 2, nh) + tile
    return pl.pallas_call(
        functools.partial(_dil_bias_kernel, present=present),
        grid=(1,),
        in_specs=[pl.BlockSpec(memory_space=pltpu.SMEM), whole(buckets.shape), whole(prev.shape)],
        out_specs=whole(out_shape),
        out_shape=jax.ShapeDtypeStruct(out_shape, F32),
        name="dil_bias",
    )(rel_bias, jnp.asarray(buckets), jnp.asarray(prev))


def _band_tile(q, k, v, bias, head_a):
    zero = jnp.zeros_like(q)
    q2 = jnp.concatenate([jnp.where(head_a, q, zero), jnp.where(head_a, zero, q)], axis=0).astype(BF16)
    s = _dot_nt(q2, k.astype(BF16)) + bias
    m = jnp.max(s, axis=-1, keepdims=True)
    p = jnp.exp2(s - m).astype(BF16)
    ones = jnp.ones(v.shape, BF16)
    out = _dot(p, jnp.concatenate([v.astype(BF16), ones], axis=1))
    h = DIL_BAND
    num = jnp.where(head_a, out[:h, :LANES], out[h:, :LANES])
    den = jnp.where(head_a, out[:h, LANES:], out[h:, LANES:])
    mx = jnp.where(head_a, m[:h], m[h:])
    return num, den, mx


def _dil_kernel(q_ref, kp_ref, kc_ref, vp_ref, vc_ref, bias_ref, o_ref, num_scr, den_scr, max_scr):
    first_block = (pl.program_id(2) == 0).astype(jnp.int32)
    head_a = lax.broadcasted_iota(jnp.int32, (1, LANES), 1) < DIL_HEAD_DIM

    def tile(j, bi, dil, rbase, n):
        g = DIL_RES // dil
        run = DIL_BAND // g
        base = j * DIL_BAND
        q_rows = pl.ds(base + n * run, run)
        qs, ks, vs = [], [], []
        for u in range(g):
            res = rbase + dil * u
            qs.append(q_ref[res, q_rows, :])
            if n == 0 and j == 0:
                ks += [kp_ref[res, pl.ds(DIL_BAND - run, run), :], kc_ref[res, pl.ds(0, run), :]]
                vs += [vp_ref[res, pl.ds(DIL_BAND - run, run), :], vc_ref[res, pl.ds(0, run), :]]
            else:
                k_rows = pl.ds(base + (n - 1) * run, 2 * run)
                ks.append(kc_ref[res, k_rows, :])
                vs.append(vc_ref[res, k_rows, :])
        cat = lambda parts: parts[0] if len(parts) == 1 else jnp.concatenate(parts, axis=0)
        variant = first_block if (n == 0 and j == 0) else 0
        num, den, mx = _band_tile(cat(qs), cat(ks), cat(vs), bias_ref[bi, variant], head_a)
        if g == 1:
            return num, den, mx
        s_rows = pl.ds(n * run, run)
        for u in range(g):
            res = rbase + dil * u
            part = slice(u * run, (u + 1) * run)
            num_scr[j, bi, res, s_rows, :] = num[part]
            den_scr[j, bi, res, s_rows, :] = den[part]
            max_scr[j, bi, res, s_rows, :] = mx[part]

    stored = len(DIL_BRANCHES) - 1
    for j in range(q_ref.shape[1] // DIL_BAND):
        for n in range(DIL_RES):
            tile(j, 0, 1, 0, n)
        for r4 in range(4):
            for n in range(4):
                tile(j, 1, 4, r4, n)
        for r in range(DIL_RES):
            num_r, den_r, max_r = tile(j, stored, DIL_RES, r, 0)
            ms = [max_scr[j, bi, r] for bi in range(stored)] + [max_r]
            nums = [num_scr[j, bi, r] for bi in range(stored)] + [num_r]
            dens = [den_scr[j, bi, r] for bi in range(stored)] + [den_r]
            m_all = functools.reduce(jnp.maximum, ms)
            es = [jnp.exp2(mb - m_all) for mb in ms]
            num = sum(e * x for e, x in zip(es, nums))
            den = sum(e * x for e, x in zip(es, dens))
            o_ref[pl.ds(j * DIL_CHUNK + r, DIL_BAND, stride=DIL_RES), :] = num / den


def _dil_call(qa, ka, va, bias):
    b, _, per, w = qa.shape
    pairs = w // LANES
    nbr = bias.shape[0]
    rows = DIL_STEP_CHUNKS * DIL_BAND
    cur = pl.BlockSpec((None, DIL_RES, rows, LANES), lambda bb, p, c: (bb, 0, c, p))
    prev = pl.BlockSpec((None, DIL_RES, DIL_BAND, LANES),
                        lambda bb, p, c: (bb, 0, jnp.maximum(DIL_STEP_CHUNKS * c - 1, 0), p))
    scr = pltpu.VMEM((DIL_STEP_CHUNKS, nbr - 1, DIL_RES, DIL_BAND, LANES), F32)
    return pl.pallas_call(
        _dil_kernel,
        grid=(b, pairs, per // rows),
        in_specs=[cur, prev, cur, prev, cur,
                  pl.BlockSpec((nbr, 2, None, 2 * DIL_BAND, 2 * DIL_BAND), lambda bb, p, c: (0, 0, p, 0, 0))],
        out_specs=pl.BlockSpec((None, DIL_STEP_CHUNKS * DIL_CHUNK, LANES), lambda bb, p, c: (bb, c, p)),
        out_shape=jax.ShapeDtypeStruct((b, per * DIL_RES, w), F32),
        scratch_shapes=[scr, scr, scr],
        compiler_params=pltpu.CompilerParams(dimension_semantics=("parallel", "parallel", "arbitrary"),
                                             vmem_limit_bytes=VMEM_LIMIT_BYTES),
        name="dil_attn",
    )(qa, ka, ka, va, va, bias)


def _mla_kernel(q_ref, k_ref, v_ref, o_ref, m_ref, acc_ref, s0_ref, s1_ref):
    tq = m_ref.shape[0]
    tk = MLA_TK
    nsub = tq // tk
    ones = jnp.ones((tk, LANES), BF16)
    s_bufs = (s0_ref, s1_ref)

    def q_tile(qi):
        q_lo = qi * tq
        m_ref[...] = jnp.full(m_ref.shape, NEG, F32)
        acc_ref[...] = jnp.zeros(acc_ref.shape, F32)

        def scores(kblk, row0=0):
            k0 = pl.multiple_of(kblk * tk, tk)
            return _dot_nt(q_ref[q_lo + row0:q_lo + tq, :], k_ref[pl.ds(k0, tk), :])

        def update(s, kblk, row0, diagonal):
            k0 = pl.multiple_of(kblk * tk, tk)
            rows = slice(row0, tq)
            if diagonal:
                row = lax.broadcasted_iota(jnp.int32, s.shape, 0)
                col = lax.broadcasted_iota(jnp.int32, s.shape, 1)
                s = jnp.where(col <= row, s, NEG)
            m_old = m_ref[rows, :]
            m_new = jnp.maximum(m_old, jnp.max(s, axis=-1, keepdims=True))
            m_ref[rows, :] = m_new
            p = jnp.concatenate([jnp.exp2(s[:, c * LANES:(c + 1) * LANES] - m_new) for c in range(tk // LANES)],
                                axis=1).astype(BF16)
            alpha = jnp.exp2(m_old - m_new)
            vext = jnp.concatenate([v_ref[pl.ds(k0, tk), :], ones], axis=1)
            acc_ref[rows, :] = acc_ref[rows, :] * jnp.concatenate([alpha, alpha], axis=1) + _dot(p, vext)

        s0_ref[...] = scores(0)

        def body(t, carry):
            for jj in range(nsub):
                s_bufs[(jj + 1) % 2][...] = scores(nsub * t + jj + 1)
                update(s_bufs[jj % 2][...], nsub * t + jj, 0, diagonal=False)
            return carry

        if qi:
            lax.fori_loop(0, qi, body, 0)
        s_cur = s0_ref[...]
        for d in range(nsub):
            s_next = scores(nsub * qi + d + 1, row0=(d + 1) * tk) if d + 1 < nsub else None
            update(s_cur, nsub * qi + d, d * tk, diagonal=True)
            s_cur = s_next
        acc = acc_ref[...]
        o_ref[q_lo:q_lo + tq, :] = acc[:, :MLA_V_DIM] / acc[:, MLA_V_DIM:]

    for qi in range(q_ref.shape[0] // tq):
        q_tile(qi)


def _mla_call(qm, km, vm, n_heads):
    b, s, _ = qm.shape
    tq = MLA_TQ
    assert tq % (2 * MLA_TK) == 0
    seq_block = lambda width: pl.BlockSpec((None, s, width), lambda bb, h: (bb, 0, h))
    return pl.pallas_call(
        _mla_kernel,
        grid=(b, n_heads),
        in_specs=[seq_block(MXU_DIM), seq_block(MXU_DIM), seq_block(MLA_V_DIM)],
        out_specs=seq_block(MLA_V_DIM),
        out_shape=jax.ShapeDtypeStruct((b, s, n_heads * MLA_V_DIM), F32),
        scratch_shapes=[pltpu.VMEM((tq, LANES), F32), pltpu.VMEM((tq, 2 * MLA_V_DIM), F32),
                        pltpu.VMEM((tq, MLA_TK), F32), pltpu.VMEM((tq, MLA_TK), F32)],
        compiler_params=pltpu.CompilerParams(dimension_semantics=("parallel", "parallel"),
                                             vmem_limit_bytes=VMEM_LIMIT_BYTES),
        name="mla_attn",
    )(qm, km, vm)


def _rope_tables(seq):
    inv_freq = ROPE_BASE ** (-np.arange(0, MLA_ROPE, 2, dtype=np.float64) / MLA_ROPE)
    ang = np.arange(seq, dtype=np.float64)[:, None] * inv_freq[None, :]
    cos, sin = np.cos(ang), np.sin(ang)
    cos_t = np.concatenate([cos, cos] * (LANES // MLA_ROPE), axis=1)
    sin_t = np.concatenate([-sin, sin] * (LANES // MLA_ROPE), axis=1)
    return jnp.asarray(cos_t, F32), jnp.asarray(sin_t, F32)


def _pad_lanes(v, width):
    return jnp.pad(v, ((0, 0), (0, width - v.shape[1])))


def kernel(x, ffn1_norm, ffn1_w_gate, ffn1_w_up, ffn1_w_down, mix_norm, w_in, dil_q_norm, dil_k_norm,
           rel_bias, mla_q_a_norm, mla_w_q_b, mla_kv_a_norm, mla_w_kv_b, mla_q_norm, mla_k_norm,
           out_norm_dil, out_norm_mla, w_out, ffn2_norm, ffn2_w_gate, ffn2_w_up, ffn2_w_down):
    batch, seq, d_model = x.shape
    depth = w_in.shape[0]
    dil_width = out_norm_dil.shape[1]
    dil_heads = dil_width // DIL_HEAD_DIM
    q_rank = mla_q_a_norm.shape[1]
    kv_rank = mla_kv_a_norm.shape[1]
    qk_dim = MLA_NOPE + MLA_ROPE
    n_mla = mla_w_q_b.shape[2] // qk_dim
    assert rel_bias.shape == (dil_heads, REL_BUCKETS)
    assert w_in.shape[2] == 3 * dil_width + q_rank + kv_rank + MLA_ROPE
    assert seq % (DIL_STEP_CHUNKS * DIL_CHUNK) == 0 and seq % MLA_TQ == 0 and seq % ROW_TILE == 0
    assert (batch * seq) % FFN_STEP_ROWS == 0
    assert all(win // dil == DIL_BAND and DIL_RES % dil == 0 for win, dil in DIL_BRANCHES)
    assert DIL_BRANCHES[-1][1] == DIL_RES

    cos_t, sin_t = _rope_tables(seq)
    buckets, prev = _band_tables()
    dil_bias = _dil_bias_call(rel_bias, buckets, prev)
    dil_bias = dil_bias.reshape(dil_bias.shape[0], 2, dil_heads // 2, 2 * DIL_BAND, 2 * DIL_BAND)
    eh = jnp.asarray(np.kron(np.eye(MXU_DIM // DIL_HEAD_DIM), np.ones((DIL_HEAD_DIM, DIL_HEAD_DIM))), BF16)
    perm = jnp.asarray(_residue_major_perm(ROW_TILE), BF16)

    x2d = x.reshape(batch * seq, d_model)
    row = lambda v: v.reshape(1, -1)
    for l in range(depth):
        x2d = _ffn_call(x2d, row(ffn1_norm[l]), ffn1_w_gate[l], ffn1_w_up[l], ffn1_w_down[l])

        w_in_t = jnp.swapaxes(w_in[l], 0, 1)
        wkpe_t = jnp.pad(w_in_t[w_in.shape[2] - MLA_ROPE:], ((0, LANES - MLA_ROPE), (0, 0))).astype(BF16)
        wqb = mla_w_q_b[l].reshape(q_rank, n_mla, qk_dim)
        wqb = jnp.concatenate(
            [wqb[:, :, :MLA_NOPE].reshape(q_rank, n_mla * MLA_NOPE),
             jnp.pad(wqb[:, :, MLA_NOPE:], ((0, 0), (0, 0), (0, LANES - MLA_ROPE))).reshape(q_rank, n_mla * LANES)],
            axis=1).astype(BF16)
        wkvb = mla_w_kv_b[l].reshape(kv_rank, n_mla, MLA_NOPE + MLA_V_DIM)
        wkvb = jnp.concatenate([wkvb[:, :, :MLA_NOPE].reshape(kv_rank, n_mla * MLA_NOPE),
                                wkvb[:, :, MLA_NOPE:].reshape(kv_rank, n_mla * MLA_V_DIM)], axis=1).astype(BF16)
        gqa = row(jnp.tile(dil_q_norm[l], dil_heads)) * (LOG2E * DIL_HEAD_DIM ** -0.5)
        gka = row(jnp.tile(dil_k_norm[l], dil_heads))
        gq = row(mla_q_norm[l]) * (LOG2E * qk_dim ** -0.5)
        gk = row(mla_k_norm[l])
        gqn, gqr = gq[:, :MLA_NOPE], _pad_lanes(gq[:, MLA_NOPE:], LANES)
        gkn, gkr = gk[:, :MLA_NOPE], _pad_lanes(gk[:, MLA_NOPE:], LANES)

        consts = [row(mix_norm[l]), w_in_t, wkpe_t, eh, perm, gqa, gka, row(mla_q_a_norm[l]), wqb,
                  row(mla_kv_a_norm[l]), wkvb, gqn, gqr, gkn, gkr]
        qa, ka, va, qm, km, vm = _proj_call(x2d, batch, seq, consts, cos_t, sin_t,
                                            n_mla_heads=n_mla, dil_width=dil_width)

        shp = lambda a: a.reshape(batch, seq, a.shape[1])
        o_dil = _dil_call(qa, ka, va, dil_bias)
        o_mla = _mla_call(shp(qm), shp(km), shp(vm), n_mla)

        x2d = _out_ffn_call(x2d, o_dil.reshape(batch * seq, -1), o_mla.reshape(batch * seq, -1),
                            row(out_norm_dil[l]), row(out_norm_mla[l]), w_out[l].astype(BF16),
                            row(ffn2_norm[l]), ffn2_w_gate[l], ffn2_w_up[l], ffn2_w_down[l])
    return x2d.reshape(batch, seq, d_model)
```

```python
import functools

import numpy as np
import jax
import jax.numpy as jnp
from jax import lax
from jax.experimental import pallas as pl
from jax.experimental.pallas import tpu as pltpu

F32 = jnp.float32
BF16 = jnp.bfloat16

DIL_HEAD_DIM = 64
DIL_BRANCHES = ((128, 1), (512, 4), (2048, 16))
DIL_BAND = 128
DIL_RES = 16
MLA_NOPE = 128
MLA_ROPE = 64
MLA_V_DIM = 128
ROPE_BASE = 10000.0
REL_BUCKETS = 32
REL_MAX_DIST = 2048
FFN_RESID = 0.5
EPS = 1e-6

LANES = 128
MXU_DIM = 256
VMEM_LIMIT_BYTES = 56 * 1024 * 1024
VMEM_LIMIT_FFN_BYTES = 60 * 1024 * 1024

NEG = float("-inf")
LOG2E = 1.4426950408889634

ROW_TILE = 512
FFN_STEP_ROWS = 2 * ROW_TILE
FFN_TF = 256
DIL_CHUNK = DIL_BAND * DIL_RES
DIL_STEP_CHUNKS = 2
MLA_TQ = 2048
MLA_TK = 512


def _dot(a, b):
    return jnp.dot(a, b, preferred_element_type=F32)


def _dot_nt(a, b):
    return lax.dot_general(a, b, (((1,), (1,)), ((), ())), preferred_element_type=F32)


def _rms(x, g):
    return x * lax.rsqrt(jnp.mean(x * x, axis=-1, keepdims=True) + EPS) * g


def _residue_major_perm(rows):
    per = rows // DIL_RES
    dst = np.arange(rows)
    src = DIL_RES * (dst % per) + dst // per
    p = np.zeros((rows, rows), np.float32)
    p[dst, src] = 1.0
    return p


def _swiglu_residual(x, g_ref, wg_ref, wu_ref, wd_ref, act_ref, before_down=None):
    h = _rms(x, g_ref[...]).astype(BF16)
    d_ff = wg_ref.shape[1]
    for c in range(d_ff // FFN_TF):
        sl = slice(c * FFN_TF, (c + 1) * FFN_TF)
        gate = _dot(h, wg_ref[:, sl].astype(BF16))
        up = _dot(h, wu_ref[:, sl].astype(BF16))
        act_ref[:, sl] = (gate * (0.5 * jnp.tanh(0.5 * gate) + 0.5) * up).astype(BF16)
    if before_down is not None:
        before_down()
    return x + FFN_RESID * _dot(act_ref[...], wd_ref[...].astype(BF16))


def _ffn_kernel(x_ref, g_ref, wg_ref, wu_ref, wd_ref, o_ref, act_ref):
    for r0 in range(0, x_ref.shape[0], ROW_TILE):
        rows = slice(r0, r0 + ROW_TILE)
        o_ref[rows, :] = _swiglu_residual(x_ref[rows, :], g_ref, wg_ref, wu_ref, wd_ref, act_ref)


def _out_ffn_kernel(x_ref, od_ref, om_ref, gd_ref, gm_ref, wo_ref, g_ref, wg_hbm, wu_hbm, wd_hbm,
                    o_ref, wg_ref, wu_ref, wd_ref, act_ref, sem):
    gate_copy, up_copy, down_copy = [
        pltpu.make_async_copy(src, dst, sem.at[i])
        for i, (src, dst) in enumerate(((wg_hbm, wg_ref), (wu_hbm, wu_ref), (wd_hbm, wd_ref)))]
    first = pl.program_id(0) == 0

    @pl.when(first)
    def _():
        gate_copy.start()
        up_copy.start()
        down_copy.start()

    od = _rms(od_ref[...], gd_ref[...]).astype(BF16)
    om = _rms(om_ref[...], gm_ref[...]).astype(BF16)
    x2 = x_ref[...] + _dot(jnp.concatenate([od, om], axis=1), wo_ref[...])

    @pl.when(first)
    def _():
        gate_copy.wait()
        up_copy.wait()

    def before_down():
        @pl.when(first)
        def _():
            down_copy.wait()

    o_ref[...] = _swiglu_residual(x2, g_ref, wg_ref, wu_ref, wd_ref, act_ref, before_down=before_down)


def _resident(shape):
    return pl.BlockSpec(shape, lambda *_: (0,) * len(shape), pipeline_mode=pl.Buffered(1))


def _residue_major_block(seq, width):
    per = ROW_TILE // DIL_RES
    blocks_per_seq = seq // ROW_TILE
    return pl.BlockSpec((None, DIL_RES, per, width), lambda i: (i // blocks_per_seq, 0, i % blocks_per_seq, 0))


def _ffn_call(x2d, g, wg, wu, wd):
    m, d = x2d.shape
    f = wg.shape[1]
    row = pl.BlockSpec((FFN_STEP_ROWS, d), lambda i: (i, 0))
    return pl.pallas_call(
        _ffn_kernel,
        grid=(m // FFN_STEP_ROWS,),
        in_specs=[row, _resident((1, d)), _resident((d, f)), _resident((d, f)), _resident((f, d))],
        out_specs=row,
        out_shape=jax.ShapeDtypeStruct((m, d), F32),
        scratch_shapes=[pltpu.VMEM((ROW_TILE, f), BF16)],
        compiler_params=pltpu.CompilerParams(dimension_semantics=("parallel",),
                                             vmem_limit_bytes=VMEM_LIMIT_FFN_BYTES),
        name="ffn",
    )(x2d, g, wg, wu, wd)


def _out_ffn_call(x2d, o_dil, o_mla, gd, gm, wo, g, wg, wu, wd):
    m, d = x2d.shape
    f = wg.shape[1]
    wdil, wmla = o_dil.shape[1], o_mla.shape[1]
    row = pl.BlockSpec((ROW_TILE, d), lambda i: (i, 0))
    hbm = pl.BlockSpec(memory_space=pl.ANY)
    return pl.pallas_call(
        _out_ffn_kernel,
        grid=(m // ROW_TILE,),
        in_specs=[row,
                  pl.BlockSpec((ROW_TILE, wdil), lambda i: (i, 0)),
                  pl.BlockSpec((ROW_TILE, wmla), lambda i: (i, 0)),
                  _resident((1, wdil)), _resident((1, wmla)),
                  _resident((wdil + wmla, d)),
                  _resident((1, d)), hbm, hbm, hbm],
        out_specs=row,
        out_shape=jax.ShapeDtypeStruct((m, d), F32),
        scratch_shapes=[pltpu.VMEM((d, f), F32), pltpu.VMEM((d, f), F32), pltpu.VMEM((f, d), F32),
                        pltpu.VMEM((ROW_TILE, f), BF16), pltpu.SemaphoreType.DMA((3,))],
        compiler_params=pltpu.CompilerParams(dimension_semantics=("arbitrary",),
                                             vmem_limit_bytes=VMEM_LIMIT_BYTES),
        name="out_ffn",
    )(x2d, o_dil, o_mla, gd, gm, wo, g, wg, wu, wd)


def _rope(x, cos, sin_signed):
    lane = lax.broadcasted_iota(jnp.int32, x.shape, 1)
    first_half = (lane % MLA_ROPE) < (MLA_ROPE // 2)
    partner = jnp.where(first_half,
                        pltpu.roll(x, LANES - MLA_ROPE // 2, 1),
                        pltpu.roll(x, MLA_ROPE // 2, 1))
    return x * cos + partner * sin_signed


def _proj_kernel(x_ref, gmix_ref, wint_ref, wkpet_ref, eh_ref, perm_ref, gqa_ref, gka_ref,
                 gcq_ref, wqb_ref, gckv_ref, wkvb_ref,
                 gqn_ref, gqr_ref, gkn_ref, gkr_ref, cos_ref, sin_ref,
                 qa_ref, ka_ref, va_ref, qm_ref, km_ref, vm_ref, *, n_mla_heads, dil_width):
    h = _rms(x_ref[...], gmix_ref[...]).astype(BF16)
    w = dil_width
    o = 3 * w
    q_rank = gcq_ref.shape[1]
    kv_rank = gckv_ref.shape[1]
    qk_dim = MLA_NOPE + MLA_ROPE
    nh = n_mla_heads
    cos = cos_ref[...]
    sin = sin_ref[...]

    w_mla = jnp.concatenate([wint_ref[o:o + q_rank + kv_rank, :].astype(BF16), wkpet_ref[...]], axis=0)
    pm = _dot_nt(h, w_mla)
    cq = pm[:, :q_rank]
    ckv = pm[:, q_rank:q_rank + kv_rank]
    kpe = pm[:, q_rank + kv_rank:]

    qb = _dot(_rms(cq, gcq_ref[...]).astype(BF16), wqb_ref[...])
    for hd in range(nh):
        qn = qb[:, hd * MLA_NOPE:(hd + 1) * MLA_NOPE]
        qr = qb[:, nh * MLA_NOPE + hd * LANES:nh * MLA_NOPE + (hd + 1) * LANES]
        ss = jnp.sum(qn * qn, axis=-1, keepdims=True) + jnp.sum(qr * qr, axis=-1, keepdims=True)
        r = lax.rsqrt(ss * (1.0 / qk_dim) + EPS)
        qm_ref[:, hd * MXU_DIM:hd * MXU_DIM + LANES] = (qn * r * gqn_ref[...]).astype(BF16)
        qm_ref[:, hd * MXU_DIM + LANES:(hd + 1) * MXU_DIM] = _rope(qr * r * gqr_ref[...], cos, sin).astype(BF16)

    kvb = _dot(_rms(ckv, gckv_ref[...]).astype(BF16), wkvb_ref[...])
    ss_pe = jnp.sum(kpe * kpe, axis=-1, keepdims=True)
    kpe_rot = _rope(kpe * gkr_ref[...], cos, sin)
    for hd in range(nh):
        kn = kvb[:, hd * MLA_NOPE:(hd + 1) * MLA_NOPE]
        ss = jnp.sum(kn * kn, axis=-1, keepdims=True) + ss_pe
        r = lax.rsqrt(ss * (1.0 / qk_dim) + EPS)
        km_ref[:, hd * MXU_DIM:hd * MXU_DIM + LANES] = (kn * r * gkn_ref[...]).astype(BF16)
        km_ref[:, hd * MXU_DIM + LANES:(hd + 1) * MXU_DIM] = (kpe_rot * r).astype(BF16)
    vm_ref[...] = kvb[:, nh * MLA_NOPE:].astype(BF16)

    hp = _dot(perm_ref[...], h).astype(BF16)
    for c, (g_ref, dst) in enumerate(((gqa_ref, qa_ref), (gka_ref, ka_ref))):
        src = _dot_nt(hp, wint_ref[c * w:(c + 1) * w, :].astype(BF16))
        sq = (src * src).astype(BF16)
        ms = jnp.concatenate([_dot(sq[:, c0:c0 + MXU_DIM], eh_ref[...]) for c0 in range(0, w, MXU_DIM)],
                             axis=1) * (1.0 / DIL_HEAD_DIM)
        dst[...] = (src * lax.rsqrt(ms + EPS) * g_ref[...]).reshape(dst.shape)
    va_ref[...] = _dot_nt(hp, wint_ref[2 * w:o, :].astype(BF16)).reshape(va_ref.shape)


def _proj_call(x2d, batch, seq, consts, cos, sin, *, n_mla_heads, dil_width):
    m, d = x2d.shape
    tm = ROW_TILE
    n_seq_blocks = seq // tm
    row = lambda width: pl.BlockSpec((tm, width), lambda i: (i, 0))
    pos = pl.BlockSpec((tm, LANES), lambda i: (i % n_seq_blocks, 0))
    mla_w = n_mla_heads * MXU_DIM
    dil_shape = jax.ShapeDtypeStruct((batch, DIL_RES, seq // DIL_RES, dil_width), F32)
    return pl.pallas_call(
        functools.partial(_proj_kernel, n_mla_heads=n_mla_heads, dil_width=dil_width),
        grid=(m // tm,),
        in_specs=[row(d)] + [_resident(c.shape) for c in consts] + [pos, pos],
        out_specs=[_residue_major_block(seq, dil_width)] * 3
        + [row(mla_w), row(mla_w), row(n_mla_heads * MLA_V_DIM)],
        out_shape=[dil_shape] * 3
        + [jax.ShapeDtypeStruct((m, mla_w), BF16)] * 2
        + [jax.ShapeDtypeStruct((m, n_mla_heads * MLA_V_DIM), BF16)],
        compiler_params=pltpu.CompilerParams(dimension_semantics=("parallel",),
                                             vmem_limit_bytes=VMEM_LIMIT_BYTES),
        name="proj",
    )(x2d, *consts, cos, sin)


def _t5_bucket(dist):
    max_exact = REL_BUCKETS // 2
    d = np.maximum(dist, 1).astype(np.float32)
    large = max_exact + (np.log(d / max_exact) / np.log(REL_MAX_DIST / max_exact)
                         * (REL_BUCKETS - max_exact)).astype(np.int32)
    large = np.minimum(large, REL_BUCKETS - 1)
    return np.where(dist < max_exact, dist, large).astype(np.int32)


def _band_tables():
    rho = np.arange(DIL_BAND)
    kap = np.arange(2 * DIL_BAND)
    buckets, prev = [], []
    for _, dil in DIL_BRANCHES:
        g = DIL_RES // dil
        run_q = DIL_BAND // g
        pos_q = g * (rho % run_q) + rho // run_q
        pos_k = g * (kap % (2 * run_q)) + kap // (2 * run_q) - DIL_BAND
        delta = pos_q[:, None] - pos_k[None, :]
        valid = (delta >= 0) & (delta <= DIL_BAND)
        buckets.append(np.where(valid, _t5_bucket(np.clip(delta, 0, None) * dil), -1))
        prev.append((pos_k < 0)[None, :])
    return np.stack(buckets).astype(np.int32), np.stack(prev).astype(np.int32)


def _dil_bias_kernel(rel_ref, bucket_ref, prev_ref, o_ref, *, present):
    nh = o_ref.shape[2]
    rb = 16
    scaled = {}
    for bi, buckets_here in enumerate(present):
        prev = prev_ref[bi] > 0
        for r0 in range(0, DIL_BAND, rb):
            bucket = bucket_ref[bi, r0:r0 + rb, :]
            accs = [jnp.full(bucket.shape, NEG, F32)] * nh
            for b in buckets_here:
                hit = bucket == b
                for hd in range(nh):
                    if (hd, b) not in scaled:
                        scaled[hd, b] = rel_ref[hd, b] * LOG2E
                accs = [jnp.where(hit, scaled[hd, b], acc) for hd, acc in enumerate(accs)]
            for hd in range(nh):
                o_ref[bi, 0, hd, r0:r0 + rb, :] = accs[hd]
                o_ref[bi, 1, hd, r0:r0 + rb, :] = jnp.where(prev, NEG, accs[hd])


def _dil_bias_call(rel_bias, buckets, prev):
    nbr = buckets.shape[0]
    nh = rel_bias.shape[0]
    tile = buckets.shape[1:]
    present = tuple(tuple(int(b) for b in np.unique(t) if b >= 0) for t in buckets)
    whole = lambda shape: pl.BlockSpec(shape, lambda i: (0,) * len(shape))
    out_shape = (nbr, 2, nh) + tile
    return pl.pallas_call(
        functools.partial(_dil_bias_kernel, present=present),
        grid=(1,),
        in_specs=[pl.BlockSpec(memory_space=pltpu.SMEM), whole(buckets.shape), whole(prev.shape)],
        out_specs=whole(out_shape),
        out_shape=jax.ShapeDtypeStruct(out_shape, F32),
        name="dil_bias",
    )(rel_bias, jnp.asarray(buckets), jnp.asarray(prev))


def _band_tile(q, k, v, bias, head_a):
    zero = jnp.zeros_like(q)
    q2 = jnp.concatenate([jnp.where(head_a, q, zero), jnp.where(head_a, zero, q)], axis=0).astype(BF16)
    s = _dot_nt(q2, k.astype(BF16)) + bias
    m = jnp.max(s, axis=-1, keepdims=True)
    p = jnp.exp2(s - m).astype(BF16)
    ones = jnp.ones(v.shape, BF16)
    out = _dot(p, jnp.concatenate([v.astype(BF16), ones], axis=1))
    h = DIL_BAND
    num = jnp.where(head_a, out[:h, :LANES], out[h:, :LANES])
    den = jnp.where(head_a, out[:h, LANES:], out[h:, LANES:])
    mx = jnp.where(head_a, m[:h], m[h:])
    return num, den, mx


def _dil_kernel(q_ref, kp_ref, kc_ref, vp_ref, vc_ref, bias_ref, o_ref, num_scr, den_scr, max_scr):
    first_block = (pl.program_id(2) == 0).astype(jnp.int32)
    head_a = lax.broadcasted_iota(jnp.int32, (1, LANES), 1) < DIL_HEAD_DIM

    def tile(j, bi, dil, rbase, n):
        g = DIL_RES // dil
        run = DIL_BAND // g
        base = j * DIL_BAND
        q_rows = pl.ds(base + n * run, run)
        qs, ks, vs = [], [], []
        for u in range(g):
            res = rbase + dil * u
            qs.append(q_ref[res, q_rows, :])
            if n == 0 and j == 0:
                ks += [kp_ref[res, pl.ds(DIL_BAND - run, run), :], kc_ref[res, pl.ds(0, run), :]]
                vs += [vp_ref[res, pl.ds(DIL_BAND - run, run), :], vc_ref[res, pl.ds(0, run), :]]
            else:
                k_rows = pl.ds(base + (n - 1) * run, 2 * run)
                ks.append(kc_ref[res, k_rows, :])
                vs.append(vc_ref[res, k_rows, :])
        cat = lambda parts: parts[0] if len(parts) == 1 else jnp.concatenate(parts, axis=0)
        variant = first_block if (n == 0 and j == 0) else 0
        num, den, mx = _band_tile(cat(qs), cat(ks), cat(vs), bias_ref[bi, variant], head_a)
        if g == 1:
            return num, den, mx
        s_rows = pl.ds(n * run, run)
        for u in range(g):
            res = rbase + dil * u
            part = slice(u * run, (u + 1) * run)
            num_scr[j, bi, res, s_rows, :] = num[part]
            den_scr[j, bi, res, s_rows, :] = den[part]
            max_scr[j, bi, res, s_rows, :] = mx[part]

    stored = len(DIL_BRANCHES) - 1
    for j in range(q_ref.shape[1] // DIL_BAND):
        for n in range(DIL_RES):
            tile(j, 0, 1, 0, n)
        for r4 in range(4):
            for n in range(4):
                tile(j, 1, 4, r4, n)
        for r in range(DIL_RES):
            num_r, den_r, max_r = tile(j, stored, DIL_RES, r, 0)
            ms = [max_scr[j, bi, r] for bi in range(stored)] + [max_r]
            nums = [num_scr[j, bi, r] for bi in range(stored)] + [num_r]
            dens = [den_scr[j, bi, r] for bi in range(stored)] + [den_r]
            m_all = functools.reduce(jnp.maximum, ms)
            es = [jnp.exp2(mb - m_all) for mb in ms]
            num = sum(e * x for e, x in zip(es, nums))
            den = sum(e * x for e, x in zip(es, dens))
            o_ref[pl.ds(j * DIL_CHUNK + r, DIL_BAND, stride=DIL_RES), :] = num / den


def _dil_call(qa, ka, va, bias):
    b, _, per, w = qa.shape
    pairs = w // LANES
    nbr = bias.shape[0]
    rows = DIL_STEP_CHUNKS * DIL_BAND
    cur = pl.BlockSpec((None, DIL_RES, rows, LANES), lambda bb, p, c: (bb, 0, c, p))
    prev = pl.BlockSpec((None, DIL_RES, DIL_BAND, LANES),
                        lambda bb, p, c: (bb, 0, jnp.maximum(DIL_STEP_CHUNKS * c - 1, 0), p))
    scr = pltpu.VMEM((DIL_STEP_CHUNKS, nbr - 1, DIL_RES, DIL_BAND, LANES), F32)
    return pl.pallas_call(
        _dil_kernel,
        grid=(b, pairs, per // rows),
        in_specs=[cur, prev, cur, prev, cur,
                  pl.BlockSpec((nbr, 2, None, 2 * DIL_BAND, 2 * DIL_BAND), lambda bb, p, c: (0, 0, p, 0, 0))],
        out_specs=pl.BlockSpec((None, DIL_STEP_CHUNKS * DIL_CHUNK, LANES), lambda bb, p, c: (bb, c, p)),
        out_shape=jax.ShapeDtypeStruct((b, per * DIL_RES, w), F32),
        scratch_shapes=[scr, scr, scr],
        compiler_params=pltpu.CompilerParams(dimension_semantics=("parallel", "parallel", "arbitrary"),
                                             vmem_limit_bytes=VMEM_LIMIT_BYTES),
        name="dil_attn",
    )(qa, ka, ka, va, va, bias)


def _mla_kernel(q_ref, k_ref, v_ref, o_ref, m_ref, acc_ref, s0_ref, s1_ref):
    tq = m_ref.shape[0]
    tk = MLA_TK
    nsub = tq // tk
    ones = jnp.ones((tk, LANES), BF16)
    s_bufs = (s0_ref, s1_ref)

    def q_tile(qi):
        q_lo = qi * tq
        m_ref[...] = jnp.full(m_ref.shape, NEG, F32)
        acc_ref[...] = jnp.zeros(acc_ref.shape, F32)

        def scores(kblk, row0=0):
            k0 = pl.multiple_of(kblk * tk, tk)
            return _dot_nt(q_ref[q_lo + row0:q_lo + tq, :], k_ref[pl.ds(k0, tk), :])

        def update(s, kblk, row0, diagonal):
            k0 = pl.multiple_of(kblk * tk, tk)
            rows = slice(row0, tq)
            if diagonal:
                row = lax.broadcasted_iota(jnp.int32, s.shape, 0)
                col = lax.broadcasted_iota(jnp.int32, s.shape, 1)
                s = jnp.where(col <= row, s, NEG)
            m_old = m_ref[rows, :]
            m_new = jnp.maximum(m_old, jnp.max(s, axis=-1, keepdims=True))
            m_ref[rows, :] = m_new
            p = jnp.concatenate([jnp.exp2(s[:, c * LANES:(c + 1) * LANES] - m_new) for c in range(tk // LANES)],
                                axis=1).astype(BF16)
            alpha = jnp.exp2(m_old - m_new)
            vext = jnp.concatenate([v_ref[pl.ds(k0, tk), :], ones], axis=1)
            acc_ref[rows, :] = acc_ref[rows, :] * jnp.concatenate([alpha, alpha], axis=1) + _dot(p, vext)

        s0_ref[...] = scores(0)

        def body(t, carry):
            for jj in range(nsub):
                s_bufs[(jj + 1) % 2][...] = scores(nsub * t + jj + 1)
                update(s_bufs[jj % 2][...], nsub * t + jj, 0, diagonal=False)
            return carry

        if qi:
            lax.fori_loop(0, qi, body, 0)
        s_cur = s0_ref[...]
        for d in range(nsub):
            s_next = scores(nsub * qi + d + 1, row0=(d + 1) * tk) if d + 1 < nsub else None
            update(s_cur, nsub * qi + d, d * tk, diagonal=True)
            s_cur = s_next
        acc = acc_ref[...]
        o_ref[q_lo:q_lo + tq, :] = acc[:, :MLA_V_DIM] / acc[:, MLA_V_DIM:]

    for qi in range(q_ref.shape[0] // tq):
        q_tile(qi)


def _mla_call(qm, km, vm, n_heads):
    b, s, _ = qm.shape
    tq = MLA_TQ
    assert tq % (2 * MLA_TK) == 0
    seq_block = lambda width: pl.BlockSpec((None, s, width), lambda bb, h: (bb, 0, h))
    return pl.pallas_call(
        _mla_kernel,
        grid=(b, n_heads),
        in_specs=[seq_block(MXU_DIM), seq_block(MXU_DIM), seq_block(MLA_V_DIM)],
        out_specs=seq_block(MLA_V_DIM),
        out_shape=jax.ShapeDtypeStruct((b, s, n_heads * MLA_V_DIM), F32),
        scratch_shapes=[pltpu.VMEM((tq, LANES), F32), pltpu.VMEM((tq, 2 * MLA_V_DIM), F32),
                        pltpu.VMEM((tq, MLA_TK), F32), pltpu.VMEM((tq, MLA_TK), F32)],
        compiler_params=pltpu.CompilerParams(dimension_semantics=("parallel", "parallel"),
                                             vmem_limit_bytes=VMEM_LIMIT_BYTES),
        name="mla_attn",
    )(qm, km, vm)


def _rope_tables(seq):
    inv_freq = ROPE_BASE ** (-np.arange(0, MLA_ROPE, 2, dtype=np.float64) / MLA_ROPE)
    ang = np.arange(seq, dtype=np.float64)[:, None] * inv_freq[None, :]
    cos, sin = np.cos(ang), np.sin(ang)
    cos_t = np.concatenate([cos, cos] * (LANES // MLA_ROPE), axis=1)
    sin_t = np.concatenate([-sin, sin] * (LANES // MLA_ROPE), axis=1)
    return jnp.asarray(cos_t, F32), jnp.asarray(sin_t, F32)


def _pad_lanes(v, width):
    return jnp.pad(v, ((0, 0), (0, width - v.shape[1])))


def kernel(x, ffn1_norm, ffn1_w_gate, ffn1_w_up, ffn1_w_down, mix_norm, w_in, dil_q_norm, dil_k_norm,
           rel_bias, mla_q_a_norm, mla_w_q_b, mla_kv_a_norm, mla_w_kv_b, mla_q_norm, mla_k_norm,
           out_norm_dil, out_norm_mla, w_out, ffn2_norm, ffn2_w_gate, ffn2_w_up, ffn2_w_down):
    batch, seq, d_model = x.shape
    depth = w_in.shape[0]
    dil_width = out_norm_dil.shape[1]
    dil_heads = dil_width // DIL_HEAD_DIM
    q_rank = mla_q_a_norm.shape[1]
    kv_rank = mla_kv_a_norm.shape[1]
    qk_dim = MLA_NOPE + MLA_ROPE
    n_mla = mla_w_q_b.shape[2] // qk_dim
    assert rel_bias.shape == (dil_heads, REL_BUCKETS)
    assert w_in.shape[2] == 3 * dil_width + q_rank + kv_rank + MLA_ROPE
    assert seq % (DIL_STEP_CHUNKS * DIL_CHUNK) == 0 and seq % MLA_TQ == 0 and seq % ROW_TILE == 0
    assert (batch * seq) % FFN_STEP_ROWS == 0
    assert all(win // dil == DIL_BAND and DIL_RES % dil == 0 for win, dil in DIL_BRANCHES)
    assert DIL_BRANCHES[-1][1] == DIL_RES

    cos_t, sin_t = _rope_tables(seq)
    buckets, prev = _band_tables()
    dil_bias = _dil_bias_call(rel_bias, buckets, prev)
    dil_bias = dil_bias.reshape(dil_bias.shape[0], 2, dil_heads // 2, 2 * DIL_BAND, 2 * DIL_BAND)
    eh = jnp.asarray(np.kron(np.eye(MXU_DIM // DIL_HEAD_DIM), np.ones((DIL_HEAD_DIM, DIL_HEAD_DIM))), BF16)
    perm = jnp.asarray(_residue_major_perm(ROW_TILE), BF16)

    x2d = x.reshape(batch * seq, d_model)
    row = lambda v: v.reshape(1, -1)
    for l in range(depth):
        x2d = _ffn_call(x2d, row(ffn1_norm[l]), ffn1_w_gate[l], ffn1_w_up[l], ffn1_w_down[l])

        w_in_t = jnp.swapaxes(w_in[l], 0, 1)
        wkpe_t = jnp.pad(w_in_t[w_in.shape[2] - MLA_ROPE:], ((0, LANES - MLA_ROPE), (0, 0))).astype(BF16)
        wqb = mla_w_q_b[l].reshape(q_rank, n_mla, qk_dim)
        wqb = jnp.concatenate(
            [wqb[:, :, :MLA_NOPE].reshape(q_rank, n_mla * MLA_NOPE),
             jnp.pad(wqb[:, :, MLA_NOPE:], ((0, 0), (0, 0), (0, LANES - MLA_ROPE))).reshape(q_rank, n_mla * LANES)],
            axis=1).astype(BF16)
        wkvb = mla_w_kv_b[l].reshape(kv_rank, n_mla, MLA_NOPE + MLA_V_DIM)
        wkvb = jnp.concatenate([wkvb[:, :, :MLA_NOPE].reshape(kv_rank, n_mla * MLA_NOPE),
                                wkvb[:, :, MLA_NOPE:].reshape(kv_rank, n_mla * MLA_V_DIM)], axis=1).astype(BF16)
        gqa = row(jnp.tile(dil_q_norm[l], dil_heads)) * (LOG2E * DIL_HEAD_DIM ** -0.5)
        gka = row(jnp.tile(dil_k_norm[l], dil_heads))
        gq = row(mla_q_norm[l]) * (LOG2E * qk_dim ** -0.5)
        gk = row(mla_k_norm[l])
        gqn, gqr = gq[:, :MLA_NOPE], _pad_lanes(gq[:, MLA_NOPE:], LANES)
        gkn, gkr = gk[:, :MLA_NOPE], _pad_lanes(gk[:, MLA_NOPE:], LANES)

        consts = [row(mix_norm[l]), w_in_t, wkpe_t, eh, perm, gqa, gka, row(mla_q_a_norm[l]), wqb,
                  row(mla_kv_a_norm[l]), wkvb, gqn, gqr, gkn, gkr]
        qa, ka, va, qm, km, vm = _proj_call(x2d, batch, seq, consts, cos_t, sin_t,
                                            n_mla_heads=n_mla, dil_width=dil_width)

        shp = lambda a: a.reshape(batch, seq, a.shape[1])
        o_dil = _dil_call(qa, ka, va, dil_bias)
        o_mla = _mla_call(shp(qm), shp(km), shp(vm), n_mla)

        x2d = _out_ffn_call(x2d, o_dil.reshape(batch * seq, -1), o_mla.reshape(batch * seq, -1),
                            row(out_norm_dil[l]), row(out_norm_mla[l]), w_out[l].astype(BF16),
                            row(ffn2_norm[l]), ffn2_w_gate[l], ffn2_w_up[l], ffn2_w_down[l])
    return x2d.reshape(batch, seq, d_model)
```

```python
import functools

import numpy as np
import jax
import jax.numpy as jnp
from jax import lax
from jax.experimental import pallas as pl
from jax.experimental.pallas import tpu as pltpu

F32 = jnp.float32
BF16 = jnp.bfloat16

DIL_HEAD_DIM = 64
DIL_BRANCHES = ((128, 1), (512, 4), (2048, 16))
DIL_BAND = 128
DIL_RES = 16
MLA_NOPE = 128
MLA_ROPE = 64
MLA_V_DIM = 128
ROPE_BASE = 10000.0
REL_BUCKETS = 32
REL_MAX_DIST = 2048
FFN_RESID = 0.5
EPS = 1e-6

LANES = 128
MXU_DIM = 256
VMEM_LIMIT_BYTES = 56 * 1024 * 1024
VMEM_LIMIT_FFN_BYTES = 60 * 1024 * 1024

NEG = float("-inf")
LOG2E = 1.4426950408889634

ROW_TILE = 512
FFN_STEP_ROWS = 2 * ROW_TILE
FFN_TF = 256
DIL_CHUNK = DIL_BAND * DIL_RES
DIL_STEP_CHUNKS = 2
MLA_TQ = 2048
MLA_TK = 512


def _dot(a, b):
    return jnp.dot(a, b, preferred_element_type=F32)


def _dot_nt(a, b):
    return lax.dot_general(a, b, (((1,), (1,)), ((), ())), preferred_element_type=F32)


def _rms(x, g):
    return x * lax.rsqrt(jnp.mean(x * x, axis=-1, keepdims=True) + EPS) * g


def _residue_major_perm(rows):
    per = rows // DIL_RES
    dst = np.arange(rows)
    src = DIL_RES * (dst % per) + dst // per
    p = np.zeros((rows, rows), np.float32)
    p[dst, src] = 1.0
    return p


def _swiglu_residual(x, g_ref, wg_ref, wu_ref, wd_ref, act_ref):
    h = _rms(x, g_ref[...]).astype(BF16)
    d_ff = wg_ref.shape[1]
    for c in range(d_ff // FFN_TF):
        sl = slice(c * FFN_TF, (c + 1) * FFN_TF)
        gate = _dot(h, wg_ref[:, sl].astype(BF16))
        up = _dot(h, wu_ref[:, sl].astype(BF16))
        act_ref[:, sl] = (gate * (0.5 * jnp.tanh(0.5 * gate) + 0.5) * up).astype(BF16)
    return x + FFN_RESID * _dot(act_ref[...], wd_ref[...].astype(BF16))


def _ffn_kernel(x_ref, g_ref, wg_ref, wu_ref, wd_ref, o_ref, act_ref):
    for r0 in range(0, x_ref.shape[0], ROW_TILE):
        rows = slice(r0, r0 + ROW_TILE)
        o_ref[rows, :] = _swiglu_residual(x_ref[rows, :], g_ref, wg_ref, wu_ref, wd_ref, act_ref)


def _out_ffn_kernel(x_hbm, od_hbm, om_hbm, gd_ref, gm_ref, wo_ref, g_ref, wg_ref, wu_ref, wd_ref,
                    o_hbm, act_ref):
    def step(x_ref, od_ref, om_ref, o_ref):
        od = _rms(od_ref[...], gd_ref[...]).astype(BF16)
        om = _rms(om_ref[...], gm_ref[...]).astype(BF16)
        x2 = x_ref[...] + _dot(jnp.concatenate([od, om], axis=1), wo_ref[...])
        o_ref[...] = _swiglu_residual(x2, g_ref, wg_ref, wu_ref, wd_ref, act_ref)

    row = lambda ref: pl.BlockSpec((ROW_TILE, ref.shape[1]), lambda i: (i, 0))
    pltpu.emit_pipeline(step, grid=(x_hbm.shape[0] // ROW_TILE,),
                        in_specs=[row(x_hbm), row(od_hbm), row(om_hbm)],
                        out_specs=[row(o_hbm)])(x_hbm, od_hbm, om_hbm, o_hbm)


def _resident(shape):
    return pl.BlockSpec(shape, lambda *_: (0,) * len(shape), pipeline_mode=pl.Buffered(1))


def _residue_major_block(seq, width):
    per = ROW_TILE // DIL_RES
    blocks_per_seq = seq // ROW_TILE
    return pl.BlockSpec((None, DIL_RES, per, width), lambda i: (i // blocks_per_seq, 0, i % blocks_per_seq, 0))


def _ffn_call(x2d, g, wg, wu, wd):
    m, d = x2d.shape
    f = wg.shape[1]
    row = pl.BlockSpec((FFN_STEP_ROWS, d), lambda i: (i, 0))
    return pl.pallas_call(
        _ffn_kernel,
        grid=(m // FFN_STEP_ROWS,),
        in_specs=[row, _resident((1, d)), _resident((d, f)), _resident((d, f)), _resident((f, d))],
        out_specs=row,
        out_shape=jax.ShapeDtypeStruct((m, d), F32),
        scratch_shapes=[pltpu.VMEM((ROW_TILE, f), BF16)],
        compiler_params=pltpu.CompilerParams(dimension_semantics=("parallel",),
                                             vmem_limit_bytes=VMEM_LIMIT_FFN_BYTES),
        name="ffn",
    )(x2d, g, wg, wu, wd)


def _out_ffn_call(x2d, o_dil, o_mla, gd, gm, wo, g, wg, wu, wd):
    m, d = x2d.shape
    f = wg.shape[1]
    wdil, wmla = o_dil.shape[1], o_mla.shape[1]
    hbm = pl.BlockSpec(memory_space=pl.ANY)
    vmem = pl.BlockSpec(memory_space=pltpu.VMEM)
    return pl.pallas_call(
        _out_ffn_kernel,
        in_specs=[hbm, hbm, hbm, vmem, vmem, vmem, vmem, vmem, vmem, vmem],
        out_specs=hbm,
        out_shape=jax.ShapeDtypeStruct((m, d), F32),
        scratch_shapes=[pltpu.VMEM((ROW_TILE, f), BF16)],
        compiler_params=pltpu.CompilerParams(vmem_limit_bytes=VMEM_LIMIT_BYTES),
        name="out_ffn",
    )(x2d, o_dil, o_mla, gd, gm, wo, g, wg, wu, wd)


def _rope(x, cos, sin_signed):
    lane = lax.broadcasted_iota(jnp.int32, x.shape, 1)
    first_half = (lane % MLA_ROPE) < (MLA_ROPE // 2)
    partner = jnp.where(first_half,
                        pltpu.roll(x, LANES - MLA_ROPE // 2, 1),
                        pltpu.roll(x, MLA_ROPE // 2, 1))
    return x * cos + partner * sin_signed


def _proj_kernel(x_ref, gmix_ref, wint_ref, wkpet_ref, eh_ref, perm_ref, gqa_ref, gka_ref,
                 gcq_ref, wqb_ref, gckv_ref, wkvb_ref,
                 gqn_ref, gqr_ref, gkn_ref, gkr_ref, cos_ref, sin_ref,
                 qa_ref, ka_ref, va_ref, qm_ref, km_ref, vm_ref, *, n_mla_heads, dil_width):
    h = _rms(x_ref[...], gmix_ref[...]).astype(BF16)
    w = dil_width
    o = 3 * w
    q_rank = gcq_ref.shape[1]
    kv_rank = gckv_ref.shape[1]
    qk_dim = MLA_NOPE + MLA_ROPE
    nh = n_mla_heads
    cos = cos_ref[...]
    sin = sin_ref[...]

    w_mla = jnp.concatenate([wint_ref[o:o + q_rank + kv_rank, :].astype(BF16), wkpet_ref[...]], axis=0)
    pm = _dot_nt(h, w_mla)
    cq = pm[:, :q_rank]
    ckv = pm[:, q_rank:q_rank + kv_rank]
    kpe = pm[:, q_rank + kv_rank:]

    qb = _dot(_rms(cq, gcq_ref[...]).astype(BF16), wqb_ref[...])
    for hd in range(nh):
        qn = qb[:, hd * MLA_NOPE:(hd + 1) * MLA_NOPE]
        qr = qb[:, nh * MLA_NOPE + hd * LANES:nh * MLA_NOPE + (hd + 1) * LANES]
        ss = jnp.sum(qn * qn, axis=-1, keepdims=True) + jnp.sum(qr * qr, axis=-1, keepdims=True)
        r = lax.rsqrt(ss * (1.0 / qk_dim) + EPS)
        qm_ref[:, hd * MXU_DIM:hd * MXU_DIM + LANES] = (qn * r * gqn_ref[...]).astype(BF16)
        qm_ref[:, hd * MXU_DIM + LANES:(hd + 1) * MXU_DIM] = _rope(qr * r * gqr_ref[...], cos, sin).astype(BF16)

    kvb = _dot(_rms(ckv, gckv_ref[...]).astype(BF16), wkvb_ref[...])
    ss_pe = jnp.sum(kpe * kpe, axis=-1, keepdims=True)
    kpe_rot = _rope(kpe * gkr_ref[...], cos, sin)
    for hd in range(nh):
        kn = kvb[:, hd * MLA_NOPE:(hd + 1) * MLA_NOPE]
        ss = jnp.sum(kn * kn, axis=-1, keepdims=True) + ss_pe
        r = lax.rsqrt(ss * (1.0 / qk_dim) + EPS)
        km_ref[:, hd * MXU_DIM:hd * MXU_DIM + LANES] = (kn * r * gkn_ref[...]).astype(BF16)
        km_ref[:, hd * MXU_DIM + LANES:(hd + 1) * MXU_DIM] = (kpe_rot * r).astype(BF16)
    vm_ref[...] = kvb[:, nh * MLA_NOPE:].astype(BF16)

    hp = _dot(perm_ref[...], h).astype(BF16)
    for c, (g_ref, dst) in enumerate(((gqa_ref, qa_ref), (gka_ref, ka_ref))):
        src = _dot_nt(hp, wint_ref[c * w:(c + 1) * w, :].astype(BF16))
        sq = (src * src).astype(BF16)
        ms = jnp.concatenate([_dot(sq[:, c0:c0 + MXU_DIM], eh_ref[...]) for c0 in range(0, w, MXU_DIM)],
                             axis=1) * (1.0 / DIL_HEAD_DIM)
        dst[...] = (src * lax.rsqrt(ms + EPS) * g_ref[...]).reshape(dst.shape)
    va_ref[...] = _dot_nt(hp, wint_ref[2 * w:o, :].astype(BF16)).reshape(va_ref.shape)


def _proj_call(x2d, batch, seq, consts, cos, sin, *, n_mla_heads, dil_width):
    m, d = x2d.shape
    tm = ROW_TILE
    n_seq_blocks = seq // tm
    row = lambda width: pl.BlockSpec((tm, width), lambda i: (i, 0))
    pos = pl.BlockSpec((tm, LANES), lambda i: (i % n_seq_blocks, 0))
    mla_w = n_mla_heads * MXU_DIM
    dil_shape = jax.ShapeDtypeStruct((batch, DIL_RES, seq // DIL_RES, dil_width), F32)
    return pl.pallas_call(
        functools.partial(_proj_kernel, n_mla_heads=n_mla_heads, dil_width=dil_width),
        grid=(m // tm,),
        in_specs=[row(d)] + [_resident(c.shape) for c in consts] + [pos, pos],
        out_specs=[_residue_major_block(seq, dil_width)] * 3
        + [row(mla_w), row(mla_w), row(n_mla_heads * MLA_V_DIM)],
        out_shape=[dil_shape] * 3
        + [jax.ShapeDtypeStruct((m, mla_w), BF16)] * 2
        + [jax.ShapeDtypeStruct((m, n_mla_heads * MLA_V_DIM), BF16)],
        compiler_params=pltpu.CompilerParams(dimension_semantics=("parallel",),
                                             vmem_limit_bytes=VMEM_LIMIT_BYTES),
        name="proj",
    )(x2d, *consts, cos, sin)


def _t5_bucket(dist):
    max_exact = REL_BUCKETS // 2
    d = np.maximum(dist, 1).astype(np.float32)
    large = max_exact + (np.log(d / max_exact) / np.log(REL_MAX_DIST / max_exact)
                         * (REL_BUCKETS - max_exact)).astype(np.int32)
    large = np.minimum(large, REL_BUCKETS - 1)
    return np.where(dist < max_exact, dist, large).astype(np.int32)


def _band_tables():
    rho = np.arange(DIL_BAND)
    kap = np.arange(2 * DIL_BAND)
    buckets, prev = [], []
    for _, dil in DIL_BRANCHES:
        g = DIL_RES // dil
        run_q = DIL_BAND // g
        pos_q = g * (rho % run_q) + rho // run_q
        pos_k = g * (kap % (2 * run_q)) + kap // (2 * run_q) - DIL_BAND
        delta = pos_q[:, None] - pos_k[None, :]
        valid = (delta >= 0) & (delta <= DIL_BAND)
        buckets.append(np.where(valid, _t5_bucket(np.clip(delta, 0, None) * dil), -1))
        prev.append((pos_k < 0)[None, :])
    return np.stack(buckets).astype(np.int32), np.stack(prev).astype(np.int32)


def _dil_bias_kernel(rel_ref, bucket_ref, prev_ref, o_ref, *, present):
    nh = o_ref.shape[2]
    rb = 16
    scaled = {}
    for bi, buckets_here in enumerate(present):
        prev = prev_ref[bi] > 0
        for r0 in range(0, DIL_BAND, rb):
            bucket = bucket_ref[bi, r0:r0 + rb, :]
            accs = [jnp.full(bucket.shape, NEG, F32)] * nh
            for b in buckets_here:
                hit = bucket == b
                for hd in range(nh):
                    if (hd, b) not in scaled:
                        scaled[hd, b] = rel_ref[hd, b] * LOG2E
                accs = [jnp.where(hit, scaled[hd, b], acc) for hd, acc in enumerate(accs)]
            for hd in range(nh):
                o_ref[bi, 0, hd, r0:r0 + rb, :] = accs[hd]
                o_ref[bi, 1, hd, r0:r0 + rb, :] = jnp.where(prev, NEG, accs[hd])


def _dil_bias_call(rel_bias, buckets, prev):
    nbr = buckets.shape[0]
    nh = rel_bias.shape[0]
    tile = buckets.shape[1:]
    present = tuple(tuple(int(b) for b in np.unique(t) if b >= 0) for t in buckets)
    whole = lambda shape: pl.BlockSpec(shape, lambda i: (0,) * len(shape))
    out_shape = (nbr, 2, nh) + tile
    return pl.pallas_call(
        functools.partial(_dil_bias_kernel, present=present),
        grid=(1,),
        in_specs=[pl.BlockSpec(memory_space=pltpu.SMEM), whole(buckets.shape), whole(prev.shape)],
        out_specs=whole(out_shape),
        out_shape=jax.ShapeDtypeStruct(out_shape, F32),
        name="dil_bias",
    )(rel_bias, jnp.asarray(buckets), jnp.asarray(prev))


def _band_tile(q, k, v, bias, head_a):
    zero = jnp.zeros_like(q)
    q2 = jnp.concatenate([jnp.where(head_a, q, zero), jnp.where(head_a, zero, q)], axis=0).astype(BF16)
    s = _dot_nt(q2, k.astype(BF16)) + bias
    m = jnp.max(s, axis=-1, keepdims=True)
    p = jnp.exp2(s - m).astype(BF16)
    ones = jnp.ones(v.shape, BF16)
    out = _dot(p, jnp.concatenate([v.astype(BF16), ones], axis=1))
    h = DIL_BAND
    num = jnp.where(head_a, out[:h, :LANES], out[h:, :LANES])
    den = jnp.where(head_a, out[:h, LANES:], out[h:, LANES:])
    mx = jnp.where(head_a, m[:h], m[h:])
    return num, den, mx


def _dil_kernel(q_ref, kp_ref, kc_ref, vp_ref, vc_ref, bias_ref, o_ref, num_scr, den_scr, max_scr):
    first_block = (pl.program_id(2) == 0).astype(jnp.int32)
    head_a = lax.broadcasted_iota(jnp.int32, (1, LANES), 1) < DIL_HEAD_DIM

    def tile(j, bi, dil, rbase, n):
        g = DIL_RES // dil
        run = DIL_BAND // g
        base = j * DIL_BAND
        q_rows = pl.ds(base + n * run, run)
        qs, ks, vs = [], [], []
        for u in range(g):
            res = rbase + dil * u
            qs.append(q_ref[res, q_rows, :])
            if n == 0 and j == 0:
                ks += [kp_ref[res, pl.ds(DIL_BAND - run, run), :], kc_ref[res, pl.ds(0, run), :]]
                vs += [vp_ref[res, pl.ds(DIL_BAND - run, run), :], vc_ref[res, pl.ds(0, run), :]]
            else:
                k_rows = pl.ds(base + (n - 1) * run, 2 * run)
                ks.append(kc_ref[res, k_rows, :])
                vs.append(vc_ref[res, k_rows, :])
        cat = lambda parts: parts[0] if len(parts) == 1 else jnp.concatenate(parts, axis=0)
        variant = first_block if (n == 0 and j == 0) else 0
        num, den, mx = _band_tile(cat(qs), cat(ks), cat(vs), bias_ref[bi, variant], head_a)
        if g == 1:
            return num, den, mx
        s_rows = pl.ds(n * run, run)
        for u in range(g):
            res = rbase + dil * u
            part = slice(u * run, (u + 1) * run)
            num_scr[j, bi, res, s_rows, :] = num[part]
            den_scr[j, bi, res, s_rows, :] = den[part]
            max_scr[j, bi, res, s_rows, :] = mx[part]

    stored = len(DIL_BRANCHES) - 1
    for j in range(q_ref.shape[1] // DIL_BAND):
        for n in range(DIL_RES):
            tile(j, 0, 1, 0, n)
        for r4 in range(4):
            for n in range(4):
                tile(j, 1, 4, r4, n)
        for r in range(DIL_RES):
            num_r, den_r, max_r = tile(j, stored, DIL_RES, r, 0)
            ms = [max_scr[j, bi, r] for bi in range(stored)] + [max_r]
            nums = [num_scr[j, bi, r] for bi in range(stored)] + [num_r]
            dens = [den_scr[j, bi, r] for bi in range(stored)] + [den_r]
            m_all = functools.reduce(jnp.maximum, ms)
            es = [jnp.exp2(mb - m_all) for mb in ms]
            num = sum(e * x for e, x in zip(es, nums))
            den = sum(e * x for e, x in zip(es, dens))
            o_ref[pl.ds(j * DIL_CHUNK + r, DIL_BAND, stride=DIL_RES), :] = num / den


def _dil_call(qa, ka, va, bias):
    b, _, per, w = qa.shape
    pairs = w // LANES
    nbr = bias.shape[0]
    rows = DIL_STEP_CHUNKS * DIL_BAND
    cur = pl.BlockSpec((None, DIL_RES, rows, LANES), lambda bb, p, c: (bb, 0, c, p))
    prev = pl.BlockSpec((None, DIL_RES, DIL_BAND, LANES),
                        lambda bb, p, c: (bb, 0, jnp.maximum(DIL_STEP_CHUNKS * c - 1, 0), p))
    scr = pltpu.VMEM((DIL_STEP_CHUNKS, nbr - 1, DIL_RES, DIL_BAND, LANES), F32)
    return pl.pallas_call(
        _dil_kernel,
        grid=(b, pairs, per // rows),
        in_specs=[cur, prev, cur, prev, cur,
                  pl.BlockSpec((nbr, 2, None, 2 * DIL_BAND, 2 * DIL_BAND), lambda bb, p, c: (0, 0, p, 0, 0))],
        out_specs=pl.BlockSpec((None, DIL_STEP_CHUNKS * DIL_CHUNK, LANES), lambda bb, p, c: (bb, c, p)),
        out_shape=jax.ShapeDtypeStruct((b, per * DIL_RES, w), F32),
        scratch_shapes=[scr, scr, scr],
        compiler_params=pltpu.CompilerParams(dimension_semantics=("parallel", "parallel", "arbitrary"),
                                             vmem_limit_bytes=VMEM_LIMIT_BYTES),
        name="dil_attn",
    )(qa, ka, ka, va, va, bias)


def _mla_kernel(q_ref, k_ref, v_ref, o_ref, m_ref, acc_ref, s0_ref, s1_ref):
    tq = m_ref.shape[0]
    tk = MLA_TK
    nsub = tq // tk
    ones = jnp.ones((tk, LANES), BF16)
    s_bufs = (s0_ref, s1_ref)

    def q_tile(qi):
        q_lo = qi * tq
        m_ref[...] = jnp.full(m_ref.shape, NEG, F32)
        acc_ref[...] = jnp.zeros(acc_ref.shape, F32)

        def scores(kblk, row0=0):
            k0 = pl.multiple_of(kblk * tk, tk)
            return _dot_nt(q_ref[q_lo + row0:q_lo + tq, :], k_ref[pl.ds(k0, tk), :])

        def update(s, kblk, row0, diagonal):
            k0 = pl.multiple_of(kblk * tk, tk)
            rows = slice(row0, tq)
            if diagonal:
                row = lax.broadcasted_iota(jnp.int32, s.shape, 0)
                col = lax.broadcasted_iota(jnp.int32, s.shape, 1)
                s = jnp.where(col <= row, s, NEG)
            m_old = m_ref[rows, :]
            m_new = jnp.maximum(m_old, jnp.max(s, axis=-1, keepdims=True))
            m_ref[rows, :] = m_new
            p = jnp.concatenate([jnp.exp2(s[:, c * LANES:(c + 1) * LANES] - m_new) for c in range(tk // LANES)],
                                axis=1).astype(BF16)
            alpha = jnp.exp2(m_old - m_new)
            vext = jnp.concatenate([v_ref[pl.ds(k0, tk), :], ones], axis=1)
            acc_ref[rows, :] = acc_ref[rows, :] * jnp.concatenate([alpha, alpha], axis=1) + _dot(p, vext)

        s0_ref[...] = scores(0)

        def body(t, carry):
            for jj in range(nsub):
                s_bufs[(jj + 1) % 2][...] = scores(nsub * t + jj + 1)
                update(s_bufs[jj % 2][...], nsub * t + jj, 0, diagonal=False)
            return carry

        if qi:
            lax.fori_loop(0, qi, body, 0)
        s_cur = s0_ref[...]
        for d in range(nsub):
            s_next = scores(nsub * qi + d + 1, row0=(d + 1) * tk) if d + 1 < nsub else None
            update(s_cur, nsub * qi + d, d * tk, diagonal=True)
            s_cur = s_next
        acc = acc_ref[...]
        o_ref[q_lo:q_lo + tq, :] = acc[:, :MLA_V_DIM] / acc[:, MLA_V_DIM:]

    for qi in range(q_ref.shape[0] // tq):
        q_tile(qi)


def _mla_call(qm, km, vm, n_heads):
    b, s, _ = qm.shape
    tq = MLA_TQ
    assert tq % (2 * MLA_TK) == 0
    seq_block = lambda width: pl.BlockSpec((None, s, width), lambda bb, h: (bb, 0, h))
    return pl.pallas_call(
        _mla_kernel,
        grid=(b, n_heads),
        in_specs=[seq_block(MXU_DIM), seq_block(MXU_DIM), seq_block(MLA_V_DIM)],
        out_specs=seq_block(MLA_V_DIM),
        out_shape=jax.ShapeDtypeStruct((b, s, n_heads * MLA_V_DIM), F32),
        scratch_shapes=[pltpu.VMEM((tq, LANES), F32), pltpu.VMEM((tq, 2 * MLA_V_DIM), F32),
                        pltpu.VMEM((tq, MLA_TK), F32), pltpu.VMEM((tq, MLA_TK), F32)],
        compiler_params=pltpu.CompilerParams(dimension_semantics=("parallel", "parallel"),
                                             vmem_limit_bytes=VMEM_LIMIT_BYTES),
        name="mla_attn",
    )(qm, km, vm)


def _rope_tables(seq):
    inv_freq = ROPE_BASE ** (-np.arange(0, MLA_ROPE, 2, dtype=np.float64) / MLA_ROPE)
    ang = np.arange(seq, dtype=np.float64)[:, None] * inv_freq[None, :]
    cos, sin = np.cos(ang), np.sin(ang)
    cos_t = np.concatenate([cos, cos] * (LANES // MLA_ROPE), axis=1)
    sin_t = np.concatenate([-sin, sin] * (LANES // MLA_ROPE), axis=1)
    return jnp.asarray(cos_t, F32), jnp.asarray(sin_t, F32)


def _pad_lanes(v, width):
    return jnp.pad(v, ((0, 0), (0, width - v.shape[1])))


def kernel(x, ffn1_norm, ffn1_w_gate, ffn1_w_up, ffn1_w_down, mix_norm, w_in, dil_q_norm, dil_k_norm,
           rel_bias, mla_q_a_norm, mla_w_q_b, mla_kv_a_norm, mla_w_kv_b, mla_q_norm, mla_k_norm,
           out_norm_dil, out_norm_mla, w_out, ffn2_norm, ffn2_w_gate, ffn2_w_up, ffn2_w_down):
    batch, seq, d_model = x.shape
    depth = w_in.shape[0]
    dil_width = out_norm_dil.shape[1]
    dil_heads = dil_width // DIL_HEAD_DIM
    q_rank = mla_q_a_norm.shape[1]
    kv_rank = mla_kv_a_norm.shape[1]
    qk_dim = MLA_NOPE + MLA_ROPE
    n_mla = mla_w_q_b.shape[2] // qk_dim
    assert rel_bias.shape == (dil_heads, REL_BUCKETS)
    assert w_in.shape[2] == 3 * dil_width + q_rank + kv_rank + MLA_ROPE
    assert seq % (DIL_STEP_CHUNKS * DIL_CHUNK) == 0 and seq % MLA_TQ == 0 and seq % ROW_TILE == 0
    assert (batch * seq) % FFN_STEP_ROWS == 0
    assert all(win // dil == DIL_BAND and DIL_RES % dil == 0 for win, dil in DIL_BRANCHES)
    assert DIL_BRANCHES[-1][1] == DIL_RES

    cos_t, sin_t = _rope_tables(seq)
    buckets, prev = _band_tables()
    dil_bias = _dil_bias_call(rel_bias, buckets, prev)
    dil_bias = dil_bias.reshape(dil_bias.shape[0], 2, dil_heads // 2, 2 * DIL_BAND, 2 * DIL_BAND)
    eh = jnp.asarray(np.kron(np.eye(MXU_DIM // DIL_HEAD_DIM), np.ones((DIL_HEAD_DIM, DIL_HEAD_DIM))), BF16)
    perm = jnp.asarray(_residue_major_perm(ROW_TILE), BF16)

    x2d = x.reshape(batch * seq, d_model)
    row = lambda v: v.reshape(1, -1)
    for l in range(depth):
        x2d = _ffn_call(x2d, row(ffn1_norm[l]), ffn1_w_gate[l], ffn1_w_up[l], ffn1_w_down[l])

        w_in_t = jnp.swapaxes(w_in[l], 0, 1)
        wkpe_t = jnp.pad(w_in_t[w_in.shape[2] - MLA_ROPE:], ((0, LANES - MLA_ROPE), (0, 0))).astype(BF16)
        wqb = mla_w_q_b[l].reshape(q_rank, n_mla, qk_dim)
        wqb = jnp.concatenate(
            [wqb[:, :, :MLA_NOPE].reshape(q_rank, n_mla * MLA_NOPE),
             jnp.pad(wqb[:, :, MLA_NOPE:], ((0, 0), (0, 0), (0, LANES - MLA_ROPE))).reshape(q_rank, n_mla * LANES)],
            axis=1).astype(BF16)
        wkvb = mla_w_kv_b[l].reshape(kv_rank, n_mla, MLA_NOPE + MLA_V_DIM)
        wkvb = jnp.concatenate([wkvb[:, :, :MLA_NOPE].reshape(kv_rank, n_mla * MLA_NOPE),
                                wkvb[:, :, MLA_NOPE:].reshape(kv_rank, n_mla * MLA_V_DIM)], axis=1).astype(BF16)
        gqa = row(jnp.tile(dil_q_norm[l], dil_heads)) * (LOG2E * DIL_HEAD_DIM ** -0.5)
        gka = row(jnp.tile(dil_k_norm[l], dil_heads))
        gq = row(mla_q_norm[l]) * (LOG2E * qk_dim ** -0.5)
        gk = row(mla_k_norm[l])
        gqn, gqr = gq[:, :MLA_NOPE], _pad_lanes(gq[:, MLA_NOPE:], LANES)
        gkn, gkr = gk[:, :MLA_NOPE], _pad_lanes(gk[:, MLA_NOPE:], LANES)

        consts = [row(mix_norm[l]), w_in_t, wkpe_t, eh, perm, gqa, gka, row(mla_q_a_norm[l]), wqb,
                  row(mla_kv_a_norm[l]), wkvb, gqn, gqr, gkn, gkr]
        qa, ka, va, qm, km, vm = _proj_call(x2d, batch, seq, consts, cos_t, sin_t,
                                            n_mla_heads=n_mla, dil_width=dil_width)

        shp = lambda a: a.reshape(batch, seq, a.shape[1])
        o_dil = _dil_call(qa, ka, va, dil_bias)
        o_mla = _mla_call(shp(qm), shp(km), shp(vm), n_mla)

        x2d = _out_ffn_call(x2d, o_dil.reshape(batch * seq, -1), o_mla.reshape(batch * seq, -1),
                            row(out_norm_dil[l]), row(out_norm_mla[l]), w_out[l].astype(BF16),
                            row(ffn2_norm[l]), ffn2_w_gate[l], ffn2_w_up[l], ffn2_w_down[l])
    return x2d.reshape(batch, seq, d_model)
```
